```python
import jax, jax.numpy as jnp
from jax import lax
import numpy as np

D_MODEL = 1024
BATCH = 32
SEQ = 2048
DEPTH = 1
DEC_BATCH = 8
DEC_SEQ = 2048
PAST_LEN = 128

D_RNN = 1280
LRU_BLOCKS = 16
LRU_BW = D_RNN // LRU_BLOCKS
LRU_C = 8.0
LRU_CONV = 4
ATT_GROUPS = ((128, 1), (512, 4), (2048, 16))
N_GROUPS = len(ATT_GROUPS)
HEADS_PER_GROUP = 8
HEAD_DIM = 64
ATT_COLS = N_GROUPS * HEADS_PER_GROUP * HEAD_DIM
ATT_OUT = HEADS_PER_GROUP * HEAD_DIM
ROT_DIM = HEAD_DIM // 4
ROPE_THETA = 500000.0
D_FF = 3 * D_MODEL
FFN_CONV = 3
EPS = 1e-6
NEG = -1e30
IN_COLS = 2 * D_RNN + 3 * ATT_COLS + 2 * D_MODEL

kernel_name = 'hybrid_rglru_dilated_attn_encoder'


def _rmsnorm(x, g):
    xf = x.astype(jnp.float32)
    y = xf * lax.rsqrt(jnp.mean(xf * xf, axis=-1, keepdims=True) + EPS)
    return (y * g.astype(jnp.float32)).astype(x.dtype)


def _dwconv(x, w, b):
    K = w.shape[0]
    S = x.shape[1]
    lo = K // 2
    xp = jnp.pad(x, ((0, 0), (lo, K - 1 - lo), (0, 0)))
    out = b
    for j in range(K):
        out = out + xp[:, j:j + S, :] * w[j]
    return out


def _lin_combine(e1, e2):
    a1, b1 = e1
    a2, b2 = e2
    return a1 * a2, a2 * b1 + b2


def _rglru_direction(xc, wa, ba, wx, bx, lam):
    B, S, _ = xc.shape
    xb = xc.reshape(B, S, LRU_BLOCKS, LRU_BW)
    r = jax.nn.sigmoid((jnp.einsum('bsnc,ncd->bsnd', xb, wa).reshape(B, S, D_RNN) + ba).astype(jnp.float32))
    i = jax.nn.sigmoid((jnp.einsum('bsnc,ncd->bsnd', xb, wx).reshape(B, S, D_RNN) + bx).astype(jnp.float32))
    log_a = -LRU_C * r * jax.nn.softplus(-lam.astype(jnp.float32))
    a = jnp.exp(log_a)
    u = jnp.sqrt(-jnp.expm1(2.0 * log_a)) * (i * xc.astype(jnp.float32))
    _, h = lax.associative_scan(_lin_combine, (a, u), axis=1)
    return h


def _rope(x, cos, sin):
    half = ROT_DIM // 2
    xf = x.astype(jnp.float32)
    x1 = xf[..., :half]
    x2 = xf[..., half:ROT_DIM]
    rot = jnp.concatenate([x1 * cos - x2 * sin, x2 * cos + x1 * sin, xf[..., ROT_DIM:]], axis=-1)
    return rot.astype(x.dtype)


def _banded_attention(q, k, v, half):
    L = q.shape[-2]
    C = half
    n = -(-L // C)
    Lp = n * C
    lead = q.shape[:-2]
    nb = len(lead)
    qb = jnp.pad(q, [(0, 0)] * nb + [(0, Lp - L), (0, 0)]).reshape(lead + (n, C, HEAD_DIM))

    def windows(t):
        tb = jnp.pad(t, [(0, 0)] * nb + [(C, Lp - L + C), (0, 0)]).reshape(lead + (n + 2, C, HEAD_DIM))
        return jnp.concatenate([tb[..., :-2, :, :], tb[..., 1:-1, :, :], tb[..., 2:, :, :]], axis=-2)

    kw = windows(k)
    vw = windows(v)
    qpos = jnp.arange(n)[:, None] * C + jnp.arange(C)[None, :]
    kpos = jnp.arange(n)[:, None] * C - C + jnp.arange(3 * C)[None, :]
    kp = kpos[:, None, :]
    mask = (jnp.abs(qpos[:, :, None] - kp) <= half) & (kp >= 0) & (kp < L)
    s = jnp.einsum('...nqd,...nkd->...nqk', qb, kw, preferred_element_type=jnp.float32) * (HEAD_DIM ** -0.5)
    s = jnp.where(mask, s, NEG)
    lse = jax.nn.logsumexp(s, axis=-1)
    p = jnp.exp(s - lse[..., None])
    o = jnp.einsum('...nqk,...nkd->...nqd', p.astype(v.dtype), vw)
    o = o.reshape(lead + (Lp, HEAD_DIM))[..., :L, :]
    lse = lse.reshape(lead + (Lp,))[..., :L]
    return o, lse


def _dilated_group(q, k, v, window, dilation):
    B, S, H, Dh = q.shape
    L = S // dilation
    half = (window // 2) // dilation

    def split(t):
        return t.reshape(B, L, dilation, H, Dh).transpose(0, 2, 3, 1, 4)

    o, lse = _banded_attention(split(q), split(k), split(v), half)
    o = o.transpose(0, 3, 1, 2, 4).reshape(B, S, H, Dh)
    lse = lse.transpose(0, 3, 1, 2).reshape(B, S, H)
    return o, lse


def _layer(x, norm1_g, w_in, lru_conv_w, lru_conv_b, lru_wa, lru_ba, lru_wx, lru_bx, lru_lambda,
           w_lru_out, q_norm_g, k_norm_g, w_att_out, w_o, norm2_g, w_up, ffn_conv_w, ffn_conv_b, w_down):
    B, S, _ = x.shape
    xn = _rmsnorm(x, norm1_g)
    proj = xn @ w_in
    c0 = D_RNN
    c1 = 2 * D_RNN
    c2 = c1 + ATT_COLS
    c3 = c2 + ATT_COLS
    c4 = c3 + ATT_COLS
    lru_x, lru_gate, q, k, v, gates = jnp.split(proj, [c0, c1, c2, c3, c4], axis=-1)

    xc = _dwconv(lru_x, lru_conv_w, lru_conv_b)
    h_fwd = _rglru_direction(xc, lru_wa[0], lru_ba[0], lru_wx[0], lru_bx[0], lru_lambda[0])
    h_bwd = jnp.flip(_rglru_direction(jnp.flip(xc, axis=1), lru_wa[1], lru_ba[1], lru_wx[1], lru_bx[1], lru_lambda[1]), axis=1)
    a_out = (jax.nn.gelu(lru_gate) * (h_fwd + h_bwd).astype(x.dtype)) @ w_lru_out

    shp = (B, S, N_GROUPS, HEADS_PER_GROUP, HEAD_DIM)
    pos = jnp.arange(S, dtype=jnp.float32)
    inv = ROPE_THETA ** (-jnp.arange(0, ROT_DIM, 2, dtype=jnp.float32) / ROT_DIM)
    ang = pos[:, None] * inv[None, :]
    cos = jnp.cos(ang)[:, None, None, :]
    sin = jnp.sin(ang)[:, None, None, :]
    q = _rope(_rmsnorm(q.reshape(shp), q_norm_g[:, None, :]), cos, sin)
    k = _rope(_rmsnorm(k.reshape(shp), k_norm_g[:, None, :]), cos, sin)
    v = v.reshape(shp)
    outs = []
    lses = []
    for g, (window, dilation) in enumerate(ATT_GROUPS):
        o_g, lse_g = _dilated_group(q[:, :, g], k[:, :, g], v[:, :, g], window, dilation)
        outs.append(o_g)
        lses.append(lse_g)
    wts = jax.nn.softmax(jnp.stack(lses, axis=0), axis=0)
    o = jnp.sum(wts[..., None] * jnp.stack(outs, axis=0).astype(jnp.float32), axis=0).astype(x.dtype)
    b_out = o.reshape(B, S, ATT_OUT) @ w_att_out

    g = jax.nn.sigmoid(gates)
    g_a, g_b = jnp.split(g, 2, axis=-1)
    x = x + (g_a * a_out + g_b * b_out) @ w_o

    xn2 = _rmsnorm(x, norm2_g)
    up = xn2 @ w_up
    gate, val = jnp.split(up, 2, axis=-1)
    gate = _dwconv(gate, ffn_conv_w, ffn_conv_b)
    x = x + (jax.nn.gelu(gate) * val) @ w_down
    return x


def _trunk(x, params):
    for l in range(DEPTH):
        x = _layer(x, **{name: p[l] for name, p in params.items()})
    return x


def setup_inputs(seed: int = 0) -> dict:
    key = jax.random.key(seed)
    ks = jax.random.split(key, 24)
    f32 = jnp.float32
    nrm = lambda k, s, sc: jax.random.normal(k, s, f32) * sc
    u = jax.random.uniform(ks[9], (DEPTH, 2, D_RNN), f32, minval=0.9, maxval=0.999)
    a0 = u ** (1.0 / LRU_C)
    lam = jnp.log(a0) - jnp.log1p(-a0)
    return {
        'x_prompt': nrm(ks[0], (BATCH, SEQ, D_MODEL), 1.0),
        'x_sample': nrm(ks[1], (DEC_BATCH, DEC_SEQ, D_MODEL), 1.0),
        'norm1_g': 1.0 + nrm(ks[2], (DEPTH, D_MODEL), 0.02),
        'w_in': nrm(ks[3], (DEPTH, D_MODEL, IN_COLS), D_MODEL ** -0.5),
        'lru_conv_w': nrm(ks[4], (DEPTH, LRU_CONV, D_RNN), LRU_CONV ** -0.5),
        'lru_conv_b': nrm(ks[5], (DEPTH, D_RNN), 0.02),
        'lru_wa': nrm(ks[6], (DEPTH, 2, LRU_BLOCKS, LRU_BW, LRU_BW), LRU_BW ** -0.5),
        'lru_ba': nrm(ks[7], (DEPTH, 2, D_RNN), 0.02),
        'lru_wx': nrm(ks[8], (DEPTH, 2, LRU_BLOCKS, LRU_BW, LRU_BW), LRU_BW ** -0.5),
        'lru_bx': nrm(ks[10], (DEPTH, 2, D_RNN), 0.02),
        'lru_lambda': lam,
        'w_lru_out': nrm(ks[11], (DEPTH, D_RNN, D_MODEL), D_RNN ** -0.5),
        'q_norm_g': 1.0 + nrm(ks[12], (DEPTH, N_GROUPS, HEAD_DIM), 0.02),
        'k_norm_g': 1.0 + nrm(ks[13], (DEPTH, N_GROUPS, HEAD_DIM), 0.02),
        'w_att_out': nrm(ks[14], (DEPTH, ATT_OUT, D_MODEL), ATT_OUT ** -0.5),
        'w_o': nrm(ks[15], (DEPTH, D_MODEL, D_MODEL), D_MODEL ** -0.5),
        'norm2_g': 1.0 + nrm(ks[16], (DEPTH, D_MODEL), 0.02),
        'w_up': nrm(ks[17], (DEPTH, D_MODEL, 2 * D_FF), D_MODEL ** -0.5),
        'ffn_conv_w': nrm(ks[18], (DEPTH, FFN_CONV, D_FF), FFN_CONV ** -0.5),
        'ffn_conv_b': nrm(ks[19], (DEPTH, D_FF), 0.02),
        'w_down': nrm(ks[20], (DEPTH, D_FF, D_MODEL), D_FF ** -0.5),
    }


def reference(x_prompt, x_sample, norm1_g, w_in, lru_conv_w, lru_conv_b, lru_wa, lru_ba, lru_wx, lru_bx,
              lru_lambda, w_lru_out, q_norm_g, k_norm_g, w_att_out, w_o, norm2_g, w_up, ffn_conv_w,
              ffn_conv_b, w_down):
    params = dict(norm1_g=norm1_g, w_in=w_in, lru_conv_w=lru_conv_w, lru_conv_b=lru_conv_b,
                  lru_wa=lru_wa, lru_ba=lru_ba, lru_wx=lru_wx, lru_bx=lru_bx, lru_lambda=lru_lambda,
                  w_lru_out=w_lru_out, q_norm_g=q_norm_g, k_norm_g=k_norm_g, w_att_out=w_att_out,
                  w_o=w_o, norm2_g=norm2_g, w_up=w_up, ffn_conv_w=ffn_conv_w, ffn_conv_b=ffn_conv_b,
                  w_down=w_down)
    y_prompt = _trunk(x_prompt, params)
    y_sample = _trunk(x_sample, params)
    return (y_prompt, y_sample)
```

```python
import functools

import jax
import jax.numpy as jnp
from jax import lax
from jax.experimental import pallas as pl
from jax.experimental.pallas import tpu as pltpu

F32 = jnp.float32
BF16 = jnp.bfloat16

D_MODEL = 1024
D_RNN = 1280
LRU_BLOCKS = 16
LRU_BW = D_RNN // LRU_BLOCKS
LRU_C = 8.0
LRU_CONV = 4
ATT_GROUPS = ((128, 1), (512, 4), (2048, 16))
N_GROUPS = len(ATT_GROUPS)
HEADS_PER_GROUP = 8
HEAD_DIM = 64
GROUP_COLS = HEADS_PER_GROUP * HEAD_DIM
ATT_COLS = N_GROUPS * GROUP_COLS
ROT_DIM = HEAD_DIM // 4
ROPE_THETA = 500000.0
D_FF = 3 * D_MODEL
FFN_CONV = 3
EPS = 1e-6
NEG = -1e30
IN_COLS = 2 * D_RNN + 3 * ATT_COLS + 2 * D_MODEL

LANES = 128
MXU_DIM = 256
VMEM_LIMIT_BYTES = 56 * 1024 * 1024

CHUNK = 512
Q_CHUNK0 = 2 * D_RNN // CHUNK
K_CHUNK0 = Q_CHUNK0 + ATT_COLS // CHUNK
V_CHUNK0 = K_CHUNK0 + ATT_COLS // CHUNK
GATE_COL_BLOCK0 = (2 * D_RNN + 3 * ATT_COLS) // D_MODEL

TOKEN_TILE = 512
ATT_QBLOCK = 128
ATT_HALF = 64

LRU_CHUNK = 256
LRU_NCHUNK = D_RNN // LRU_CHUNK
LRU_KSLABS = 4
LRU_TBLOCK = 256
LRU_NSUB = 8
LRU_PITCH = 264


def _cparams(sem):
    return pltpu.CompilerParams(dimension_semantics=sem, vmem_limit_bytes=VMEM_LIMIT_BYTES)


def _resident(shape):
    return pl.BlockSpec(shape, lambda *_: (0,) * len(shape), pipeline_mode=pl.Buffered(1))


def _sigmoid(x):
    return 1.0 / (1.0 + jnp.exp(-x))


def _gelu_tanh(x):
    return 0.5 * x * (1.0 + jnp.tanh(0.7978845608028654 * (x + 0.044715 * (x * x * x))))


def _rms_rows(x, g):
    ms = jnp.mean(x * x, axis=-1, keepdims=True)
    return x * lax.rsqrt(ms + EPS) * g


def _inproj_kernel(x_ref, g1_ref, w_ref, bd_ref, qg_ref, kg_ref, ra_ref, rm_ref, rp_ref, o_ref):
    xn = _rms_rows(x_ref[...], g1_ref[...]).astype(BF16)
    ra = ra_ref[...]
    rm = rm_ref[...]
    rp = rp_ref[...]
    for j in range(IN_COLS // CHUNK):
        c0 = j * CHUNK
        acc = jnp.dot(xn, w_ref[:, c0:c0 + CHUNK], preferred_element_type=F32)
        if Q_CHUNK0 <= j < V_CHUNK0:
            is_q = j < K_CHUNK0
            g = j - (Q_CHUNK0 if is_q else K_CHUNK0)
            gain = (qg_ref if is_q else kg_ref)[g:g + 1, :]
            sq = (acc * acc).astype(BF16)
            bd = bd_ref[...]
            for t in range(CHUNK // MXU_DIM):
                sl = slice(t * MXU_DIM, (t + 1) * MXU_DIM)
                ss = jnp.dot(sq[:, sl], bd, preferred_element_type=F32)
                y = acc[:, sl] * lax.rsqrt(ss * (1.0 / HEAD_DIM) + EPS) * gain[:, sl]
                for u in range(MXU_DIM // LANES):
                    yt = y[:, u * LANES:(u + 1) * LANES]
                    rot = (yt * ra + pltpu.roll(yt, LANES - ROT_DIM // 2, 1) * rm
                           + pltpu.roll(yt, ROT_DIM // 2, 1) * rp)
                    o0 = c0 + t * MXU_DIM + u * LANES
                    o_ref[:, o0:o0 + LANES] = rot.astype(BF16)
        else:
            o_ref[:, c0:c0 + CHUNK] = acc.astype(BF16)


def _inproj(x2, seq, g1, w_in, bd, qg, kg, ra, rm, rp):
    n = x2.shape[0]
    tm = TOKEN_TILE
    tiles_per_seq = seq // tm
    rope_spec = pl.BlockSpec((tm, LANES), lambda i: (i % tiles_per_seq, 0))
    return pl.pallas_call(
        _inproj_kernel,
        grid=(n // tm,),
        in_specs=[
            pl.BlockSpec((tm, D_MODEL), lambda i: (i, 0)),
            _resident((1, D_MODEL)),
            _resident((D_MODEL, IN_COLS)),
            _resident((MXU_DIM, MXU_DIM)),
            _resident((N_GROUPS, GROUP_COLS)),
            _resident((N_GROUPS, GROUP_COLS)),
            rope_spec, rope_spec, rope_spec,
        ],
        out_specs=pl.BlockSpec((tm, IN_COLS), lambda i: (i, 0)),
        out_shape=jax.ShapeDtypeStruct((n, IN_COLS), BF16),
        compiler_params=_cparams(("parallel",)),
        name="inproj",
    )(x2, g1, w_in, bd, qg, kg, ra, rm, rp)


def _lru_kslab0(c):
    return jnp.minimum(jnp.maximum(2 * c - 1, 0), D_RNN // LANES - LRU_KSLABS)


def _lru_kernel(x_ref, gate_ref, cw_ref, cb_ref, wg_ref, gb_ref, lam_ref, wout_ref, o_ref,
                xc_scr, acc_scr, a_scr, u_scr, carry_scr, pad_scr):
    c = pl.program_id(1)
    seq = x_ref.shape[0]
    nsl = LRU_CHUNK // LANES

    @pl.when(c == 0)
    def _conv():
        pad_scr[0:8, :] = jnp.zeros((8, LANES), F32)
        pad_scr[seq + 8:seq + 16, :] = jnp.zeros((8, LANES), F32)
        for j in range(D_RNN // LANES):
            sl = slice(j * LANES, (j + 1) * LANES)
            pad_scr[8:seq + 8, :] = x_ref[:, sl].astype(F32)
            xc = cb_ref[:, sl]
            for k in range(LRU_CONV):
                off = 8 + k - LRU_CONV // 2
                xc = xc + pad_scr[off:off + seq, :] * cw_ref[k:k + 1, sl]
            xc_scr[j] = xc
        acc_scr[...] = jnp.zeros_like(acc_scr)

    k0 = _lru_kslab0(c)
    lam = lam_ref[...]
    spc = -LRU_C * (jnp.maximum(-lam, 0.0) + jnp.log1p(jnp.exp(-jnp.abs(lam))))

    def gates_body(s, carry):
        r0 = pl.multiple_of(s * LRU_TBLOCK, LRU_TBLOCK)
        w0 = pl.multiple_of(s * LRU_PITCH, 8)
        lhs = jnp.concatenate([xc_scr[k0 + i, pl.ds(r0, LRU_TBLOCK), :] for i in range(LRU_KSLABS)],
                              axis=1).astype(BF16)
        xc = jnp.concatenate([xc_scr[nsl * c + i, pl.ds(r0, LRU_TBLOCK), :] for i in range(nsl)], axis=1)
        for d in range(2):
            ga = jnp.dot(lhs, wg_ref[d, 0], preferred_element_type=F32) + gb_ref[2 * d:2 * d + 1, :]
            gx = jnp.dot(lhs, wg_ref[d, 1], preferred_element_type=F32) + gb_ref[2 * d + 1:2 * d + 2, :]
            a = jnp.exp(_sigmoid(ga) * spc[d:d + 1, :])
            u = jnp.sqrt(1.0 - a * a) * (_sigmoid(gx) * xc)
            for i in range(nsl):
                a_scr[d, i, pl.ds(w0, LRU_TBLOCK), :] = a[:, i * LANES:(i + 1) * LANES]
                u_scr[d, i, pl.ds(w0, LRU_TBLOCK), :] = u[:, i * LANES:(i + 1) * LANES]
        return carry

    lax.fori_loop(0, LRU_NSUB, gates_body, 0)

    def scan_body(t, carry):
        out = []
        for d in range(2):
            r = t if d == 0 else LRU_TBLOCK - 1 - t
            for i in range(nsl):
                h, cum = carry[2 * (d * nsl + i)], carry[2 * (d * nsl + i) + 1]
                idx = (d, i, pl.ds(r, LRU_NSUB, stride=LRU_PITCH), slice(None))
                av = a_scr[idx]
                h = av * h + u_scr[idx]
                cum = av * cum
                u_scr[idx] = h
                a_scr[idx] = cum
                out += [h, cum]
        return tuple(out)

    init = (jnp.zeros((LRU_NSUB, LANES), F32), jnp.ones((LRU_NSUB, LANES), F32)) * (2 * nsl)
    ends = lax.fori_loop(0, LRU_TBLOCK, scan_body, init, unroll=4)

    sub = lax.broadcasted_iota(jnp.int32, (LRU_NSUB, LANES), 0)
    for d in range(2):
        for i in range(nsl):
            h_end, cum_end = ends[2 * (d * nsl + i)], ends[2 * (d * nsl + i) + 1]
            cin = jnp.zeros((LRU_NSUB, LANES), F32)
            for step in range(LRU_NSUB - 1):
                nxt = h_end + cum_end * cin
                if d == 0:
                    cin = jnp.where(sub == step + 1, pltpu.roll(nxt, 1, 0), cin)
                else:
                    cin = jnp.where(sub == LRU_NSUB - 2 - step, pltpu.roll(nxt, LRU_NSUB - 1, 0), cin)
            carry_scr[d, i] = cin

    def out_body(s, carry):
        r0 = pl.multiple_of(s * LRU_TBLOCK, LRU_TBLOCK)
        w0 = pl.multiple_of(s * LRU_PITCH, 8)
        parts = []
        for i in range(nsl):
            tot = None
            for d in range(2):
                h = (u_scr[d, i, pl.ds(w0, LRU_TBLOCK), :]
                     + a_scr[d, i, pl.ds(w0, LRU_TBLOCK), :] * carry_scr[d, i, pl.ds(s, 1), :])
                tot = h if tot is None else tot + h
            parts.append(tot)
        hsum = jnp.concatenate(parts, axis=1)
        y = (_gelu_tanh(gate_ref[pl.ds(r0, LRU_TBLOCK), :].astype(F32)) * hsum).astype(BF16)
        acc_scr[pl.ds(r0, LRU_TBLOCK), :] += jnp.dot(y, wout_ref[...], preferred_element_type=F32)
        return carry

    lax.fori_loop(0, LRU_NSUB, out_body, 0)

    @pl.when(c == LRU_NCHUNK - 1)
    def _emit():
        o_ref[...] = acc_scr[...].astype(BF16)


def _lru(proj3, cw, cb, wg, gb, lam, wout):
    b, seq, _ = proj3.shape
    assert seq == LRU_NSUB * LRU_TBLOCK
    nsl = LRU_CHUNK // LANES
    gate_blk0 = D_RNN // LRU_CHUNK
    return pl.pallas_call(
        _lru_kernel,
        grid=(b, LRU_NCHUNK),
        in_specs=[
            pl.BlockSpec((None, seq, D_RNN), lambda i, c: (i, 0, 0), pipeline_mode=pl.Buffered(1)),
            pl.BlockSpec((None, seq, LRU_CHUNK), lambda i, c: (i, 0, gate_blk0 + c)),
            _resident((LRU_CONV, D_RNN)),
            _resident((1, D_RNN)),
            pl.BlockSpec((None, 2, 2, LRU_KSLABS * LANES, LRU_CHUNK), lambda i, c: (c, 0, 0, 0, 0)),
            pl.BlockSpec((None, 4, LRU_CHUNK), lambda i, c: (c, 0, 0)),
            pl.BlockSpec((None, 2, LRU_CHUNK), lambda i, c: (c, 0, 0)),
            pl.BlockSpec((LRU_CHUNK, D_MODEL), lambda i, c: (c, 0)),
        ],
        out_specs=pl.BlockSpec((None, seq, D_MODEL), lambda i, c: (i, 0, 0)),
        out_shape=jax.ShapeDtypeStruct((b, seq, D_MODEL), BF16),
        scratch_shapes=[
            pltpu.VMEM((D_RNN // LANES, seq, LANES), F32),
            pltpu.VMEM((seq, D_MODEL), F32),
            pltpu.VMEM((2, nsl, LRU_NSUB * LRU_PITCH, LANES), F32),
            pltpu.VMEM((2, nsl, LRU_NSUB * LRU_PITCH, LANES), F32),
            pltpu.VMEM((2, nsl, LRU_NSUB, LANES), F32),
            pltpu.VMEM((seq + 16, LANES), F32),
        ],
        compiler_params=_cparams(("parallel", "arbitrary")),
        name="rglru",
    )(proj3, proj3, cw, cb, wg, gb, lam, wout)


def _attn_kernel(q_ref, k_ref, v_ref, bias_ref, o_ref, lse_ref):
    length = q_ref.shape[0]
    nqb = length // ATT_QBLOCK
    win = bias_ref.shape[2]
    lane = lax.broadcasted_iota(jnp.int32, (ATT_QBLOCK, LANES), 1)
    lo_half = lane < HEAD_DIM

    def qblock(qb, carry):
        q0 = pl.multiple_of(qb * ATT_QBLOCK, ATT_QBLOCK)
        if nqb == 1:
            ws = 0
            bias = bias_ref[0]
        else:
            ws = pl.multiple_of(jnp.clip(q0 - ATT_HALF, 0, length - win), ATT_HALF)
            bias = bias_ref[jnp.where(qb == 0, 0, jnp.where(qb == nqb - 1, 2, 1))]
        lse_tile = jnp.zeros((ATT_QBLOCK, LANES), F32)
        for p in range(GROUP_COLS // LANES):
            sl = slice(p * LANES, (p + 1) * LANES)
            qp = q_ref[pl.ds(q0, ATT_QBLOCK), sl]
            kw = k_ref[pl.ds(ws, win), sl]
            vw = v_ref[pl.ds(ws, win), sl]
            outs = []
            for hh in range(LANES // HEAD_DIM):
                keep = lo_half if hh == 0 else jnp.logical_not(lo_half)
                qm = jnp.where(keep, qp, jnp.zeros_like(qp))
                s = lax.dot_general(qm, kw, (((1,), (1,)), ((), ())), preferred_element_type=F32) + bias
                m = jnp.max(s, axis=-1, keepdims=True)
                e = jnp.exp(s - m)
                l = jnp.sum(e, axis=-1, keepdims=True)
                outs.append(jnp.dot(e.astype(BF16), vw, preferred_element_type=F32) / l)
                lse_tile = jnp.where(lane == p * (LANES // HEAD_DIM) + hh, m + jnp.log(l), lse_tile)
            o_ref[pl.ds(q0, ATT_QBLOCK), sl] = jnp.where(lo_half, outs[0], outs[1]).astype(BF16)
        lse_ref[pl.ds(q0, ATT_QBLOCK), :] = lse_tile
        return carry

    lax.fori_loop(0, nqb, qblock, 0)


def _attn_bias(length):
    win = min(2 * ATT_QBLOCK, length)
    offs = (0,) if length == ATT_QBLOCK else (0, ATT_HALF, win - ATT_QBLOCK)
    i = jnp.arange(ATT_QBLOCK)[:, None]
    j = jnp.arange(win)[None, :]
    return jnp.stack([jnp.where(jnp.abs(i + off - j) <= ATT_HALF, 0.0, NEG) for off in offs]).astype(F32)


def _attention_group(proj3, g, window, dilation):
    b, seq, _ = proj3.shape
    length = seq // dilation
    assert (window // 2) // dilation == ATT_HALF and length % ATT_QBLOCK == 0
    pv = proj3.reshape(b, length, dilation * IN_COLS)
    cb = IN_COLS // GROUP_COLS
    bias = _attn_bias(length)

    def col_spec(chunk0):
        return pl.BlockSpec((None, length, GROUP_COLS), lambda i, m: (i, 0, m * cb + chunk0 + g))

    o, lse = pl.pallas_call(
        _attn_kernel,
        grid=(b, dilation),
        in_specs=[col_spec(Q_CHUNK0), col_spec(K_CHUNK0), col_spec(V_CHUNK0), _resident(bias.shape)],
        out_specs=[
            pl.BlockSpec((None, length, GROUP_COLS), lambda i, m: (i, 0, m)),
            pl.BlockSpec((None, length, LANES), lambda i, m: (i, 0, m)),
        ],
        out_shape=[
            jax.ShapeDtypeStruct((b, length, dilation * GROUP_COLS), BF16),
            jax.ShapeDtypeStruct((b, length, dilation * LANES), F32),
        ],
        compiler_params=_cparams(("parallel", "parallel")),
        name=f"attn_d{dilation}",
    )(pv, pv, pv, bias)
    return o.reshape(b * seq, GROUP_COLS), lse.reshape(b * seq, LANES)


def _merge_kernel(x_ref, a_ref, o0_ref, o1_ref, o2_ref, l0_ref, l1_ref, l2_ref, ga_ref, gb_ref,
                  watt_ref, wo_ref, ex_ref, y_ref):
    lses = [l0_ref[...], l1_ref[...], l2_ref[...]]
    mx = jnp.maximum(jnp.maximum(lses[0], lses[1]), lses[2])
    es = [jnp.exp(l - mx) for l in lses]
    den = es[0] + es[1] + es[2]
    ex = ex_ref[...]
    o = None
    for e, o_ref in zip(es, (o0_ref, o1_ref, o2_ref)):
        w = e / den
        hi = w.astype(BF16)
        lo = (w - hi.astype(F32)).astype(BF16)
        wx = jnp.dot(hi, ex, preferred_element_type=F32) + jnp.dot(lo, ex, preferred_element_type=F32)
        t = wx * o_ref[...].astype(F32)
        o = t if o is None else o + t
    b_out = jnp.dot(o.astype(BF16), watt_ref[...], preferred_element_type=F32)
    mix = (_sigmoid(ga_ref[...].astype(F32)) * a_ref[...].astype(F32)
           + _sigmoid(gb_ref[...].astype(F32)) * b_out)
    y_ref[...] = x_ref[...] + jnp.dot(mix.astype(BF16), wo_ref[...], preferred_element_type=F32)


def _merge(x2, a_out, outs, lses, proj2, watt, wo, ex):
    n = x2.shape[0]
    tm = TOKEN_TILE
    row = lambda cols: pl.BlockSpec((tm, cols), lambda i: (i, 0))
    return pl.pallas_call(
        _merge_kernel,
        grid=(n // tm,),
        in_specs=[
            row(D_MODEL), row(D_MODEL),
            row(GROUP_COLS), row(GROUP_COLS), row(GROUP_COLS),
            row(LANES), row(LANES), row(LANES),
            pl.BlockSpec((tm, D_MODEL), lambda i: (i, GATE_COL_BLOCK0)),
            pl.BlockSpec((tm, D_MODEL), lambda i: (i, GATE_COL_BLOCK0 + 1)),
            _resident((GROUP_COLS, D_MODEL)),
            _resident((D_MODEL, D_MODEL)),
            _resident((LANES, GROUP_COLS)),
        ],
        out_specs=row(D_MODEL),
        out_shape=jax.ShapeDtypeStruct((n, D_MODEL), F32),
        compiler_params=_cparams(("parallel",)),
        name="merge",
    )(x2, a_out, *outs, *lses, proj2, proj2, watt, wo, ex)


FFN_HALO = 16


def _ffn_kernel(x_ref, xp_ref, xn_ref, g2_ref, wup_ref, cw_ref, cb_ref, wdn_ref, y_ref, lhs_scr, gate_scr,
                *, tiles_per_seq):
    tm = x_ref.shape[0]
    ti = pl.program_id(0) % tiles_per_seq
    g2 = g2_ref[...]
    x = x_ref[...]
    keep_prev = jnp.where(ti == 0, 0.0, 1.0)
    keep_next = jnp.where(ti == tiles_per_seq - 1, 0.0, 1.0)
    xn = _rms_rows(x, g2).astype(BF16)
    lhs_scr[0:FFN_HALO, :] = (_rms_rows(xp_ref[...], g2) * keep_prev).astype(BF16)
    lhs_scr[FFN_HALO:FFN_HALO + tm, :] = xn
    lhs_scr[FFN_HALO + tm:, :] = (_rms_rows(xn_ref[...], g2) * keep_next).astype(BF16)
    acc = x
    for j in range(D_FF // CHUNK):
        sl = slice(j * CHUNK, (j + 1) * CHUNK)
        gate_scr[...] = jnp.dot(lhs_scr[...], wup_ref[:, sl], preferred_element_type=F32)
        gc = cb_ref[:, sl]
        for k in range(FFN_CONV):
            off = FFN_HALO + k - FFN_CONV // 2
            gc = gc + gate_scr[off:off + tm, :] * cw_ref[k:k + 1, sl]
        val = jnp.dot(xn, wup_ref[:, D_FF + j * CHUNK:D_FF + (j + 1) * CHUNK], preferred_element_type=F32)
        h = (_gelu_tanh(gc) * val).astype(BF16)
        acc = acc + jnp.dot(h, wdn_ref[sl, :], preferred_element_type=F32)
    y_ref[...] = acc


def _ffn(x2, seq, g2, wup, cw, cb, wdn):
    n = x2.shape[0]
    tm = TOKEN_TILE
    hb = tm // FFN_HALO
    last = n // FFN_HALO - 1
    return pl.pallas_call(
        functools.partial(_ffn_kernel, tiles_per_seq=seq // tm),
        grid=(n // tm,),
        in_specs=[
            pl.BlockSpec((tm, D_MODEL), lambda i: (i, 0)),
            pl.BlockSpec((FFN_HALO, D_MODEL), lambda i: (jnp.maximum(i * hb - 1, 0), 0)),
            pl.BlockSpec((FFN_HALO, D_MODEL), lambda i: (jnp.minimum((i + 1) * hb, last), 0)),
            _resident((1, D_MODEL)),
            _resident((D_MODEL, 2 * D_FF)),
            _resident((FFN_CONV, D_FF)),
            _resident((1, D_FF)),
            _resident((D_FF, D_MODEL)),
        ],
        out_specs=pl.BlockSpec((tm, D_MODEL), lambda i: (i, 0)),
        out_shape=jax.ShapeDtypeStruct((n, D_MODEL), F32),
        scratch_shapes=[
            pltpu.VMEM((tm + 2 * FFN_HALO, D_MODEL), BF16),
            pltpu.VMEM((tm + 2 * FFN_HALO, CHUNK), F32),
        ],
        compiler_params=_cparams(("parallel",)),
        name="convffn",
    )(x2, x2, x2, g2, wup, cw, cb, wdn)


def _rope_tables(seq):
    pos = jnp.arange(seq, dtype=F32)
    inv = ROPE_THETA ** (-jnp.arange(0, ROT_DIM, 2, dtype=F32) / ROT_DIM)
    ang = pos[:, None] * inv[None, :]
    cos, sin = jnp.cos(ang), jnp.sin(ang)
    half = ROT_DIM // 2
    pad = HEAD_DIM - ROT_DIM
    one = jnp.ones((seq, pad), F32)
    zero = jnp.zeros((seq, pad), F32)
    zh = jnp.zeros((seq, half), F32)
    per_head = lambda parts: jnp.tile(jnp.concatenate(parts, axis=1), (1, LANES // HEAD_DIM))
    ra = per_head([cos, cos, one])
    rm = per_head([-sin, zh, zero])
    rp = per_head([zh, sin, zero])
    return ra, rm, rp


def _block_diag_dense(w):
    nb, c, d = w.shape
    return jnp.einsum('ncd,nm->ncmd', w, jnp.eye(nb, dtype=w.dtype)).reshape(nb * c, nb * d)


def _prep_layer(p, seq):
    q = {}
    q['g1'] = p['norm1_g'].reshape(1, D_MODEL)
    q['w_in'] = p['w_in'].astype(BF16)
    head = jnp.arange(MXU_DIM) // HEAD_DIM
    q['bd'] = (head[:, None] == head[None, :]).astype(BF16)
    q['qg'] = jnp.tile(p['q_norm_g'], (1, HEADS_PER_GROUP)) * (HEAD_DIM ** -0.5)
    q['kg'] = jnp.tile(p['k_norm_g'], (1, HEADS_PER_GROUP))
    q['rope'] = _rope_tables(seq)
    q['lru_cw'] = p['lru_conv_w']
    q['lru_cb'] = p['lru_conv_b'].reshape(1, D_RNN)
    dense = jnp.stack([jnp.stack([_block_diag_dense(p['lru_wa'][d]), _block_diag_dense(p['lru_wx'][d])])
                       for d in range(2)]).astype(BF16)
    chunks = []
    for c in range(LRU_NCHUNK):
        k0 = min(max(2 * c - 1, 0), D_RNN // LANES - LRU_KSLABS) * LANES
        chunks.append(dense[:, :, k0:k0 + LRU_KSLABS * LANES, c * LRU_CHUNK:(c + 1) * LRU_CHUNK])
    q['lru_wg'] = jnp.stack(chunks)
    gb = jnp.stack([p['lru_ba'][0], p['lru_bx'][0], p['lru_ba'][1], p['lru_bx'][1]])
    q['lru_gb'] = gb.reshape(4, LRU_NCHUNK, LRU_CHUNK).transpose(1, 0, 2)
    q['lru_lam'] = p['lru_lambda'].reshape(2, LRU_NCHUNK, LRU_CHUNK).transpose(1, 0, 2)
    q['lru_wout'] = p['w_lru_out'].astype(BF16)
    q['watt'] = p['w_att_out'].astype(BF16)
    q['wo'] = p['w_o'].astype(BF16)
    q['expand'] = (jnp.arange(LANES)[:, None] == jnp.arange(GROUP_COLS)[None, :] // HEAD_DIM).astype(BF16)
    q['g2'] = p['norm2_g'].reshape(1, D_MODEL)
    q['wup'] = p['w_up'].astype(BF16)
    q['ffn_cw'] = p['ffn_conv_w']
    q['ffn_cb'] = p['ffn_conv_b'].reshape(1, D_FF)
    q['wdn'] = p['w_down'].astype(BF16)
    return q


def _layer(x, q):
    b, seq, _ = x.shape
    x2 = x.reshape(b * seq, D_MODEL)
    proj2 = _inproj(x2, seq, q['g1'], q['w_in'], q['bd'], q['qg'], q['kg'], *q['rope'])
    proj3 = proj2.reshape(b, seq, IN_COLS)
    a_out = _lru(proj3, q['lru_cw'], q['lru_cb'], q['lru_wg'], q['lru_gb'], q['lru_lam'], q['lru_wout'])
    outs, lses = [], []
    for g, (window, dilation) in enumerate(ATT_GROUPS):
        o, lse = _attention_group(proj3, g, window, dilation)
        outs.append(o)
        lses.append(lse)
    x1 = _merge(x2, a_out.reshape(b * seq, D_MODEL), outs, lses, proj2, q['watt'], q['wo'], q['expand'])
    y = _ffn(x1, seq, q['g2'], q['wup'], q['ffn_cw'], q['ffn_cb'], q['wdn'])
    return y.reshape(b, seq, D_MODEL)


def kernel(x_prompt, x_sample, norm1_g, w_in, lru_conv_w, lru_conv_b, lru_wa, lru_ba, lru_wx, lru_bx,
           lru_lambda, w_lru_out, q_norm_g, k_norm_g, w_att_out, w_o, norm2_g, w_up, ffn_conv_w,
           ffn_conv_b, w_down):
    params = dict(norm1_g=norm1_g, w_in=w_in, lru_conv_w=lru_conv_w, lru_conv_b=lru_conv_b,
                  lru_wa=lru_wa, lru_ba=lru_ba, lru_wx=lru_wx, lru_bx=lru_bx, lru_lambda=lru_lambda,
                  w_lru_out=w_lru_out, q_norm_g=q_norm_g, k_norm_g=k_norm_g, w_att_out=w_att_out,
                  w_o=w_o, norm2_g=norm2_g, w_up=w_up, ffn_conv_w=ffn_conv_w, ffn_conv_b=ffn_conv_b,
                  w_down=w_down)
    depth = norm1_g.shape[0]
    seq = x_prompt.shape[1]
    assert x_sample.shape[1] == seq
    layers = [_prep_layer({name: p[l] for name, p in params.items()}, seq) for l in range(depth)]
    ys = []
    for x in (x_prompt, x_sample):
        for q in layers:
            x = _layer(x, q)
        ys.append(x)
    return tuple(ys)
```

```python
import functools

import jax
import jax.numpy as jnp
from jax import lax
from jax.experimental import pallas as pl
from jax.experimental.pallas import tpu as pltpu

F32 = jnp.float32
BF16 = jnp.bfloat16

D_MODEL = 1024
D_RNN = 1280
LRU_BLOCKS = 16
LRU_BW = D_RNN // LRU_BLOCKS
LRU_C = 8.0
LRU_CONV = 4
ATT_GROUPS = ((128, 1), (512, 4), (2048, 16))
N_GROUPS = len(ATT_GROUPS)
HEADS_PER_GROUP = 8
HEAD_DIM = 64
GROUP_COLS = HEADS_PER_GROUP * HEAD_DIM
ATT_COLS = N_GROUPS * GROUP_COLS
ROT_DIM = HEAD_DIM // 4
ROPE_THETA = 500000.0
D_FF = 3 * D_MODEL
FFN_CONV = 3
EPS = 1e-6
NEG = -1e30
IN_COLS = 2 * D_RNN + 3 * ATT_COLS + 2 * D_MODEL

LANES = 128
MXU_DIM = 256
VMEM_LIMIT_BYTES = 56 * 1024 * 1024

CHUNK = 512
MAIN_COLS = 2 * D_RNN + 3 * GROUP_COLS + 2 * D_MODEL
MAIN_CHUNKS = MAIN_COLS // CHUNK
QKV0_CHUNK0 = 2 * D_RNN // CHUNK
GATE_COL_BLOCK0 = (2 * D_RNN + 3 * GROUP_COLS) // D_MODEL

TOKEN_TILE = 512
ATT_QBLOCK = 128
ATT_HALF = 64

LRU_CHUNK = 256
LRU_NCHUNK = D_RNN // LRU_CHUNK
LRU_KSLABS = 4
LRU_TBLOCK = 256
LRU_NSUB = 8
LRU_PITCH = 264


def _cparams(sem):
    return pltpu.CompilerParams(dimension_semantics=sem, vmem_limit_bytes=VMEM_LIMIT_BYTES)


def _resident(shape):
    return pl.BlockSpec(shape, lambda *_: (0,) * len(shape), pipeline_mode=pl.Buffered(1))


def _sigmoid(x):
    return 1.0 / (1.0 + jnp.exp(-x))


def _gelu_tanh(x):
    return 0.5 * x * (1.0 + jnp.tanh(0.7978845608028654 * (x + 0.044715 * (x * x * x))))


def _rms_rows(x, g):
    ms = jnp.mean(x * x, axis=-1, keepdims=True)
    return x * lax.rsqrt(ms + EPS) * g


def _chunk_role(j):
    if QKV0_CHUNK0 <= j < QKV0_CHUNK0 + 3:
        return 0, j - QKV0_CHUNK0
    if j >= MAIN_CHUNKS:
        return 1 + (j - MAIN_CHUNKS) // 3, (j - MAIN_CHUNKS) % 3
    return None


def _inproj_kernel(x_ref, g1_ref, w_ref, bd_ref, qg_ref, kg_ref, ra_ref, rm_ref, rp_ref,
                   o_ref, d1_ref, d2_ref, tile_scr):
    tm = x_ref.shape[0]
    xn = _rms_rows(x_ref[...], g1_ref[...]).astype(BF16)
    ra = ra_ref[...]
    rm = rm_ref[...]
    rp = rp_ref[...]
    for j in range(IN_COLS // CHUNK):
        c0 = j * CHUNK
        acc = jnp.dot(xn, w_ref[:, c0:c0 + CHUNK], preferred_element_type=F32)
        role = _chunk_role(j)
        if role is None:
            o_ref[:, c0:c0 + CHUNK] = acc.astype(BF16)
            continue
        g, part = role
        tiles = []
        if part < 2:
            gain = (qg_ref if part == 0 else kg_ref)[g:g + 1, :]
            sq = (acc * acc).astype(BF16)
            bd = bd_ref[...]
            for t in range(CHUNK // MXU_DIM):
                sl = slice(t * MXU_DIM, (t + 1) * MXU_DIM)
                ss = jnp.dot(sq[:, sl], bd, preferred_element_type=F32)
                y = acc[:, sl] * lax.rsqrt(ss * (1.0 / HEAD_DIM) + EPS) * gain[:, sl]
                for u in range(MXU_DIM // LANES):
                    yt = y[:, u * LANES:(u + 1) * LANES]
                    tiles.append(yt * ra + pltpu.roll(yt, LANES - ROT_DIM // 2, 1) * rm
                                 + pltpu.roll(yt, ROT_DIM // 2, 1) * rp)
        else:
            tiles = [acc[:, t * LANES:(t + 1) * LANES] for t in range(CHUNK // LANES)]
        if g == 0:
            for t, val in enumerate(tiles):
                o_ref[:, c0 + t * LANES:c0 + (t + 1) * LANES] = val.astype(BF16)
        else:
            dil = ATT_GROUPS[g][1]
            dst = d1_ref if g == 1 else d2_ref
            for t, val in enumerate(tiles):
                tile_scr[t] = val
            for m in range(dil):
                for t in range(CHUNK // LANES):
                    col = (m * 3 + part) * GROUP_COLS + t * LANES
                    dst[:, col:col + LANES] = tile_scr[t, pl.ds(m, tm // dil, stride=dil), :].astype(BF16)


def _inproj(x2, seq, g1, w_in, bd, qg, kg, ra, rm, rp):
    n = x2.shape[0]
    tm = TOKEN_TILE
    tiles_per_seq = seq // tm
    rope_spec = pl.BlockSpec((tm, LANES), lambda i: (i % tiles_per_seq, 0))
    d1, d2 = ATT_GROUPS[1][1], ATT_GROUPS[2][1]
    return pl.pallas_call(
        _inproj_kernel,
        grid=(n // tm,),
        in_specs=[
            pl.BlockSpec((tm, D_MODEL), lambda i: (i, 0)),
            _resident((1, D_MODEL)),
            _resident((D_MODEL, IN_COLS)),
            _resident((MXU_DIM, MXU_DIM)),
            _resident((N_GROUPS, GROUP_COLS)),
            _resident((N_GROUPS, GROUP_COLS)),
            rope_spec, rope_spec, rope_spec,
        ],
        out_specs=[
            pl.BlockSpec((tm, MAIN_COLS), lambda i: (i, 0)),
            pl.BlockSpec((tm // d1, d1 * 3 * GROUP_COLS), lambda i: (i, 0)),
            pl.BlockSpec((tm // d2, d2 * 3 * GROUP_COLS), lambda i: (i, 0)),
        ],
        out_shape=[
            jax.ShapeDtypeStruct((n, MAIN_COLS), BF16),
            jax.ShapeDtypeStruct((n // d1, d1 * 3 * GROUP_COLS), BF16),
            jax.ShapeDtypeStruct((n // d2, d2 * 3 * GROUP_COLS), BF16),
        ],
        scratch_shapes=[pltpu.VMEM((CHUNK // LANES, tm, LANES), F32)],
        compiler_params=_cparams(("parallel",)),
        name="inproj",
    )(x2, g1, w_in, bd, qg, kg, ra, rm, rp)


def _lru_kslab0(c):
    return jnp.minimum(jnp.maximum(2 * c - 1, 0), D_RNN // LANES - LRU_KSLABS)


def _lru_kernel(x_ref, gate_ref, cw_ref, cb_ref, wg_ref, gb_ref, lam_ref, wout_ref, o_ref,
                xc_scr, acc_scr, a_scr, u_scr, carry_scr, pad_scr):
    c = pl.program_id(1)
    seq = x_ref.shape[0]
    nsl = LRU_CHUNK // LANES

    @pl.when(c == 0)
    def _conv():
        pad_scr[0:8, :] = jnp.zeros((8, LANES), F32)
        pad_scr[seq + 8:seq + 16, :] = jnp.zeros((8, LANES), F32)
        for j in range(D_RNN // LANES):
            sl = slice(j * LANES, (j + 1) * LANES)
            pad_scr[8:seq + 8, :] = x_ref[:, sl].astype(F32)
            xc = cb_ref[:, sl]
            for k in range(LRU_CONV):
                off = 8 + k - LRU_CONV // 2
                xc = xc + pad_scr[off:off + seq, :] * cw_ref[k:k + 1, sl]
            xc_scr[j] = xc
        acc_scr[...] = jnp.zeros_like(acc_scr)

    k0 = _lru_kslab0(c)
    lam = lam_ref[...]
    spc = -LRU_C * (jnp.maximum(-lam, 0.0) + jnp.log1p(jnp.exp(-jnp.abs(lam))))

    def gates_body(s, carry):
        r0 = pl.multiple_of(s * LRU_TBLOCK, LRU_TBLOCK)
        w0 = pl.multiple_of(s * LRU_PITCH, 8)
        lhs = jnp.concatenate([xc_scr[k0 + i, pl.ds(r0, LRU_TBLOCK), :] for i in range(LRU_KSLABS)],
                              axis=1).astype(BF16)
        xc = jnp.concatenate([xc_scr[nsl * c + i, pl.ds(r0, LRU_TBLOCK), :] for i in range(nsl)], axis=1)
        for d in range(2):
            ga = jnp.dot(lhs, wg_ref[d, 0], preferred_element_type=F32) + gb_ref[2 * d:2 * d + 1, :]
            gx = jnp.dot(lhs, wg_ref[d, 1], preferred_element_type=F32) + gb_ref[2 * d + 1:2 * d + 2, :]
            a = jnp.exp(_sigmoid(ga) * spc[d:d + 1, :])
            u = jnp.sqrt(1.0 - a * a) * (_sigmoid(gx) * xc)
            for i in range(nsl):
                a_scr[d, i, pl.ds(w0, LRU_TBLOCK), :] = a[:, i * LANES:(i + 1) * LANES]
                u_scr[d, i, pl.ds(w0, LRU_TBLOCK), :] = u[:, i * LANES:(i + 1) * LANES]
        return carry

    lax.fori_loop(0, LRU_NSUB, gates_body, 0)

    def scan_body(t, carry):
        out = []
        for d in range(2):
            r = t if d == 0 else LRU_TBLOCK - 1 - t
            for i in range(nsl):
                h, cum = carry[2 * (d * nsl + i)], carry[2 * (d * nsl + i) + 1]
                idx = (d, i, pl.ds(r, LRU_NSUB, stride=LRU_PITCH), slice(None))
                av = a_scr[idx]
                h = av * h + u_scr[idx]
                cum = av * cum
                u_scr[idx] = h
                a_scr[idx] = cum
                out += [h, cum]
        return tuple(out)

    init = (jnp.zeros((LRU_NSUB, LANES), F32), jnp.ones((LRU_NSUB, LANES), F32)) * (2 * nsl)
    ends = lax.fori_loop(0, LRU_TBLOCK, scan_body, init, unroll=4)

    sub = lax.broadcasted_iota(jnp.int32, (LRU_NSUB, LANES), 0)
    for d in range(2):
        for i in range(nsl):
            h_end, cum_end = ends[2 * (d * nsl + i)], ends[2 * (d * nsl + i) + 1]
            cin = jnp.zeros((LRU_NSUB, LANES), F32)
            for step in range(LRU_NSUB - 1):
                nxt = h_end + cum_end * cin
                if d == 0:
                    cin = jnp.where(sub == step + 1, pltpu.roll(nxt, 1, 0), cin)
                else:
                    cin = jnp.where(sub == LRU_NSUB - 2 - step, pltpu.roll(nxt, LRU_NSUB - 1, 0), cin)
            carry_scr[d, i] = cin

    def out_body(s, carry):
        r0 = pl.multiple_of(s * LRU_TBLOCK, LRU_TBLOCK)
        w0 = pl.multiple_of(s * LRU_PITCH, 8)
        parts = []
        for i in range(nsl):
            tot = None
            for d in range(2):
                h = (u_scr[d, i, pl.ds(w0, LRU_TBLOCK), :]
                     + a_scr[d, i, pl.ds(w0, LRU_TBLOCK), :] * carry_scr[d, i, pl.ds(s, 1), :])
                tot = h if tot is None else tot + h
            parts.append(tot)
        hsum = jnp.concatenate(parts, axis=1)
        y = (_gelu_tanh(gate_ref[pl.ds(r0, LRU_TBLOCK), :].astype(F32)) * hsum).astype(BF16)
        acc_scr[pl.ds(r0, LRU_TBLOCK), :] += jnp.dot(y, wout_ref[...], preferred_element_type=F32)
        return carry

    lax.fori_loop(0, LRU_NSUB, out_body, 0)

    @pl.when(c == LRU_NCHUNK - 1)
    def _emit():
        o_ref[...] = acc_scr[...].astype(BF16)


def _lru(proj3, cw, cb, wg, gb, lam, wout):
    b, seq, _ = proj3.shape
    assert seq == LRU_NSUB * LRU_TBLOCK
    nsl = LRU_CHUNK // LANES
    gate_blk0 = D_RNN // LRU_CHUNK
    return pl.pallas_call(
        _lru_kernel,
        grid=(b, LRU_NCHUNK),
        in_specs=[
            pl.BlockSpec((None, seq, D_RNN), lambda i, c: (i, 0, 0), pipeline_mode=pl.Buffered(1)),
            pl.BlockSpec((None, seq, LRU_CHUNK), lambda i, c: (i, 0, gate_blk0 + c)),
            _resident((LRU_CONV, D_RNN)),
            _resident((1, D_RNN)),
            pl.BlockSpec((None, 2, 2, LRU_KSLABS * LANES, LRU_CHUNK), lambda i, c: (c, 0, 0, 0, 0)),
            pl.BlockSpec((None, 4, LRU_CHUNK), lambda i, c: (c, 0, 0)),
            pl.BlockSpec((None, 2, LRU_CHUNK), lambda i, c: (c, 0, 0)),
            pl.BlockSpec((LRU_CHUNK, D_MODEL), lambda i, c: (c, 0)),
        ],
        out_specs=pl.BlockSpec((None, seq, D_MODEL), lambda i, c: (i, 0, 0)),
        out_shape=jax.ShapeDtypeStruct((b, seq, D_MODEL), BF16),
        scratch_shapes=[
            pltpu.VMEM((D_RNN // LANES, seq, LANES), F32),
            pltpu.VMEM((seq, D_MODEL), F32),
            pltpu.VMEM((2, nsl, LRU_NSUB * LRU_PITCH, LANES), F32),
            pltpu.VMEM((2, nsl, LRU_NSUB * LRU_PITCH, LANES), F32),
            pltpu.VMEM((2, nsl, LRU_NSUB, LANES), F32),
            pltpu.VMEM((seq + 16, LANES), F32),
        ],
        compiler_params=_cparams(("parallel", "arbitrary")),
        name="rglru",
    )(proj3, proj3, cw, cb, wg, gb, lam, wout)


def _attn_kernel(q_ref, k_ref, v_ref, bias_ref, o_ref, lse_ref):
    length = q_ref.shape[0]
    nqb = length // ATT_QBLOCK
    win = bias_ref.shape[2]
    lane = lax.broadcasted_iota(jnp.int32, (ATT_QBLOCK, LANES), 1)
    lo_half = lane < HEAD_DIM

    def qblock(qb, carry):
        q0 = pl.multiple_of(qb * ATT_QBLOCK, ATT_QBLOCK)
        if nqb == 1:
            ws = 0
            bias = bias_ref[0]
        else:
            ws = pl.multiple_of(jnp.clip(q0 - ATT_HALF, 0, length - win), ATT_HALF)
            bias = bias_ref[jnp.where(qb == 0, 0, jnp.where(qb == nqb - 1, 2, 1))]
        lse_tile = jnp.zeros((ATT_QBLOCK, LANES), F32)
        for p in range(GROUP_COLS // LANES):
            sl = slice(p * LANES, (p + 1) * LANES)
            qp = q_ref[pl.ds(q0, ATT_QBLOCK), sl]
            kw = k_ref[pl.ds(ws, win), sl]
            vw = v_ref[pl.ds(ws, win), sl]
            outs = []
            for hh in range(LANES // HEAD_DIM):
                keep = lo_half if hh == 0 else jnp.logical_not(lo_half)
                qm = jnp.where(keep, qp, jnp.zeros_like(qp))
                s = lax.dot_general(qm, kw, (((1,), (1,)), ((), ())), preferred_element_type=F32) + bias
                m = jnp.max(s, axis=-1, keepdims=True)
                e = jnp.exp(s - m)
                l = jnp.sum(e, axis=-1, keepdims=True)
                outs.append(jnp.dot(e.astype(BF16), vw, preferred_element_type=F32) / l)
                lse_tile = jnp.where(lane == p * (LANES // HEAD_DIM) + hh, m + jnp.log(l), lse_tile)
            o_ref[pl.ds(q0, ATT_QBLOCK), sl] = jnp.where(lo_half, outs[0], outs[1]).astype(BF16)
        lse_ref[pl.ds(q0, ATT_QBLOCK), :] = lse_tile
        return carry

    lax.fori_loop(0, nqb, qblock, 0)


def _attn_bias(length):
    win = min(2 * ATT_QBLOCK, length)
    offs = (0,) if length == ATT_QBLOCK else (0, ATT_HALF, win - ATT_QBLOCK)
    i = jnp.arange(ATT_QBLOCK)[:, None]
    j = jnp.arange(win)[None, :]
    return jnp.stack([jnp.where(jnp.abs(i + off - j) <= ATT_HALF, 0.0, NEG) for off in offs]).astype(F32)


def _attention_group(qkv, b, seq, window, dilation):
    length = seq // dilation
    assert (window // 2) // dilation == ATT_HALF and length % ATT_QBLOCK == 0
    pv = qkv.reshape(b, length, qkv.shape[1])
    chunk0 = QKV0_CHUNK0 if dilation == 1 else 0
    bias = _attn_bias(length)

    def col_spec(part):
        return pl.BlockSpec((None, length, GROUP_COLS), lambda i, m: (i, 0, m * 3 + chunk0 + part))

    o, lse = pl.pallas_call(
        _attn_kernel,
        grid=(b, dilation),
        in_specs=[col_spec(0), col_spec(1), col_spec(2), _resident(bias.shape)],
        out_specs=[
            pl.BlockSpec((None, length, GROUP_COLS), lambda i, m: (i, 0, m)),
            pl.BlockSpec((None, length, LANES), lambda i, m: (i, 0, m)),
        ],
        out_shape=[
            jax.ShapeDtypeStruct((b, length, dilation * GROUP_COLS), BF16),
            jax.ShapeDtypeStruct((b, length, dilation * LANES), F32),
        ],
        compiler_params=_cparams(("parallel", "parallel")),
        name=f"attn_d{dilation}",
    )(pv, pv, pv, bias)
    return o.reshape(b * length, dilation * GROUP_COLS), lse.reshape(b * length, dilation * LANES)


def _merge_kernel(x_ref, a_ref, o0_ref, o1_ref, o2_ref, l0_ref, l1_ref, l2_ref, ga_ref, gb_ref,
                  watt_ref, wo_ref, ex_ref, y_ref, o_scr, l_scr):
    tm = x_ref.shape[0]
    nt = GROUP_COLS // LANES
    for gi, (og_ref, lg_ref) in enumerate(((o1_ref, l1_ref), (o2_ref, l2_ref))):
        dil = ATT_GROUPS[gi + 1][1]
        for m in range(dil):
            rows = pl.ds(m, tm // dil, stride=dil)
            l_scr[gi, rows, :] = lg_ref[:, m * LANES:(m + 1) * LANES]
            for t in range(nt):
                col = m * GROUP_COLS + t * LANES
                o_scr[gi, t, rows, :] = og_ref[:, col:col + LANES].astype(F32)
    lses = [l0_ref[...], l_scr[0], l_scr[1]]
    group_out = [o0_ref[...].astype(F32)] + [
        jnp.concatenate([o_scr[gi, t] for t in range(nt)], axis=1) for gi in range(2)]
    mx = jnp.maximum(jnp.maximum(lses[0], lses[1]), lses[2])
    es = [jnp.exp(l - mx) for l in lses]
    den = es[0] + es[1] + es[2]
    ex = ex_ref[...]
    o = None
    for e, og in zip(es, group_out):
        w = e / den
        hi = w.astype(BF16)
        lo = (w - hi.astype(F32)).astype(BF16)
        wx = jnp.dot(hi, ex, preferred_element_type=F32) + jnp.dot(lo, ex, preferred_element_type=F32)
        t = wx * og
        o = t if o is None else o + t
    b_out = jnp.dot(o.astype(BF16), watt_ref[...], preferred_element_type=F32)
    mix = (_sigmoid(ga_ref[...].astype(F32)) * a_ref[...].astype(F32)
           + _sigmoid(gb_ref[...].astype(F32)) * b_out)
    y_ref[...] = x_ref[...] + jnp.dot(mix.astype(BF16), wo_ref[...], preferred_element_type=F32)


def _merge(x2, a_out, outs, lses, proj2, watt, wo, ex):
    n = x2.shape[0]
    tm = TOKEN_TILE
    row = lambda cols, dil=1: pl.BlockSpec((tm // dil, cols * dil), lambda i: (i, 0))
    d1, d2 = ATT_GROUPS[1][1], ATT_GROUPS[2][1]
    return pl.pallas_call(
        _merge_kernel,
        grid=(n // tm,),
        in_specs=[
            row(D_MODEL), row(D_MODEL),
            row(GROUP_COLS), row(GROUP_COLS, d1), row(GROUP_COLS, d2),
            row(LANES), row(LANES, d1), row(LANES, d2),
            pl.BlockSpec((tm, D_MODEL), lambda i: (i, GATE_COL_BLOCK0)),
            pl.BlockSpec((tm, D_MODEL), lambda i: (i, GATE_COL_BLOCK0 + 1)),
            _resident((GROUP_COLS, D_MODEL)),
            _resident((D_MODEL, D_MODEL)),
            _resident((LANES, GROUP_COLS)),
        ],
        out_specs=row(D_MODEL),
        out_shape=jax.ShapeDtypeStruct((n, D_MODEL), F32),
        scratch_shapes=[
            pltpu.VMEM((2, GROUP_COLS // LANES, tm, LANES), F32),
            pltpu.VMEM((2, tm, LANES), F32),
        ],
        compiler_params=_cparams(("parallel",)),
        name="merge",
    )(x2, a_out, *outs, *lses, proj2, proj2, watt, wo, ex)


FFN_HALO = 16


def _ffn_kernel(x_ref, xp_ref, xn_ref, g2_ref, wup_ref, cw_ref, cb_ref, wdn_ref, y_ref, lhs_scr, gate_scr,
                *, tiles_per_seq):
    tm = x_ref.shape[0]
    ti = pl.program_id(0) % tiles_per_seq
    g2 = g2_ref[...]
    x = x_ref[...]
    keep_prev = jnp.where(ti == 0, 0.0, 1.0)
    keep_next = jnp.where(ti == tiles_per_seq - 1, 0.0, 1.0)
    xn = _rms_rows(x, g2).astype(BF16)
    lhs_scr[0:FFN_HALO, :] = (_rms_rows(xp_ref[...], g2) * keep_prev).astype(BF16)
    lhs_scr[FFN_HALO:FFN_HALO + tm, :] = xn
    lhs_scr[FFN_HALO + tm:, :] = (_rms_rows(xn_ref[...], g2) * keep_next).astype(BF16)
    acc = x
    for j in range(D_FF // CHUNK):
        sl = slice(j * CHUNK, (j + 1) * CHUNK)
        gate_scr[...] = jnp.dot(lhs_scr[...], wup_ref[:, sl], preferred_element_type=F32)
        gc = cb_ref[:, sl]
        for k in range(FFN_CONV):
            off = FFN_HALO + k - FFN_CONV // 2
            gc = gc + gate_scr[off:off + tm, :] * cw_ref[k:k + 1, sl]
        val = jnp.dot(xn, wup_ref[:, D_FF + j * CHUNK:D_FF + (j + 1) * CHUNK], preferred_element_type=F32)
        h = (_gelu_tanh(gc) * val).astype(BF16)
        acc = acc + jnp.dot(h, wdn_ref[sl, :], preferred_element_type=F32)
    y_ref[...] = acc


def _ffn(x2, seq, g2, wup, cw, cb, wdn):
    n = x2.shape[0]
    tm = TOKEN_TILE
    hb = tm // FFN_HALO
    last = n // FFN_HALO - 1
    return pl.pallas_call(
        functools.partial(_ffn_kernel, tiles_per_seq=seq // tm),
        grid=(n // tm,),
        in_specs=[
            pl.BlockSpec((tm, D_MODEL), lambda i: (i, 0)),
            pl.BlockSpec((FFN_HALO, D_MODEL), lambda i: (jnp.maximum(i * hb - 1, 0), 0)),
            pl.BlockSpec((FFN_HALO, D_MODEL), lambda i: (jnp.minimum((i + 1) * hb, last), 0)),
            _resident((1, D_MODEL)),
            _resident((D_MODEL, 2 * D_FF)),
            _resident((FFN_CONV, D_FF)),
            _resident((1, D_FF)),
            _resident((D_FF, D_MODEL)),
        ],
        out_specs=pl.BlockSpec((tm, D_MODEL), lambda i: (i, 0)),
        out_shape=jax.ShapeDtypeStruct((n, D_MODEL), F32),
        scratch_shapes=[
            pltpu.VMEM((tm + 2 * FFN_HALO, D_MODEL), BF16),
            pltpu.VMEM((tm + 2 * FFN_HALO, CHUNK), F32),
        ],
        compiler_params=_cparams(("parallel",)),
        name="convffn",
    )(x2, x2, x2, g2, wup, cw, cb, wdn)


def _rope_tables(seq):
    pos = jnp.arange(seq, dtype=F32)
    inv = ROPE_THETA ** (-jnp.arange(0, ROT_DIM, 2, dtype=F32) / ROT_DIM)
    ang = pos[:, None] * inv[None, :]
    cos, sin = jnp.cos(ang), jnp.sin(ang)
    half = ROT_DIM // 2
    pad = HEAD_DIM - ROT_DIM
    one = jnp.ones((seq, pad), F32)
    zero = jnp.zeros((seq, pad), F32)
    zh = jnp.zeros((seq, half), F32)
    per_head = lambda parts: jnp.tile(jnp.concatenate(parts, axis=1), (1, LANES // HEAD_DIM))
    ra = per_head([cos, cos, one])
    rm = per_head([-sin, zh, zero])
    rp = per_head([zh, sin, zero])
    return ra, rm, rp


def _block_diag_dense(w):
    nb, c, d = w.shape
    return jnp.einsum('ncd,nm->ncmd', w, jnp.eye(nb, dtype=w.dtype)).reshape(nb * c, nb * d)


def _prep_layer(p, seq):
    q = {}
    q['g1'] = p['norm1_g'].reshape(1, D_MODEL)
    q_lo = 2 * D_RNN
    seg = lambda part, g: (q_lo + part * ATT_COLS + g * GROUP_COLS, q_lo + part * ATT_COLS + (g + 1) * GROUP_COLS)
    qkv = lambda g: [seg(part, g) for part in range(3)]
    order = [(0, q_lo)] + qkv(0) + [(q_lo + 3 * ATT_COLS, IN_COLS)] + qkv(1) + qkv(2)
    w_in = p['w_in'].astype(BF16)
    q['w_in'] = jnp.concatenate([w_in[:, lo:hi] for lo, hi in order], axis=1)
    head = jnp.arange(MXU_DIM) // HEAD_DIM
    q['bd'] = (head[:, None] == head[None, :]).astype(BF16)
    q['qg'] = jnp.tile(p['q_norm_g'], (1, HEADS_PER_GROUP)) * (HEAD_DIM ** -0.5)
    q['kg'] = jnp.tile(p['k_norm_g'], (1, HEADS_PER_GROUP))
    q['rope'] = _rope_tables(seq)
    q['lru_cw'] = p['lru_conv_w']
    q['lru_cb'] = p['lru_conv_b'].reshape(1, D_RNN)
    dense = jnp.stack([jnp.stack([_block_diag_dense(p['lru_wa'][d]), _block_diag_dense(p['lru_wx'][d])])
                       for d in range(2)]).astype(BF16)
    chunks = []
    for c in range(LRU_NCHUNK):
        k0 = min(max(2 * c - 1, 0), D_RNN // LANES - LRU_KSLABS) * LANES
        chunks.append(dense[:, :, k0:k0 + LRU_KSLABS * LANES, c * LRU_CHUNK:(c + 1) * LRU_CHUNK])
    q['lru_wg'] = jnp.stack(chunks)
    gb = jnp.stack([p['lru_ba'][0], p['lru_bx'][0], p['lru_ba'][1], p['lru_bx'][1]])
    q['lru_gb'] = gb.reshape(4, LRU_NCHUNK, LRU_CHUNK).transpose(1, 0, 2)
    q['lru_lam'] = p['lru_lambda'].reshape(2, LRU_NCHUNK, LRU_CHUNK).transpose(1, 0, 2)
    q['lru_wout'] = p['w_lru_out'].astype(BF16)
    q['watt'] = p['w_att_out'].astype(BF16)
    q['wo'] = p['w_o'].astype(BF16)
    q['expand'] = (jnp.arange(LANES)[:, None] == jnp.arange(GROUP_COLS)[None, :] // HEAD_DIM).astype(BF16)
    q['g2'] = p['norm2_g'].reshape(1, D_MODEL)
    q['wup'] = p['w_up'].astype(BF16)
    q['ffn_cw'] = p['ffn_conv_w']
    q['ffn_cb'] = p['ffn_conv_b'].reshape(1, D_FF)
    q['wdn'] = p['w_down'].astype(BF16)
    return q


def _layer(x, q):
    b, seq, _ = x.shape
    x2 = x.reshape(b * seq, D_MODEL)
    main, qkv1, qkv2 = _inproj(x2, seq, q['g1'], q['w_in'], q['bd'], q['qg'], q['kg'], *q['rope'])
    a_out = _lru(main.reshape(b, seq, MAIN_COLS), q['lru_cw'], q['lru_cb'], q['lru_wg'], q['lru_gb'],
                 q['lru_lam'], q['lru_wout'])
    outs, lses = [], []
    for qkv, (window, dilation) in zip((main, qkv1, qkv2), ATT_GROUPS):
        o, lse = _attention_group(qkv, b, seq, window, dilation)
        outs.append(o)
        lses.append(lse)
    x1 = _merge(x2, a_out.reshape(b * seq, D_MODEL), outs, lses, main, q['watt'], q['wo'], q['expand'])
    y = _ffn(x1, seq, q['g2'], q['wup'], q['ffn_cw'], q['ffn_cb'], q['wdn'])
    return y.reshape(b, seq, D_MODEL)


def kernel(x_prompt, x_sample, norm1_g, w_in, lru_conv_w, lru_conv_b, lru_wa, lru_ba, lru_wx, lru_bx,
           lru_lambda, w_lru_out, q_norm_g, k_norm_g, w_att_out, w_o, norm2_g, w_up, ffn_conv_w,
           ffn_conv_b, w_down):
    params = dict(norm1_g=norm1_g, w_in=w_in, lru_conv_w=lru_conv_w, lru_conv_b=lru_conv_b,
                  lru_wa=lru_wa, lru_ba=lru_ba, lru_wx=lru_wx, lru_bx=lru_bx, lru_lambda=lru_lambda,
                  w_lru_out=w_lru_out, q_norm_g=q_norm_g, k_norm_g=k_norm_g, w_att_out=w_att_out,
                  w_o=w_o, norm2_g=norm2_g, w_up=w_up, ffn_conv_w=ffn_conv_w, ffn_conv_b=ffn_conv_b,
                  w_down=w_down)
    depth = norm1_g.shape[0]
    seq = x_prompt.shape[1]
    assert x_sample.shape[1] == seq
    layers = [_prep_layer({name: p[l] for name, p in params.items()}, seq) for l in range(depth)]
    ys = []
    for x in (x_prompt, x_sample):
        for q in layers:
            x = _layer(x, q)
        ys.append(x)
    return tuple(ys)
```

```python
import functools

import jax
import jax.numpy as jnp
from jax import lax
from jax.experimental import pallas as pl
from jax.experimental.pallas import tpu as pltpu

F32 = jnp.float32
BF16 = jnp.bfloat16

D_MODEL = 1024
D_RNN = 1280
LRU_BLOCKS = 16
LRU_BW = D_RNN // LRU_BLOCKS
LRU_C = 8.0
LRU_CONV = 4
ATT_GROUPS = ((128, 1), (512, 4), (2048, 16))
N_GROUPS = len(ATT_GROUPS)
HEADS_PER_GROUP = 8
HEAD_DIM = 64
GROUP_COLS = HEADS_PER_GROUP * HEAD_DIM
ATT_COLS = N_GROUPS * GROUP_COLS
ROT_DIM = HEAD_DIM // 4
ROPE_THETA = 500000.0
D_FF = 3 * D_MODEL
FFN_CONV = 3
EPS = 1e-6
NEG = -1e30
IN_COLS = 2 * D_RNN + 3 * ATT_COLS + 2 * D_MODEL

LANES = 128
MXU_DIM = 256
VMEM_LIMIT_BYTES = 56 * 1024 * 1024

CHUNK = 512
MAIN_COLS = 2 * D_RNN + 3 * GROUP_COLS + 2 * D_MODEL
MAIN_CHUNKS = MAIN_COLS // CHUNK
QKV0_CHUNK0 = 2 * D_RNN // CHUNK
GATE_COL_BLOCK0 = (2 * D_RNN + 3 * GROUP_COLS) // D_MODEL

TOKEN_TILE = 512
ATT_QBLOCK = 128
ATT_HALF = 64

LRU_CHUNK = 256
LRU_NCHUNK = D_RNN // LRU_CHUNK
LRU_KSLABS = 4
LRU_TBLOCK = 256
LRU_NSUB = 8
LRU_PITCH = 264


def _cparams(sem):
    return pltpu.CompilerParams(dimension_semantics=sem, vmem_limit_bytes=VMEM_LIMIT_BYTES)


def _resident(shape):
    return pl.BlockSpec(shape, lambda *_: (0,) * len(shape), pipeline_mode=pl.Buffered(1))


def _sigmoid(x):
    return 0.5 * jnp.tanh(0.5 * x) + 0.5


def _gelu_tanh(x):
    return 0.5 * x * (1.0 + jnp.tanh(0.7978845608028654 * (x + 0.044715 * (x * x * x))))


def _rms_rows(x, g):
    ms = jnp.mean(x * x, axis=-1, keepdims=True)
    return x * lax.rsqrt(ms + EPS) * g


def _chunk_role(j):
    if QKV0_CHUNK0 <= j < QKV0_CHUNK0 + 3:
        return 0, j - QKV0_CHUNK0
    if j >= MAIN_CHUNKS:
        return 1 + (j - MAIN_CHUNKS) // 3, (j - MAIN_CHUNKS) % 3
    return None


def _inproj_kernel(x_ref, g1_ref, w_ref, bd_ref, qg_ref, kg_ref, ra_ref, rm_ref, rp_ref,
                   o_ref, d1_ref, d2_ref, tile_scr):
    tm = x_ref.shape[0]
    xn = _rms_rows(x_ref[...], g1_ref[...]).astype(BF16)
    ra = ra_ref[...]
    rm = rm_ref[...]
    rp = rp_ref[...]
    for j in range(IN_COLS // CHUNK):
        c0 = j * CHUNK
        acc = jnp.dot(xn, w_ref[:, c0:c0 + CHUNK], preferred_element_type=F32)
        role = _chunk_role(j)
        if role is None:
            o_ref[:, c0:c0 + CHUNK] = acc.astype(BF16)
            continue
        g, part = role
        tiles = []
        if part < 2:
            gain = (qg_ref if part == 0 else kg_ref)[g:g + 1, :]
            sq = (acc * acc).astype(BF16)
            bd = bd_ref[...]
            for t in range(CHUNK // MXU_DIM):
                sl = slice(t * MXU_DIM, (t + 1) * MXU_DIM)
                ss = jnp.dot(sq[:, sl], bd, preferred_element_type=F32)
                y = acc[:, sl] * lax.rsqrt(ss * (1.0 / HEAD_DIM) + EPS) * gain[:, sl]
                for u in range(MXU_DIM // LANES):
                    yt = y[:, u * LANES:(u + 1) * LANES]
                    tiles.append(yt * ra + pltpu.roll(yt, LANES - ROT_DIM // 2, 1) * rm
                                 + pltpu.roll(yt, ROT_DIM // 2, 1) * rp)
        else:
            tiles = [acc[:, t * LANES:(t + 1) * LANES] for t in range(CHUNK // LANES)]
        if g == 0:
            for t, val in enumerate(tiles):
                o_ref[:, c0 + t * LANES:c0 + (t + 1) * LANES] = val.astype(BF16)
        else:
            dil = ATT_GROUPS[g][1]
            dst = d1_ref if g == 1 else d2_ref
            for t, val in enumerate(tiles):
                tile_scr[t] = val
            for m in range(dil):
                for t in range(CHUNK // LANES):
                    col = (m * 3 + part) * GROUP_COLS + t * LANES
                    dst[:, col:col + LANES] = tile_scr[t, pl.ds(m, tm // dil, stride=dil), :].astype(BF16)


def _inproj(x2, seq, g1, w_in, bd, qg, kg, ra, rm, rp):
    n = x2.shape[0]
    tm = TOKEN_TILE
    tiles_per_seq = seq // tm
    rope_spec = pl.BlockSpec((tm, LANES), lambda i: (i % tiles_per_seq, 0))
    d1, d2 = ATT_GROUPS[1][1], ATT_GROUPS[2][1]
    return pl.pallas_call(
        _inproj_kernel,
        grid=(n // tm,),
        in_specs=[
            pl.BlockSpec((tm, D_MODEL), lambda i: (i, 0)),
            _resident((1, D_MODEL)),
            _resident((D_MODEL, IN_COLS)),
            _resident((MXU_DIM, MXU_DIM)),
            _resident((N_GROUPS, GROUP_COLS)),
            _resident((N_GROUPS, GROUP_COLS)),
            rope_spec, rope_spec, rope_spec,
        ],
        out_specs=[
            pl.BlockSpec((tm, MAIN_COLS), lambda i: (i, 0)),
            pl.BlockSpec((tm // d1, d1 * 3 * GROUP_COLS), lambda i: (i, 0)),
            pl.BlockSpec((tm // d2, d2 * 3 * GROUP_COLS), lambda i: (i, 0)),
        ],
        out_shape=[
            jax.ShapeDtypeStruct((n, MAIN_COLS), BF16),
            jax.ShapeDtypeStruct((n // d1, d1 * 3 * GROUP_COLS), BF16),
            jax.ShapeDtypeStruct((n // d2, d2 * 3 * GROUP_COLS), BF16),
        ],
        scratch_shapes=[pltpu.VMEM((CHUNK // LANES, tm, LANES), F32)],
        compiler_params=_cparams(("parallel",)),
        name="inproj",
    )(x2, g1, w_in, bd, qg, kg, ra, rm, rp)


def _lru_kslab0(c):
    return jnp.minimum(jnp.maximum(2 * c - 1, 0), D_RNN // LANES - LRU_KSLABS)


LRU_CONV_ROWS = 128


def _lru_kernel(x_ref, gate_ref, cw_ref, cb_ref, wg_ref, gb_ref, lam_ref, wout_ref, o_ref,
                xc_scr, acc_scr, a_scr, u_scr, h_scr, pad_scr):
    c = pl.program_id(1)
    seq = x_ref.shape[0]
    nsl = LRU_CHUNK // LANES
    chains = [(d, i) for d in range(2) for i in range(nsl)]

    @pl.when(c == 0)
    def _conv():
        pad_scr[0:8, :] = jnp.zeros((8, LANES), F32)
        pad_scr[seq + 8:seq + 16, :] = jnp.zeros((8, LANES), F32)
        for j in range(D_RNN // LANES):
            sl = slice(j * LANES, (j + 1) * LANES)
            pad_scr[8:seq + 8, :] = x_ref[:, sl].astype(F32)
            bias = cb_ref[:, sl]
            taps = [cw_ref[k:k + 1, sl] for k in range(LRU_CONV)]

            def conv_body(rb, carry):
                r0 = pl.multiple_of(rb * LRU_CONV_ROWS, LRU_CONV_ROWS)
                xc = bias
                for k in range(LRU_CONV):
                    off = 8 + k - LRU_CONV // 2
                    xc = xc + pad_scr[pl.ds(r0 + off, LRU_CONV_ROWS), :] * taps[k]
                xc_scr[j, pl.ds(r0, LRU_CONV_ROWS), :] = xc
                return carry

            lax.fori_loop(0, seq // LRU_CONV_ROWS, conv_body, 0)

    k0 = _lru_kslab0(c)
    lam = lam_ref[...]
    half_l2 = (-0.5 * LRU_C * 1.4426950408889634) * (jnp.maximum(-lam, 0.0) + jnp.log1p(jnp.exp(-jnp.abs(lam))))

    def gates_body(s, carry):
        r0 = pl.multiple_of(s * LRU_TBLOCK, LRU_TBLOCK)
        w0 = pl.multiple_of(s * LRU_PITCH, 8)
        lhs = jnp.concatenate([xc_scr[k0 + i, pl.ds(r0, LRU_TBLOCK), :] for i in range(LRU_KSLABS)],
                              axis=1).astype(BF16)
        half_xc = 0.5 * jnp.concatenate([xc_scr[nsl * c + i, pl.ds(r0, LRU_TBLOCK), :] for i in range(nsl)],
                                        axis=1)
        for d in range(2):
            za = jnp.dot(lhs, wg_ref[d, 0], preferred_element_type=F32) + gb_ref[2 * d:2 * d + 1, :]
            zx = jnp.dot(lhs, wg_ref[d, 1], preferred_element_type=F32) + gb_ref[2 * d + 1:2 * d + 2, :]
            hl = half_l2[d:d + 1, :]
            a = jnp.exp2(jnp.tanh(za) * hl + hl)
            t = 1.0 - a * a
            root = jnp.where(t > 0.0, t * lax.rsqrt(t), 0.0)
            u = root * ((jnp.tanh(zx) + 1.0) * half_xc)
            for i in range(nsl):
                a_scr[d, i, pl.ds(w0, LRU_TBLOCK), :] = a[:, i * LANES:(i + 1) * LANES]
                u_scr[d, i, pl.ds(w0, LRU_TBLOCK), :] = u[:, i * LANES:(i + 1) * LANES]
        return carry

    lax.fori_loop(0, LRU_NSUB, gates_body, 0, unroll=2)

    def step_rows(d, t):
        r = t if d == 0 else LRU_TBLOCK - 1 - t
        return pl.ds(r, LRU_NSUB, stride=LRU_PITCH)

    def ends_body(t, carry):
        out = []
        for n, (d, i) in enumerate(chains):
            av = a_scr[d, i, step_rows(d, t), :]
            out += [av * carry[2 * n] + u_scr[d, i, step_rows(d, t), :], av * carry[2 * n + 1]]
        return tuple(out)

    init = (jnp.zeros((LRU_NSUB, LANES), F32), jnp.ones((LRU_NSUB, LANES), F32)) * len(chains)
    ends = lax.fori_loop(0, LRU_TBLOCK, ends_body, init, unroll=8)

    sub = lax.broadcasted_iota(jnp.int32, (LRU_NSUB, LANES), 0)
    starts = []
    for n, (d, i) in enumerate(chains):
        h_end, cum_end = ends[2 * n], ends[2 * n + 1]
        cin = jnp.zeros((LRU_NSUB, LANES), F32)
        for step in range(LRU_NSUB - 1):
            nxt = h_end + cum_end * cin
            if d == 0:
                cin = jnp.where(sub == step + 1, pltpu.roll(nxt, 1, 0), cin)
            else:
                cin = jnp.where(sub == LRU_NSUB - 2 - step, pltpu.roll(nxt, LRU_NSUB - 1, 0), cin)
        starts.append(cin)

    def scan_body(t, carry):
        out = []
        for n, (d, i) in enumerate(chains):
            h = a_scr[d, i, step_rows(d, t), :] * carry[n] + u_scr[d, i, step_rows(d, t), :]
            h_scr[d, i, step_rows(d, t), :] = h
            out.append(h)
        return tuple(out)

    lax.fori_loop(0, LRU_TBLOCK, scan_body, tuple(starts), unroll=8)

    def out_body(first, s, carry):
        r0 = pl.multiple_of(s * LRU_TBLOCK, LRU_TBLOCK)
        w0 = pl.multiple_of(s * LRU_PITCH, 8)
        hsum = jnp.concatenate([h_scr[0, i, pl.ds(w0, LRU_TBLOCK), :] + h_scr[1, i, pl.ds(w0, LRU_TBLOCK), :]
                                for i in range(nsl)], axis=1)
        y = (_gelu_tanh(gate_ref[pl.ds(r0, LRU_TBLOCK), :].astype(F32)) * hsum).astype(BF16)
        part = jnp.dot(y, wout_ref[...], preferred_element_type=F32)
        if first:
            acc_scr[pl.ds(r0, LRU_TBLOCK), :] = part
        else:
            acc_scr[pl.ds(r0, LRU_TBLOCK), :] += part
        return carry

    @pl.when(c == 0)
    def _out_first():
        lax.fori_loop(0, LRU_NSUB, functools.partial(out_body, True), 0)

    @pl.when(c > 0)
    def _out_rest():
        lax.fori_loop(0, LRU_NSUB, functools.partial(out_body, False), 0)

    @pl.when(c == LRU_NCHUNK - 1)
    def _emit():
        o_ref[...] = acc_scr[...].astype(BF16)


def _lru(proj3, cw, cb, wg, gb, lam, wout):
    b, seq, _ = proj3.shape
    assert seq == LRU_NSUB * LRU_TBLOCK
    nsl = LRU_CHUNK // LANES
    gate_blk0 = D_RNN // LRU_CHUNK
    return pl.pallas_call(
        _lru_kernel,
        grid=(b, LRU_NCHUNK),
        in_specs=[
            pl.BlockSpec((None, seq, D_RNN), lambda i, c: (i, 0, 0), pipeline_mode=pl.Buffered(1)),
            pl.BlockSpec((None, seq, LRU_CHUNK), lambda i, c: (i, 0, gate_blk0 + c)),
            _resident((LRU_CONV, D_RNN)),
            _resident((1, D_RNN)),
            pl.BlockSpec((None, 2, 2, LRU_KSLABS * LANES, LRU_CHUNK), lambda i, c: (c, 0, 0, 0, 0)),
            pl.BlockSpec((None, 4, LRU_CHUNK), lambda i, c: (c, 0, 0)),
            pl.BlockSpec((None, 2, LRU_CHUNK), lambda i, c: (c, 0, 0)),
            pl.BlockSpec((LRU_CHUNK, D_MODEL), lambda i, c: (c, 0)),
        ],
        out_specs=pl.BlockSpec((None, seq, D_MODEL), lambda i, c: (i, 0, 0)),
        out_shape=jax.ShapeDtypeStruct((b, seq, D_MODEL), BF16),
        scratch_shapes=[
            pltpu.VMEM((D_RNN // LANES, seq, LANES), F32),
            pltpu.VMEM((seq, D_MODEL), F32),
            pltpu.VMEM((2, nsl, LRU_NSUB * LRU_PITCH, LANES), F32),
            pltpu.VMEM((2, nsl, LRU_NSUB * LRU_PITCH, LANES), F32),
            pltpu.VMEM((2, nsl, LRU_NSUB * LRU_PITCH, LANES), F32),
            pltpu.VMEM((seq + 16, LANES), F32),
        ],
        compiler_params=_cparams(("parallel", "arbitrary")),
        name="rglru",
    )(proj3, proj3, cw, cb, wg, gb, lam, wout)


def _attn_kernel(q_ref, k_ref, v_ref, bias_ref, o_ref, lse_ref):
    length = q_ref.shape[0]
    nqb = length // ATT_QBLOCK
    win = bias_ref.shape[2]
    lane = lax.broadcasted_iota(jnp.int32, (ATT_QBLOCK, LANES), 1)
    lo_half = lane < HEAD_DIM

    def qblock(qb, carry):
        q0 = pl.multiple_of(qb * ATT_QBLOCK, ATT_QBLOCK)
        if nqb == 1:
            ws = 0
            bias = bias_ref[0]
        else:
            ws = pl.multiple_of(jnp.clip(q0 - ATT_HALF, 0, length - win), ATT_HALF)
            bias = bias_ref[jnp.where(qb == 0, 0, jnp.where(qb == nqb - 1, 2, 1))]
        lse_tile = jnp.zeros((ATT_QBLOCK, LANES), F32)
        for p in range(GROUP_COLS // LANES):
            sl = slice(p * LANES, (p + 1) * LANES)
            qp = q_ref[pl.ds(q0, ATT_QBLOCK), sl]
            kw = k_ref[pl.ds(ws, win), sl]
            vw = v_ref[pl.ds(ws, win), sl]
            outs = []
            for hh in range(LANES // HEAD_DIM):
                keep = lo_half if hh == 0 else jnp.logical_not(lo_half)
                qm = jnp.where(keep, qp, jnp.zeros_like(qp))
                s = lax.dot_general(qm, kw, (((1,), (1,)), ((), ())), preferred_element_type=F32) + bias
                m = jnp.max(s, axis=-1, keepdims=True)
                e = jnp.exp(s - m)
                l = jnp.sum(e, axis=-1, keepdims=True)
                outs.append(jnp.dot(e.astype(BF16), vw, preferred_element_type=F32) / l)
                lse_tile = jnp.where(lane == p * (LANES // HEAD_DIM) + hh, m + jnp.log(l), lse_tile)
            o_ref[pl.ds(q0, ATT_QBLOCK), sl] = jnp.where(lo_half, outs[0], outs[1]).astype(BF16)
        lse_ref[pl.ds(q0, ATT_QBLOCK), :] = lse_tile
        return carry

    lax.fori_loop(0, nqb, qblock, 0)


def _attn_bias(length):
    win = min(2 * ATT_QBLOCK, length)
    offs = (0,) if length == ATT_QBLOCK else (0, ATT_HALF, win - ATT_QBLOCK)
    i = jnp.arange(ATT_QBLOCK)[:, None]
    j = jnp.arange(win)[None, :]
    return jnp.stack([jnp.where(jnp.abs(i + off - j) <= ATT_HALF, 0.0, NEG) for off in offs]).astype(F32)


def _attention_group(qkv, b, seq, window, dilation):
    length = seq // dilation
    assert (window // 2) // dilation == ATT_HALF and length % ATT_QBLOCK == 0
    pv = qkv.reshape(b, length, qkv.shape[1])
    chunk0 = QKV0_CHUNK0 if dilation == 1 else 0
    bias = _attn_bias(length)

    def col_spec(part):
        return pl.BlockSpec((None, length, GROUP_COLS), lambda i, m: (i, 0, m * 3 + chunk0 + part))

    o, lse = pl.pallas_call(
        _attn_kernel,
        grid=(b, dilation),
        in_specs=[col_spec(0), col_spec(1), col_spec(2), _resident(bias.shape)],
        out_specs=[
            pl.BlockSpec((None, length, GROUP_COLS), lambda i, m: (i, 0, m)),
            pl.BlockSpec((None, length, LANES), lambda i, m: (i, 0, m)),
        ],
        out_shape=[
            jax.ShapeDtypeStruct((b, length, dilation * GROUP_COLS), BF16),
            jax.ShapeDtypeStruct((b, length, dilation * LANES), F32),
        ],
        compiler_params=_cparams(("parallel", "parallel")),
        name=f"attn_d{dilation}",
    )(pv, pv, pv, bias)
    return o.reshape(b * length, dilation * GROUP_COLS), lse.reshape(b * length, dilation * LANES)


def _merge_kernel(x_ref, a_ref, o0_ref, o1_ref, o2_ref, l0_ref, l1_ref, l2_ref, ga_ref, gb_ref,
                  watt_ref, wo_ref, ex_ref, y_ref, o_scr, l_scr):
    tm = x_ref.shape[0]
    nt = GROUP_COLS // LANES
    for gi, (og_ref, lg_ref) in enumerate(((o1_ref, l1_ref), (o2_ref, l2_ref))):
        dil = ATT_GROUPS[gi + 1][1]
        for m in range(dil):
            rows = pl.ds(m, tm // dil, stride=dil)
            l_scr[gi, rows, :] = lg_ref[:, m * LANES:(m + 1) * LANES]
            for t in range(nt):
                col = m * GROUP_COLS + t * LANES
                o_scr[gi, t, rows, :] = og_ref[:, col:col + LANES].astype(F32)
    lses = [l0_ref[...], l_scr[0], l_scr[1]]
    group_out = [o0_ref[...].astype(F32)] + [
        jnp.concatenate([o_scr[gi, t] for t in range(nt)], axis=1) for gi in range(2)]
    mx = jnp.maximum(jnp.maximum(lses[0], lses[1]), lses[2])
    es = [jnp.exp(l - mx) for l in lses]
    den = es[0] + es[1] + es[2]
    ex = ex_ref[...]
    o = None
    for e, og in zip(es, group_out):
        w = e / den
        hi = w.astype(BF16)
        lo = (w - hi.astype(F32)).astype(BF16)
        wx = jnp.dot(hi, ex, preferred_element_type=F32) + jnp.dot(lo, ex, preferred_element_type=F32)
        t = wx * og
        o = t if o is None else o + t
    b_out = jnp.dot(o.astype(BF16), watt_ref[...], preferred_element_type=F32)
    mix = (_sigmoid(ga_ref[...].astype(F32)) * a_ref[...].astype(F32)
           + _sigmoid(gb_ref[...].astype(F32)) * b_out)
    y_ref[...] = x_ref[...] + jnp.dot(mix.astype(BF16), wo_ref[...], preferred_element_type=F32)


def _merge(x2, a_out, outs, lses, proj2, watt, wo, ex):
    n = x2.shape[0]
    tm = TOKEN_TILE
    row = lambda cols, dil=1: pl.BlockSpec((tm // dil, cols * dil), lambda i: (i, 0))
    d1, d2 = ATT_GROUPS[1][1], ATT_GROUPS[2][1]
    return pl.pallas_call(
        _merge_kernel,
        grid=(n // tm,),
        in_specs=[
            row(D_MODEL), row(D_MODEL),
            row(GROUP_COLS), row(GROUP_COLS, d1), row(GROUP_COLS, d2),
            row(LANES), row(LANES, d1), row(LANES, d2),
            pl.BlockSpec((tm, D_MODEL), lambda i: (i, GATE_COL_BLOCK0)),
            pl.BlockSpec((tm, D_MODEL), lambda i: (i, GATE_COL_BLOCK0 + 1)),
            _resident((GROUP_COLS, D_MODEL)),
            _resident((D_MODEL, D_MODEL)),
            _resident((LANES, GROUP_COLS)),
        ],
        out_specs=row(D_MODEL),
        out_shape=jax.ShapeDtypeStruct((n, D_MODEL), F32),
        scratch_shapes=[
            pltpu.VMEM((2, GROUP_COLS // LANES, tm, LANES), F32),
            pltpu.VMEM((2, tm, LANES), F32),
        ],
        compiler_params=_cparams(("parallel",)),
        name="merge",
    )(x2, a_out, *outs, *lses, proj2, proj2, watt, wo, ex)


FFN_HALO = 16


def _ffn_kernel(x_ref, xp_ref, xn_ref, g2_ref, wup_ref, cw_ref, cb_ref, wdn_ref, y_ref, lhs_scr,
                *, tiles_per_seq):
    tm = x_ref.shape[0]
    ti = pl.program_id(0) % tiles_per_seq
    g2 = g2_ref[...]
    x = x_ref[...]
    keep_prev = jnp.where(ti == 0, 0.0, 1.0)
    keep_next = jnp.where(ti == tiles_per_seq - 1, 0.0, 1.0)
    xn = _rms_rows(x, g2).astype(BF16)
    lhs_scr[0:FFN_HALO, :] = (_rms_rows(xp_ref[...], g2) * keep_prev).astype(BF16)
    lhs_scr[FFN_HALO:FFN_HALO + tm, :] = xn
    lhs_scr[FFN_HALO + tm:, :] = (_rms_rows(xn_ref[...], g2) * keep_next).astype(BF16)
    acc = x
    for j in range(D_FF // CHUNK):
        sl = slice(j * CHUNK, (j + 1) * CHUNK)
        gext = jnp.dot(lhs_scr[...], wup_ref[:, sl], preferred_element_type=F32)
        rows = gext.shape[0]
        gc = cb_ref[:, sl]
        for k in range(FFN_CONV):
            shift = (FFN_CONV // 2 - k) % rows
            gk = gext if shift == 0 else pltpu.roll(gext, shift, 0)
            gc = gc + gk[FFN_HALO:FFN_HALO + tm, :] * cw_ref[k:k + 1, sl]
        val = jnp.dot(xn, wup_ref[:, D_FF + j * CHUNK:D_FF + (j + 1) * CHUNK], preferred_element_type=F32)
        h = (_gelu_tanh(gc) * val).astype(BF16)
        acc = acc + jnp.dot(h, wdn_ref[sl, :], preferred_element_type=F32)
    y_ref[...] = acc


def _ffn(x2, seq, g2, wup, cw, cb, wdn):
    n = x2.shape[0]
    tm = TOKEN_TILE
    hb = tm // FFN_HALO
    last = n // FFN_HALO - 1
    return pl.pallas_call(
        functools.partial(_ffn_kernel, tiles_per_seq=seq // tm),
        grid=(n // tm,),
        in_specs=[
            pl.BlockSpec((tm, D_MODEL), lambda i: (i, 0)),
            pl.BlockSpec((FFN_HALO, D_MODEL), lambda i: (jnp.maximum(i * hb - 1, 0), 0)),
            pl.BlockSpec((FFN_HALO, D_MODEL), lambda i: (jnp.minimum((i + 1) * hb, last), 0)),
            _resident((1, D_MODEL)),
            _resident((D_MODEL, 2 * D_FF)),
            _resident((FFN_CONV, D_FF)),
            _resident((1, D_FF)),
            _resident((D_FF, D_MODEL)),
        ],
        out_specs=pl.BlockSpec((tm, D_MODEL), lambda i: (i, 0)),
        out_shape=jax.ShapeDtypeStruct((n, D_MODEL), F32),
        scratch_shapes=[pltpu.VMEM((tm + 2 * FFN_HALO, D_MODEL), BF16)],
        compiler_params=_cparams(("parallel",)),
        name="convffn",
    )(x2, x2, x2, g2, wup, cw, cb, wdn)


def _rope_tables(seq):
    pos = jnp.arange(seq, dtype=F32)
    inv = ROPE_THETA ** (-jnp.arange(0, ROT_DIM, 2, dtype=F32) / ROT_DIM)
    ang = pos[:, None] * inv[None, :]
    cos, sin = jnp.cos(ang), jnp.sin(ang)
    half = ROT_DIM // 2
    pad = HEAD_DIM - ROT_DIM
    one = jnp.ones((seq, pad), F32)
    zero = jnp.zeros((seq, pad), F32)
    zh = jnp.zeros((seq, half), F32)
    per_head = lambda parts: jnp.tile(jnp.concatenate(parts, axis=1), (1, LANES // HEAD_DIM))
    ra = per_head([cos, cos, one])
    rm = per_head([-sin, zh, zero])
    rp = per_head([zh, sin, zero])
    return ra, rm, rp


def _block_diag_dense(w):
    nb, c, d = w.shape
    return jnp.einsum('ncd,nm->ncmd', w, jnp.eye(nb, dtype=w.dtype)).reshape(nb * c, nb * d)


def _prep_layer(p, seq):
    q = {}
    q['g1'] = p['norm1_g'].reshape(1, D_MODEL)
    q_lo = 2 * D_RNN
    seg = lambda part, g: (q_lo + part * ATT_COLS + g * GROUP_COLS, q_lo + part * ATT_COLS + (g + 1) * GROUP_COLS)
    qkv = lambda g: [seg(part, g) for part in range(3)]
    order = [(0, q_lo)] + qkv(0) + [(q_lo + 3 * ATT_COLS, IN_COLS)] + qkv(1) + qkv(2)
    w_in = p['w_in'].astype(BF16)
    q['w_in'] = jnp.concatenate([w_in[:, lo:hi] for lo, hi in order], axis=1)
    head = jnp.arange(MXU_DIM) // HEAD_DIM
    q['bd'] = (head[:, None] == head[None, :]).astype(BF16)
    q['qg'] = jnp.tile(p['q_norm_g'], (1, HEADS_PER_GROUP)) * (HEAD_DIM ** -0.5)
    q['kg'] = jnp.tile(p['k_norm_g'], (1, HEADS_PER_GROUP))
    q['rope'] = _rope_tables(seq)
    q['lru_cw'] = p['lru_conv_w']
    q['lru_cb'] = p['lru_conv_b'].reshape(1, D_RNN)
    dense = jnp.stack([jnp.stack([_block_diag_dense(p['lru_wa'][d]), _block_diag_dense(p['lru_wx'][d])])
                       for d in range(2)])
    dense = (0.5 * dense).astype(BF16)
    chunks = []
    for c in range(LRU_NCHUNK):
        k0 = min(max(2 * c - 1, 0), D_RNN // LANES - LRU_KSLABS) * LANES
        chunks.append(dense[:, :, k0:k0 + LRU_KSLABS * LANES, c * LRU_CHUNK:(c + 1) * LRU_CHUNK])
    q['lru_wg'] = jnp.stack(chunks)
    gb = 0.5 * jnp.stack([p['lru_ba'][0], p['lru_bx'][0], p['lru_ba'][1], p['lru_bx'][1]])
    q['lru_gb'] = gb.reshape(4, LRU_NCHUNK, LRU_CHUNK).transpose(1, 0, 2)
    q['lru_lam'] = p['lru_lambda'].reshape(2, LRU_NCHUNK, LRU_CHUNK).transpose(1, 0, 2)
    q['lru_wout'] = p['w_lru_out'].astype(BF16)
    q['watt'] = p['w_att_out'].astype(BF16)
    q['wo'] = p['w_o'].astype(BF16)
    q['expand'] = (jnp.arange(LANES)[:, None] == jnp.arange(GROUP_COLS)[None, :] // HEAD_DIM).astype(BF16)
    q['g2'] = p['norm2_g'].reshape(1, D_MODEL)
    q['wup'] = p['w_up'].astype(BF16)
    q['ffn_cw'] = p['ffn_conv_w']
    q['ffn_cb'] = p['ffn_conv_b'].reshape(1, D_FF)
    q['wdn'] = p['w_down'].astype(BF16)
    return q


def _layer(x, q):
    b, seq, _ = x.shape
    x2 = x.reshape(b * seq, D_MODEL)
    main, qkv1, qkv2 = _inproj(x2, seq, q['g1'], q['w_in'], q['bd'], q['qg'], q['kg'], *q['rope'])
    a_out = _lru(main.reshape(b, seq, MAIN_COLS), q['lru_cw'], q['lru_cb'], q['lru_wg'], q['lru_gb'],
                 q['lru_lam'], q['lru_wout'])
    outs, lses = [], []
    for qkv, (window, dilation) in zip((main, qkv1, qkv2), ATT_GROUPS):
        o, lse = _attention_group(qkv, b, seq, window, dilation)
        outs.append(o)
        lses.append(lse)
    x1 = _merge(x2, a_out.reshape(b * seq, D_MODEL), outs, lses, main, q['watt'], q['wo'], q['expand'])
    y = _ffn(x1, seq, q['g2'], q['wup'], q['ffn_cw'], q['ffn_cb'], q['wdn'])
    return y.reshape(b, seq, D_MODEL)


def kernel(x_prompt, x_sample, norm1_g, w_in, lru_conv_w, lru_conv_b, lru_wa, lru_ba, lru_wx, lru_bx,
           lru_lambda, w_lru_out, q_norm_g, k_norm_g, w_att_out, w_o, norm2_g, w_up, ffn_conv_w,
           ffn_conv_b, w_down):
    params = dict(norm1_g=norm1_g, w_in=w_in, lru_conv_w=lru_conv_w, lru_conv_b=lru_conv_b,
                  lru_wa=lru_wa, lru_ba=lru_ba, lru_wx=lru_wx, lru_bx=lru_bx, lru_lambda=lru_lambda,
                  w_lru_out=w_lru_out, q_norm_g=q_norm_g, k_norm_g=k_norm_g, w_att_out=w_att_out,
                  w_o=w_o, norm2_g=norm2_g, w_up=w_up, ffn_conv_w=ffn_conv_w, ffn_conv_b=ffn_conv_b,
                  w_down=w_down)
    depth = norm1_g.shape[0]
    seq = x_prompt.shape[1]
    assert x_sample.shape[1] == seq
    layers = [_prep_layer({name: p[l] for name, p in params.items()}, seq) for l in range(depth)]
    ys = []
    for x in (x_prompt, x_sample):
        for q in layers:
            x = _layer(x, q)
        ys.append(x)
    return tuple(ys)
```

```python
import functools

import jax
import jax.numpy as jnp
from jax import lax
from jax.experimental import pallas as pl
from jax.experimental.pallas import tpu as pltpu

F32 = jnp.float32
BF16 = jnp.bfloat16

D_MODEL = 1024
D_RNN = 1280
LRU_BLOCKS = 16
LRU_BW = D_RNN // LRU_BLOCKS
LRU_C = 8.0
LRU_CONV = 4
ATT_GROUPS = ((128, 1), (512, 4), (2048, 16))
N_GROUPS = len(ATT_GROUPS)
HEADS_PER_GROUP = 8
HEAD_DIM = 64
GROUP_COLS = HEADS_PER_GROUP * HEAD_DIM
ATT_COLS = N_GROUPS * GROUP_COLS
ROT_DIM = HEAD_DIM // 4
ROPE_THETA = 500000.0
D_FF = 3 * D_MODEL
FFN_CONV = 3
EPS = 1e-6
NEG = -1e30
LOG2E = 1.4426950408889634
LN2 = 0.6931471805599453
IN_COLS = 2 * D_RNN + 3 * ATT_COLS + 2 * D_MODEL

LANES = 128
MXU_DIM = 256
VMEM_LIMIT_BYTES = 56 * 1024 * 1024

CHUNK = 512
MAIN_COLS = 2 * D_RNN + 3 * GROUP_COLS + 2 * D_MODEL
MAIN_CHUNKS = MAIN_COLS // CHUNK
QKV0_CHUNK0 = 2 * D_RNN // CHUNK
GATE_COL_BLOCK0 = (2 * D_RNN + 3 * GROUP_COLS) // D_MODEL

TOKEN_TILE = 512
ATT_QBLOCK = 128
ATT_HALF = 64
ATT_MIN_STEP_TOKENS = 512
ATT_MAX_CONST_SHIFT = 60.0
ATT_BOUND_MARGIN = 1.02

LRU_CHUNK = 256
LRU_NCHUNK = D_RNN // LRU_CHUNK
LRU_KSLABS = 4
LRU_TBLOCK = 256
LRU_NSUB = 8
LRU_PITCH = 264


def _cparams(sem):
    return pltpu.CompilerParams(dimension_semantics=sem, vmem_limit_bytes=VMEM_LIMIT_BYTES)


def _resident(shape):
    return pl.BlockSpec(shape, lambda *_: (0,) * len(shape), pipeline_mode=pl.Buffered(1))


def _sigmoid(x):
    return 0.5 * jnp.tanh(0.5 * x) + 0.5


def _gelu_tanh(x):
    return 0.5 * x * (1.0 + jnp.tanh(0.7978845608028654 * (x + 0.044715 * (x * x * x))))


def _rms_rows(x, g):
    ms = jnp.mean(x * x, axis=-1, keepdims=True)
    return x * lax.rsqrt(ms + EPS) * g


def _chunk_role(j):
    if QKV0_CHUNK0 <= j < QKV0_CHUNK0 + 3:
        return 0, j - QKV0_CHUNK0
    if j >= MAIN_CHUNKS:
        return 1 + (j - MAIN_CHUNKS) // 3, (j - MAIN_CHUNKS) % 3
    return None


def _inproj_kernel(x_ref, g1_ref, w_ref, bd_ref, qg_ref, kg_ref, ra_ref, rm_ref, rp_ref,
                   o_ref, d1_ref, d2_ref, tile_scr):
    tm = x_ref.shape[0]
    xn = _rms_rows(x_ref[...], g1_ref[...]).astype(BF16)
    ra = ra_ref[...]
    rm = rm_ref[...]
    rp = rp_ref[...]
    for j in range(IN_COLS // CHUNK):
        c0 = j * CHUNK
        acc = jnp.dot(xn, w_ref[:, c0:c0 + CHUNK], preferred_element_type=F32)
        role = _chunk_role(j)
        if role is None:
            o_ref[:, c0:c0 + CHUNK] = acc.astype(BF16)
            continue
        g, part = role
        tiles = []
        if part < 2:
            gain = (qg_ref if part == 0 else kg_ref)[g:g + 1, :]
            sq = (acc * acc).astype(BF16)
            bd = bd_ref[...]
            for t in range(CHUNK // MXU_DIM):
                sl = slice(t * MXU_DIM, (t + 1) * MXU_DIM)
                ss = jnp.dot(sq[:, sl], bd, preferred_element_type=F32)
                y = acc[:, sl] * lax.rsqrt(ss * (1.0 / HEAD_DIM) + EPS) * gain[:, sl]
                for u in range(MXU_DIM // LANES):
                    yt = y[:, u * LANES:(u + 1) * LANES]
                    tiles.append(yt * ra + pltpu.roll(yt, LANES - ROT_DIM // 2, 1) * rm
                                 + pltpu.roll(yt, ROT_DIM // 2, 1) * rp)
        else:
            tiles = [acc[:, t * LANES:(t + 1) * LANES] for t in range(CHUNK // LANES)]
        if g == 0:
            for t, val in enumerate(tiles):
                o_ref[:, c0 + t * LANES:c0 + (t + 1) * LANES] = val.astype(BF16)
        else:
            dil = ATT_GROUPS[g][1]
            dst = d1_ref if g == 1 else d2_ref
            for t, val in enumerate(tiles):
                tile_scr[t] = val
            for m in range(dil):
                for t in range(CHUNK // LANES):
                    col = (m * 3 + part) * GROUP_COLS + t * LANES
                    dst[:, col:col + LANES] = tile_scr[t, pl.ds(m, tm // dil, stride=dil), :].astype(BF16)


def _inproj(x2, seq, g1, w_in, bd, qg, kg, ra, rm, rp):
    n = x2.shape[0]
    tm = TOKEN_TILE
    tiles_per_seq = seq // tm
    rope_spec = pl.BlockSpec((tm, LANES), lambda i: (i % tiles_per_seq, 0))
    d1, d2 = ATT_GROUPS[1][1], ATT_GROUPS[2][1]
    return pl.pallas_call(
        _inproj_kernel,
        grid=(n // tm,),
        in_specs=[
            pl.BlockSpec((tm, D_MODEL), lambda i: (i, 0)),
            _resident((1, D_MODEL)),
            _resident((D_MODEL, IN_COLS)),
            _resident((MXU_DIM, MXU_DIM)),
            _resident((N_GROUPS, GROUP_COLS)),
            _resident((N_GROUPS, GROUP_COLS)),
            rope_spec, rope_spec, rope_spec,
        ],
        out_specs=[
            pl.BlockSpec((tm, MAIN_COLS), lambda i: (i, 0)),
            pl.BlockSpec((tm // d1, d1 * 3 * GROUP_COLS), lambda i: (i, 0)),
            pl.BlockSpec((tm // d2, d2 * 3 * GROUP_COLS), lambda i: (i, 0)),
        ],
        out_shape=[
            jax.ShapeDtypeStruct((n, MAIN_COLS), BF16),
            jax.ShapeDtypeStruct((n // d1, d1 * 3 * GROUP_COLS), BF16),
            jax.ShapeDtypeStruct((n // d2, d2 * 3 * GROUP_COLS), BF16),
        ],
        scratch_shapes=[pltpu.VMEM((CHUNK // LANES, tm, LANES), F32)],
        compiler_params=_cparams(("parallel",)),
        name="inproj",
    )(x2, g1, w_in, bd, qg, kg, ra, rm, rp)


def _lru_kslab0(c):
    return jnp.minimum(jnp.maximum(2 * c - 1, 0), D_RNN // LANES - LRU_KSLABS)


LRU_CONV_ROWS = 128


def _lru_kernel(x_ref, gate_ref, cw_ref, cb_ref, wg_ref, gb_ref, lam_ref, wout_ref, o_ref,
                xc_scr, acc_scr, a_scr, u_scr, h_scr, pad_scr):
    c = pl.program_id(1)
    seq = x_ref.shape[0]
    nsl = LRU_CHUNK // LANES
    chains = [(d, i) for d in range(2) for i in range(nsl)]

    @pl.when(c == 0)
    def _conv():
        pad_scr[0:8, :] = jnp.zeros((8, LANES), F32)
        pad_scr[seq + 8:seq + 16, :] = jnp.zeros((8, LANES), F32)
        for j in range(D_RNN // LANES):
            sl = slice(j * LANES, (j + 1) * LANES)
            pad_scr[8:seq + 8, :] = x_ref[:, sl].astype(F32)
            bias = cb_ref[:, sl]
            taps = [cw_ref[k:k + 1, sl] for k in range(LRU_CONV)]

            def conv_body(rb, carry):
                r0 = pl.multiple_of(rb * LRU_CONV_ROWS, LRU_CONV_ROWS)
                xc = bias
                for k in range(LRU_CONV):
                    off = 8 + k - LRU_CONV // 2
                    xc = xc + pad_scr[pl.ds(r0 + off, LRU_CONV_ROWS), :] * taps[k]
                xc_scr[j, pl.ds(r0, LRU_CONV_ROWS), :] = xc
                return carry

            lax.fori_loop(0, seq // LRU_CONV_ROWS, conv_body, 0)

    k0 = _lru_kslab0(c)
    lam = lam_ref[...]
    half_l2 = (-0.5 * LRU_C * LOG2E) * (jnp.maximum(-lam, 0.0) + jnp.log1p(jnp.exp(-jnp.abs(lam))))

    def gates_body(s, carry):
        r0 = pl.multiple_of(s * LRU_TBLOCK, LRU_TBLOCK)
        w0 = pl.multiple_of(s * LRU_PITCH, 8)
        lhs = jnp.concatenate([xc_scr[k0 + i, pl.ds(r0, LRU_TBLOCK), :] for i in range(LRU_KSLABS)],
                              axis=1).astype(BF16)
        half_xc = 0.5 * jnp.concatenate([xc_scr[nsl * c + i, pl.ds(r0, LRU_TBLOCK), :] for i in range(nsl)],
                                        axis=1)
        for d in range(2):
            za = jnp.dot(lhs, wg_ref[d, 0], preferred_element_type=F32) + gb_ref[2 * d:2 * d + 1, :]
            zx = jnp.dot(lhs, wg_ref[d, 1], preferred_element_type=F32) + gb_ref[2 * d + 1:2 * d + 2, :]
            hl = half_l2[d:d + 1, :]
            a = jnp.exp2(jnp.tanh(za) * hl + hl)
            t = 1.0 - a * a
            root = jnp.where(t > 0.0, t * lax.rsqrt(t), 0.0)
            u = root * ((jnp.tanh(zx) + 1.0) * half_xc)
            for i in range(nsl):
                a_scr[d, i, pl.ds(w0, LRU_TBLOCK), :] = a[:, i * LANES:(i + 1) * LANES]
                u_scr[d, i, pl.ds(w0, LRU_TBLOCK), :] = u[:, i * LANES:(i + 1) * LANES]
        return carry

    lax.fori_loop(0, LRU_NSUB, gates_body, 0, unroll=2)

    def step_rows(d, t):
        r = t if d == 0 else LRU_TBLOCK - 1 - t
        return pl.ds(r, LRU_NSUB, stride=LRU_PITCH)

    def ends_body(t, carry):
        out = []
        for n, (d, i) in enumerate(chains):
            av = a_scr[d, i, step_rows(d, t), :]
            out += [av * carry[2 * n] + u_scr[d, i, step_rows(d, t), :], av * carry[2 * n + 1]]
        return tuple(out)

    init = (jnp.zeros((LRU_NSUB, LANES), F32), jnp.ones((LRU_NSUB, LANES), F32)) * len(chains)
    ends = lax.fori_loop(0, LRU_TBLOCK, ends_body, init, unroll=8)

    sub = lax.broadcasted_iota(jnp.int32, (LRU_NSUB, LANES), 0)
    starts = []
    for n, (d, i) in enumerate(chains):
        h_end, cum_end = ends[2 * n], ends[2 * n + 1]
        cin = jnp.zeros((LRU_NSUB, LANES), F32)
        for step in range(LRU_NSUB - 1):
            nxt = h_end + cum_end * cin
            if d == 0:
                cin = jnp.where(sub == step + 1, pltpu.roll(nxt, 1, 0), cin)
            else:
                cin = jnp.where(sub == LRU_NSUB - 2 - step, pltpu.roll(nxt, LRU_NSUB - 1, 0), cin)
        starts.append(cin)

    def scan_body(t, carry):
        out = []
        for n, (d, i) in enumerate(chains):
            h = a_scr[d, i, step_rows(d, t), :] * carry[n] + u_scr[d, i, step_rows(d, t), :]
            h_scr[d, i, step_rows(d, t), :] = h
            out.append(h)
        return tuple(out)

    lax.fori_loop(0, LRU_TBLOCK, scan_body, tuple(starts), unroll=8)

    def out_body(first, s, carry):
        r0 = pl.multiple_of(s * LRU_TBLOCK, LRU_TBLOCK)
        w0 = pl.multiple_of(s * LRU_PITCH, 8)
        hsum = jnp.concatenate([h_scr[0, i, pl.ds(w0, LRU_TBLOCK), :] + h_scr[1, i, pl.ds(w0, LRU_TBLOCK), :]
                                for i in range(nsl)], axis=1)
        y = (_gelu_tanh(gate_ref[pl.ds(r0, LRU_TBLOCK), :].astype(F32)) * hsum).astype(BF16)
        part = jnp.dot(y, wout_ref[...], preferred_element_type=F32)
        if first:
            acc_scr[pl.ds(r0, LRU_TBLOCK), :] = part
        else:
            acc_scr[pl.ds(r0, LRU_TBLOCK), :] += part
        return carry

    @pl.when(c == 0)
    def _out_first():
        lax.fori_loop(0, LRU_NSUB, functools.partial(out_body, True), 0)

    @pl.when(c > 0)
    def _out_rest():
        lax.fori_loop(0, LRU_NSUB, functools.partial(out_body, False), 0)

    @pl.when(c == LRU_NCHUNK - 1)
    def _emit():
        o_ref[...] = acc_scr[...].astype(BF16)


def _lru(proj3, cw, cb, wg, gb, lam, wout):
    b, seq, _ = proj3.shape
    assert seq == LRU_NSUB * LRU_TBLOCK
    nsl = LRU_CHUNK // LANES
    gate_blk0 = D_RNN // LRU_CHUNK
    return pl.pallas_call(
        _lru_kernel,
        grid=(b, LRU_NCHUNK),
        in_specs=[
            pl.BlockSpec((None, seq, D_RNN), lambda i, c: (i, 0, 0), pipeline_mode=pl.Buffered(1)),
            pl.BlockSpec((None, seq, LRU_CHUNK), lambda i, c: (i, 0, gate_blk0 + c)),
            _resident((LRU_CONV, D_RNN)),
            _resident((1, D_RNN)),
            pl.BlockSpec((None, 2, 2, LRU_KSLABS * LANES, LRU_CHUNK), lambda i, c: (c, 0, 0, 0, 0)),
            pl.BlockSpec((None, 4, LRU_CHUNK), lambda i, c: (c, 0, 0)),
            pl.BlockSpec((None, 2, LRU_CHUNK), lambda i, c: (c, 0, 0)),
            pl.BlockSpec((LRU_CHUNK, D_MODEL), lambda i, c: (c, 0)),
        ],
        out_specs=pl.BlockSpec((None, seq, D_MODEL), lambda i, c: (i, 0, 0)),
        out_shape=jax.ShapeDtypeStruct((b, seq, D_MODEL), BF16),
        scratch_shapes=[
            pltpu.VMEM((D_RNN // LANES, seq, LANES), F32),
            pltpu.VMEM((seq, D_MODEL), F32),
            pltpu.VMEM((2, nsl, LRU_NSUB * LRU_PITCH, LANES), F32),
            pltpu.VMEM((2, nsl, LRU_NSUB * LRU_PITCH, LANES), F32),
            pltpu.VMEM((2, nsl, LRU_NSUB * LRU_PITCH, LANES), F32),
            pltpu.VMEM((seq + 16, LANES), F32),
        ],
        compiler_params=_cparams(("parallel", "arbitrary")),
        name="rglru",
    )(proj3, proj3, cw, cb, wg, gb, lam, wout)


def _attn_kernel(*refs, rpb, fused, row_max):
    n_in = 1 if fused else 3
    qkv_refs = refs[:n_in]
    bias_ref, shift_ref, o_ref, lse_ref, kt_scr = refs[n_in:]
    length = o_ref.shape[0]
    nkb = length // ATT_QBLOCK
    lane = lax.broadcasted_iota(jnp.int32, (ATT_QBLOCK, LANES), 1)
    lo_half = lane < HEAD_DIM
    lo_keys = lax.broadcasted_iota(jnp.int32, (2 * ATT_QBLOCK, LANES), 1) < HEAD_DIM
    npairs = GROUP_COLS // LANES

    def scores_block(qp_of, kt_of, vw_of, bias):
        outs = []
        lse_tile = jnp.zeros((ATT_QBLOCK, LANES), F32)
        for p in range(npairs):
            qp, kt, vw = qp_of(p), kt_of(p), vw_of(p)
            full, shifts = [], []
            for hh in range(LANES // HEAD_DIM):
                own = lo_half if hh == 0 else jnp.logical_not(lo_half)
                qm = jnp.where(own, qp, jnp.zeros_like(qp))
                s = jnp.dot(qm, kt, preferred_element_type=F32) + bias
                if row_max:
                    m = jnp.max(s, axis=-1, keepdims=True)
                    s = s - m
                    shifts.append(m)
                e = jnp.exp2(s).astype(BF16)
                v1 = jnp.where(lo_keys if hh == 0 else jnp.logical_not(lo_keys), vw, jnp.ones_like(vw))
                full.append(jnp.dot(e, v1, preferred_element_type=F32))
            numer = jnp.where(lo_half, full[0], full[1])
            denom = pltpu.roll(jnp.where(lo_half, full[1], full[0]), HEAD_DIM, 1)
            outs.append(numer * (1.0 / denom))
            shift = jnp.where(lo_half, shifts[0], shifts[1]) if row_max else shift_ref[...]
            lse = LN2 * (shift + jnp.log2(denom))
            lse_tile = jnp.where(jnp.logical_or(lane == p, lane == HEAD_DIM + p), lse, lse_tile)
        return outs, lse_tile

    for r in range(rpb):
        def cols(part, p, r=r):
            base = ((r * 3 + part) * GROUP_COLS if fused else 0) + p * LANES
            return slice(base, base + LANES)

        q_ref, k_ref, v_ref = (qkv_refs[0],) * 3 if fused else qkv_refs

        def transpose_body(kb, carry):
            k0 = pl.multiple_of(kb * ATT_QBLOCK, ATT_QBLOCK)
            for p in range(npairs):
                kt_scr[p, kb] = k_ref[pl.ds(k0, ATT_QBLOCK), cols(1, p)].T
            return carry

        lax.fori_loop(0, nkb, transpose_body, 0, unroll=min(nkb, 4))

        def store(rows_list, outs, lse_tile, r=r):
            for dst_rows, src in rows_list:
                for p in range(npairs):
                    c0 = r * GROUP_COLS + p * LANES
                    o_ref[dst_rows, c0:c0 + LANES] = outs[p][src].astype(BF16)
                lse_ref[dst_rows, r * LANES:(r + 1) * LANES] = lse_tile[src]

        def interior(i, carry):
            k0 = pl.multiple_of(i * ATT_QBLOCK, ATT_QBLOCK)
            q0 = pl.multiple_of(k0 + ATT_HALF, ATT_HALF)
            outs, lse_tile = scores_block(
                lambda p: q_ref[pl.ds(q0, ATT_QBLOCK), cols(0, p)],
                lambda p: jnp.concatenate([kt_scr[p, i], kt_scr[p, i + 1]], axis=1),
                lambda p: v_ref[pl.ds(k0, 2 * ATT_QBLOCK), cols(2, p)],
                bias_ref[0])
            store([(pl.ds(q0, ATT_QBLOCK), slice(None))], outs, lse_tile)
            return carry

        if nkb > 1:
            lax.fori_loop(0, nkb - 1, interior, 0, unroll=3)

        head_rows, tail_rows = slice(0, ATT_HALF), slice(length - ATT_HALF, length)
        tail_keys = slice(length - ATT_QBLOCK, length)
        outs, lse_tile = scores_block(
            lambda p: jnp.concatenate([q_ref[head_rows, cols(0, p)], q_ref[tail_rows, cols(0, p)]], axis=0),
            lambda p: jnp.concatenate([kt_scr[p, 0], kt_scr[p, nkb - 1]], axis=1),
            lambda p: jnp.concatenate([v_ref[0:ATT_QBLOCK, cols(2, p)], v_ref[tail_keys, cols(2, p)]], axis=0),
            bias_ref[1])
        store([(head_rows, slice(0, ATT_HALF)), (tail_rows, slice(ATT_HALF, ATT_QBLOCK))], outs, lse_tile)


def _attn_bias():
    i = jnp.arange(ATT_QBLOCK)[:, None]
    j = jnp.arange(2 * ATT_QBLOCK)[None, :]
    interior = jnp.abs(i + ATT_HALF - j) <= ATT_HALF
    first = (i < ATT_HALF) & (j < ATT_QBLOCK) & (jnp.abs(i - j) <= ATT_HALF)
    last = (i >= ATT_HALF) & (j >= ATT_QBLOCK) & (jnp.abs(i - (j - ATT_QBLOCK)) <= ATT_HALF)
    return jnp.stack([jnp.where(interior, 0.0, NEG), jnp.where(first | last, 0.0, NEG)]).astype(F32)


def _attention_group(qkv, b, seq, window, dilation, score_bound):
    length = seq // dilation
    assert (window // 2) // dilation == ATT_HALF and length % ATT_QBLOCK == 0
    pv = qkv.reshape(b, length, qkv.shape[1])
    fused = dilation > 1
    rpb = min(dilation, max(1, ATT_MIN_STEP_TOKENS // length))
    if fused:
        qkv_specs = [pl.BlockSpec((None, length, rpb * 3 * GROUP_COLS), lambda i, m: (i, 0, m))]
    else:
        qkv_specs = [pl.BlockSpec((None, length, GROUP_COLS), lambda i, m, part=part: (i, 0, QKV0_CHUNK0 + part))
                     for part in range(3)]

    def call(row_max, bias, shift):
        return pl.pallas_call(
            functools.partial(_attn_kernel, rpb=rpb, fused=fused, row_max=row_max),
            grid=(b, dilation // rpb),
            in_specs=qkv_specs + [_resident(bias.shape), _resident(shift.shape)],
            out_specs=[
                pl.BlockSpec((None, length, rpb * GROUP_COLS), lambda i, m: (i, 0, m)),
                pl.BlockSpec((None, length, rpb * LANES), lambda i, m: (i, 0, m)),
            ],
            out_shape=[
                jax.ShapeDtypeStruct((b, length, dilation * GROUP_COLS), BF16),
                jax.ShapeDtypeStruct((b, length, dilation * LANES), F32),
            ],
            scratch_shapes=[pltpu.VMEM((GROUP_COLS // LANES, length // ATT_QBLOCK, LANES, ATT_QBLOCK), BF16)],
            compiler_params=_cparams(("parallel", "parallel")),
            name=f"attn_d{dilation}" + ("_rowmax" if row_max else ""),
        )(*([pv] * len(qkv_specs)), bias, shift)

    bias = _attn_bias()
    shift = jnp.full((1, LANES), score_bound, F32)
    o, lse = lax.cond(score_bound <= ATT_MAX_CONST_SHIFT,
                      lambda: call(False, bias - score_bound, shift),
                      lambda: call(True, bias, shift))
    return o.reshape(b * length, dilation * GROUP_COLS), lse.reshape(b * length, dilation * LANES)


def _merge_kernel(x_ref, a_ref, o0_ref, o1_ref, o2_ref, l0_ref, l1_ref, l2_ref, ga_ref, gb_ref,
                  watt_ref, wo_ref, ex_ref, y_ref, o_scr, l_scr):
    tm = x_ref.shape[0]
    nt = GROUP_COLS // LANES
    for gi, (og_ref, lg_ref) in enumerate(((o1_ref, l1_ref), (o2_ref, l2_ref))):
        dil = ATT_GROUPS[gi + 1][1]
        for m in range(dil):
            rows = pl.ds(m, tm // dil, stride=dil)
            l_scr[gi, rows, :] = lg_ref[:, m * LANES:(m + 1) * LANES]
            for t in range(nt):
                col = m * GROUP_COLS + t * LANES
                o_scr[gi, t, rows, :] = og_ref[:, col:col + LANES].astype(F32)
    lses = [l0_ref[...], l_scr[0], l_scr[1]]
    group_out = [o0_ref[...].astype(F32)] + [
        jnp.concatenate([o_scr[gi, t] for t in range(nt)], axis=1) for gi in range(2)]
    mx = jnp.maximum(jnp.maximum(lses[0], lses[1]), lses[2])
    es = [jnp.exp(l - mx) for l in lses]
    den = es[0] + es[1] + es[2]
    ex = ex_ref[...]
    o = None
    for e, og in zip(es, group_out):
        w = e / den
        hi = w.astype(BF16)
        lo = (w - hi.astype(F32)).astype(BF16)
        wx = jnp.dot(hi, ex, preferred_element_type=F32) + jnp.dot(lo, ex, preferred_element_type=F32)
        t = wx * og
        o = t if o is None else o + t
    b_out = jnp.dot(o.astype(BF16), watt_ref[...], preferred_element_type=F32)
    mix = (_sigmoid(ga_ref[...].astype(F32)) * a_ref[...].astype(F32)
           + _sigmoid(gb_ref[...].astype(F32)) * b_out)
    y_ref[...] = x_ref[...] + jnp.dot(mix.astype(BF16), wo_ref[...], preferred_element_type=F32)


def _merge(x2, a_out, outs, lses, proj2, watt, wo, ex):
    n = x2.shape[0]
    tm = TOKEN_TILE
    row = lambda cols, dil=1: pl.BlockSpec((tm // dil, cols * dil), lambda i: (i, 0))
    d1, d2 = ATT_GROUPS[1][1], ATT_GROUPS[2][1]
    return pl.pallas_call(
        _merge_kernel,
        grid=(n // tm,),
        in_specs=[
            row(D_MODEL), row(D_MODEL),
            row(GROUP_COLS), row(GROUP_COLS, d1), row(GROUP_COLS, d2),
            row(LANES), row(LANES, d1), row(LANES, d2),
            pl.BlockSpec((tm, D_MODEL), lambda i: (i, GATE_COL_BLOCK0)),
            pl.BlockSpec((tm, D_MODEL), lambda i: (i, GATE_COL_BLOCK0 + 1)),
            _resident((GROUP_COLS, D_MODEL)),
            _resident((D_MODEL, D_MODEL)),
            _resident((LANES, GROUP_COLS)),
        ],
        out_specs=row(D_MODEL),
        out_shape=jax.ShapeDtypeStruct((n, D_MODEL), F32),
        scratch_shapes=[
            pltpu.VMEM((2, GROUP_COLS // LANES, tm, LANES), F32),
            pltpu.VMEM((2, tm, LANES), F32),
        ],
        compiler_params=_cparams(("parallel",)),
        name="merge",
    )(x2, a_out, *outs, *lses, proj2, proj2, watt, wo, ex)


FFN_HALO = 16


def _ffn_kernel(x_ref, xp_ref, xn_ref, g2_ref, wup_ref, cw_ref, cb_ref, wdn_ref, y_ref, lhs_scr,
                *, tiles_per_seq):
    tm = x_ref.shape[0]
    ti = pl.program_id(0) % tiles_per_seq
    g2 = g2_ref[...]
    x = x_ref[...]
    keep_prev = jnp.where(ti == 0, 0.0, 1.0)
    keep_next = jnp.where(ti == tiles_per_seq - 1, 0.0, 1.0)
    xn = _rms_rows(x, g2).astype(BF16)
    lhs_scr[0:FFN_HALO, :] = (_rms_rows(xp_ref[...], g2) * keep_prev).astype(BF16)
    lhs_scr[FFN_HALO:FFN_HALO + tm, :] = xn
    lhs_scr[FFN_HALO + tm:, :] = (_rms_rows(xn_ref[...], g2) * keep_next).astype(BF16)
    acc = x
    for j in range(D_FF // CHUNK):
        sl = slice(j * CHUNK, (j + 1) * CHUNK)
        gext = jnp.dot(lhs_scr[...], wup_ref[:, sl], preferred_element_type=F32)
        rows = gext.shape[0]
        gc = cb_ref[:, sl]
        for k in range(FFN_CONV):
            shift = (FFN_CONV // 2 - k) % rows
            gk = gext if shift == 0 else pltpu.roll(gext, shift, 0)
            gc = gc + gk[FFN_HALO:FFN_HALO + tm, :] * cw_ref[k:k + 1, sl]
        val = jnp.dot(xn, wup_ref[:, D_FF + j * CHUNK:D_FF + (j + 1) * CHUNK], preferred_element_type=F32)
        h = (_gelu_tanh(gc) * val).astype(BF16)
        acc = acc + jnp.dot(h, wdn_ref[sl, :], preferred_element_type=F32)
    y_ref[...] = acc


def _ffn(x2, seq, g2, wup, cw, cb, wdn):
    n = x2.shape[0]
    tm = TOKEN_TILE
    hb = tm // FFN_HALO
    last = n // FFN_HALO - 1
    return pl.pallas_call(
        functools.partial(_ffn_kernel, tiles_per_seq=seq // tm),
        grid=(n // tm,),
        in_specs=[
            pl.BlockSpec((tm, D_MODEL), lambda i: (i, 0)),
            pl.BlockSpec((FFN_HALO, D_MODEL), lambda i: (jnp.maximum(i * hb - 1, 0), 0)),
            pl.BlockSpec((FFN_HALO, D_MODEL), lambda i: (jnp.minimum((i + 1) * hb, last), 0)),
            _resident((1, D_MODEL)),
            _resident((D_MODEL, 2 * D_FF)),
            _resident((FFN_CONV, D_FF)),
            _resident((1, D_FF)),
            _resident((D_FF, D_MODEL)),
        ],
        out_specs=pl.BlockSpec((tm, D_MODEL), lambda i: (i, 0)),
        out_shape=jax.ShapeDtypeStruct((n, D_MODEL), F32),
        scratch_shapes=[pltpu.VMEM((tm + 2 * FFN_HALO, D_MODEL), BF16)],
        compiler_params=_cparams(("parallel",)),
        name="convffn",
    )(x2, x2, x2, g2, wup, cw, cb, wdn)


def _rope_tables(seq):
    pos = jnp.arange(seq, dtype=F32)
    inv = ROPE_THETA ** (-jnp.arange(0, ROT_DIM, 2, dtype=F32) / ROT_DIM)
    ang = pos[:, None] * inv[None, :]
    cos, sin = jnp.cos(ang), jnp.sin(ang)
    half = ROT_DIM // 2
    pad = HEAD_DIM - ROT_DIM
    one = jnp.ones((seq, pad), F32)
    zero = jnp.zeros((seq, pad), F32)
    zh = jnp.zeros((seq, half), F32)
    per_head = lambda parts: jnp.tile(jnp.concatenate(parts, axis=1), (1, LANES // HEAD_DIM))
    ra = per_head([cos, cos, one])
    rm = per_head([-sin, zh, zero])
    rp = per_head([zh, sin, zero])
    return ra, rm, rp


def _block_diag_dense(w):
    nb, c, d = w.shape
    return jnp.einsum('ncd,nm->ncmd', w, jnp.eye(nb, dtype=w.dtype)).reshape(nb * c, nb * d)


def _prep_layer(p, seq):
    q = {}
    q['g1'] = p['norm1_g'].reshape(1, D_MODEL)
    q_lo = 2 * D_RNN
    seg = lambda part, g: (q_lo + part * ATT_COLS + g * GROUP_COLS, q_lo + part * ATT_COLS + (g + 1) * GROUP_COLS)
    qkv = lambda g: [seg(part, g) for part in range(3)]
    order = [(0, q_lo)] + qkv(0) + [(q_lo + 3 * ATT_COLS, IN_COLS)] + qkv(1) + qkv(2)
    w_in = p['w_in'].astype(BF16)
    q['w_in'] = jnp.concatenate([w_in[:, lo:hi] for lo, hi in order], axis=1)
    head = jnp.arange(MXU_DIM) // HEAD_DIM
    q['bd'] = (head[:, None] == head[None, :]).astype(BF16)
    q['qg'] = jnp.tile(p['q_norm_g'], (1, HEADS_PER_GROUP)) * (HEAD_DIM ** -0.5 * LOG2E)
    q['kg'] = jnp.tile(p['k_norm_g'], (1, HEADS_PER_GROUP))
    q['score_bound'] = (HEAD_DIM * ATT_BOUND_MARGIN) * (jnp.max(jnp.abs(q['qg']), axis=1)
                                                        * jnp.max(jnp.abs(q['kg']), axis=1))
    q['rope'] = _rope_tables(seq)
    q['lru_cw'] = p['lru_conv_w']
    q['lru_cb'] = p['lru_conv_b'].reshape(1, D_RNN)
    dense = jnp.stack([jnp.stack([_block_diag_dense(p['lru_wa'][d]), _block_diag_dense(p['lru_wx'][d])])
                       for d in range(2)])
    dense = (0.5 * dense).astype(BF16)
    chunks = []
    for c in range(LRU_NCHUNK):
        k0 = min(max(2 * c - 1, 0), D_RNN // LANES - LRU_KSLABS) * LANES
        chunks.append(dense[:, :, k0:k0 + LRU_KSLABS * LANES, c * LRU_CHUNK:(c + 1) * LRU_CHUNK])
    q['lru_wg'] = jnp.stack(chunks)
    gb = 0.5 * jnp.stack([p['lru_ba'][0], p['lru_bx'][0], p['lru_ba'][1], p['lru_bx'][1]])
    q['lru_gb'] = gb.reshape(4, LRU_NCHUNK, LRU_CHUNK).transpose(1, 0, 2)
    q['lru_lam'] = p['lru_lambda'].reshape(2, LRU_NCHUNK, LRU_CHUNK).transpose(1, 0, 2)
    q['lru_wout'] = p['w_lru_out'].astype(BF16)
    q['watt'] = p['w_att_out'].astype(BF16)
    q['wo'] = p['w_o'].astype(BF16)
    lse_lane = jnp.arange(LANES)
    lane_head = jnp.where(lse_lane % HEAD_DIM < GROUP_COLS // LANES,
                          2 * (lse_lane % HEAD_DIM) + lse_lane // HEAD_DIM, -1)
    q['expand'] = (lane_head[:, None] == jnp.arange(GROUP_COLS)[None, :] // HEAD_DIM).astype(BF16)
    q['g2'] = p['norm2_g'].reshape(1, D_MODEL)
    q['wup'] = p['w_up'].astype(BF16)
    q['ffn_cw'] = p['ffn_conv_w']
    q['ffn_cb'] = p['ffn_conv_b'].reshape(1, D_FF)
    q['wdn'] = p['w_down'].astype(BF16)
    return q


def _layer(x, q):
    b, seq, _ = x.shape
    x2 = x.reshape(b * seq, D_MODEL)
    main, qkv1, qkv2 = _inproj(x2, seq, q['g1'], q['w_in'], q['bd'], q['qg'], q['kg'], *q['rope'])
    a_out = _lru(main.reshape(b, seq, MAIN_COLS), q['lru_cw'], q['lru_cb'], q['lru_wg'], q['lru_gb'],
                 q['lru_lam'], q['lru_wout'])
    outs, lses = [], []
    for g, (qkv, (window, dilation)) in enumerate(zip((main, qkv1, qkv2), ATT_GROUPS)):
        o, lse = _attention_group(qkv, b, seq, window, dilation, q['score_bound'][g])
        outs.append(o)
        lses.append(lse)
    x1 = _merge(x2, a_out.reshape(b * seq, D_MODEL), outs, lses, main, q['watt'], q['wo'], q['expand'])
    y = _ffn(x1, seq, q['g2'], q['wup'], q['ffn_cw'], q['ffn_cb'], q['wdn'])
    return y.reshape(b, seq, D_MODEL)


def kernel(x_prompt, x_sample, norm1_g, w_in, lru_conv_w, lru_conv_b, lru_wa, lru_ba, lru_wx, lru_bx,
           lru_lambda, w_lru_out, q_norm_g, k_norm_g, w_att_out, w_o, norm2_g, w_up, ffn_conv_w,
           ffn_conv_b, w_down):
    params = dict(norm1_g=norm1_g, w_in=w_in, lru_conv_w=lru_conv_w, lru_conv_b=lru_conv_b,
                  lru_wa=lru_wa, lru_ba=lru_ba, lru_wx=lru_wx, lru_bx=lru_bx, lru_lambda=lru_lambda,
                  w_lru_out=w_lru_out, q_norm_g=q_norm_g, k_norm_g=k_norm_g, w_att_out=w_att_out,
                  w_o=w_o, norm2_g=norm2_g, w_up=w_up, ffn_conv_w=ffn_conv_w, ffn_conv_b=ffn_conv_b,
                  w_down=w_down)
    depth = norm1_g.shape[0]
    seq = x_prompt.shape[1]
    assert x_sample.shape[1] == seq
    layers = [_prep_layer({name: p[l] for name, p in params.items()}, seq) for l in range(depth)]
    ys = []
    for x in (x_prompt, x_sample):
        for q in layers:
            x = _layer(x, q)
        ys.append(x)
    return tuple(ys)
```

```python
import functools

import jax
import jax.numpy as jnp
from jax import lax
from jax.experimental import pallas as pl
from jax.experimental.pallas import tpu as pltpu

F32 = jnp.float32
BF16 = jnp.bfloat16

D_MODEL = 1024
D_RNN = 1280
LRU_BLOCKS = 16
LRU_BW = D_RNN // LRU_BLOCKS
LRU_C = 8.0
LRU_CONV = 4
ATT_GROUPS = ((128, 1), (512, 4), (2048, 16))
N_GROUPS = len(ATT_GROUPS)
HEADS_PER_GROUP = 8
HEAD_DIM = 64
GROUP_COLS = HEADS_PER_GROUP * HEAD_DIM
ATT_COLS = N_GROUPS * GROUP_COLS
ROT_DIM = HEAD_DIM // 4
ROPE_THETA = 500000.0
D_FF = 3 * D_MODEL
FFN_CONV = 3
EPS = 1e-6
NEG = -1e30
LOG2E = 1.4426950408889634
LN2 = 0.6931471805599453
IN_COLS = 2 * D_RNN + 3 * ATT_COLS + 2 * D_MODEL

LANES = 128
MXU_DIM = 256
VMEM_LIMIT_BYTES = 56 * 1024 * 1024

CHUNK = 512
MAIN_COLS = 2 * D_RNN + 3 * GROUP_COLS + 2 * D_MODEL
MAIN_CHUNKS = MAIN_COLS // CHUNK
QKV0_CHUNK0 = 2 * D_RNN // CHUNK
GATE_COL_BLOCK0 = (2 * D_RNN + 3 * GROUP_COLS) // D_MODEL

TOKEN_TILE = 512
ATT_QBLOCK = 128
ATT_HALF = 64
ATT_MIN_STEP_TOKENS = 512
ATT_MAX_CONST_SHIFT = 60.0
ATT_BOUND_MARGIN = 1.02

LRU_CHUNK = 256
LRU_NCHUNK = D_RNN // LRU_CHUNK
LRU_KSLABS = 4
LRU_TBLOCK = 256
LRU_SUBLEN = 128
LRU_NSUB = 16
LRU_PITCH = LRU_SUBLEN + 8
LRU_SUBS_PER_BLOCK = LRU_TBLOCK // LRU_SUBLEN


def _cparams(sem):
    return pltpu.CompilerParams(dimension_semantics=sem, vmem_limit_bytes=VMEM_LIMIT_BYTES)


def _resident(shape):
    return pl.BlockSpec(shape, lambda *_: (0,) * len(shape), pipeline_mode=pl.Buffered(1))


def _sigmoid(x):
    return 0.5 * jnp.tanh(0.5 * x) + 0.5


def _gelu_tanh(x):
    return 0.5 * x * (1.0 + jnp.tanh(0.7978845608028654 * (x + 0.044715 * (x * x * x))))


def _rms_rows(x, g):
    ms = jnp.mean(x * x, axis=-1, keepdims=True)
    return x * lax.rsqrt(ms + EPS) * g


def _chunk_role(j):
    if QKV0_CHUNK0 <= j < QKV0_CHUNK0 + 3:
        return 0, j - QKV0_CHUNK0
    if j >= MAIN_CHUNKS:
        return 1 + (j - MAIN_CHUNKS) // 3, (j - MAIN_CHUNKS) % 3
    return None


def _inproj_kernel(x_ref, g1_ref, w_ref, bd_ref, qg_ref, kg_ref, ra_ref, rm_ref, rp_ref,
                   o_ref, d1_ref, d2_ref, tile_scr):
    tm = x_ref.shape[0]
    xn = _rms_rows(x_ref[...], g1_ref[...]).astype(BF16)
    ra = ra_ref[...]
    rm = rm_ref[...]
    rp = rp_ref[...]
    nchunks = IN_COLS // CHUNK
    order = sorted(range(nchunks), key=lambda j: (_chunk_role(j) is None, -j))
    for j in order:
        c0 = j * CHUNK
        acc = jnp.dot(xn, w_ref[:, c0:c0 + CHUNK], preferred_element_type=F32)
        role = _chunk_role(j)
        if role is None:
            o_ref[:, c0:c0 + CHUNK] = acc.astype(BF16)
            continue
        g, part = role
        tiles = []
        if part < 2:
            gain = (qg_ref if part == 0 else kg_ref)[g:g + 1, :]
            sq = (acc * acc).astype(BF16)
            bd = bd_ref[...]
            for t in range(CHUNK // MXU_DIM):
                sl = slice(t * MXU_DIM, (t + 1) * MXU_DIM)
                ss = jnp.dot(sq[:, sl], bd, preferred_element_type=F32)
                y = acc[:, sl] * lax.rsqrt(ss * (1.0 / HEAD_DIM) + EPS) * gain[:, sl]
                for u in range(MXU_DIM // LANES):
                    yt = y[:, u * LANES:(u + 1) * LANES]
                    tiles.append(yt * ra + pltpu.roll(yt, LANES - ROT_DIM // 2, 1) * rm
                                 + pltpu.roll(yt, ROT_DIM // 2, 1) * rp)
        else:
            tiles = [acc[:, t * LANES:(t + 1) * LANES] for t in range(CHUNK // LANES)]
        if g == 0:
            for t, val in enumerate(tiles):
                o_ref[:, c0 + t * LANES:c0 + (t + 1) * LANES] = val.astype(BF16)
        else:
            dil = ATT_GROUPS[g][1]
            dst = d1_ref if g == 1 else d2_ref
            for t, val in enumerate(tiles):
                tile_scr[t] = val
            for m in range(dil):
                for t in range(CHUNK // LANES):
                    col = (m * 3 + part) * GROUP_COLS + t * LANES
                    dst[:, col:col + LANES] = tile_scr[t, pl.ds(m, tm // dil, stride=dil), :].astype(BF16)


def _inproj(x2, seq, g1, w_in, bd, qg, kg, ra, rm, rp):
    n = x2.shape[0]
    tm = TOKEN_TILE
    tiles_per_seq = seq // tm
    rope_spec = pl.BlockSpec((tm, LANES), lambda i: (i % tiles_per_seq, 0))
    d1, d2 = ATT_GROUPS[1][1], ATT_GROUPS[2][1]
    return pl.pallas_call(
        _inproj_kernel,
        grid=(n // tm,),
        in_specs=[
            pl.BlockSpec((tm, D_MODEL), lambda i: (i, 0)),
            _resident((1, D_MODEL)),
            _resident((D_MODEL, IN_COLS)),
            _resident((MXU_DIM, MXU_DIM)),
            _resident((N_GROUPS, GROUP_COLS)),
            _resident((N_GROUPS, GROUP_COLS)),
            rope_spec, rope_spec, rope_spec,
        ],
        out_specs=[
            pl.BlockSpec((tm, MAIN_COLS), lambda i: (i, 0)),
            pl.BlockSpec((tm // d1, d1 * 3 * GROUP_COLS), lambda i: (i, 0)),
            pl.BlockSpec((tm // d2, d2 * 3 * GROUP_COLS), lambda i: (i, 0)),
        ],
        out_shape=[
            jax.ShapeDtypeStruct((n, MAIN_COLS), BF16),
            jax.ShapeDtypeStruct((n // d1, d1 * 3 * GROUP_COLS), BF16),
            jax.ShapeDtypeStruct((n // d2, d2 * 3 * GROUP_COLS), BF16),
        ],
        scratch_shapes=[pltpu.VMEM((CHUNK // LANES, tm, LANES), F32)],
        compiler_params=_cparams(("parallel",)),
        name="inproj",
    )(x2, g1, w_in, bd, qg, kg, ra, rm, rp)


def _lru_kslab0(c):
    return jnp.minimum(jnp.maximum(2 * c - 1, 0), D_RNN // LANES - LRU_KSLABS)


LRU_CONV_ROWS = 128


def _lru_kernel(x_ref, gate_ref, cw_ref, cb_ref, wg_ref, gb_ref, lam_ref, wout_ref, o_ref,
                xc_scr, acc_scr, a_scr, u_scr, h_scr, pad_scr):
    c = pl.program_id(1)
    seq = x_ref.shape[0]
    nsl = LRU_CHUNK // LANES
    chains = [(d, i) for d in range(2) for i in range(nsl)]

    @pl.when(c == 0)
    def _conv():
        pad_scr[0:8, :] = jnp.zeros((8, LANES), F32)
        pad_scr[seq + 8:seq + 16, :] = jnp.zeros((8, LANES), F32)
        for j in range(D_RNN // LANES):
            sl = slice(j * LANES, (j + 1) * LANES)
            pad_scr[8:seq + 8, :] = x_ref[:, sl].astype(F32)
            bias = cb_ref[:, sl]
            taps = [cw_ref[k:k + 1, sl] for k in range(LRU_CONV)]

            def conv_body(rb, carry):
                r0 = pl.multiple_of(rb * LRU_CONV_ROWS, LRU_CONV_ROWS)
                xc = bias
                for k in range(LRU_CONV):
                    off = 8 + k - LRU_CONV // 2
                    xc = xc + pad_scr[pl.ds(r0 + off, LRU_CONV_ROWS), :] * taps[k]
                xc_scr[j, pl.ds(r0, LRU_CONV_ROWS), :] = xc
                return carry

            lax.fori_loop(0, seq // LRU_CONV_ROWS, conv_body, 0)

    k0 = _lru_kslab0(c)
    lam = lam_ref[...]
    half_l2 = (-0.5 * LRU_C * LOG2E) * (jnp.maximum(-lam, 0.0) + jnp.log1p(jnp.exp(-jnp.abs(lam))))

    def pitched_rows(s, sb):
        return pl.ds(pl.multiple_of((s * LRU_SUBS_PER_BLOCK + sb) * LRU_PITCH, 8), LRU_SUBLEN)

    def gates_body(s, carry):
        r0 = pl.multiple_of(s * LRU_TBLOCK, LRU_TBLOCK)
        lhs =jnp.concatenate([xc_scr[k0 + i, pl.ds(r0, LRU_TBLOCK), :] for i in range(LRU_KSLABS)],
                              axis=1).astype(BF16)
        half_xc = 0.5 * jnp.concatenate([xc_scr[nsl * c + i, pl.ds(r0, LRU_TBLOCK), :] for i in range(nsl)],
                                        axis=1)
        for d in range(2):
            za = jnp.dot(lhs, wg_ref[d, 0], preferred_element_type=F32) + gb_ref[2 * d:2 * d + 1, :]
            zx = jnp.dot(lhs, wg_ref[d, 1], preferred_element_type=F32) + gb_ref[2 * d + 1:2 * d + 2, :]
            hl = half_l2[d:d + 1, :]
            a = jnp.exp2(jnp.tanh(za) * hl + hl)
            t = 1.0 - a * a
            root = jnp.where(t > 0.0, t * lax.rsqrt(t), 0.0)
            u = root * ((jnp.tanh(zx) + 1.0) * half_xc)
            for i in range(nsl):
                for sb in range(LRU_SUBS_PER_BLOCK):
                    rows = slice(sb * LRU_SUBLEN, (sb + 1) * LRU_SUBLEN)
                    a_scr[d, i, pitched_rows(s, sb), :] = a[rows, i * LANES:(i + 1) * LANES]
                    u_scr[d, i, pitched_rows(s, sb), :] = u[rows, i * LANES:(i + 1) * LANES]
        return carry

    lax.fori_loop(0, seq // LRU_TBLOCK, gates_body, 0, unroll=2)

    def step_rows(d, t):
        r = t if d == 0 else LRU_SUBLEN - 1 - t
        return pl.ds(r, LRU_NSUB, stride=LRU_PITCH)

    def ends_body(t, carry):
        out = []
        for n, (d, i) in enumerate(chains):
            av = a_scr[d, i, step_rows(d, t), :]
            out += [av * carry[2 * n] + u_scr[d, i, step_rows(d, t), :], av * carry[2 * n + 1]]
        return tuple(out)

    init = (jnp.zeros((LRU_NSUB, LANES), F32), jnp.ones((LRU_NSUB, LANES), F32)) * len(chains)
    ends = lax.fori_loop(0, LRU_SUBLEN, ends_body, init, unroll=8)

    sub = lax.broadcasted_iota(jnp.int32, (LRU_NSUB, LANES), 0)
    starts = []
    for n, (d, i) in enumerate(chains):
        h_end, cum_end = ends[2 * n], ends[2 * n + 1]
        cin = jnp.zeros((LRU_NSUB, LANES), F32)
        for step in range(LRU_NSUB - 1):
            nxt = h_end + cum_end * cin
            if d == 0:
                cin = jnp.where(sub == step + 1, pltpu.roll(nxt, 1, 0), cin)
            else:
                cin = jnp.where(sub == LRU_NSUB - 2 - step, pltpu.roll(nxt, LRU_NSUB - 1, 0), cin)
        starts.append(cin)

    def scan_body(t, carry):
        out = []
        for n, (d, i) in enumerate(chains):
            h = a_scr[d, i, step_rows(d, t), :] * carry[n] + u_scr[d, i, step_rows(d, t), :]
            h_scr[d, i, step_rows(d, t), :] = h
            out.append(h)
        return tuple(out)

    lax.fori_loop(0, LRU_SUBLEN, scan_body, tuple(starts), unroll=8)

    def out_body(first, s, carry):
        r0 = pl.multiple_of(s * LRU_TBLOCK, LRU_TBLOCK)
        hsum = jnp.concatenate(
            [jnp.concatenate([h_scr[0, i, pitched_rows(s, sb), :] + h_scr[1, i, pitched_rows(s, sb), :]
                              for sb in range(LRU_SUBS_PER_BLOCK)], axis=0) for i in range(nsl)], axis=1)
        y = (_gelu_tanh(gate_ref[pl.ds(r0, LRU_TBLOCK), :].astype(F32)) * hsum).astype(BF16)
        part = jnp.dot(y, wout_ref[...], preferred_element_type=F32)
        if first:
            acc_scr[pl.ds(r0, LRU_TBLOCK), :] = part
        else:
            acc_scr[pl.ds(r0, LRU_TBLOCK), :] += part
        return carry

    @pl.when(c == 0)
    def _out_first():
        lax.fori_loop(0, seq // LRU_TBLOCK, functools.partial(out_body, True), 0)

    @pl.when(c > 0)
    def _out_rest():
        lax.fori_loop(0, seq // LRU_TBLOCK, functools.partial(out_body, False), 0)

    @pl.when(c == LRU_NCHUNK - 1)
    def _emit():
        o_ref[...] = acc_scr[...].astype(BF16)


def _lru(proj3, cw, cb, wg, gb, lam, wout):
    b, seq, _ = proj3.shape
    assert seq == LRU_NSUB * LRU_SUBLEN and seq % LRU_TBLOCK == 0
    nsl = LRU_CHUNK // LANES
    gate_blk0 = D_RNN // LRU_CHUNK
    return pl.pallas_call(
        _lru_kernel,
        grid=(b, LRU_NCHUNK),
        in_specs=[
            pl.BlockSpec((None, seq, D_RNN), lambda i, c: (i, 0, 0), pipeline_mode=pl.Buffered(1)),
            pl.BlockSpec((None, seq, LRU_CHUNK), lambda i, c: (i, 0, gate_blk0 + c)),
            _resident((LRU_CONV, D_RNN)),
            _resident((1, D_RNN)),
            pl.BlockSpec((None, 2, 2, LRU_KSLABS * LANES, LRU_CHUNK), lambda i, c: (c, 0, 0, 0, 0)),
            pl.BlockSpec((None, 4, LRU_CHUNK), lambda i, c: (c, 0, 0)),
            pl.BlockSpec((None, 2, LRU_CHUNK), lambda i, c: (c, 0, 0)),
            pl.BlockSpec((LRU_CHUNK, D_MODEL), lambda i, c: (c, 0)),
        ],
        out_specs=pl.BlockSpec((None, seq, D_MODEL), lambda i, c: (i, 0, 0)),
        out_shape=jax.ShapeDtypeStruct((b, seq, D_MODEL), BF16),
        scratch_shapes=[
            pltpu.VMEM((D_RNN // LANES, seq, LANES), F32),
            pltpu.VMEM((seq, D_MODEL), F32),
            pltpu.VMEM((2, nsl, LRU_NSUB * LRU_PITCH, LANES), F32),
            pltpu.VMEM((2, nsl, LRU_NSUB * LRU_PITCH, LANES), F32),
            pltpu.VMEM((2, nsl, LRU_NSUB * LRU_PITCH, LANES), F32),
            pltpu.VMEM((seq + 16, LANES), F32),
        ],
        compiler_params=_cparams(("parallel", "arbitrary")),
        name="rglru",
    )(proj3, proj3, cw, cb, wg, gb, lam, wout)


def _attn_kernel(*refs, rpb, fused, row_max):
    n_in = 1 if fused else 3
    qkv_refs = refs[:n_in]
    bias_ref, shift_ref, o_ref, lse_ref, kt_scr = refs[n_in:]
    length = o_ref.shape[0]
    nkb = length // ATT_QBLOCK
    lane = lax.broadcasted_iota(jnp.int32, (ATT_QBLOCK, LANES), 1)
    lo_half = lane < HEAD_DIM
    lo_keys = lax.broadcasted_iota(jnp.int32, (2 * ATT_QBLOCK, LANES), 1) < HEAD_DIM
    npairs = GROUP_COLS // LANES

    def scores_block(qp_of, kt_of, vw_of, bias):
        outs = []
        lse_tile = jnp.zeros((ATT_QBLOCK, LANES), F32)
        for p in range(npairs):
            qp, kt, vw = qp_of(p), kt_of(p), vw_of(p)
            full, shifts = [], []
            for hh in range(LANES // HEAD_DIM):
                own = lo_half if hh == 0 else jnp.logical_not(lo_half)
                qm = jnp.where(own, qp, jnp.zeros_like(qp))
                s = jnp.dot(qm, kt, preferred_element_type=F32) + bias
                if row_max:
                    m = jnp.max(s, axis=-1, keepdims=True)
                    s = s - m
                    shifts.append(m)
                e = jnp.exp2(s).astype(BF16)
                v1 = jnp.where(lo_keys if hh == 0 else jnp.logical_not(lo_keys), vw, jnp.ones_like(vw))
                full.append(jnp.dot(e, v1, preferred_element_type=F32))
            numer = jnp.where(lo_half, full[0], full[1])
            denom = pltpu.roll(jnp.where(lo_half, full[1], full[0]), HEAD_DIM, 1)
            outs.append(numer * (1.0 / denom))
            shift = jnp.where(lo_half, shifts[0], shifts[1]) if row_max else shift_ref[...]
            lse = LN2 * (shift + jnp.log2(denom))
            lse_tile = jnp.where(jnp.logical_or(lane == p, lane == HEAD_DIM + p), lse, lse_tile)
        return outs, lse_tile

    for r in range(rpb):
        def cols(part, p, r=r):
            base = ((r * 3 + part) * GROUP_COLS if fused else 0) + p * LANES
            return slice(base, base + LANES)

        q_ref, k_ref, v_ref = (qkv_refs[0],) * 3 if fused else qkv_refs

        def transpose_body(kb, carry):
            k0 = pl.multiple_of(kb * ATT_QBLOCK, ATT_QBLOCK)
            for p in range(npairs):
                kt_scr[p, kb] = k_ref[pl.ds(k0, ATT_QBLOCK), cols(1, p)].T
            return carry

        lax.fori_loop(0, nkb, transpose_body, 0, unroll=min(nkb, 4))

        def store(rows_list, outs, lse_tile, r=r):
            for dst_rows, src in rows_list:
                for p in range(npairs):
                    c0 = r * GROUP_COLS + p * LANES
                    o_ref[dst_rows, c0:c0 + LANES] = outs[p][src].astype(BF16)
                lse_ref[dst_rows, r * LANES:(r + 1) * LANES] = lse_tile[src]

        def interior(i, carry):
            k0 = pl.multiple_of(i * ATT_QBLOCK, ATT_QBLOCK)
            q0 = pl.multiple_of(k0 + ATT_HALF, ATT_HALF)
            outs, lse_tile = scores_block(
                lambda p: q_ref[pl.ds(q0, ATT_QBLOCK), cols(0, p)],
                lambda p: jnp.concatenate([kt_scr[p, i], kt_scr[p, i + 1]], axis=1),
                lambda p: v_ref[pl.ds(k0, 2 * ATT_QBLOCK), cols(2, p)],
                bias_ref[0])
            store([(pl.ds(q0, ATT_QBLOCK), slice(None))], outs, lse_tile)
            return carry

        if nkb > 1:
            lax.fori_loop(0, nkb - 1, interior, 0, unroll=3)

        head_rows, tail_rows = slice(0, ATT_HALF), slice(length - ATT_HALF, length)
        tail_keys = slice(length - ATT_QBLOCK, length)
        outs, lse_tile = scores_block(
            lambda p: jnp.concatenate([q_ref[head_rows, cols(0, p)], q_ref[tail_rows, cols(0, p)]], axis=0),
            lambda p: jnp.concatenate([kt_scr[p, 0], kt_scr[p, nkb - 1]], axis=1),
            lambda p: jnp.concatenate([v_ref[0:ATT_QBLOCK, cols(2, p)], v_ref[tail_keys, cols(2, p)]], axis=0),
            bias_ref[1])
        store([(head_rows, slice(0, ATT_HALF)), (tail_rows, slice(ATT_HALF, ATT_QBLOCK))], outs, lse_tile)


def _attn_bias():
    i = jnp.arange(ATT_QBLOCK)[:, None]
    j = jnp.arange(2 * ATT_QBLOCK)[None, :]
    interior = jnp.abs(i + ATT_HALF - j) <= ATT_HALF
    first = (i < ATT_HALF) & (j < ATT_QBLOCK) & (jnp.abs(i - j) <= ATT_HALF)
    last = (i >= ATT_HALF) & (j >= ATT_QBLOCK) & (jnp.abs(i - (j - ATT_QBLOCK)) <= ATT_HALF)
    return jnp.stack([jnp.where(interior, 0.0, NEG), jnp.where(first | last, 0.0, NEG)]).astype(F32)


def _attention_group(qkv, b, seq, window, dilation, score_bound):
    length = seq // dilation
    assert (window // 2) // dilation == ATT_HALF and length % ATT_QBLOCK == 0
    pv = qkv.reshape(b, length, qkv.shape[1])
    fused = dilation > 1
    rpb = min(dilation, max(1, ATT_MIN_STEP_TOKENS // length))
    if fused:
        qkv_specs = [pl.BlockSpec((None, length, rpb * 3 * GROUP_COLS), lambda i, m: (i, 0, m))]
    else:
        qkv_specs = [pl.BlockSpec((None, length, GROUP_COLS), lambda i, m, part=part: (i, 0, QKV0_CHUNK0 + part))
                     for part in range(3)]

    def call(row_max, bias, shift):
        return pl.pallas_call(
            functools.partial(_attn_kernel, rpb=rpb, fused=fused, row_max=row_max),
            grid=(b, dilation // rpb),
            in_specs=qkv_specs + [_resident(bias.shape), _resident(shift.shape)],
            out_specs=[
                pl.BlockSpec((None, length, rpb * GROUP_COLS), lambda i, m: (i, 0, m)),
                pl.BlockSpec((None, length, rpb * LANES), lambda i, m: (i, 0, m)),
            ],
            out_shape=[
                jax.ShapeDtypeStruct((b, length, dilation * GROUP_COLS), BF16),
                jax.ShapeDtypeStruct((b, length, dilation * LANES), F32),
            ],
            scratch_shapes=[pltpu.VMEM((GROUP_COLS // LANES, length // ATT_QBLOCK, LANES, ATT_QBLOCK), BF16)],
            compiler_params=_cparams(("parallel", "parallel")),
            name=f"attn_d{dilation}" + ("_rowmax" if row_max else ""),
        )(*([pv] * len(qkv_specs)), bias, shift)

    bias = _attn_bias()
    shift = jnp.full((1, LANES), score_bound, F32)
    o, lse = lax.cond(score_bound <= ATT_MAX_CONST_SHIFT,
                      lambda: call(False, bias - score_bound, shift),
                      lambda: call(True, bias, shift))
    return o.reshape(b * length, dilation * GROUP_COLS), lse.reshape(b * length, dilation * LANES)


def _merge_kernel(x_ref, a_ref, o0_ref, o1_ref, o2_ref, l0_ref, l1_ref, l2_ref, ga_ref, gb_ref,
                  watt_ref, wo_ref, ex_ref, y_ref, o_scr, l_scr):
    tm = x_ref.shape[0]
    nt = GROUP_COLS // LANES
    for gi, (og_ref, lg_ref) in enumerate(((o1_ref, l1_ref), (o2_ref, l2_ref))):
        dil = ATT_GROUPS[gi + 1][1]
        for m in range(dil):
            rows = pl.ds(m, tm // dil, stride=dil)
            l_scr[gi, rows, :] = lg_ref[:, m * LANES:(m + 1) * LANES]
            for t in range(nt):
                col = m * GROUP_COLS + t * LANES
                o_scr[gi, t, rows, :] = og_ref[:, col:col + LANES].astype(F32)
    lses = [l0_ref[...], l_scr[0], l_scr[1]]
    group_out = [o0_ref[...].astype(F32)] + [
        jnp.concatenate([o_scr[gi, t] for t in range(nt)], axis=1) for gi in range(2)]
    mx = jnp.maximum(jnp.maximum(lses[0], lses[1]), lses[2])
    es = [jnp.exp(l - mx) for l in lses]
    den = es[0] + es[1] + es[2]
    ex = ex_ref[...]
    o = None
    for e, og in zip(es, group_out):
        w = e / den
        hi = w.astype(BF16)
        lo = (w - hi.astype(F32)).astype(BF16)
        wx = jnp.dot(hi, ex, preferred_element_type=F32) + jnp.dot(lo, ex, preferred_element_type=F32)
        t = wx * og
        o = t if o is None else o + t
    b_out = jnp.dot(o.astype(BF16), watt_ref[...], preferred_element_type=F32)
    mix = (_sigmoid(ga_ref[...].astype(F32)) * a_ref[...].astype(F32)
           + _sigmoid(gb_ref[...].astype(F32)) * b_out)
    y_ref[...] = x_ref[...] + jnp.dot(mix.astype(BF16), wo_ref[...], preferred_element_type=F32)


def _merge(x2, a_out, outs, lses, proj2, watt, wo, ex):
    n = x2.shape[0]
    tm = TOKEN_TILE
    row = lambda cols, dil=1: pl.BlockSpec((tm // dil, cols * dil), lambda i: (i, 0))
    d1, d2 = ATT_GROUPS[1][1], ATT_GROUPS[2][1]
    return pl.pallas_call(
        _merge_kernel,
        grid=(n // tm,),
        in_specs=[
            row(D_MODEL), row(D_MODEL),
            row(GROUP_COLS), row(GROUP_COLS, d1), row(GROUP_COLS, d2),
            row(LANES), row(LANES, d1), row(LANES, d2),
            pl.BlockSpec((tm, D_MODEL), lambda i: (i, GATE_COL_BLOCK0)),
            pl.BlockSpec((tm, D_MODEL), lambda i: (i, GATE_COL_BLOCK0 + 1)),
            _resident((GROUP_COLS, D_MODEL)),
            _resident((D_MODEL, D_MODEL)),
            _resident((LANES, GROUP_COLS)),
        ],
        out_specs=row(D_MODEL),
        out_shape=jax.ShapeDtypeStruct((n, D_MODEL), F32),
        scratch_shapes=[
            pltpu.VMEM((2, GROUP_COLS // LANES, tm, LANES), F32),
            pltpu.VMEM((2, tm, LANES), F32),
        ],
        compiler_params=_cparams(("parallel",)),
        name="merge",
    )(x2, a_out, *outs, *lses, proj2, proj2, watt, wo, ex)


FFN_HALO = 16
FFN_TOKEN_TILE = 1024
FFN_CHUNK = 1024


def _ffn_kernel(x_ref, xp_ref, xn_ref, g2_ref, wup_ref, cw_ref, cb_ref, wdn_ref, y_ref, lhs_scr,
                *, tiles_per_seq):
    tm = x_ref.shape[0]
    ti = pl.program_id(0) % tiles_per_seq
    g2 = g2_ref[...]
    x = x_ref[...]
    keep_prev = jnp.where(ti == 0, 0.0, 1.0)
    keep_next = jnp.where(ti == tiles_per_seq - 1, 0.0, 1.0)
    xn = _rms_rows(x, g2).astype(BF16)
    lhs_scr[0:FFN_HALO, :] = (_rms_rows(xp_ref[...], g2) * keep_prev).astype(BF16)
    lhs_scr[FFN_HALO:FFN_HALO + tm, :] = xn
    lhs_scr[FFN_HALO + tm:, :] = (_rms_rows(xn_ref[...], g2) * keep_next).astype(BF16)
    acc = x
    for j in range(D_FF // FFN_CHUNK):
        sl = slice(j * FFN_CHUNK, (j + 1) * FFN_CHUNK)
        gext =jnp.dot(lhs_scr[...], wup_ref[:, sl], preferred_element_type=F32)
        rows = gext.shape[0]
        gc = cb_ref[:, sl]
        for k in range(FFN_CONV):
            shift = (FFN_CONV // 2 - k) % rows
            gk = gext if shift == 0 else pltpu.roll(gext, shift, 0)
            gc = gc + gk[FFN_HALO:FFN_HALO + tm, :] * cw_ref[k:k + 1, sl]
        val = jnp.dot(xn, wup_ref[:, D_FF + j * FFN_CHUNK:D_FF + (j + 1) * FFN_CHUNK],
                      preferred_element_type=F32)
        h = (_gelu_tanh(gc) * val).astype(BF16)
        acc = acc + jnp.dot(h, wdn_ref[sl, :], preferred_element_type=F32)
    y_ref[...] = acc


def _ffn(x2, seq, g2, wup, cw, cb, wdn):
    n = x2.shape[0]
    tm = FFN_TOKEN_TILE
    hb = tm // FFN_HALO
    last = n // FFN_HALO - 1
    return pl.pallas_call(
        functools.partial(_ffn_kernel, tiles_per_seq=seq // tm),
        grid=(n // tm,),
        in_specs=[
            pl.BlockSpec((tm, D_MODEL), lambda i: (i, 0)),
            pl.BlockSpec((FFN_HALO, D_MODEL), lambda i: (jnp.maximum(i * hb - 1, 0), 0)),
            pl.BlockSpec((FFN_HALO, D_MODEL), lambda i: (jnp.minimum((i + 1) * hb, last), 0)),
            _resident((1, D_MODEL)),
            _resident((D_MODEL, 2 * D_FF)),
            _resident((FFN_CONV, D_FF)),
            _resident((1, D_FF)),
            _resident((D_FF, D_MODEL)),
        ],
        out_specs=pl.BlockSpec((tm, D_MODEL), lambda i: (i, 0)),
        out_shape=jax.ShapeDtypeStruct((n, D_MODEL), F32),
        scratch_shapes=[pltpu.VMEM((tm + 2 * FFN_HALO, D_MODEL), BF16)],
        compiler_params=_cparams(("parallel",)),
        name="convffn",
    )(x2, x2, x2, g2, wup, cw, cb, wdn)


def _rope_tables(seq):
    pos = jnp.arange(seq, dtype=F32)
    inv = ROPE_THETA ** (-jnp.arange(0, ROT_DIM, 2, dtype=F32) / ROT_DIM)
    ang = pos[:, None] * inv[None, :]
    cos, sin = jnp.cos(ang), jnp.sin(ang)
    half = ROT_DIM // 2
    pad = HEAD_DIM - ROT_DIM
    one = jnp.ones((seq, pad), F32)
    zero = jnp.zeros((seq, pad), F32)
    zh = jnp.zeros((seq, half), F32)
    per_head = lambda parts: jnp.tile(jnp.concatenate(parts, axis=1), (1, LANES // HEAD_DIM))
    ra = per_head([cos, cos, one])
    rm = per_head([-sin, zh, zero])
    rp = per_head([zh, sin, zero])
    return ra, rm, rp


def _block_diag_dense(w):
    nb, c, d = w.shape
    return jnp.einsum('ncd,nm->ncmd', w, jnp.eye(nb, dtype=w.dtype)).reshape(nb * c, nb * d)


def _prep_layer(p, seq):
    q = {}
    q['g1'] = p['norm1_g'].reshape(1, D_MODEL)
    q_lo = 2 * D_RNN
    seg = lambda part, g: (q_lo + part * ATT_COLS + g * GROUP_COLS, q_lo + part * ATT_COLS + (g + 1) * GROUP_COLS)
    qkv = lambda g: [seg(part, g) for part in range(3)]
    order = [(0, q_lo)] + qkv(0) + [(q_lo + 3 * ATT_COLS, IN_COLS)] + qkv(1) + qkv(2)
    w_in = p['w_in'].astype(BF16)
    q['w_in'] = jnp.concatenate([w_in[:, lo:hi] for lo, hi in order], axis=1)
    head = jnp.arange(MXU_DIM) // HEAD_DIM
    q['bd'] = (head[:, None] == head[None, :]).astype(BF16)
    q['qg'] = jnp.tile(p['q_norm_g'], (1, HEADS_PER_GROUP)) * (HEAD_DIM ** -0.5 * LOG2E)
    q['kg'] = jnp.tile(p['k_norm_g'], (1, HEADS_PER_GROUP))
    q['score_bound'] = (HEAD_DIM * ATT_BOUND_MARGIN) * (jnp.max(jnp.abs(q['qg']), axis=1)
                                                        * jnp.max(jnp.abs(q['kg']), axis=1))
    q['rope'] = _rope_tables(seq)
    q['lru_cw'] = p['lru_conv_w']
    q['lru_cb'] = p['lru_conv_b'].reshape(1, D_RNN)
    dense = jnp.stack([jnp.stack([_block_diag_dense(p['lru_wa'][d]), _block_diag_dense(p['lru_wx'][d])])
                       for d in range(2)])
    dense = (0.5 * dense).astype(BF16)
    chunks = []
    for c in range(LRU_NCHUNK):
        k0 = min(max(2 * c - 1, 0), D_RNN // LANES - LRU_KSLABS) * LANES
        chunks.append(dense[:, :, k0:k0 + LRU_KSLABS * LANES, c * LRU_CHUNK:(c + 1) * LRU_CHUNK])
    q['lru_wg'] = jnp.stack(chunks)
    gb = 0.5 * jnp.stack([p['lru_ba'][0], p['lru_bx'][0], p['lru_ba'][1], p['lru_bx'][1]])
    q['lru_gb'] = gb.reshape(4, LRU_NCHUNK, LRU_CHUNK).transpose(1, 0, 2)
    q['lru_lam'] = p['lru_lambda'].reshape(2, LRU_NCHUNK, LRU_CHUNK).transpose(1, 0, 2)
    q['lru_wout'] = p['w_lru_out'].astype(BF16)
    q['watt'] = p['w_att_out'].astype(BF16)
    q['wo'] = p['w_o'].astype(BF16)
    lse_lane = jnp.arange(LANES)
    lane_head = jnp.where(lse_lane % HEAD_DIM < GROUP_COLS // LANES,
                          2 * (lse_lane % HEAD_DIM) + lse_lane // HEAD_DIM, -1)
    q['expand'] = (lane_head[:, None] == jnp.arange(GROUP_COLS)[None, :] // HEAD_DIM).astype(BF16)
    q['g2'] = p['norm2_g'].reshape(1, D_MODEL)
    q['wup'] = p['w_up'].astype(BF16)
    q['ffn_cw'] = p['ffn_conv_w']
    q['ffn_cb'] = p['ffn_conv_b'].reshape(1, D_FF)
    q['wdn'] = p['w_down'].astype(BF16)
    return q


def _layer(x, q):
    b, seq, _ = x.shape
    x2 = x.reshape(b * seq, D_MODEL)
    main, qkv1, qkv2 = _inproj(x2, seq, q['g1'], q['w_in'], q['bd'], q['qg'], q['kg'], *q['rope'])
    a_out = _lru(main.reshape(b, seq, MAIN_COLS), q['lru_cw'], q['lru_cb'], q['lru_wg'], q['lru_gb'],
                 q['lru_lam'], q['lru_wout'])
    outs, lses = [], []
    for g, (qkv, (window, dilation)) in enumerate(zip((main, qkv1, qkv2), ATT_GROUPS)):
        o, lse = _attention_group(qkv, b, seq, window, dilation, q['score_bound'][g])
        outs.append(o)
        lses.append(lse)
    x1 = _merge(x2, a_out.reshape(b * seq, D_MODEL), outs, lses, main, q['watt'], q['wo'], q['expand'])
    y = _ffn(x1, seq, q['g2'], q['wup'], q['ffn_cw'], q['ffn_cb'], q['wdn'])
    return y.reshape(b, seq, D_MODEL)


def kernel(x_prompt, x_sample, norm1_g, w_in, lru_conv_w, lru_conv_b, lru_wa, lru_ba, lru_wx, lru_bx,
           lru_lambda, w_lru_out, q_norm_g, k_norm_g, w_att_out, w_o, norm2_g, w_up, ffn_conv_w,
           ffn_conv_b, w_down):
    params = dict(norm1_g=norm1_g, w_in=w_in, lru_conv_w=lru_conv_w, lru_conv_b=lru_conv_b,
                  lru_wa=lru_wa, lru_ba=lru_ba, lru_wx=lru_wx, lru_bx=lru_bx, lru_lambda=lru_lambda,
                  w_lru_out=w_lru_out, q_norm_g=q_norm_g, k_norm_g=k_norm_g, w_att_out=w_att_out,
                  w_o=w_o, norm2_g=norm2_g, w_up=w_up, ffn_conv_w=ffn_conv_w, ffn_conv_b=ffn_conv_b,
                  w_down=w_down)
    depth = norm1_g.shape[0]
    seq = x_prompt.shape[1]
    assert x_sample.shape[1] == seq
    layers = [_prep_layer({name: p[l] for name, p in params.items()}, seq) for l in range(depth)]
    ys = []
    for x in (x_prompt, x_sample):
        for q in layers:
            x = _layer(x, q)
        ys.append(x)
    return tuple(ys)
```

```python
import functools

import jax
import jax.numpy as jnp
from jax import lax
from jax.experimental import pallas as pl
from jax.experimental.pallas import tpu as pltpu

F32 = jnp.float32
BF16 = jnp.bfloat16

D_MODEL = 1024
D_RNN = 1280
LRU_BLOCKS = 16
LRU_BW = D_RNN // LRU_BLOCKS
LRU_C = 8.0
LRU_CONV = 4
ATT_GROUPS = ((128, 1), (512, 4), (2048, 16))
N_GROUPS = len(ATT_GROUPS)
HEADS_PER_GROUP = 8
HEAD_DIM = 64
GROUP_COLS = HEADS_PER_GROUP * HEAD_DIM
ATT_COLS = N_GROUPS * GROUP_COLS
ROT_DIM = HEAD_DIM // 4
ROPE_THETA = 500000.0
D_FF = 3 * D_MODEL
FFN_CONV = 3
EPS = 1e-6
NEG = -1e30
LOG2E = 1.4426950408889634
LN2 = 0.6931471805599453
IN_COLS = 2 * D_RNN + 3 * ATT_COLS + 2 * D_MODEL

LANES = 128
MXU_DIM = 256
VMEM_LIMIT_BYTES = 56 * 1024 * 1024

CHUNK = 512
MAIN_COLS = 2 * D_RNN + 3 * GROUP_COLS + 2 * D_MODEL
MAIN_CHUNKS = MAIN_COLS // CHUNK
QKV0_CHUNK0 = 2 * D_RNN // CHUNK
GATE_COL_BLOCK0 = (2 * D_RNN + 3 * GROUP_COLS) // D_MODEL

TOKEN_TILE = 512
ATT_QBLOCK = 128
ATT_HALF = 64
ATT_MIN_STEP_TOKENS = 512
ATT_MAX_CONST_SHIFT = 60.0
ATT_BOUND_MARGIN = 1.02

LRU_CHUNK = 256
LRU_NCHUNK = D_RNN // LRU_CHUNK
LRU_KSLABS = 4
LRU_TBLOCK = 256
LRU_SUBLEN = 128
LRU_NSUB = 16
LRU_PITCH = LRU_SUBLEN + 8
LRU_SUBS_PER_BLOCK = LRU_TBLOCK // LRU_SUBLEN


def _cparams(sem):
    return pltpu.CompilerParams(dimension_semantics=sem, vmem_limit_bytes=VMEM_LIMIT_BYTES)


def _resident(shape):
    return pl.BlockSpec(shape, lambda *_: (0,) * len(shape), pipeline_mode=pl.Buffered(1))


def _sigmoid(x):
    return 0.5 * jnp.tanh(0.5 * x) + 0.5


def _gelu_tanh(x):
    return 0.5 * x * (1.0 + jnp.tanh(0.7978845608028654 * (x + 0.044715 * (x * x * x))))


def _rms_rows(x, g):
    ms = jnp.mean(x * x, axis=-1, keepdims=True)
    return x * lax.rsqrt(ms + EPS) * g


def _chunk_role(j):
    if QKV0_CHUNK0 <= j < QKV0_CHUNK0 + 3:
        return 0, j - QKV0_CHUNK0
    if j >= MAIN_CHUNKS:
        return 1 + (j - MAIN_CHUNKS) // 3, (j - MAIN_CHUNKS) % 3
    return None


def _inproj_kernel(x_ref, g1_ref, w_ref, bd_ref, qg_ref, kg_ref, ra_ref, rm_ref, rp_ref,
                   o_ref, d1_ref, d2_ref, tile_scr):
    tm = x_ref.shape[0]
    xn = _rms_rows(x_ref[...], g1_ref[...]).astype(BF16)
    ra = ra_ref[...]
    rm = rm_ref[...]
    rp = rp_ref[...]
    nchunks = IN_COLS // CHUNK
    order = sorted(range(nchunks), key=lambda j: (_chunk_role(j) is None, -j))
    for j in order:
        c0 = j * CHUNK
        acc = jnp.dot(xn, w_ref[:, c0:c0 + CHUNK], preferred_element_type=F32)
        role = _chunk_role(j)
        if role is None:
            o_ref[:, c0:c0 + CHUNK] = acc.astype(BF16)
            continue
        g, part = role
        tiles = []
        if part < 2:
            gain = (qg_ref if part == 0 else kg_ref)[g:g + 1, :]
            sq = (acc * acc).astype(BF16)
            bd = bd_ref[...]
            for t in range(CHUNK // MXU_DIM):
                sl = slice(t * MXU_DIM, (t + 1) * MXU_DIM)
                ss = jnp.dot(sq[:, sl], bd, preferred_element_type=F32)
                y = acc[:, sl] * lax.rsqrt(ss * (1.0 / HEAD_DIM) + EPS) * gain[:, sl]
                for u in range(MXU_DIM // LANES):
                    yt = y[:, u * LANES:(u + 1) * LANES]
                    tiles.append(yt * ra + pltpu.roll(yt, LANES - ROT_DIM // 2, 1) * rm
                                 + pltpu.roll(yt, ROT_DIM // 2, 1) * rp)
        else:
            tiles = [acc[:, t * LANES:(t + 1) * LANES] for t in range(CHUNK // LANES)]
        if g == 0:
            for t, val in enumerate(tiles):
                o_ref[:, c0 + t * LANES:c0 + (t + 1) * LANES] = val.astype(BF16)
        else:
            dil = ATT_GROUPS[g][1]
            dst = d1_ref if g == 1 else d2_ref
            for t, val in enumerate(tiles):
                tile_scr[t] = val
            for m in range(dil):
                for t in range(CHUNK // LANES):
                    col = (m * 3 + part) * GROUP_COLS + t * LANES
                    dst[:, col:col + LANES] = tile_scr[t, pl.ds(m, tm // dil, stride=dil), :].astype(BF16)


def _inproj(x2, seq, g1, w_in, bd, qg, kg, ra, rm, rp):
    n = x2.shape[0]
    tm = TOKEN_TILE
    tiles_per_seq = seq // tm
    rope_spec = pl.BlockSpec((tm, LANES), lambda i: (i % tiles_per_seq, 0))
    d1, d2 = ATT_GROUPS[1][1], ATT_GROUPS[2][1]
    return pl.pallas_call(
        _inproj_kernel,
        grid=(n // tm,),
        in_specs=[
            pl.BlockSpec((tm, D_MODEL), lambda i: (i, 0)),
            _resident((1, D_MODEL)),
            _resident((D_MODEL, IN_COLS)),
            _resident((MXU_DIM, MXU_DIM)),
            _resident((N_GROUPS, GROUP_COLS)),
            _resident((N_GROUPS, GROUP_COLS)),
            rope_spec, rope_spec, rope_spec,
        ],
        out_specs=[
            pl.BlockSpec((tm, MAIN_COLS), lambda i: (i, 0)),
            pl.BlockSpec((tm // d1, d1 * 3 * GROUP_COLS), lambda i: (i, 0)),
            pl.BlockSpec((tm // d2, d2 * 3 * GROUP_COLS), lambda i: (i, 0)),
        ],
        out_shape=[
            jax.ShapeDtypeStruct((n, MAIN_COLS), BF16),
            jax.ShapeDtypeStruct((n // d1, d1 * 3 * GROUP_COLS), BF16),
            jax.ShapeDtypeStruct((n // d2, d2 * 3 * GROUP_COLS), BF16),
        ],
        scratch_shapes=[pltpu.VMEM((CHUNK // LANES, tm, LANES), F32)],
        compiler_params=_cparams(("parallel",)),
        name="inproj",
    )(x2, g1, w_in, bd, qg, kg, ra, rm, rp)


def _lru_kslab0(c):
    return jnp.minimum(jnp.maximum(2 * c - 1, 0), D_RNN // LANES - LRU_KSLABS)


LRU_CONV_ROWS = 128


def _lru_kernel(x_ref, gate_ref, cw_ref, cb_ref, wg_ref, gb_ref, lam_ref, wout_ref, o_ref,
                xc_scr, acc_scr, a_scr, u_scr, h_scr, pad_scr):
    c = pl.program_id(1)
    seq = x_ref.shape[0]
    nsl = LRU_CHUNK // LANES
    chains = [(d, i) for d in range(2) for i in range(nsl)]

    @pl.when(c == 0)
    def _conv():
        pad_scr[0:8, :] = jnp.zeros((8, LANES), F32)
        pad_scr[seq + 8:seq + 16, :] = jnp.zeros((8, LANES), F32)
        for j in range(D_RNN // LANES):
            sl = slice(j * LANES, (j + 1) * LANES)
            pad_scr[8:seq + 8, :] = x_ref[:, sl].astype(F32)
            bias = cb_ref[:, sl]
            taps = [cw_ref[k:k + 1, sl] for k in range(LRU_CONV)]

            def conv_body(rb, carry):
                r0 = pl.multiple_of(rb * LRU_CONV_ROWS, LRU_CONV_ROWS)
                xc = bias
                for k in range(LRU_CONV):
                    off = 8 + k - LRU_CONV // 2
                    xc = xc + pad_scr[pl.ds(r0 + off, LRU_CONV_ROWS), :] * taps[k]
                xc_scr[j, pl.ds(r0, LRU_CONV_ROWS), :] = xc
                return carry

            lax.fori_loop(0, seq // LRU_CONV_ROWS, conv_body, 0)

    k0 = _lru_kslab0(c)
    lam = lam_ref[...]
    half_l2 = (-0.5 * LRU_C * LOG2E) * (jnp.maximum(-lam, 0.0) + jnp.log1p(jnp.exp(-jnp.abs(lam))))

    def pitched_rows(s, sb):
        return pl.ds(pl.multiple_of((s * LRU_SUBS_PER_BLOCK + sb) * LRU_PITCH, 8), LRU_SUBLEN)

    def gates_body(s, carry):
        r0 = pl.multiple_of(s * LRU_TBLOCK, LRU_TBLOCK)
        lhs =jnp.concatenate([xc_scr[k0 + i, pl.ds(r0, LRU_TBLOCK), :] for i in range(LRU_KSLABS)],
                              axis=1).astype(BF16)
        half_xc = 0.5 * jnp.concatenate([xc_scr[nsl * c + i, pl.ds(r0, LRU_TBLOCK), :] for i in range(nsl)],
                                        axis=1)
        for d in range(2):
            za = jnp.dot(lhs, wg_ref[d, 0], preferred_element_type=F32) + gb_ref[2 * d:2 * d + 1, :]
            zx = jnp.dot(lhs, wg_ref[d, 1], preferred_element_type=F32) + gb_ref[2 * d + 1:2 * d + 2, :]
            hl = half_l2[d:d + 1, :]
            a = jnp.exp2(jnp.tanh(za) * hl + hl)
            t = 1.0 - a * a
            root = jnp.where(t > 0.0, t * lax.rsqrt(t), 0.0)
            u = root * ((jnp.tanh(zx) + 1.0) * half_xc)
            for i in range(nsl):
                for sb in range(LRU_SUBS_PER_BLOCK):
                    rows = slice(sb * LRU_SUBLEN, (sb + 1) * LRU_SUBLEN)
                    a_scr[d, i, pitched_rows(s, sb), :] = a[rows, i * LANES:(i + 1) * LANES]
                    u_scr[d, i, pitched_rows(s, sb), :] = u[rows, i * LANES:(i + 1) * LANES]
        return carry

    lax.fori_loop(0, seq // LRU_TBLOCK, gates_body, 0, unroll=4)

    def step_rows(d, t):
        r = t if d == 0 else LRU_SUBLEN - 1 - t
        return pl.ds(r, LRU_NSUB, stride=LRU_PITCH)

    def ends_body(t, carry):
        out = []
        for n, (d, i) in enumerate(chains):
            av = a_scr[d, i, step_rows(d, t), :]
            out += [av * carry[2 * n] + u_scr[d, i, step_rows(d, t), :], av * carry[2 * n + 1]]
        return tuple(out)

    init = (jnp.zeros((LRU_NSUB, LANES), F32), jnp.ones((LRU_NSUB, LANES), F32)) * len(chains)
    ends = lax.fori_loop(0, LRU_SUBLEN, ends_body, init, unroll=8)

    sub = lax.broadcasted_iota(jnp.int32, (LRU_NSUB, LANES), 0)
    starts = []
    for n, (d, i) in enumerate(chains):
        h_end, cum_end = ends[2 * n], ends[2 * n + 1]
        cin = jnp.zeros((LRU_NSUB, LANES), F32)
        for step in range(LRU_NSUB - 1):
            nxt = h_end + cum_end * cin
            if d == 0:
                cin = jnp.where(sub == step + 1, pltpu.roll(nxt, 1, 0), cin)
            else:
                cin = jnp.where(sub == LRU_NSUB - 2 - step, pltpu.roll(nxt, LRU_NSUB - 1, 0), cin)
        starts.append(cin)

    def scan_body(t, carry):
        out = []
        for n, (d, i) in enumerate(chains):
            h = a_scr[d, i, step_rows(d, t), :] * carry[n] + u_scr[d, i, step_rows(d, t), :]
            h_scr[d, i, step_rows(d, t), :] = h
            out.append(h)
        return tuple(out)

    lax.fori_loop(0, LRU_SUBLEN, scan_body, tuple(starts), unroll=8)

    def out_body(first, s, carry):
        r0 = pl.multiple_of(s * LRU_TBLOCK, LRU_TBLOCK)
        hsum = jnp.concatenate(
            [jnp.concatenate([h_scr[0, i, pitched_rows(s, sb), :] + h_scr[1, i, pitched_rows(s, sb), :]
                              for sb in range(LRU_SUBS_PER_BLOCK)], axis=0) for i in range(nsl)], axis=1)
        y = (_gelu_tanh(gate_ref[pl.ds(r0, LRU_TBLOCK), :].astype(F32)) * hsum).astype(BF16)
        part = jnp.dot(y, wout_ref[...], preferred_element_type=F32)
        if first:
            acc_scr[pl.ds(r0, LRU_TBLOCK), :] = part
        else:
            acc_scr[pl.ds(r0, LRU_TBLOCK), :] += part
        return carry

    @pl.when(c == 0)
    def _out_first():
        lax.fori_loop(0, seq // LRU_TBLOCK, functools.partial(out_body, True), 0, unroll=4)

    @pl.when(c > 0)
    def _out_rest():
        lax.fori_loop(0, seq // LRU_TBLOCK, functools.partial(out_body, False), 0, unroll=4)

    @pl.when(c == LRU_NCHUNK - 1)
    def _emit():
        o_ref[...] = acc_scr[...].astype(BF16)


def _lru(proj3, cw, cb, wg, gb, lam, wout):
    b, seq, _ = proj3.shape
    assert seq == LRU_NSUB * LRU_SUBLEN and seq % LRU_TBLOCK == 0
    nsl = LRU_CHUNK // LANES
    gate_blk0 = D_RNN // LRU_CHUNK
    return pl.pallas_call(
        _lru_kernel,
        grid=(b, LRU_NCHUNK),
        in_specs=[
            pl.BlockSpec((None, seq, D_RNN), lambda i, c: (i, 0, 0), pipeline_mode=pl.Buffered(1)),
            pl.BlockSpec((None, seq, LRU_CHUNK), lambda i, c: (i, 0, gate_blk0 + c)),
            _resident((LRU_CONV, D_RNN)),
            _resident((1, D_RNN)),
            pl.BlockSpec((None, 2, 2, LRU_KSLABS * LANES, LRU_CHUNK), lambda i, c: (c, 0, 0, 0, 0)),
            pl.BlockSpec((None, 4, LRU_CHUNK), lambda i, c: (c, 0, 0)),
            pl.BlockSpec((None, 2, LRU_CHUNK), lambda i, c: (c, 0, 0)),
            pl.BlockSpec((LRU_CHUNK, D_MODEL), lambda i, c: (c, 0)),
        ],
        out_specs=pl.BlockSpec((None, seq, D_MODEL), lambda i, c: (i, 0, 0)),
        out_shape=jax.ShapeDtypeStruct((b, seq, D_MODEL), BF16),
        scratch_shapes=[
            pltpu.VMEM((D_RNN // LANES, seq, LANES), F32),
            pltpu.VMEM((seq, D_MODEL), F32),
            pltpu.VMEM((2, nsl, LRU_NSUB * LRU_PITCH, LANES), F32),
            pltpu.VMEM((2, nsl, LRU_NSUB * LRU_PITCH, LANES), F32),
            pltpu.VMEM((2, nsl, LRU_NSUB * LRU_PITCH, LANES), F32),
            pltpu.VMEM((seq + 16, LANES), F32),
        ],
        compiler_params=_cparams(("parallel", "arbitrary")),
        name="rglru",
    )(proj3, proj3, cw, cb, wg, gb, lam, wout)


def _attn_kernel(*refs, rpb, fused, row_max):
    n_in = 1 if fused else 3
    qkv_refs = refs[:n_in]
    bias_ref, shift_ref, o_ref, lse_ref, kt_scr = refs[n_in:]
    length = o_ref.shape[0]
    nkb = length // ATT_QBLOCK
    lane = lax.broadcasted_iota(jnp.int32, (ATT_QBLOCK, LANES), 1)
    lo_half = lane < HEAD_DIM
    lo_keys = lax.broadcasted_iota(jnp.int32, (2 * ATT_QBLOCK, LANES), 1) < HEAD_DIM
    npairs = GROUP_COLS // LANES

    def scores_block(qp_of, kt_of, vw_of, bias):
        outs = []
        lse_tile = jnp.zeros((ATT_QBLOCK, LANES), F32)
        for p in range(npairs):
            qp, kt, vw = qp_of(p), kt_of(p), vw_of(p)
            full, shifts = [], []
            for hh in range(LANES // HEAD_DIM):
                own = lo_half if hh == 0 else jnp.logical_not(lo_half)
                qm = jnp.where(own, qp, jnp.zeros_like(qp))
                s = jnp.dot(qm, kt, preferred_element_type=F32) + bias
                if row_max:
                    m = jnp.max(s, axis=-1, keepdims=True)
                    s = s - m
                    shifts.append(m)
                e = jnp.exp2(s).astype(BF16)
                v1 = jnp.where(lo_keys if hh == 0 else jnp.logical_not(lo_keys), vw, jnp.ones_like(vw))
                full.append(jnp.dot(e, v1, preferred_element_type=F32))
            numer = jnp.where(lo_half, full[0], full[1])
            denom = pltpu.roll(jnp.where(lo_half, full[1], full[0]), HEAD_DIM, 1)
            outs.append(numer * (1.0 / denom))
            shift = jnp.where(lo_half, shifts[0], shifts[1]) if row_max else shift_ref[...]
            lse = LN2 * (shift + jnp.log2(denom))
            lse_tile = jnp.where(jnp.logical_or(lane == p, lane == HEAD_DIM + p), lse, lse_tile)
        return outs, lse_tile

    for r in range(rpb):
        def cols(part, p, r=r):
            base = ((r * 3 + part) * GROUP_COLS if fused else 0) + p * LANES
            return slice(base, base + LANES)

        q_ref, k_ref, v_ref = (qkv_refs[0],) * 3 if fused else qkv_refs

        def transpose_body(kb, carry):
            k0 = pl.multiple_of(kb * ATT_QBLOCK, ATT_QBLOCK)
            for p in range(npairs):
                kt_scr[p, kb] = k_ref[pl.ds(k0, ATT_QBLOCK), cols(1, p)].T
            return carry

        lax.fori_loop(0, nkb, transpose_body, 0, unroll=min(nkb, 4))

        def store(rows_list, outs, lse_tile, r=r):
            for dst_rows, src in rows_list:
                for p in range(npairs):
                    c0 = r * GROUP_COLS + p * LANES
                    o_ref[dst_rows, c0:c0 + LANES] = outs[p][src].astype(BF16)
                lse_ref[dst_rows, r * LANES:(r + 1) * LANES] = lse_tile[src]

        def interior(i, carry):
            k0 = pl.multiple_of(i * ATT_QBLOCK, ATT_QBLOCK)
            q0 = pl.multiple_of(k0 + ATT_HALF, ATT_HALF)
            outs, lse_tile = scores_block(
                lambda p: q_ref[pl.ds(q0, ATT_QBLOCK), cols(0, p)],
                lambda p: jnp.concatenate([kt_scr[p, i], kt_scr[p, i + 1]], axis=1),
                lambda p: v_ref[pl.ds(k0, 2 * ATT_QBLOCK), cols(2, p)],
                bias_ref[0])
            store([(pl.ds(q0, ATT_QBLOCK), slice(None))], outs, lse_tile)
            return carry

        if nkb > 1:
            lax.fori_loop(0, nkb - 1, interior, 0, unroll=3)

        head_rows, tail_rows = slice(0, ATT_HALF), slice(length - ATT_HALF, length)
        tail_keys = slice(length - ATT_QBLOCK, length)
        outs, lse_tile = scores_block(
            lambda p: jnp.concatenate([q_ref[head_rows, cols(0, p)], q_ref[tail_rows, cols(0, p)]], axis=0),
            lambda p: jnp.concatenate([kt_scr[p, 0], kt_scr[p, nkb - 1]], axis=1),
            lambda p: jnp.concatenate([v_ref[0:ATT_QBLOCK, cols(2, p)], v_ref[tail_keys, cols(2, p)]], axis=0),
            bias_ref[1])
        store([(head_rows, slice(0, ATT_HALF)), (tail_rows, slice(ATT_HALF, ATT_QBLOCK))], outs, lse_tile)


def _attn_bias():
    i = jnp.arange(ATT_QBLOCK)[:, None]
    j = jnp.arange(2 * ATT_QBLOCK)[None, :]
    interior = jnp.abs(i + ATT_HALF - j) <= ATT_HALF
    first = (i < ATT_HALF) & (j < ATT_QBLOCK) & (jnp.abs(i - j) <= ATT_HALF)
    last = (i >= ATT_HALF) & (j >= ATT_QBLOCK) & (jnp.abs(i - (j - ATT_QBLOCK)) <= ATT_HALF)
    return jnp.stack([jnp.where(interior, 0.0, NEG), jnp.where(first | last, 0.0, NEG)]).astype(F32)


def _attention_group(qkv, b, seq, window, dilation, score_bound):
    length = seq // dilation
    assert (window // 2) // dilation == ATT_HALF and length % ATT_QBLOCK == 0
    pv = qkv.reshape(b, length, qkv.shape[1])
    fused = dilation > 1
    rpb = min(dilation, max(1, ATT_MIN_STEP_TOKENS // length))
    if fused:
        qkv_specs = [pl.BlockSpec((None, length, rpb * 3 * GROUP_COLS), lambda i, m: (i, 0, m))]
    else:
        qkv_specs = [pl.BlockSpec((None, length, GROUP_COLS), lambda i, m, part=part: (i, 0, QKV0_CHUNK0 + part))
                     for part in range(3)]

    def call(row_max, bias, shift):
        return pl.pallas_call(
            functools.partial(_attn_kernel, rpb=rpb, fused=fused, row_max=row_max),
            grid=(b, dilation // rpb),
            in_specs=qkv_specs + [_resident(bias.shape), _resident(shift.shape)],
            out_specs=[
                pl.BlockSpec((None, length, rpb * GROUP_COLS), lambda i, m: (i, 0, m)),
                pl.BlockSpec((None, length, rpb * LANES), lambda i, m: (i, 0, m)),
            ],
            out_shape=[
                jax.ShapeDtypeStruct((b, length, dilation * GROUP_COLS), BF16),
                jax.ShapeDtypeStruct((b, length, dilation * LANES), F32),
            ],
            scratch_shapes=[pltpu.VMEM((GROUP_COLS // LANES, length // ATT_QBLOCK, LANES, ATT_QBLOCK), BF16)],
            compiler_params=_cparams(("parallel", "parallel")),
            name=f"attn_d{dilation}" + ("_rowmax" if row_max else ""),
        )(*([pv] * len(qkv_specs)), bias, shift)

    bias = _attn_bias()
    shift = jnp.full((1, LANES), score_bound, F32)
    o, lse = lax.cond(score_bound <= ATT_MAX_CONST_SHIFT,
                      lambda: call(False, bias - score_bound, shift),
                      lambda: call(True, bias, shift))
    return o.reshape(b * length, dilation * GROUP_COLS), lse.reshape(b * length, dilation * LANES)


def _merge_kernel(x_ref, a_ref, o0_ref, o1_ref, o2_ref, l0_ref, l1_ref, l2_ref, ga_ref, gb_ref,
                  watt_ref, wo_ref, ex_ref, y_ref, o_scr, l_scr):
    tm = x_ref.shape[0]
    nt = GROUP_COLS // LANES
    for gi, (og_ref, lg_ref) in enumerate(((o1_ref, l1_ref), (o2_ref, l2_ref))):
        dil = ATT_GROUPS[gi + 1][1]
        for m in range(dil):
            rows = pl.ds(m, tm // dil, stride=dil)
            l_scr[gi, rows, :] = lg_ref[:, m * LANES:(m + 1) * LANES]
            for t in range(nt):
                col = m * GROUP_COLS + t * LANES
                o_scr[gi, t, rows, :] = og_ref[:, col:col + LANES].astype(F32)
    lses = [l0_ref[...], l_scr[0], l_scr[1]]
    group_out = [o0_ref[...].astype(F32)] + [
        jnp.concatenate([o_scr[gi, t] for t in range(nt)], axis=1) for gi in range(2)]
    mx = jnp.maximum(jnp.maximum(lses[0], lses[1]), lses[2])
    es = [jnp.exp(l - mx) for l in lses]
    den = es[0] + es[1] + es[2]
    ex = ex_ref[...]
    o = None
    for e, og in zip(es, group_out):
        w = e / den
        hi = w.astype(BF16)
        lo = (w - hi.astype(F32)).astype(BF16)
        wx = jnp.dot(hi, ex, preferred_element_type=F32) + jnp.dot(lo, ex, preferred_element_type=F32)
        t = wx * og
        o = t if o is None else o + t
    b_out = jnp.dot(o.astype(BF16), watt_ref[...], preferred_element_type=F32)
    mix = (_sigmoid(ga_ref[...].astype(F32)) * a_ref[...].astype(F32)
           + _sigmoid(gb_ref[...].astype(F32)) * b_out)
    y_ref[...] = x_ref[...] + jnp.dot(mix.astype(BF16), wo_ref[...], preferred_element_type=F32)


def _merge(x2, a_out, outs, lses, proj2, watt, wo, ex):
    n = x2.shape[0]
    tm = TOKEN_TILE
    row = lambda cols, dil=1: pl.BlockSpec((tm // dil, cols * dil), lambda i: (i, 0))
    d1, d2 = ATT_GROUPS[1][1], ATT_GROUPS[2][1]
    return pl.pallas_call(
        _merge_kernel,
        grid=(n // tm,),
        in_specs=[
            row(D_MODEL), row(D_MODEL),
            row(GROUP_COLS), row(GROUP_COLS, d1), row(GROUP_COLS, d2),
            row(LANES), row(LANES, d1), row(LANES, d2),
            pl.BlockSpec((tm, D_MODEL), lambda i: (i, GATE_COL_BLOCK0)),
            pl.BlockSpec((tm, D_MODEL), lambda i: (i, GATE_COL_BLOCK0 + 1)),
            _resident((GROUP_COLS, D_MODEL)),
            _resident((D_MODEL, D_MODEL)),
            _resident((LANES, GROUP_COLS)),
        ],
        out_specs=row(D_MODEL),
        out_shape=jax.ShapeDtypeStruct((n, D_MODEL), F32),
        scratch_shapes=[
            pltpu.VMEM((2, GROUP_COLS // LANES, tm, LANES), F32),
            pltpu.VMEM((2, tm, LANES), F32),
        ],
        compiler_params=_cparams(("parallel",)),
        name="merge",
    )(x2, a_out, *outs, *lses, proj2, proj2, watt, wo, ex)


FFN_HALO = 16
FFN_TOKEN_TILE = 1024
FFN_CHUNK = 1024


def _ffn_kernel(x_ref, xp_ref, xn_ref, g2_ref, wup_ref, cw_ref, cb_ref, wdn_ref, y_ref, lhs_scr,
                *, tiles_per_seq):
    tm = x_ref.shape[0]
    ti = pl.program_id(0) % tiles_per_seq
    g2 = g2_ref[...]
    x = x_ref[...]
    keep_prev = jnp.where(ti == 0, 0.0, 1.0)
    keep_next = jnp.where(ti == tiles_per_seq - 1, 0.0, 1.0)
    xn = _rms_rows(x, g2).astype(BF16)
    lhs_scr[0:FFN_HALO, :] = (_rms_rows(xp_ref[...], g2) * keep_prev).astype(BF16)
    lhs_scr[FFN_HALO:FFN_HALO + tm, :] = xn
    lhs_scr[FFN_HALO + tm:, :] = (_rms_rows(xn_ref[...], g2) * keep_next).astype(BF16)
    acc = x
    for j in range(D_FF // FFN_CHUNK):
        sl = slice(j * FFN_CHUNK, (j + 1) * FFN_CHUNK)
        gext =jnp.dot(lhs_scr[...], wup_ref[:, sl], preferred_element_type=F32)
        rows = gext.shape[0]
        gc = cb_ref[:, sl]
        for k in range(FFN_CONV):
            shift = (FFN_CONV // 2 - k) % rows
            gk = gext if shift == 0 else pltpu.roll(gext, shift, 0)
            gc = gc + gk[FFN_HALO:FFN_HALO + tm, :] * cw_ref[k:k + 1, sl]
        val = jnp.dot(xn, wup_ref[:, D_FF + j * FFN_CHUNK:D_FF + (j + 1) * FFN_CHUNK],
                      preferred_element_type=F32)
        h = (_gelu_tanh(gc) * val).astype(BF16)
        acc = acc + jnp.dot(h, wdn_ref[sl, :], preferred_element_type=F32)
    y_ref[...] = acc


def _ffn(x2, seq, g2, wup, cw, cb, wdn):
    n = x2.shape[0]
    tm = FFN_TOKEN_TILE
    hb = tm // FFN_HALO
    last = n // FFN_HALO - 1
    return pl.pallas_call(
        functools.partial(_ffn_kernel, tiles_per_seq=seq // tm),
        grid=(n // tm,),
        in_specs=[
            pl.BlockSpec((tm, D_MODEL), lambda i: (i, 0)),
            pl.BlockSpec((FFN_HALO, D_MODEL), lambda i: (jnp.maximum(i * hb - 1, 0), 0)),
            pl.BlockSpec((FFN_HALO, D_MODEL), lambda i: (jnp.minimum((i + 1) * hb, last), 0)),
            _resident((1, D_MODEL)),
            _resident((D_MODEL, 2 * D_FF)),
            _resident((FFN_CONV, D_FF)),
            _resident((1, D_FF)),
            _resident((D_FF, D_MODEL)),
        ],
        out_specs=pl.BlockSpec((tm, D_MODEL), lambda i: (i, 0)),
        out_shape=jax.ShapeDtypeStruct((n, D_MODEL), F32),
        scratch_shapes=[pltpu.VMEM((tm + 2 * FFN_HALO, D_MODEL), BF16)],
        compiler_params=_cparams(("parallel",)),
        name="convffn",
    )(x2, x2, x2, g2, wup, cw, cb, wdn)


def _rope_tables(seq):
    pos = jnp.arange(seq, dtype=F32)
    inv = ROPE_THETA ** (-jnp.arange(0, ROT_DIM, 2, dtype=F32) / ROT_DIM)
    ang = pos[:, None] * inv[None, :]
    cos, sin = jnp.cos(ang), jnp.sin(ang)
    half = ROT_DIM // 2
    pad = HEAD_DIM - ROT_DIM
    one = jnp.ones((seq, pad), F32)
    zero = jnp.zeros((seq, pad), F32)
    zh = jnp.zeros((seq, half), F32)
    per_head = lambda parts: jnp.tile(jnp.concatenate(parts, axis=1), (1, LANES // HEAD_DIM))
    ra = per_head([cos, cos, one])
    rm = per_head([-sin, zh, zero])
    rp = per_head([zh, sin, zero])
    return ra, rm, rp


def _block_diag_dense(w):
    nb, c, d = w.shape
    return jnp.einsum('ncd,nm->ncmd', w, jnp.eye(nb, dtype=w.dtype)).reshape(nb * c, nb * d)


def _prep_layer(p, seq):
    q = {}
    q['g1'] = p['norm1_g'].reshape(1, D_MODEL)
    q_lo = 2 * D_RNN
    seg = lambda part, g: (q_lo + part * ATT_COLS + g * GROUP_COLS, q_lo + part * ATT_COLS + (g + 1) * GROUP_COLS)
    qkv = lambda g: [seg(part, g) for part in range(3)]
    order = [(0, q_lo)] + qkv(0) + [(q_lo + 3 * ATT_COLS, IN_COLS)] + qkv(1) + qkv(2)
    w_in = p['w_in'].astype(BF16)
    q['w_in'] = jnp.concatenate([w_in[:, lo:hi] for lo, hi in order], axis=1)
    head = jnp.arange(MXU_DIM) // HEAD_DIM
    q['bd'] = (head[:, None] == head[None, :]).astype(BF16)
    q['qg'] = jnp.tile(p['q_norm_g'], (1, HEADS_PER_GROUP)) * (HEAD_DIM ** -0.5 * LOG2E)
    q['kg'] = jnp.tile(p['k_norm_g'], (1, HEADS_PER_GROUP))
    q['score_bound'] = (HEAD_DIM * ATT_BOUND_MARGIN) * (jnp.max(jnp.abs(q['qg']), axis=1)
                                                        * jnp.max(jnp.abs(q['kg']), axis=1))
    q['rope'] = _rope_tables(seq)
    q['lru_cw'] = p['lru_conv_w']
    q['lru_cb'] = p['lru_conv_b'].reshape(1, D_RNN)
    dense = jnp.stack([jnp.stack([_block_diag_dense(p['lru_wa'][d]), _block_diag_dense(p['lru_wx'][d])])
                       for d in range(2)])
    dense = (0.5 * dense).astype(BF16)
    chunks = []
    for c in range(LRU_NCHUNK):
        k0 = min(max(2 * c - 1, 0), D_RNN // LANES - LRU_KSLABS) * LANES
        chunks.append(dense[:, :, k0:k0 + LRU_KSLABS * LANES, c * LRU_CHUNK:(c + 1) * LRU_CHUNK])
    q['lru_wg'] = jnp.stack(chunks)
    gb = 0.5 * jnp.stack([p['lru_ba'][0], p['lru_bx'][0], p['lru_ba'][1], p['lru_bx'][1]])
    q['lru_gb'] = gb.reshape(4, LRU_NCHUNK, LRU_CHUNK).transpose(1, 0, 2)
    q['lru_lam'] = p['lru_lambda'].reshape(2, LRU_NCHUNK, LRU_CHUNK).transpose(1, 0, 2)
    q['lru_wout'] = p['w_lru_out'].astype(BF16)
    q['watt'] = p['w_att_out'].astype(BF16)
    q['wo'] = p['w_o'].astype(BF16)
    lse_lane = jnp.arange(LANES)
    lane_head = jnp.where(lse_lane % HEAD_DIM < GROUP_COLS // LANES,
                          2 * (lse_lane % HEAD_DIM) + lse_lane // HEAD_DIM, -1)
    q['expand'] = (lane_head[:, None] == jnp.arange(GROUP_COLS)[None, :] // HEAD_DIM).astype(BF16)
    q['g2'] = p['norm2_g'].reshape(1, D_MODEL)
    q['wup'] = p['w_up'].astype(BF16)
    q['ffn_cw'] = p['ffn_conv_w']
    q['ffn_cb'] = p['ffn_conv_b'].reshape(1, D_FF)
    q['wdn'] = p['w_down'].astype(BF16)
    return q


def _layer(x, q):
    b, seq, _ = x.shape
    x2 = x.reshape(b * seq, D_MODEL)
    main, qkv1, qkv2 = _inproj(x2, seq, q['g1'], q['w_in'], q['bd'], q['qg'], q['kg'], *q['rope'])
    a_out = _lru(main.reshape(b, seq, MAIN_COLS), q['lru_cw'], q['lru_cb'], q['lru_wg'], q['lru_gb'],
                 q['lru_lam'], q['lru_wout'])
    outs, lses = [], []
    for g, (qkv, (window, dilation)) in enumerate(zip((main, qkv1, qkv2), ATT_GROUPS)):
        o, lse = _attention_group(qkv, b, seq, window, dilation, q['score_bound'][g])
        outs.append(o)
        lses.append(lse)
    x1 = _merge(x2, a_out.reshape(b * seq, D_MODEL), outs, lses, main, q['watt'], q['wo'], q['expand'])
    y = _ffn(x1, seq, q['g2'], q['wup'], q['ffn_cw'], q['ffn_cb'], q['wdn'])
    return y.reshape(b, seq, D_MODEL)


def kernel(x_prompt, x_sample, norm1_g, w_in, lru_conv_w, lru_conv_b, lru_wa, lru_ba, lru_wx, lru_bx,
           lru_lambda, w_lru_out, q_norm_g, k_norm_g, w_att_out, w_o, norm2_g, w_up, ffn_conv_w,
           ffn_conv_b, w_down):
    params = dict(norm1_g=norm1_g, w_in=w_in, lru_conv_w=lru_conv_w, lru_conv_b=lru_conv_b,
                  lru_wa=lru_wa, lru_ba=lru_ba, lru_wx=lru_wx, lru_bx=lru_bx, lru_lambda=lru_lambda,
                  w_lru_out=w_lru_out, q_norm_g=q_norm_g, k_norm_g=k_norm_g, w_att_out=w_att_out,
                  w_o=w_o, norm2_g=norm2_g, w_up=w_up, ffn_conv_w=ffn_conv_w, ffn_conv_b=ffn_conv_b,
                  w_down=w_down)
    depth = norm1_g.shape[0]
    seq = x_prompt.shape[1]
    assert x_sample.shape[1] == seq
    layers = [_prep_layer({name: p[l] for name, p in params.items()}, seq) for l in range(depth)]
    ys = []
    for x in (x_prompt, x_sample):
        for q in layers:
            x = _layer(x, q)
        ys.append(x)
    return tuple(ys)
```

```python
import functools

import jax
import jax.numpy as jnp
from jax import lax
from jax.experimental import pallas as pl
from jax.experimental.pallas import tpu as pltpu

F32 = jnp.float32
BF16 = jnp.bfloat16

D_MODEL = 1024
D_RNN = 1280
LRU_BLOCKS = 16
LRU_BW = D_RNN // LRU_BLOCKS
LRU_C = 8.0
LRU_CONV = 4
ATT_GROUPS = ((128, 1), (512, 4), (2048, 16))
N_GROUPS = len(ATT_GROUPS)
HEADS_PER_GROUP = 8
HEAD_DIM = 64
GROUP_COLS = HEADS_PER_GROUP * HEAD_DIM
ATT_COLS = N_GROUPS * GROUP_COLS
ROT_DIM = HEAD_DIM // 4
ROPE_THETA = 500000.0
D_FF = 3 * D_MODEL
FFN_CONV = 3
EPS = 1e-6
NEG = -1e30
LOG2E = 1.4426950408889634
LN2 = 0.6931471805599453
IN_COLS = 2 * D_RNN + 3 * ATT_COLS + 2 * D_MODEL

LANES = 128
MXU_DIM = 256
VMEM_LIMIT_BYTES = 56 * 1024 * 1024

CHUNK = 512
MAIN_COLS = 2 * D_RNN + 3 * GROUP_COLS
QKV0_CHUNK0 = 2 * D_RNN // CHUNK
MGATE_COLS = 2 * D_MODEL

TOKEN_TILE = 512
INPROJ_TOKEN_TILE = 1024
ATT_QBLOCK = 128
ATT_HALF = 64
ATT_MIN_STEP_TOKENS = 512
ATT_MAX_CONST_SHIFT = 60.0
ATT_BOUND_MARGIN = 1.02

LRU_CHUNK = 256
LRU_NCHUNK = D_RNN // LRU_CHUNK
LRU_KSLABS = 4
LRU_TBLOCK = 256
LRU_SUBLEN = 128
LRU_NSUB = 16
LRU_PITCH = LRU_SUBLEN + 8
LRU_SUBS_PER_BLOCK = LRU_TBLOCK // LRU_SUBLEN


def _cparams(sem):
    return pltpu.CompilerParams(dimension_semantics=sem, vmem_limit_bytes=VMEM_LIMIT_BYTES)


def _resident(shape):
    return pl.BlockSpec(shape, lambda *_: (0,) * len(shape), pipeline_mode=pl.Buffered(1))


def _sigmoid(x):
    return 0.5 * jnp.tanh(0.5 * x) + 0.5


def _gelu_tanh(x):
    return 0.5 * x * (1.0 + jnp.tanh(0.7978845608028654 * (x + 0.044715 * (x * x * x))))


def _rms_rows(x, g):
    ms = jnp.mean(x * x, axis=-1, keepdims=True)
    return x * lax.rsqrt(ms + EPS) * g


def _inproj_plan(call):
    if call == 0:
        return [((0, j - QKV0_CHUNK0) if j >= QKV0_CHUNK0 else None, 0, j * CHUNK)
                for j in range(MAIN_COLS // CHUNK)]
    plain = [(None, 0, j * CHUNK) for j in range(MGATE_COLS // CHUNK)]
    return plain + [((g, part), g, None) for g in (1, 2) for part in range(3)]


def _inproj_kernel(x_ref, g1_ref, w_ref, bd_ref, qg_ref, kg_ref, ra_ref, rm_ref, rp_ref, *outs, plan):
    tm = x_ref.shape[0]
    tile_scr = outs[-1]
    xn = _rms_rows(x_ref[...], g1_ref[...]).astype(BF16)
    ra = ra_ref[...]
    rm = rm_ref[...]
    rp = rp_ref[...]
    heavy = [j for j in range(len(plan)) if plan[j][0] is not None][::-1]
    plain = [j for j in range(len(plan)) if plan[j][0] is None]
    order = []
    while heavy or plain:
        take = -(-len(heavy) // max(len(plain), 1))
        order += heavy[:take] + plain[:1]
        heavy, plain = heavy[take:], plain[1:]
    for j in order:
        role, dst, c0 = plan[j]
        o_ref = outs[dst]
        acc = jnp.dot(xn, w_ref[:, j * CHUNK:(j + 1) * CHUNK], preferred_element_type=F32)
        if role is None:
            o_ref[:, c0:c0 + CHUNK] = acc.astype(BF16)
            continue
        g, part = role
        tiles = []
        if part < 2:
            gain = (qg_ref if part == 0 else kg_ref)[g:g + 1, :]
            sq = (acc * acc).astype(BF16)
            bd = bd_ref[...]
            for t in range(CHUNK // MXU_DIM):
                sl = slice(t * MXU_DIM, (t + 1) * MXU_DIM)
                ss = jnp.dot(sq[:, sl], bd, preferred_element_type=F32)
                y = acc[:, sl] * lax.rsqrt(ss * (1.0 / HEAD_DIM) + EPS) * gain[:, sl]
                for u in range(MXU_DIM // LANES):
                    yt = y[:, u * LANES:(u + 1) * LANES]
                    tiles.append(yt * ra + pltpu.roll(yt, LANES - ROT_DIM // 2, 1) * rm
                                 + pltpu.roll(yt, ROT_DIM // 2, 1) * rp)
        else:
            tiles = [acc[:, t * LANES:(t + 1) * LANES] for t in range(CHUNK // LANES)]
        if g == 0:
            for t, val in enumerate(tiles):
                o_ref[:, c0 + t * LANES:c0 + (t + 1) * LANES] = val.astype(BF16)
        else:
            dil = ATT_GROUPS[g][1]
            for t, val in enumerate(tiles):
                tile_scr[t] = val
            for m in range(dil):
                for t in range(CHUNK // LANES):
                    col = (m * 3 + part) * GROUP_COLS + t * LANES
                    o_ref[:, col:col + LANES] = tile_scr[t, pl.ds(m, tm // dil, stride=dil), :].astype(BF16)


def _inproj(x2, seq, g1, w_main, w_rest, bd, qg, kg, ra, rm, rp):
    n = x2.shape[0]
    tm = INPROJ_TOKEN_TILE
    tiles_per_seq = seq // tm
    rope_spec = pl.BlockSpec((tm, LANES), lambda i: (i % tiles_per_seq, 0))

    def call(which, w, widths, dils, scratch, name):
        return pl.pallas_call(
            functools.partial(_inproj_kernel, plan=_inproj_plan(which)),
            grid=(n // tm,),
            in_specs=[
                pl.BlockSpec((tm, D_MODEL), lambda i: (i, 0)),
                _resident((1, D_MODEL)),
                _resident(w.shape),
                _resident((MXU_DIM, MXU_DIM)),
                _resident((N_GROUPS, GROUP_COLS)),
                _resident((N_GROUPS, GROUP_COLS)),
                rope_spec, rope_spec, rope_spec,
            ],
            out_specs=[pl.BlockSpec((tm // d, c * d), lambda i: (i, 0)) for c, d in zip(widths, dils)],
            out_shape=[jax.ShapeDtypeStruct((n // d, c * d), BF16) for c, d in zip(widths, dils)],
            scratch_shapes=scratch,
            compiler_params=_cparams(("parallel",)),
            name=name,
        )(x2, g1, w, bd, qg, kg, ra, rm, rp)

    d1, d2 = ATT_GROUPS[1][1], ATT_GROUPS[2][1]
    main, = call(0, w_main, [MAIN_COLS], [1], [], "inproj_main")
    mgates, qkv1, qkv2 = call(1, w_rest, [MGATE_COLS, 3 * GROUP_COLS, 3 * GROUP_COLS], [1, d1, d2],
                              [pltpu.VMEM((CHUNK // LANES, tm, LANES), F32)], "inproj_rest")
    return main, mgates, qkv1, qkv2


def _lru_kslab0(c):
    return jnp.minimum(jnp.maximum(2 * c - 1, 0), D_RNN // LANES - LRU_KSLABS)


LRU_CONV_ROWS = 128


def _lru_kernel(x_ref, gate_ref, cw_ref, cb_ref, wg_ref, gb_ref, lam_ref, wout_ref, o_ref,
                xc_scr, acc_scr, a_scr, u_scr, h_scr, pad_scr):
    c = pl.program_id(1)
    seq = x_ref.shape[0]
    nsl = LRU_CHUNK // LANES
    chains = [(d, i) for d in range(2) for i in range(nsl)]

    @pl.when(c == 0)
    def _conv():
        pad_scr[0:8, :] = jnp.zeros((8, LANES), F32)
        pad_scr[seq + 8:seq + 16, :] = jnp.zeros((8, LANES), F32)
        for j in range(D_RNN // LANES):
            sl = slice(j * LANES, (j + 1) * LANES)
            pad_scr[8:seq + 8, :] = x_ref[:, sl].astype(F32)
            bias = cb_ref[:, sl]
            taps = [cw_ref[k:k + 1, sl] for k in range(LRU_CONV)]

            def conv_body(rb, carry):
                r0 = pl.multiple_of(rb * LRU_CONV_ROWS, LRU_CONV_ROWS)
                xc = bias
                for k in range(LRU_CONV):
                    off = 8 + k - LRU_CONV // 2
                    xc = xc + pad_scr[pl.ds(r0 + off, LRU_CONV_ROWS), :] * taps[k]
                xc_scr[j, pl.ds(r0, LRU_CONV_ROWS), :] = xc
                return carry

            lax.fori_loop(0, seq // LRU_CONV_ROWS, conv_body, 0)

    k0 = _lru_kslab0(c)
    lam = lam_ref[...]
    half_l2 = (-0.5 * LRU_C * LOG2E) * (jnp.maximum(-lam, 0.0) + jnp.log1p(jnp.exp(-jnp.abs(lam))))

    def pitched_rows(s, sb):
        return pl.ds(pl.multiple_of((s * LRU_SUBS_PER_BLOCK + sb) * LRU_PITCH, 8), LRU_SUBLEN)

    def gates_body(s, carry):
        r0 = pl.multiple_of(s * LRU_TBLOCK, LRU_TBLOCK)
        lhs =jnp.concatenate([xc_scr[k0 + i, pl.ds(r0, LRU_TBLOCK), :] for i in range(LRU_KSLABS)],
                              axis=1).astype(BF16)
        half_xc = 0.5 * jnp.concatenate([xc_scr[nsl * c + i, pl.ds(r0, LRU_TBLOCK), :] for i in range(nsl)],
                                        axis=1)
        for d in range(2):
            za = jnp.dot(lhs, wg_ref[d, 0], preferred_element_type=F32) + gb_ref[2 * d:2 * d + 1, :]
            zx = jnp.dot(lhs, wg_ref[d, 1], preferred_element_type=F32) + gb_ref[2 * d + 1:2 * d + 2, :]
            hl = half_l2[d:d + 1, :]
            a = jnp.exp2(jnp.tanh(za) * hl + hl)
            t = 1.0 - a * a
            root = jnp.where(t > 0.0, t * lax.rsqrt(t), 0.0)
            u = root * ((jnp.tanh(zx) + 1.0) * half_xc)
            for i in range(nsl):
                for sb in range(LRU_SUBS_PER_BLOCK):
                    rows = slice(sb * LRU_SUBLEN, (sb + 1) * LRU_SUBLEN)
                    a_scr[d, i, pitched_rows(s, sb), :] = a[rows, i * LANES:(i + 1) * LANES]
                    u_scr[d, i, pitched_rows(s, sb), :] = u[rows, i * LANES:(i + 1) * LANES]
        return carry

    lax.fori_loop(0, seq // LRU_TBLOCK, gates_body, 0, unroll=4)

    def step_rows(d, t):
        r = t if d == 0 else LRU_SUBLEN - 1 - t
        return pl.ds(r, LRU_NSUB, stride=LRU_PITCH)

    def ends_body(t, carry):
        out = []
        for n, (d, i) in enumerate(chains):
            av = a_scr[d, i, step_rows(d, t), :]
            out += [av * carry[2 * n] + u_scr[d, i, step_rows(d, t), :], av * carry[2 * n + 1]]
        return tuple(out)

    init = (jnp.zeros((LRU_NSUB, LANES), F32), jnp.ones((LRU_NSUB, LANES), F32)) * len(chains)
    ends = lax.fori_loop(0, LRU_SUBLEN, ends_body, init, unroll=8)

    sub = lax.broadcasted_iota(jnp.int32, (LRU_NSUB, LANES), 0)
    starts = []
    for n, (d, i) in enumerate(chains):
        h_end, cum_end = ends[2 * n], ends[2 * n + 1]
        cin = jnp.zeros((LRU_NSUB, LANES), F32)
        for step in range(LRU_NSUB - 1):
            nxt = h_end + cum_end * cin
            if d == 0:
                cin = jnp.where(sub == step + 1, pltpu.roll(nxt, 1, 0), cin)
            else:
                cin = jnp.where(sub == LRU_NSUB - 2 - step, pltpu.roll(nxt, LRU_NSUB - 1, 0), cin)
        starts.append(cin)

    def scan_body(t, carry):
        out = []
        for n, (d, i) in enumerate(chains):
            h = a_scr[d, i, step_rows(d, t), :] * carry[n] + u_scr[d, i, step_rows(d, t), :]
            h_scr[d, i, step_rows(d, t), :] = h
            out.append(h)
        return tuple(out)

    lax.fori_loop(0, LRU_SUBLEN, scan_body, tuple(starts), unroll=8)

    def out_body(first, s, carry):
        r0 = pl.multiple_of(s * LRU_TBLOCK, LRU_TBLOCK)
        hsum = jnp.concatenate(
            [jnp.concatenate([h_scr[0, i, pitched_rows(s, sb), :] + h_scr[1, i, pitched_rows(s, sb), :]
                              for sb in range(LRU_SUBS_PER_BLOCK)], axis=0) for i in range(nsl)], axis=1)
        y = (_gelu_tanh(gate_ref[pl.ds(r0, LRU_TBLOCK), :].astype(F32)) * hsum).astype(BF16)
        part = jnp.dot(y, wout_ref[...], preferred_element_type=F32)
        if first:
            acc_scr[pl.ds(r0, LRU_TBLOCK), :] = part
        else:
            acc_scr[pl.ds(r0, LRU_TBLOCK), :] += part
        return carry

    @pl.when(c == 0)
    def _out_first():
        lax.fori_loop(0, seq // LRU_TBLOCK, functools.partial(out_body, True), 0, unroll=4)

    @pl.when(c > 0)
    def _out_rest():
        lax.fori_loop(0, seq // LRU_TBLOCK, functools.partial(out_body, False), 0, unroll=4)

    @pl.when(c == LRU_NCHUNK - 1)
    def _emit():
        o_ref[...] = acc_scr[...].astype(BF16)


def _lru(proj3, cw, cb, wg, gb, lam, wout):
    b, seq, _ = proj3.shape
    assert seq == LRU_NSUB * LRU_SUBLEN and seq % LRU_TBLOCK == 0
    nsl = LRU_CHUNK // LANES
    gate_blk0 = D_RNN // LRU_CHUNK
    return pl.pallas_call(
        _lru_kernel,
        grid=(b, LRU_NCHUNK),
        in_specs=[
            pl.BlockSpec((None, seq, D_RNN), lambda i, c: (i, 0, 0), pipeline_mode=pl.Buffered(1)),
            pl.BlockSpec((None, seq, LRU_CHUNK), lambda i, c: (i, 0, gate_blk0 + c)),
            _resident((LRU_CONV, D_RNN)),
            _resident((1, D_RNN)),
            pl.BlockSpec((None, 2, 2, LRU_KSLABS * LANES, LRU_CHUNK), lambda i, c: (c, 0, 0, 0, 0)),
            pl.BlockSpec((None, 4, LRU_CHUNK), lambda i, c: (c, 0, 0)),
            pl.BlockSpec((None, 2, LRU_CHUNK), lambda i, c: (c, 0, 0)),
            pl.BlockSpec((LRU_CHUNK, D_MODEL), lambda i, c: (c, 0)),
        ],
        out_specs=pl.BlockSpec((None, seq, D_MODEL), lambda i, c: (i, 0, 0)),
        out_shape=jax.ShapeDtypeStruct((b, seq, D_MODEL), BF16),
        scratch_shapes=[
            pltpu.VMEM((D_RNN // LANES, seq, LANES), F32),
            pltpu.VMEM((seq, D_MODEL), F32),
            pltpu.VMEM((2, nsl, LRU_NSUB * LRU_PITCH, LANES), F32),
            pltpu.VMEM((2, nsl, LRU_NSUB * LRU_PITCH, LANES), F32),
            pltpu.VMEM((2, nsl, LRU_NSUB * LRU_PITCH, LANES), F32),
            pltpu.VMEM((seq + 16, LANES), F32),
        ],
        compiler_params=_cparams(("parallel", "arbitrary")),
        name="rglru",
    )(proj3, proj3, cw, cb, wg, gb, lam, wout)


def _attn_kernel(*refs, rpb, fused, row_max):
    n_in = 1 if fused else 3
    qkv_refs = refs[:n_in]
    bias_ref, shift_ref, o_ref, lse_ref, kt_scr = refs[n_in:]
    length = o_ref.shape[0]
    nkb = length // ATT_QBLOCK
    lane = lax.broadcasted_iota(jnp.int32, (ATT_QBLOCK, LANES), 1)
    lo_half = lane < HEAD_DIM
    lo_keys = lax.broadcasted_iota(jnp.int32, (2 * ATT_QBLOCK, LANES), 1) < HEAD_DIM
    npairs = GROUP_COLS // LANES

    def scores_block(qp_of, kt_of, vw_of, bias):
        outs = []
        lse_tile = jnp.zeros((ATT_QBLOCK, LANES), F32)
        for p in range(npairs):
            qp, kt, vw = qp_of(p), kt_of(p), vw_of(p)
            full, shifts = [], []
            for hh in range(LANES // HEAD_DIM):
                own = lo_half if hh == 0 else jnp.logical_not(lo_half)
                qm = jnp.where(own, qp, jnp.zeros_like(qp))
                s = jnp.dot(qm, kt, preferred_element_type=F32) + bias
                if row_max:
                    m = jnp.max(s, axis=-1, keepdims=True)
                    s = s - m
                    shifts.append(m)
                e = jnp.exp2(s).astype(BF16)
                v1 = jnp.where(lo_keys if hh == 0 else jnp.logical_not(lo_keys), vw, jnp.ones_like(vw))
                full.append(jnp.dot(e, v1, preferred_element_type=F32))
            numer = jnp.where(lo_half, full[0], full[1])
            denom = pltpu.roll(jnp.where(lo_half, full[1], full[0]), HEAD_DIM, 1)
            outs.append(numer * (1.0 / denom))
            shift = jnp.where(lo_half, shifts[0], shifts[1]) if row_max else shift_ref[...]
            lse = LN2 * (shift + jnp.log2(denom))
            lse_tile = jnp.where(jnp.logical_or(lane == p, lane == HEAD_DIM + p), lse, lse_tile)
        return outs, lse_tile

    for r in range(rpb):
        def cols(part, p, r=r):
            base = ((r * 3 + part) * GROUP_COLS if fused else 0) + p * LANES
            return slice(base, base + LANES)

        q_ref, k_ref, v_ref = (qkv_refs[0],) * 3 if fused else qkv_refs

        def transpose_body(kb, carry):
            k0 = pl.multiple_of(kb * ATT_QBLOCK, ATT_QBLOCK)
            for p in range(npairs):
                kt_scr[p, kb] = k_ref[pl.ds(k0, ATT_QBLOCK), cols(1, p)].T
            return carry

        lax.fori_loop(0, nkb, transpose_body, 0, unroll=min(nkb, 4))

        def store(rows_list, outs, lse_tile, r=r):
            for dst_rows, src in rows_list:
                for p in range(npairs):
                    c0 = r * GROUP_COLS + p * LANES
                    o_ref[dst_rows, c0:c0 + LANES] = outs[p][src].astype(BF16)
                lse_ref[dst_rows, r * LANES:(r + 1) * LANES] = lse_tile[src]

        def interior(i, carry):
            k0 = pl.multiple_of(i * ATT_QBLOCK, ATT_QBLOCK)
            q0 = pl.multiple_of(k0 + ATT_HALF, ATT_HALF)
            outs, lse_tile = scores_block(
                lambda p: q_ref[pl.ds(q0, ATT_QBLOCK), cols(0, p)],
                lambda p: jnp.concatenate([kt_scr[p, i], kt_scr[p, i + 1]], axis=1),
                lambda p: v_ref[pl.ds(k0, 2 * ATT_QBLOCK), cols(2, p)],
                bias_ref[0])
            store([(pl.ds(q0, ATT_QBLOCK), slice(None))], outs, lse_tile)
            return carry

        if nkb > 1:
            lax.fori_loop(0, nkb - 1, interior, 0, unroll=3)

        head_rows, tail_rows = slice(0, ATT_HALF), slice(length - ATT_HALF, length)
        tail_keys = slice(length - ATT_QBLOCK, length)
        outs, lse_tile = scores_block(
            lambda p: jnp.concatenate([q_ref[head_rows, cols(0, p)], q_ref[tail_rows, cols(0, p)]], axis=0),
            lambda p: jnp.concatenate([kt_scr[p, 0], kt_scr[p, nkb - 1]], axis=1),
            lambda p: jnp.concatenate([v_ref[0:ATT_QBLOCK, cols(2, p)], v_ref[tail_keys, cols(2, p)]], axis=0),
            bias_ref[1])
        store([(head_rows, slice(0, ATT_HALF)), (tail_rows, slice(ATT_HALF, ATT_QBLOCK))], outs, lse_tile)


def _attn_bias():
    i = jnp.arange(ATT_QBLOCK)[:, None]
    j = jnp.arange(2 * ATT_QBLOCK)[None, :]
    interior = jnp.abs(i + ATT_HALF - j) <= ATT_HALF
    first = (i < ATT_HALF) & (j < ATT_QBLOCK) & (jnp.abs(i - j) <= ATT_HALF)
    last = (i >= ATT_HALF) & (j >= ATT_QBLOCK) & (jnp.abs(i - (j - ATT_QBLOCK)) <= ATT_HALF)
    return jnp.stack([jnp.where(interior, 0.0, NEG), jnp.where(first | last, 0.0, NEG)]).astype(F32)


def _attention_group(qkv, b, seq, window, dilation, score_bound):
    length = seq // dilation
    assert (window // 2) // dilation == ATT_HALF and length % ATT_QBLOCK == 0
    pv = qkv.reshape(b, length, qkv.shape[1])
    fused = dilation > 1
    rpb = min(dilation, max(1, ATT_MIN_STEP_TOKENS // length))
    if fused:
        qkv_specs = [pl.BlockSpec((None, length, rpb * 3 * GROUP_COLS), lambda i, m: (i, 0, m))]
    else:
        qkv_specs = [pl.BlockSpec((None, length, GROUP_COLS), lambda i, m, part=part: (i, 0, QKV0_CHUNK0 + part))
                     for part in range(3)]

    def call(row_max, bias, shift):
        return pl.pallas_call(
            functools.partial(_attn_kernel, rpb=rpb, fused=fused, row_max=row_max),
            grid=(b, dilation // rpb),
            in_specs=qkv_specs + [_resident(bias.shape), _resident(shift.shape)],
            out_specs=[
                pl.BlockSpec((None, length, rpb * GROUP_COLS), lambda i, m: (i, 0, m)),
                pl.BlockSpec((None, length, rpb * LANES), lambda i, m: (i, 0, m)),
            ],
            out_shape=[
                jax.ShapeDtypeStruct((b, length, dilation * GROUP_COLS), BF16),
                jax.ShapeDtypeStruct((b, length, dilation * LANES), F32),
            ],
            scratch_shapes=[pltpu.VMEM((GROUP_COLS // LANES, length // ATT_QBLOCK, LANES, ATT_QBLOCK), BF16)],
            compiler_params=_cparams(("parallel", "parallel")),
            name=f"attn_d{dilation}" + ("_rowmax" if row_max else ""),
        )(*([pv] * len(qkv_specs)), bias, shift)

    bias = _attn_bias()
    shift = jnp.full((1, LANES), score_bound, F32)
    o, lse = lax.cond(score_bound <= ATT_MAX_CONST_SHIFT,
                      lambda: call(False, bias - score_bound, shift),
                      lambda: call(True, bias, shift))
    return o.reshape(b * length, dilation * GROUP_COLS), lse.reshape(b * length, dilation * LANES)


def _merge_kernel(x_ref, a_ref, o0_ref, o1_ref, o2_ref, l0_ref, l1_ref, l2_ref, ga_ref, gb_ref,
                  watt_ref, wo_ref, ex_ref, y_ref, o_scr, l_scr):
    tm = x_ref.shape[0]
    nt = GROUP_COLS // LANES
    for gi, (og_ref, lg_ref) in enumerate(((o1_ref, l1_ref), (o2_ref, l2_ref))):
        dil = ATT_GROUPS[gi + 1][1]
        for m in range(dil):
            rows = pl.ds(m, tm // dil, stride=dil)
            l_scr[gi, rows, :] = lg_ref[:, m * LANES:(m + 1) * LANES]
            for t in range(nt):
                col = m * GROUP_COLS + t * LANES
                o_scr[gi, t, rows, :] = og_ref[:, col:col + LANES].astype(F32)
    lses = [l0_ref[...], l_scr[0], l_scr[1]]
    group_out = [o0_ref[...].astype(F32)] + [
        jnp.concatenate([o_scr[gi, t] for t in range(nt)], axis=1) for gi in range(2)]
    mx = jnp.maximum(jnp.maximum(lses[0], lses[1]), lses[2])
    es = [jnp.exp(l - mx) for l in lses]
    den = es[0] + es[1] + es[2]
    ex = ex_ref[...]
    o = None
    for e, og in zip(es, group_out):
        w = e / den
        hi = w.astype(BF16)
        lo = (w - hi.astype(F32)).astype(BF16)
        wx = jnp.dot(hi, ex, preferred_element_type=F32) + jnp.dot(lo, ex, preferred_element_type=F32)
        t = wx * og
        o = t if o is None else o + t
    b_out = jnp.dot(o.astype(BF16), watt_ref[...], preferred_element_type=F32)
    mix = (_sigmoid(ga_ref[...].astype(F32)) * a_ref[...].astype(F32)
           + _sigmoid(gb_ref[...].astype(F32)) * b_out)
    y_ref[...] = x_ref[...] + jnp.dot(mix.astype(BF16), wo_ref[...], preferred_element_type=F32)


def _merge(x2, a_out, outs, lses, mgates, watt, wo, ex):
    n = x2.shape[0]
    tm = TOKEN_TILE
    row = lambda cols, dil=1: pl.BlockSpec((tm // dil, cols * dil), lambda i: (i, 0))
    d1, d2 = ATT_GROUPS[1][1], ATT_GROUPS[2][1]
    return pl.pallas_call(
        _merge_kernel,
        grid=(n // tm,),
        in_specs=[
            row(D_MODEL), row(D_MODEL),
            row(GROUP_COLS), row(GROUP_COLS, d1), row(GROUP_COLS, d2),
            row(LANES), row(LANES, d1), row(LANES, d2),
            pl.BlockSpec((tm, D_MODEL), lambda i: (i, 0)),
            pl.BlockSpec((tm, D_MODEL), lambda i: (i, 1)),
            _resident((GROUP_COLS, D_MODEL)),
            _resident((D_MODEL, D_MODEL)),
            _resident((LANES, GROUP_COLS)),
        ],
        out_specs=row(D_MODEL),
        out_shape=jax.ShapeDtypeStruct((n, D_MODEL), F32),
        scratch_shapes=[
            pltpu.VMEM((2, GROUP_COLS // LANES, tm, LANES), F32),
            pltpu.VMEM((2, tm, LANES), F32),
        ],
        compiler_params=_cparams(("parallel",)),
        name="merge",
    )(x2, a_out, *outs, *lses, mgates, mgates, watt, wo, ex)


FFN_HALO = 16
FFN_TOKEN_TILE = 1024
FFN_CHUNK = 1024


def _ffn_kernel(x_ref, xp_ref, xn_ref, g2_ref, wup_ref, cw_ref, cb_ref, wdn_ref, y_ref, lhs_scr,
                *, tiles_per_seq):
    tm = x_ref.shape[0]
    ti = pl.program_id(0) % tiles_per_seq
    g2 = g2_ref[...]
    x = x_ref[...]
    keep_prev = jnp.where(ti == 0, 0.0, 1.0)
    keep_next = jnp.where(ti == tiles_per_seq - 1, 0.0, 1.0)
    xn = _rms_rows(x, g2).astype(BF16)
    lhs_scr[0:FFN_HALO, :] = (_rms_rows(xp_ref[...], g2) * keep_prev).astype(BF16)
    lhs_scr[FFN_HALO:FFN_HALO + tm, :] = xn
    lhs_scr[FFN_HALO + tm:, :] = (_rms_rows(xn_ref[...], g2) * keep_next).astype(BF16)
    acc = x
    for j in range(D_FF // FFN_CHUNK):
        sl = slice(j * FFN_CHUNK, (j + 1) * FFN_CHUNK)
        gext =jnp.dot(lhs_scr[...], wup_ref[:, sl], preferred_element_type=F32)
        rows = gext.shape[0]
        gc = cb_ref[:, sl]
        for k in range(FFN_CONV):
            shift = (FFN_CONV // 2 - k) % rows
            gk = gext if shift == 0 else pltpu.roll(gext, shift, 0)
            gc = gc + gk[FFN_HALO:FFN_HALO + tm, :] * cw_ref[k:k + 1, sl]
        val = jnp.dot(xn, wup_ref[:, D_FF + j * FFN_CHUNK:D_FF + (j + 1) * FFN_CHUNK],
                      preferred_element_type=F32)
        h = (_gelu_tanh(gc) * val).astype(BF16)
        acc = acc + jnp.dot(h, wdn_ref[sl, :], preferred_element_type=F32)
    y_ref[...] = acc


def _ffn(x2, seq, g2, wup, cw, cb, wdn):
    n = x2.shape[0]
    tm = FFN_TOKEN_TILE
    hb = tm // FFN_HALO
    last = n // FFN_HALO - 1
    return pl.pallas_call(
        functools.partial(_ffn_kernel, tiles_per_seq=seq // tm),
        grid=(n // tm,),
        in_specs=[
            pl.BlockSpec((tm, D_MODEL), lambda i: (i, 0)),
            pl.BlockSpec((FFN_HALO, D_MODEL), lambda i: (jnp.maximum(i * hb - 1, 0), 0)),
            pl.BlockSpec((FFN_HALO, D_MODEL), lambda i: (jnp.minimum((i + 1) * hb, last), 0)),
            _resident((1, D_MODEL)),
            _resident((D_MODEL, 2 * D_FF)),
            _resident((FFN_CONV, D_FF)),
            _resident((1, D_FF)),
            _resident((D_FF, D_MODEL)),
        ],
        out_specs=pl.BlockSpec((tm, D_MODEL), lambda i: (i, 0)),
        out_shape=jax.ShapeDtypeStruct((n, D_MODEL), F32),
        scratch_shapes=[pltpu.VMEM((tm + 2 * FFN_HALO, D_MODEL), BF16)],
        compiler_params=_cparams(("parallel",)),
        name="convffn",
    )(x2, x2, x2, g2, wup, cw, cb, wdn)


def _rope_tables(seq):
    pos = jnp.arange(seq, dtype=F32)
    inv = ROPE_THETA ** (-jnp.arange(0, ROT_DIM, 2, dtype=F32) / ROT_DIM)
    ang = pos[:, None] * inv[None, :]
    cos, sin = jnp.cos(ang), jnp.sin(ang)
    half = ROT_DIM // 2
    pad = HEAD_DIM - ROT_DIM
    one = jnp.ones((seq, pad), F32)
    zero = jnp.zeros((seq, pad), F32)
    zh = jnp.zeros((seq, half), F32)
    per_head = lambda parts: jnp.tile(jnp.concatenate(parts, axis=1), (1, LANES // HEAD_DIM))
    ra = per_head([cos, cos, one])
    rm = per_head([-sin, zh, zero])
    rp = per_head([zh, sin, zero])
    return ra, rm, rp


def _block_diag_dense(w):
    nb, c, d = w.shape
    return jnp.einsum('ncd,nm->ncmd', w, jnp.eye(nb, dtype=w.dtype)).reshape(nb * c, nb * d)


def _prep_layer(p, seq):
    q = {}
    q['g1'] = p['norm1_g'].reshape(1, D_MODEL)
    q_lo = 2 * D_RNN
    seg = lambda part, g: (q_lo + part * ATT_COLS + g * GROUP_COLS, q_lo + part * ATT_COLS + (g + 1) * GROUP_COLS)
    qkv = lambda g: [seg(part, g) for part in range(3)]
    w_in = p['w_in'].astype(BF16)
    take = lambda segs: jnp.concatenate([w_in[:, lo:hi] for lo, hi in segs], axis=1)
    q['w_main'] = take([(0, q_lo)] + qkv(0))
    q['w_rest'] = take([(q_lo + 3 * ATT_COLS, IN_COLS)] + qkv(1) + qkv(2))
    head = jnp.arange(MXU_DIM) // HEAD_DIM
    q['bd'] = (head[:, None] == head[None, :]).astype(BF16)
    q['qg'] = jnp.tile(p['q_norm_g'], (1, HEADS_PER_GROUP)) * (HEAD_DIM ** -0.5 * LOG2E)
    q['kg'] = jnp.tile(p['k_norm_g'], (1, HEADS_PER_GROUP))
    q['score_bound'] = (HEAD_DIM * ATT_BOUND_MARGIN) * (jnp.max(jnp.abs(q['qg']), axis=1)
                                                        * jnp.max(jnp.abs(q['kg']), axis=1))
    q['rope'] = _rope_tables(seq)
    q['lru_cw'] = p['lru_conv_w']
    q['lru_cb'] = p['lru_conv_b'].reshape(1, D_RNN)
    dense = jnp.stack([jnp.stack([_block_diag_dense(p['lru_wa'][d]), _block_diag_dense(p['lru_wx'][d])])
                       for d in range(2)])
    dense = (0.5 * dense).astype(BF16)
    chunks = []
    for c in range(LRU_NCHUNK):
        k0 = min(max(2 * c - 1, 0), D_RNN // LANES - LRU_KSLABS) * LANES
        chunks.append(dense[:, :, k0:k0 + LRU_KSLABS * LANES, c * LRU_CHUNK:(c + 1) * LRU_CHUNK])
    q['lru_wg'] = jnp.stack(chunks)
    gb = 0.5 * jnp.stack([p['lru_ba'][0], p['lru_bx'][0], p['lru_ba'][1], p['lru_bx'][1]])
    q['lru_gb'] = gb.reshape(4, LRU_NCHUNK, LRU_CHUNK).transpose(1, 0, 2)
    q['lru_lam'] = p['lru_lambda'].reshape(2, LRU_NCHUNK, LRU_CHUNK).transpose(1, 0, 2)
    q['lru_wout'] = p['w_lru_out'].astype(BF16)
    q['watt'] = p['w_att_out'].astype(BF16)
    q['wo'] = p['w_o'].astype(BF16)
    lse_lane = jnp.arange(LANES)
    lane_head = jnp.where(lse_lane % HEAD_DIM < GROUP_COLS // LANES,
                          2 * (lse_lane % HEAD_DIM) + lse_lane // HEAD_DIM, -1)
    q['expand'] = (lane_head[:, None] == jnp.arange(GROUP_COLS)[None, :] // HEAD_DIM).astype(BF16)
    q['g2'] = p['norm2_g'].reshape(1, D_MODEL)
    q['wup'] = p['w_up'].astype(BF16)
    q['ffn_cw'] = p['ffn_conv_w']
    q['ffn_cb'] = p['ffn_conv_b'].reshape(1, D_FF)
    q['wdn'] = p['w_down'].astype(BF16)
    return q


def _layer(x, q):
    b, seq, _ = x.shape
    x2 = x.reshape(b * seq, D_MODEL)
    main, mgates, qkv1, qkv2 = _inproj(x2, seq, q['g1'], q['w_main'], q['w_rest'], q['bd'], q['qg'], q['kg'],
                                       *q['rope'])
    a_out = _lru(main.reshape(b, seq, MAIN_COLS), q['lru_cw'], q['lru_cb'], q['lru_wg'], q['lru_gb'],
                 q['lru_lam'], q['lru_wout'])
    outs, lses = [], []
    for g, (qkv, (window, dilation)) in enumerate(zip((main, qkv1, qkv2), ATT_GROUPS)):
        o, lse = _attention_group(qkv, b, seq, window, dilation, q['score_bound'][g])
        outs.append(o)
        lses.append(lse)
    x1 = _merge(x2, a_out.reshape(b * seq, D_MODEL), outs, lses, mgates, q['watt'], q['wo'], q['expand'])
    y = _ffn(x1, seq, q['g2'], q['wup'], q['ffn_cw'], q['ffn_cb'], q['wdn'])
    return y.reshape(b, seq, D_MODEL)


def kernel(x_prompt, x_sample, norm1_g, w_in, lru_conv_w, lru_conv_b, lru_wa, lru_ba, lru_wx, lru_bx,
           lru_lambda, w_lru_out, q_norm_g, k_norm_g, w_att_out, w_o, norm2_g, w_up, ffn_conv_w,
           ffn_conv_b, w_down):
    params = dict(norm1_g=norm1_g, w_in=w_in, lru_conv_w=lru_conv_w, lru_conv_b=lru_conv_b,
                  lru_wa=lru_wa, lru_ba=lru_ba, lru_wx=lru_wx, lru_bx=lru_bx, lru_lambda=lru_lambda,
                  w_lru_out=w_lru_out, q_norm_g=q_norm_g, k_norm_g=k_norm_g, w_att_out=w_att_out,
                  w_o=w_o, norm2_g=norm2_g, w_up=w_up, ffn_conv_w=ffn_conv_w, ffn_conv_b=ffn_conv_b,
                  w_down=w_down)
    depth = norm1_g.shape[0]
    seq = x_prompt.shape[1]
    assert x_sample.shape[1] == seq
    layers = [_prep_layer({name: p[l] for name, p in params.items()}, seq) for l in range(depth)]
    ys = []
    for x in (x_prompt, x_sample):
        for q in layers:
            x = _layer(x, q)
        ys.append(x)
    return tuple(ys)
```

```python
import functools

import jax
import jax.numpy as jnp
from jax import lax
from jax.experimental import pallas as pl
from jax.experimental.pallas import tpu as pltpu

F32 = jnp.float32
BF16 = jnp.bfloat16

D_MODEL = 1024
D_RNN = 1280
LRU_BLOCKS = 16
LRU_BW = D_RNN // LRU_BLOCKS
LRU_C = 8.0
LRU_CONV = 4
ATT_GROUPS = ((128, 1), (512, 4), (2048, 16))
N_GROUPS = len(ATT_GROUPS)
HEADS_PER_GROUP = 8
HEAD_DIM = 64
GROUP_COLS = HEADS_PER_GROUP * HEAD_DIM
ATT_COLS = N_GROUPS * GROUP_COLS
ROT_DIM = HEAD_DIM // 4
ROPE_THETA = 500000.0
D_FF = 3 * D_MODEL
FFN_CONV = 3
EPS = 1e-6
NEG = -1e30
LOG2E = 1.4426950408889634
LN2 = 0.6931471805599453
IN_COLS = 2 * D_RNN + 3 * ATT_COLS + 2 * D_MODEL

LANES = 128
MXU_DIM = 256
VMEM_LIMIT_BYTES = 56 * 1024 * 1024

CHUNK = 512
MAIN_COLS = 2 * D_RNN + 3 * GROUP_COLS
QKV0_CHUNK0 = 2 * D_RNN // CHUNK
MGATE_COLS = 2 * D_MODEL

TOKEN_TILE = 512
INPROJ_TOKEN_TILE = 1024
ATT_QBLOCK = 128
ATT_HALF = 64
ATT_MIN_STEP_TOKENS = 512
ATT_MAX_CONST_SHIFT = 60.0
ATT_BOUND_MARGIN = 1.02

LRU_CHUNK = 256
LRU_NCHUNK = D_RNN // LRU_CHUNK
LRU_KSLABS = 4
LRU_TBLOCK = 512
LRU_SUBLEN = 128
LRU_NSUB = 16
LRU_PITCH = LRU_SUBLEN + 8
LRU_SUBS_PER_BLOCK = LRU_TBLOCK // LRU_SUBLEN


def _cparams(sem):
    return pltpu.CompilerParams(dimension_semantics=sem, vmem_limit_bytes=VMEM_LIMIT_BYTES)


def _resident(shape):
    return pl.BlockSpec(shape, lambda *_: (0,) * len(shape), pipeline_mode=pl.Buffered(1))


def _sigmoid(x):
    return 0.5 * jnp.tanh(0.5 * x) + 0.5


def _gelu_tanh(x):
    return 0.5 * x * (1.0 + jnp.tanh(0.7978845608028654 * (x + 0.044715 * (x * x * x))))


def _rms_rows(x, g):
    ms = jnp.mean(x * x, axis=-1, keepdims=True)
    return x * lax.rsqrt(ms + EPS) * g


def _inproj_plan(call):
    if call == 0:
        return [((0, j - QKV0_CHUNK0) if j >= QKV0_CHUNK0 else None, 0, j * CHUNK)
                for j in range(MAIN_COLS // CHUNK)]
    plain = [(None, 0, j * CHUNK) for j in range(MGATE_COLS // CHUNK)]
    return plain + [((g, part), g, None) for g in (1, 2) for part in range(3)]


def _inproj_kernel(x_ref, g1_ref, w_ref, bd_ref, qg_ref, kg_ref, ra_ref, rm_ref, rp_ref, *outs, plan):
    tm = x_ref.shape[0]
    tile_scr = outs[-1]
    xn = _rms_rows(x_ref[...], g1_ref[...]).astype(BF16)
    ra = ra_ref[...]
    rm = rm_ref[...]
    rp = rp_ref[...]
    heavy = [j for j in range(len(plan)) if plan[j][0] is not None][::-1]
    plain = [j for j in range(len(plan)) if plan[j][0] is None]
    order = []
    while heavy or plain:
        take = -(-len(heavy) // max(len(plain), 1))
        order += heavy[:take] + plain[:1]
        heavy, plain = heavy[take:], plain[1:]
    for j in order:
        role, dst, c0 = plan[j]
        o_ref = outs[dst]
        acc = jnp.dot(xn, w_ref[:, j * CHUNK:(j + 1) * CHUNK], preferred_element_type=F32)
        if role is None:
            o_ref[:, c0:c0 + CHUNK] = acc.astype(BF16)
            continue
        g, part = role
        tiles = []
        if part < 2:
            gain = (qg_ref if part == 0 else kg_ref)[g:g + 1, :]
            sq = (acc * acc).astype(BF16)
            bd = bd_ref[...]
            for t in range(CHUNK // MXU_DIM):
                sl = slice(t * MXU_DIM, (t + 1) * MXU_DIM)
                ss = jnp.dot(sq[:, sl], bd, preferred_element_type=F32)
                y = acc[:, sl] * lax.rsqrt(ss * (1.0 / HEAD_DIM) + EPS) * gain[:, sl]
                for u in range(MXU_DIM // LANES):
                    yt = y[:, u * LANES:(u + 1) * LANES]
                    tiles.append(yt * ra + pltpu.roll(yt, LANES - ROT_DIM // 2, 1) * rm
                                 + pltpu.roll(yt, ROT_DIM // 2, 1) * rp)
        else:
            tiles = [acc[:, t * LANES:(t + 1) * LANES] for t in range(CHUNK // LANES)]
        if g == 0:
            for t, val in enumerate(tiles):
                o_ref[:, c0 + t * LANES:c0 + (t + 1) * LANES] = val.astype(BF16)
        else:
            dil = ATT_GROUPS[g][1]
            for t, val in enumerate(tiles):
                tile_scr[t] = val
            for m in range(dil):
                for t in range(CHUNK // LANES):
                    col = (m * 3 + part) * GROUP_COLS + t * LANES
                    o_ref[:, col:col + LANES] = tile_scr[t, pl.ds(m, tm // dil, stride=dil), :].astype(BF16)


def _inproj(x2, seq, g1, w_main, w_rest, bd, qg, kg, ra, rm, rp):
    n = x2.shape[0]
    tm = INPROJ_TOKEN_TILE
    tiles_per_seq = seq // tm
    rope_spec = pl.BlockSpec((tm, LANES), lambda i: (i % tiles_per_seq, 0))

    def call(which, w, widths, dils, scratch, name):
        return pl.pallas_call(
            functools.partial(_inproj_kernel, plan=_inproj_plan(which)),
            grid=(n // tm,),
            in_specs=[
                pl.BlockSpec((tm, D_MODEL), lambda i: (i, 0)),
                _resident((1, D_MODEL)),
                _resident(w.shape),
                _resident((MXU_DIM, MXU_DIM)),
                _resident((N_GROUPS, GROUP_COLS)),
                _resident((N_GROUPS, GROUP_COLS)),
                rope_spec, rope_spec, rope_spec,
            ],
            out_specs=[pl.BlockSpec((tm // d, c * d), lambda i: (i, 0)) for c, d in zip(widths, dils)],
            out_shape=[jax.ShapeDtypeStruct((n // d, c * d), BF16) for c, d in zip(widths, dils)],
            scratch_shapes=scratch,
            compiler_params=_cparams(("parallel",)),
            name=name,
        )(x2, g1, w, bd, qg, kg, ra, rm, rp)

    d1, d2 = ATT_GROUPS[1][1], ATT_GROUPS[2][1]
    main, = call(0, w_main, [MAIN_COLS], [1], [], "inproj_main")
    mgates, qkv1, qkv2 = call(1, w_rest, [MGATE_COLS, 3 * GROUP_COLS, 3 * GROUP_COLS], [1, d1, d2],
                              [pltpu.VMEM((CHUNK // LANES, tm, LANES), F32)], "inproj_rest")
    return main, mgates, qkv1, qkv2


def _lru_kslab0(c):
    return jnp.minimum(jnp.maximum(2 * c - 1, 0), D_RNN // LANES - LRU_KSLABS)


LRU_CONV_ROWS = 128


def _lru_kernel(x_ref, gate_ref, cw_ref, cb_ref, wg_ref, gb_ref, lam_ref, wout_ref, o_ref,
                xc_scr, acc_scr, a_scr, u_scr, h_scr, pad_scr):
    c = pl.program_id(1)
    seq = x_ref.shape[0]
    nsl = LRU_CHUNK // LANES
    chains = [(d, i) for d in range(2) for i in range(nsl)]

    @pl.when(c == 0)
    def _conv():
        pad_scr[0:8, :] = jnp.zeros((8, LANES), F32)
        pad_scr[seq + 8:seq + 16, :] = jnp.zeros((8, LANES), F32)
        for j in range(D_RNN // LANES):
            sl = slice(j * LANES, (j + 1) * LANES)
            pad_scr[8:seq + 8, :] = x_ref[:, sl].astype(F32)
            bias = cb_ref[:, sl]
            taps = [cw_ref[k:k + 1, sl] for k in range(LRU_CONV)]

            def conv_body(rb, carry):
                r0 = pl.multiple_of(rb * LRU_CONV_ROWS, LRU_CONV_ROWS)
                xc = bias
                for k in range(LRU_CONV):
                    off = 8 + k - LRU_CONV // 2
                    xc = xc + pad_scr[pl.ds(r0 + off, LRU_CONV_ROWS), :] * taps[k]
                xc_scr[j, pl.ds(r0, LRU_CONV_ROWS), :] = xc
                return carry

            lax.fori_loop(0, seq // LRU_CONV_ROWS, conv_body, 0)

    k0 = _lru_kslab0(c)
    lam = lam_ref[...]
    half_l2 = (-0.5 * LRU_C * LOG2E) * (jnp.maximum(-lam, 0.0) + jnp.log1p(jnp.exp(-jnp.abs(lam))))

    def pitched_rows(s, sb):
        return pl.ds(pl.multiple_of((s * LRU_SUBS_PER_BLOCK + sb) * LRU_PITCH, 8), LRU_SUBLEN)

    def gates_body(s, carry):
        r0 = pl.multiple_of(s * LRU_TBLOCK, LRU_TBLOCK)
        lhs =jnp.concatenate([xc_scr[k0 + i, pl.ds(r0, LRU_TBLOCK), :] for i in range(LRU_KSLABS)],
                              axis=1).astype(BF16)
        half_xc = 0.5 * jnp.concatenate([xc_scr[nsl * c + i, pl.ds(r0, LRU_TBLOCK), :] for i in range(nsl)],
                                        axis=1)
        for d in range(2):
            za = jnp.dot(lhs, wg_ref[d, 0], preferred_element_type=F32) + gb_ref[2 * d:2 * d + 1, :]
            zx = jnp.dot(lhs, wg_ref[d, 1], preferred_element_type=F32) + gb_ref[2 * d + 1:2 * d + 2, :]
            hl = half_l2[d:d + 1, :]
            a = jnp.exp2(jnp.tanh(za) * hl + hl)
            t = 1.0 - a * a
            root = jnp.where(t > 0.0, t * lax.rsqrt(t), 0.0)
            u = root * ((jnp.tanh(zx) + 1.0) * half_xc)
            for i in range(nsl):
                for sb in range(LRU_SUBS_PER_BLOCK):
                    rows = slice(sb * LRU_SUBLEN, (sb + 1) * LRU_SUBLEN)
                    a_scr[d, i, pitched_rows(s, sb), :] = a[rows, i * LANES:(i + 1) * LANES]
                    u_scr[d, i, pitched_rows(s, sb), :] = u[rows, i * LANES:(i + 1) * LANES]
        return carry

    lax.fori_loop(0, seq // LRU_TBLOCK, gates_body, 0, unroll=2)

    def step_rows(d, t):
        r = t if d == 0 else LRU_SUBLEN - 1 - t
        return pl.ds(r, LRU_NSUB, stride=LRU_PITCH)

    def ends_body(t, carry):
        out = []
        for n, (d, i) in enumerate(chains):
            av = a_scr[d, i, step_rows(d, t), :]
            out += [av * carry[2 * n] + u_scr[d, i, step_rows(d, t), :], av * carry[2 * n + 1]]
        return tuple(out)

    init = (jnp.zeros((LRU_NSUB, LANES), F32), jnp.ones((LRU_NSUB, LANES), F32)) * len(chains)
    ends = lax.fori_loop(0, LRU_SUBLEN, ends_body, init, unroll=8)

    sub = lax.broadcasted_iota(jnp.int32, (LRU_NSUB, LANES), 0)
    starts = []
    for n, (d, i) in enumerate(chains):
        h_end, cum_end = ends[2 * n], ends[2 * n + 1]
        cin = jnp.zeros((LRU_NSUB, LANES), F32)
        for step in range(LRU_NSUB - 1):
            nxt = h_end + cum_end * cin
            if d == 0:
                cin = jnp.where(sub == step + 1, pltpu.roll(nxt, 1, 0), cin)
            else:
                cin = jnp.where(sub == LRU_NSUB - 2 - step, pltpu.roll(nxt, LRU_NSUB - 1, 0), cin)
        starts.append(cin)

    def scan_body(t, carry):
        out = []
        for n, (d, i) in enumerate(chains):
            h = a_scr[d, i, step_rows(d, t), :] * carry[n] + u_scr[d, i, step_rows(d, t), :]
            h_scr[d, i, step_rows(d, t), :] = h
            out.append(h)
        return tuple(out)

    lax.fori_loop(0, LRU_SUBLEN, scan_body, tuple(starts), unroll=8)

    def out_body(first, s, carry):
        r0 = pl.multiple_of(s * LRU_TBLOCK, LRU_TBLOCK)
        hsum = jnp.concatenate(
            [jnp.concatenate([h_scr[0, i, pitched_rows(s, sb), :] + h_scr[1, i, pitched_rows(s, sb), :]
                              for sb in range(LRU_SUBS_PER_BLOCK)], axis=0) for i in range(nsl)], axis=1)
        y = (_gelu_tanh(gate_ref[pl.ds(r0, LRU_TBLOCK), :].astype(F32)) * hsum).astype(BF16)
        part = jnp.dot(y, wout_ref[...], preferred_element_type=F32)
        if first:
            acc_scr[pl.ds(r0, LRU_TBLOCK), :] = part
        else:
            acc_scr[pl.ds(r0, LRU_TBLOCK), :] += part
        return carry

    @pl.when(c == 0)
    def _out_first():
        lax.fori_loop(0, seq // LRU_TBLOCK, functools.partial(out_body, True), 0, unroll=4)

    @pl.when(c > 0)
    def _out_rest():
        lax.fori_loop(0, seq // LRU_TBLOCK, functools.partial(out_body, False), 0, unroll=4)

    @pl.when(c == LRU_NCHUNK - 1)
    def _emit():
        o_ref[...] = acc_scr[...].astype(BF16)


def _lru(proj3, cw, cb, wg, gb, lam, wout):
    b, seq, _ = proj3.shape
    assert seq == LRU_NSUB * LRU_SUBLEN and seq % LRU_TBLOCK == 0
    nsl = LRU_CHUNK // LANES
    gate_blk0 = D_RNN // LRU_CHUNK
    return pl.pallas_call(
        _lru_kernel,
        grid=(b, LRU_NCHUNK),
        in_specs=[
            pl.BlockSpec((None, seq, D_RNN), lambda i, c: (i, 0, 0), pipeline_mode=pl.Buffered(1)),
            pl.BlockSpec((None, seq, LRU_CHUNK), lambda i, c: (i, 0, gate_blk0 + c)),
            _resident((LRU_CONV, D_RNN)),
            _resident((1, D_RNN)),
            pl.BlockSpec((None, 2, 2, LRU_KSLABS * LANES, LRU_CHUNK), lambda i, c: (c, 0, 0, 0, 0)),
            pl.BlockSpec((None, 4, LRU_CHUNK), lambda i, c: (c, 0, 0)),
            pl.BlockSpec((None, 2, LRU_CHUNK), lambda i, c: (c, 0, 0)),
            pl.BlockSpec((LRU_CHUNK, D_MODEL), lambda i, c: (c, 0)),
        ],
        out_specs=pl.BlockSpec((None, seq, D_MODEL), lambda i, c: (i, 0, 0)),
        out_shape=jax.ShapeDtypeStruct((b, seq, D_MODEL), BF16),
        scratch_shapes=[
            pltpu.VMEM((D_RNN // LANES, seq, LANES), F32),
            pltpu.VMEM((seq, D_MODEL), F32),
            pltpu.VMEM((2, nsl, LRU_NSUB * LRU_PITCH, LANES), F32),
            pltpu.VMEM((2, nsl, LRU_NSUB * LRU_PITCH, LANES), F32),
            pltpu.VMEM((2, nsl, LRU_NSUB * LRU_PITCH, LANES), F32),
            pltpu.VMEM((seq + 16, LANES), F32),
        ],
        compiler_params=_cparams(("parallel", "arbitrary")),
        name="rglru",
    )(proj3, proj3, cw, cb, wg, gb, lam, wout)


def _attn_kernel(*refs, rpb, fused, row_max):
    n_in = 1 if fused else 3
    qkv_refs = refs[:n_in]
    bias_ref, shift_ref, o_ref, lse_ref, kt_scr = refs[n_in:]
    length = o_ref.shape[0]
    nkb = length // ATT_QBLOCK
    lane = lax.broadcasted_iota(jnp.int32, (ATT_QBLOCK, LANES), 1)
    lo_half = lane < HEAD_DIM
    lo_keys = lax.broadcasted_iota(jnp.int32, (2 * ATT_QBLOCK, LANES), 1) < HEAD_DIM
    npairs = GROUP_COLS // LANES

    def scores_block(qp_of, kt_of, vw_of, bias):
        outs = []
        lse_tile = jnp.zeros((ATT_QBLOCK, LANES), F32)
        for p in range(npairs):
            qp, kt, vw = qp_of(p), kt_of(p), vw_of(p)
            full, shifts = [], []
            for hh in range(LANES // HEAD_DIM):
                own = lo_half if hh == 0 else jnp.logical_not(lo_half)
                qm = jnp.where(own, qp, jnp.zeros_like(qp))
                s = jnp.dot(qm, kt, preferred_element_type=F32) + bias
                if row_max:
                    m = jnp.max(s, axis=-1, keepdims=True)
                    s = s - m
                    shifts.append(m)
                e = jnp.exp2(s).astype(BF16)
                v1 = jnp.where(lo_keys if hh == 0 else jnp.logical_not(lo_keys), vw, jnp.ones_like(vw))
                full.append(jnp.dot(e, v1, preferred_element_type=F32))
            numer = jnp.where(lo_half, full[0], full[1])
            denom = pltpu.roll(jnp.where(lo_half, full[1], full[0]), HEAD_DIM, 1)
            outs.append(numer * (1.0 / denom))
            shift = jnp.where(lo_half, shifts[0], shifts[1]) if row_max else shift_ref[...]
            lse = LN2 * (shift + jnp.log2(denom))
            lse_tile = jnp.where(jnp.logical_or(lane == p, lane == HEAD_DIM + p), lse, lse_tile)
        return outs, lse_tile

    for r in range(rpb):
        def cols(part, p, r=r):
            base = ((r * 3 + part) * GROUP_COLS if fused else 0) + p * LANES
            return slice(base, base + LANES)

        q_ref, k_ref, v_ref = (qkv_refs[0],) * 3 if fused else qkv_refs

        def transpose_body(kb, carry):
            k0 = pl.multiple_of(kb * ATT_QBLOCK, ATT_QBLOCK)
            for p in range(npairs):
                kt_scr[p, kb] = k_ref[pl.ds(k0, ATT_QBLOCK), cols(1, p)].T
            return carry

        lax.fori_loop(0, nkb, transpose_body, 0, unroll=min(nkb, 4))

        def store(rows_list, outs, lse_tile, r=r):
            for dst_rows, src in rows_list:
                for p in range(npairs):
                    c0 = r * GROUP_COLS + p * LANES
                    o_ref[dst_rows, c0:c0 + LANES] = outs[p][src].astype(BF16)
                lse_ref[dst_rows, r * LANES:(r + 1) * LANES] = lse_tile[src]

        def interior(i, carry):
            k0 = pl.multiple_of(i * ATT_QBLOCK, ATT_QBLOCK)
            q0 = pl.multiple_of(k0 + ATT_HALF, ATT_HALF)
            outs, lse_tile = scores_block(
                lambda p: q_ref[pl.ds(q0, ATT_QBLOCK), cols(0, p)],
                lambda p: jnp.concatenate([kt_scr[p, i], kt_scr[p, i + 1]], axis=1),
                lambda p: v_ref[pl.ds(k0, 2 * ATT_QBLOCK), cols(2, p)],
                bias_ref[0])
            store([(pl.ds(q0, ATT_QBLOCK), slice(None))], outs, lse_tile)
            return carry

        if nkb > 1:
            lax.fori_loop(0, nkb - 1, interior, 0, unroll=3)

        head_rows, tail_rows = slice(0, ATT_HALF), slice(length - ATT_HALF, length)
        tail_keys = slice(length - ATT_QBLOCK, length)
        outs, lse_tile = scores_block(
            lambda p: jnp.concatenate([q_ref[head_rows, cols(0, p)], q_ref[tail_rows, cols(0, p)]], axis=0),
            lambda p: jnp.concatenate([kt_scr[p, 0], kt_scr[p, nkb - 1]], axis=1),
            lambda p: jnp.concatenate([v_ref[0:ATT_QBLOCK, cols(2, p)], v_ref[tail_keys, cols(2, p)]], axis=0),
            bias_ref[1])
        store([(head_rows, slice(0, ATT_HALF)), (tail_rows, slice(ATT_HALF, ATT_QBLOCK))], outs, lse_tile)


def _attn_bias():
    i = jnp.arange(ATT_QBLOCK)[:, None]
    j = jnp.arange(2 * ATT_QBLOCK)[None, :]
    interior = jnp.abs(i + ATT_HALF - j) <= ATT_HALF
    first = (i < ATT_HALF) & (j < ATT_QBLOCK) & (jnp.abs(i - j) <= ATT_HALF)
    last = (i >= ATT_HALF) & (j >= ATT_QBLOCK) & (jnp.abs(i - (j - ATT_QBLOCK)) <= ATT_HALF)
    return jnp.stack([jnp.where(interior, 0.0, NEG), jnp.where(first | last, 0.0, NEG)]).astype(F32)


def _attention_group(qkv, b, seq, window, dilation, score_bound):
    length = seq // dilation
    assert (window // 2) // dilation == ATT_HALF and length % ATT_QBLOCK == 0
    pv = qkv.reshape(b, length, qkv.shape[1])
    fused = dilation > 1
    rpb = min(dilation, max(1, ATT_MIN_STEP_TOKENS // length))
    if fused:
        qkv_specs = [pl.BlockSpec((None, length, rpb * 3 * GROUP_COLS), lambda i, m: (i, 0, m))]
    else:
        qkv_specs = [pl.BlockSpec((None, length, GROUP_COLS), lambda i, m, part=part: (i, 0, QKV0_CHUNK0 + part))
                     for part in range(3)]

    def call(row_max, bias, shift):
        return pl.pallas_call(
            functools.partial(_attn_kernel, rpb=rpb, fused=fused, row_max=row_max),
            grid=(b, dilation // rpb),
            in_specs=qkv_specs + [_resident(bias.shape), _resident(shift.shape)],
            out_specs=[
                pl.BlockSpec((None, length, rpb * GROUP_COLS), lambda i, m: (i, 0, m)),
                pl.BlockSpec((None, length, rpb * LANES), lambda i, m: (i, 0, m)),
            ],
            out_shape=[
                jax.ShapeDtypeStruct((b, length, dilation * GROUP_COLS), BF16),
                jax.ShapeDtypeStruct((b, length, dilation * LANES), F32),
            ],
            scratch_shapes=[pltpu.VMEM((GROUP_COLS // LANES, length // ATT_QBLOCK, LANES, ATT_QBLOCK), BF16)],
            compiler_params=_cparams(("parallel", "parallel")),
            name=f"attn_d{dilation}" + ("_rowmax" if row_max else ""),
        )(*([pv] * len(qkv_specs)), bias, shift)

    bias = _attn_bias()
    shift = jnp.full((1, LANES), score_bound, F32)
    o, lse = lax.cond(score_bound <= ATT_MAX_CONST_SHIFT,
                      lambda: call(False, bias - score_bound, shift),
                      lambda: call(True, bias, shift))
    return o.reshape(b * length, dilation * GROUP_COLS), lse.reshape(b * length, dilation * LANES)


def _merge_kernel(x_ref, a_ref, o0_ref, o1_ref, o2_ref, l0_ref, l1_ref, l2_ref, ga_ref, gb_ref,
                  watt_ref, wo_ref, ex_ref, y_ref, o_scr, l_scr):
    tm = x_ref.shape[0]
    nt = GROUP_COLS // LANES
    for gi, (og_ref, lg_ref) in enumerate(((o1_ref, l1_ref), (o2_ref, l2_ref))):
        dil = ATT_GROUPS[gi + 1][1]
        for m in range(dil):
            rows = pl.ds(m, tm // dil, stride=dil)
            l_scr[gi, rows, :] = lg_ref[:, m * LANES:(m + 1) * LANES]
            for t in range(nt):
                col = m * GROUP_COLS + t * LANES
                o_scr[gi, t, rows, :] = og_ref[:, col:col + LANES].astype(F32)
    lses = [l0_ref[...], l_scr[0], l_scr[1]]
    group_out = [o0_ref[...].astype(F32)] + [
        jnp.concatenate([o_scr[gi, t] for t in range(nt)], axis=1) for gi in range(2)]
    mx = jnp.maximum(jnp.maximum(lses[0], lses[1]), lses[2])
    es = [jnp.exp(l - mx) for l in lses]
    den = es[0] + es[1] + es[2]
    ex = ex_ref[...]
    o = None
    for e, og in zip(es, group_out):
        w = e / den
        hi = w.astype(BF16)
        lo = (w - hi.astype(F32)).astype(BF16)
        wx = jnp.dot(hi, ex, preferred_element_type=F32) + jnp.dot(lo, ex, preferred_element_type=F32)
        t = wx * og
        o = t if o is None else o + t
    b_out = jnp.dot(o.astype(BF16), watt_ref[...], preferred_element_type=F32)
    mix = (_sigmoid(ga_ref[...].astype(F32)) * a_ref[...].astype(F32)
           + _sigmoid(gb_ref[...].astype(F32)) * b_out)
    y_ref[...] = x_ref[...] + jnp.dot(mix.astype(BF16), wo_ref[...], preferred_element_type=F32)


def _merge(x2, a_out, outs, lses, mgates, watt, wo, ex):
    n = x2.shape[0]
    tm = TOKEN_TILE
    row = lambda cols, dil=1: pl.BlockSpec((tm // dil, cols * dil), lambda i: (i, 0))
    d1, d2 = ATT_GROUPS[1][1], ATT_GROUPS[2][1]
    return pl.pallas_call(
        _merge_kernel,
        grid=(n // tm,),
        in_specs=[
            row(D_MODEL), row(D_MODEL),
            row(GROUP_COLS), row(GROUP_COLS, d1), row(GROUP_COLS, d2),
            row(LANES), row(LANES, d1), row(LANES, d2),
            pl.BlockSpec((tm, D_MODEL), lambda i: (i, 0)),
            pl.BlockSpec((tm, D_MODEL), lambda i: (i, 1)),
            _resident((GROUP_COLS, D_MODEL)),
            _resident((D_MODEL, D_MODEL)),
            _resident((LANES, GROUP_COLS)),
        ],
        out_specs=row(D_MODEL),
        out_shape=jax.ShapeDtypeStruct((n, D_MODEL), F32),
        scratch_shapes=[
            pltpu.VMEM((2, GROUP_COLS // LANES, tm, LANES), F32),
            pltpu.VMEM((2, tm, LANES), F32),
        ],
        compiler_params=_cparams(("parallel",)),
        name="merge",
    )(x2, a_out, *outs, *lses, mgates, mgates, watt, wo, ex)


FFN_HALO = 16
FFN_TOKEN_TILE = 1024
FFN_CHUNK = 1024


def _ffn_kernel(x_ref, xp_ref, xn_ref, g2_ref, wup_ref, cw_ref, cb_ref, wdn_ref, y_ref, lhs_scr,
                *, tiles_per_seq):
    tm = x_ref.shape[0]
    ti = pl.program_id(0) % tiles_per_seq
    g2 = g2_ref[...]
    x = x_ref[...]
    keep_prev = jnp.where(ti == 0, 0.0, 1.0)
    keep_next = jnp.where(ti == tiles_per_seq - 1, 0.0, 1.0)
    xn = _rms_rows(x, g2).astype(BF16)
    lhs_scr[0:FFN_HALO, :] = (_rms_rows(xp_ref[...], g2) * keep_prev).astype(BF16)
    lhs_scr[FFN_HALO:FFN_HALO + tm, :] = xn
    lhs_scr[FFN_HALO + tm:, :] = (_rms_rows(xn_ref[...], g2) * keep_next).astype(BF16)
    acc = x
    for j in range(D_FF // FFN_CHUNK):
        sl = slice(j * FFN_CHUNK, (j + 1) * FFN_CHUNK)
        gext =jnp.dot(lhs_scr[...], wup_ref[:, sl], preferred_element_type=F32)
        rows = gext.shape[0]
        gc = cb_ref[:, sl]
        for k in range(FFN_CONV):
            shift = (FFN_CONV // 2 - k) % rows
            gk = gext if shift == 0 else pltpu.roll(gext, shift, 0)
            gc = gc + gk[FFN_HALO:FFN_HALO + tm, :] * cw_ref[k:k + 1, sl]
        val = jnp.dot(xn, wup_ref[:, D_FF + j * FFN_CHUNK:D_FF + (j + 1) * FFN_CHUNK],
                      preferred_element_type=F32)
        h = (_gelu_tanh(gc) * val).astype(BF16)
        acc = acc + jnp.dot(h, wdn_ref[sl, :], preferred_element_type=F32)
    y_ref[...] = acc


def _ffn(x2, seq, g2, wup, cw, cb, wdn):
    n = x2.shape[0]
    tm = FFN_TOKEN_TILE
    hb = tm // FFN_HALO
    last = n // FFN_HALO - 1
    return pl.pallas_call(
        functools.partial(_ffn_kernel, tiles_per_seq=seq // tm),
        grid=(n // tm,),
        in_specs=[
            pl.BlockSpec((tm, D_MODEL), lambda i: (i, 0)),
            pl.BlockSpec((FFN_HALO, D_MODEL), lambda i: (jnp.maximum(i * hb - 1, 0), 0)),
            pl.BlockSpec((FFN_HALO, D_MODEL), lambda i: (jnp.minimum((i + 1) * hb, last), 0)),
            _resident((1, D_MODEL)),
            _resident((D_MODEL, 2 * D_FF)),
            _resident((FFN_CONV, D_FF)),
            _resident((1, D_FF)),
            _resident((D_FF, D_MODEL)),
        ],
        out_specs=pl.BlockSpec((tm, D_MODEL), lambda i: (i, 0)),
        out_shape=jax.ShapeDtypeStruct((n, D_MODEL), F32),
        scratch_shapes=[pltpu.VMEM((tm + 2 * FFN_HALO, D_MODEL), BF16)],
        compiler_params=_cparams(("parallel",)),
        name="convffn",
    )(x2, x2, x2, g2, wup, cw, cb, wdn)


def _rope_tables(seq):
    pos = jnp.arange(seq, dtype=F32)
    inv = ROPE_THETA ** (-jnp.arange(0, ROT_DIM, 2, dtype=F32) / ROT_DIM)
    ang = pos[:, None] * inv[None, :]
    cos, sin = jnp.cos(ang), jnp.sin(ang)
    half = ROT_DIM // 2
    pad = HEAD_DIM - ROT_DIM
    one = jnp.ones((seq, pad), F32)
    zero = jnp.zeros((seq, pad), F32)
    zh = jnp.zeros((seq, half), F32)
    per_head = lambda parts: jnp.tile(jnp.concatenate(parts, axis=1), (1, LANES // HEAD_DIM))
    ra = per_head([cos, cos, one])
    rm = per_head([-sin, zh, zero])
    rp = per_head([zh, sin, zero])
    return ra, rm, rp


def _block_diag_dense(w):
    nb, c, d = w.shape
    return jnp.einsum('ncd,nm->ncmd', w, jnp.eye(nb, dtype=w.dtype)).reshape(nb * c, nb * d)


def _prep_layer(p, seq):
    q = {}
    q['g1'] = p['norm1_g'].reshape(1, D_MODEL)
    q_lo = 2 * D_RNN
    seg = lambda part, g: (q_lo + part * ATT_COLS + g * GROUP_COLS, q_lo + part * ATT_COLS + (g + 1) * GROUP_COLS)
    qkv = lambda g: [seg(part, g) for part in range(3)]
    w_in = p['w_in'].astype(BF16)
    take = lambda segs: jnp.concatenate([w_in[:, lo:hi] for lo, hi in segs], axis=1)
    q['w_main'] = take([(0, q_lo)] + qkv(0))
    q['w_rest'] = take([(q_lo + 3 * ATT_COLS, IN_COLS)] + qkv(1) + qkv(2))
    head = jnp.arange(MXU_DIM) // HEAD_DIM
    q['bd'] = (head[:, None] == head[None, :]).astype(BF16)
    q['qg'] = jnp.tile(p['q_norm_g'], (1, HEADS_PER_GROUP)) * (HEAD_DIM ** -0.5 * LOG2E)
    q['kg'] = jnp.tile(p['k_norm_g'], (1, HEADS_PER_GROUP))
    q['score_bound'] = (HEAD_DIM * ATT_BOUND_MARGIN) * (jnp.max(jnp.abs(q['qg']), axis=1)
                                                        * jnp.max(jnp.abs(q['kg']), axis=1))
    q['rope'] = _rope_tables(seq)
    q['lru_cw'] = p['lru_conv_w']
    q['lru_cb'] = p['lru_conv_b'].reshape(1, D_RNN)
    dense = jnp.stack([jnp.stack([_block_diag_dense(p['lru_wa'][d]), _block_diag_dense(p['lru_wx'][d])])
                       for d in range(2)])
    dense = (0.5 * dense).astype(BF16)
    chunks = []
    for c in range(LRU_NCHUNK):
        k0 = min(max(2 * c - 1, 0), D_RNN // LANES - LRU_KSLABS) * LANES
        chunks.append(dense[:, :, k0:k0 + LRU_KSLABS * LANES, c * LRU_CHUNK:(c + 1) * LRU_CHUNK])
    q['lru_wg'] = jnp.stack(chunks)
    gb = 0.5 * jnp.stack([p['lru_ba'][0], p['lru_bx'][0], p['lru_ba'][1], p['lru_bx'][1]])
    q['lru_gb'] = gb.reshape(4, LRU_NCHUNK, LRU_CHUNK).transpose(1, 0, 2)
    q['lru_lam'] = p['lru_lambda'].reshape(2, LRU_NCHUNK, LRU_CHUNK).transpose(1, 0, 2)
    q['lru_wout'] = p['w_lru_out'].astype(BF16)
    q['watt'] = p['w_att_out'].astype(BF16)
    q['wo'] = p['w_o'].astype(BF16)
    lse_lane = jnp.arange(LANES)
    lane_head = jnp.where(lse_lane % HEAD_DIM < GROUP_COLS // LANES,
                          2 * (lse_lane % HEAD_DIM) + lse_lane // HEAD_DIM, -1)
    q['expand'] = (lane_head[:, None] == jnp.arange(GROUP_COLS)[None, :] // HEAD_DIM).astype(BF16)
    q['g2'] = p['norm2_g'].reshape(1, D_MODEL)
    q['wup'] = p['w_up'].astype(BF16)
    q['ffn_cw'] = p['ffn_conv_w']
    q['ffn_cb'] = p['ffn_conv_b'].reshape(1, D_FF)
    q['wdn'] = p['w_down'].astype(BF16)
    return q


def _layer(x, q):
    b, seq, _ = x.shape
    x2 = x.reshape(b * seq, D_MODEL)
    main, mgates, qkv1, qkv2 = _inproj(x2, seq, q['g1'], q['w_main'], q['w_rest'], q['bd'], q['qg'], q['kg'],
                                       *q['rope'])
    a_out = _lru(main.reshape(b, seq, MAIN_COLS), q['lru_cw'], q['lru_cb'], q['lru_wg'], q['lru_gb'],
                 q['lru_lam'], q['lru_wout'])
    outs, lses = [], []
    for g, (qkv, (window, dilation)) in enumerate(zip((main, qkv1, qkv2), ATT_GROUPS)):
        o, lse = _attention_group(qkv, b, seq, window, dilation, q['score_bound'][g])
        outs.append(o)
        lses.append(lse)
    x1 = _merge(x2, a_out.reshape(b * seq, D_MODEL), outs, lses, mgates, q['watt'], q['wo'], q['expand'])
    y = _ffn(x1, seq, q['g2'], q['wup'], q['ffn_cw'], q['ffn_cb'], q['wdn'])
    return y.reshape(b, seq, D_MODEL)


def kernel(x_prompt, x_sample, norm1_g, w_in, lru_conv_w, lru_conv_b, lru_wa, lru_ba, lru_wx, lru_bx,
           lru_lambda, w_lru_out, q_norm_g, k_norm_g, w_att_out, w_o, norm2_g, w_up, ffn_conv_w,
           ffn_conv_b, w_down):
    params = dict(norm1_g=norm1_g, w_in=w_in, lru_conv_w=lru_conv_w, lru_conv_b=lru_conv_b,
                  lru_wa=lru_wa, lru_ba=lru_ba, lru_wx=lru_wx, lru_bx=lru_bx, lru_lambda=lru_lambda,
                  w_lru_out=w_lru_out, q_norm_g=q_norm_g, k_norm_g=k_norm_g, w_att_out=w_att_out,
                  w_o=w_o, norm2_g=norm2_g, w_up=w_up, ffn_conv_w=ffn_conv_w, ffn_conv_b=ffn_conv_b,
                  w_down=w_down)
    depth = norm1_g.shape[0]
    seq = x_prompt.shape[1]
    assert x_sample.shape[1] == seq
    layers = [_prep_layer({name: p[l] for name, p in params.items()}, seq) for l in range(depth)]
    ys = []
    for x in (x_prompt, x_sample):
        for q in layers:
            x = _layer(x, q)
        ys.append(x)
    return tuple(ys)
```

```python
import functools

import jax
import jax.numpy as jnp
from jax import lax
from jax.experimental import pallas as pl
from jax.experimental.pallas import tpu as pltpu

F32 = jnp.float32
BF16 = jnp.bfloat16

D_MODEL = 1024
D_RNN = 1280
LRU_BLOCKS = 16
LRU_BW = D_RNN // LRU_BLOCKS
LRU_C = 8.0
LRU_CONV = 4
ATT_GROUPS = ((128, 1), (512, 4), (2048, 16))
N_GROUPS = len(ATT_GROUPS)
HEADS_PER_GROUP = 8
HEAD_DIM = 64
GROUP_COLS = HEADS_PER_GROUP * HEAD_DIM
ATT_COLS = N_GROUPS * GROUP_COLS
ROT_DIM = HEAD_DIM // 4
ROPE_THETA = 500000.0
D_FF = 3 * D_MODEL
FFN_CONV = 3
EPS = 1e-6
NEG = -1e30
LOG2E = 1.4426950408889634
LN2 = 0.6931471805599453
IN_COLS = 2 * D_RNN + 3 * ATT_COLS + 2 * D_MODEL

LANES = 128
MXU_DIM = 256
VMEM_LIMIT_BYTES = 56 * 1024 * 1024

CHUNK = 512
MAIN_COLS = 2 * D_RNN + 3 * GROUP_COLS
QKV0_CHUNK0 = 2 * D_RNN // CHUNK
MGATE_COLS = 2 * D_MODEL

TOKEN_TILE = 512
INPROJ_TOKEN_TILE = 1024
ATT_QBLOCK = 128
ATT_HALF = 64
ATT_MIN_STEP_TOKENS = 512
ATT_MAX_CONST_SHIFT = 60.0
ATT_BOUND_MARGIN = 1.02

LRU_CHUNK = 256
LRU_NCHUNK = D_RNN // LRU_CHUNK
LRU_KSLABS = 4
LRU_TBLOCK = 512
LRU_SUBLEN = 128
LRU_NSUB = 16
LRU_PITCH = LRU_SUBLEN + 8
LRU_SUBS_PER_BLOCK = LRU_TBLOCK // LRU_SUBLEN


def _cparams(sem):
    return pltpu.CompilerParams(dimension_semantics=sem, vmem_limit_bytes=VMEM_LIMIT_BYTES)


def _resident(shape):
    return pl.BlockSpec(shape, lambda *_: (0,) * len(shape), pipeline_mode=pl.Buffered(1))


def _sigmoid(x):
    return 0.5 * jnp.tanh(0.5 * x) + 0.5


def _gelu_tanh(x):
    return 0.5 * x * (1.0 + jnp.tanh(0.7978845608028654 * (x + 0.044715 * (x * x * x))))


def _rms_rows(x, g):
    ms = jnp.mean(x * x, axis=-1, keepdims=True)
    return x * lax.rsqrt(ms + EPS) * g


def _inproj_plan(call):
    if call == 0:
        return [((0, j - QKV0_CHUNK0) if j >= QKV0_CHUNK0 else None, 0, j * CHUNK)
                for j in range(MAIN_COLS // CHUNK)]
    plain = [(None, 0, j * CHUNK) for j in range(MGATE_COLS // CHUNK)]
    return plain + [((g, part), g, None) for g in (1, 2) for part in range(3)]


def _inproj_kernel(x_ref, g1_ref, w_ref, bd_ref, qg_ref, kg_ref, ra_ref, rm_ref, rp_ref, *outs, plan):
    tm = x_ref.shape[0]
    tile_scr = outs[-1]
    xn = _rms_rows(x_ref[...], g1_ref[...]).astype(BF16)
    ra = ra_ref[...]
    rm = rm_ref[...]
    rp = rp_ref[...]
    heavy = [j for j in range(len(plan)) if plan[j][0] is not None][::-1]
    plain = [j for j in range(len(plan)) if plan[j][0] is None]
    order = []
    while heavy or plain:
        take = -(-len(heavy) // max(len(plain), 1))
        order += heavy[:take] + plain[:1]
        heavy, plain = heavy[take:], plain[1:]
    for j in order:
        role, dst, c0 = plan[j]
        o_ref = outs[dst]
        acc = jnp.dot(xn, w_ref[:, j * CHUNK:(j + 1) * CHUNK], preferred_element_type=F32)
        if role is None:
            o_ref[:, c0:c0 + CHUNK] = acc.astype(BF16)
            continue
        g, part = role
        tiles = []
        if part < 2:
            gain = (qg_ref if part == 0 else kg_ref)[g:g + 1, :]
            sq = (acc * acc).astype(BF16)
            bd = bd_ref[...]
            for t in range(CHUNK // MXU_DIM):
                sl = slice(t * MXU_DIM, (t + 1) * MXU_DIM)
                ss = jnp.dot(sq[:, sl], bd, preferred_element_type=F32)
                y = acc[:, sl] * lax.rsqrt(ss * (1.0 / HEAD_DIM) + EPS) * gain[:, sl]
                for u in range(MXU_DIM // LANES):
                    yt = y[:, u * LANES:(u + 1) * LANES]
                    tiles.append(yt * ra + pltpu.roll(yt, LANES - ROT_DIM // 2, 1) * rm
                                 + pltpu.roll(yt, ROT_DIM // 2, 1) * rp)
        else:
            tiles = [acc[:, t * LANES:(t + 1) * LANES] for t in range(CHUNK // LANES)]
        if g == 0:
            for t, val in enumerate(tiles):
                o_ref[:, c0 + t * LANES:c0 + (t + 1) * LANES] = val.astype(BF16)
        else:
            dil = ATT_GROUPS[g][1]
            for t, val in enumerate(tiles):
                tile_scr[t] = val
            for m in range(dil):
                for t in range(CHUNK // LANES):
                    col = (m * 3 + part) * GROUP_COLS + t * LANES
                    o_ref[:, col:col + LANES] = tile_scr[t, pl.ds(m, tm // dil, stride=dil), :].astype(BF16)


def _inproj(x2, seq, g1, w_main, w_rest, bd, qg, kg, ra, rm, rp):
    n = x2.shape[0]
    tm = INPROJ_TOKEN_TILE
    tiles_per_seq = seq // tm
    rope_spec = pl.BlockSpec((tm, LANES), lambda i: (i % tiles_per_seq, 0))

    def call(which, w, widths, dils, scratch, name):
        return pl.pallas_call(
            functools.partial(_inproj_kernel, plan=_inproj_plan(which)),
            grid=(n // tm,),
            in_specs=[
                pl.BlockSpec((tm, D_MODEL), lambda i: (i, 0)),
                _resident((1, D_MODEL)),
                _resident(w.shape),
                _resident((MXU_DIM, MXU_DIM)),
                _resident((N_GROUPS, GROUP_COLS)),
                _resident((N_GROUPS, GROUP_COLS)),
                rope_spec, rope_spec, rope_spec,
            ],
            out_specs=[pl.BlockSpec((tm // d, c * d), lambda i: (i, 0)) for c, d in zip(widths, dils)],
            out_shape=[jax.ShapeDtypeStruct((n // d, c * d), BF16) for c, d in zip(widths, dils)],
            scratch_shapes=scratch,
            compiler_params=_cparams(("parallel",)),
            name=name,
        )(x2, g1, w, bd, qg, kg, ra, rm, rp)

    d1, d2 = ATT_GROUPS[1][1], ATT_GROUPS[2][1]
    main, = call(0, w_main, [MAIN_COLS], [1], [], "inproj_main")
    mgates, qkv1, qkv2 = call(1, w_rest, [MGATE_COLS, 3 * GROUP_COLS, 3 * GROUP_COLS], [1, d1, d2],
                              [pltpu.VMEM((CHUNK // LANES, tm, LANES), F32)], "inproj_rest")
    return main, mgates, qkv1, qkv2


def _lru_kslab0(c):
    return jnp.minimum(jnp.maximum(2 * c - 1, 0), D_RNN // LANES - LRU_KSLABS)


LRU_CONV_ROWS = 128


def _lru_kernel(x_ref, gate_ref, cw_ref, cb_ref, wg_ref, gb_ref, lam_ref, wout_ref, o_ref,
                xc_scr, y_scr, a_scr, u_scr, h_scr, pad_scr):
    c = pl.program_id(1)
    seq = x_ref.shape[0]
    nsl = LRU_CHUNK // LANES
    chains = [(d, i) for d in range(2) for i in range(nsl)]

    @pl.when(c == 0)
    def _conv():
        pad_scr[0:8, :] = jnp.zeros((8, LANES), F32)
        pad_scr[seq + 8:seq + 16, :] = jnp.zeros((8, LANES), F32)
        for j in range(D_RNN // LANES):
            sl = slice(j * LANES, (j + 1) * LANES)
            pad_scr[8:seq + 8, :] = x_ref[:, sl].astype(F32)
            bias = cb_ref[:, sl]
            taps = [cw_ref[k:k + 1, sl] for k in range(LRU_CONV)]

            def conv_body(rb, carry):
                r0 = pl.multiple_of(rb * LRU_CONV_ROWS, LRU_CONV_ROWS)
                xc = bias
                for k in range(LRU_CONV):
                    off = 8 + k - LRU_CONV // 2
                    xc = xc + pad_scr[pl.ds(r0 + off, LRU_CONV_ROWS), :] * taps[k]
                xc_scr[j, pl.ds(r0, LRU_CONV_ROWS), :] = xc
                return carry

            lax.fori_loop(0, seq // LRU_CONV_ROWS, conv_body, 0)

    k0 = _lru_kslab0(c)
    lam = lam_ref[...]
    half_l2 = (-0.5 * LRU_C * LOG2E) * (jnp.maximum(-lam, 0.0) + jnp.log1p(jnp.exp(-jnp.abs(lam))))

    def pitched_rows(s, sb):
        return pl.ds(pl.multiple_of((s * LRU_SUBS_PER_BLOCK + sb) * LRU_PITCH, 8), LRU_SUBLEN)

    def gates_body(s, carry):
        r0 = pl.multiple_of(s * LRU_TBLOCK, LRU_TBLOCK)
        lhs =jnp.concatenate([xc_scr[k0 + i, pl.ds(r0, LRU_TBLOCK), :] for i in range(LRU_KSLABS)],
                              axis=1).astype(BF16)
        half_xc = 0.5 * jnp.concatenate([xc_scr[nsl * c + i, pl.ds(r0, LRU_TBLOCK), :] for i in range(nsl)],
                                        axis=1)
        for d in range(2):
            za = jnp.dot(lhs, wg_ref[d, 0], preferred_element_type=F32) + gb_ref[2 * d:2 * d + 1, :]
            zx = jnp.dot(lhs, wg_ref[d, 1], preferred_element_type=F32) + gb_ref[2 * d + 1:2 * d + 2, :]
            hl = half_l2[d:d + 1, :]
            a = jnp.exp2(jnp.tanh(za) * hl + hl)
            t = 1.0 - a * a
            root = jnp.where(t > 0.0, t * lax.rsqrt(t), 0.0)
            u = root * ((jnp.tanh(zx) + 1.0) * half_xc)
            for i in range(nsl):
                for sb in range(LRU_SUBS_PER_BLOCK):
                    rows = slice(sb * LRU_SUBLEN, (sb + 1) * LRU_SUBLEN)
                    a_scr[d, i, pitched_rows(s, sb), :] = a[rows, i * LANES:(i + 1) * LANES]
                    u_scr[d, i, pitched_rows(s, sb), :] = u[rows, i * LANES:(i + 1) * LANES]
        return carry

    lax.fori_loop(0, seq // LRU_TBLOCK, gates_body, 0, unroll=2)

    def step_rows(d, t):
        r = t if d == 0 else LRU_SUBLEN - 1 - t
        return pl.ds(r, LRU_NSUB, stride=LRU_PITCH)

    def ends_body(t, carry):
        out = []
        for n, (d, i) in enumerate(chains):
            av = a_scr[d, i, step_rows(d, t), :]
            out += [av * carry[2 * n] + u_scr[d, i, step_rows(d, t), :], av * carry[2 * n + 1]]
        return tuple(out)

    init = (jnp.zeros((LRU_NSUB, LANES), F32), jnp.ones((LRU_NSUB, LANES), F32)) * len(chains)
    ends = lax.fori_loop(0, LRU_SUBLEN, ends_body, init, unroll=8)

    sub = lax.broadcasted_iota(jnp.int32, (LRU_NSUB, LANES), 0)
    starts = []
    for n, (d, i) in enumerate(chains):
        h_end, cum_end = ends[2 * n], ends[2 * n + 1]
        cin = jnp.zeros((LRU_NSUB, LANES), F32)
        for step in range(LRU_NSUB - 1):
            nxt = h_end + cum_end * cin
            if d == 0:
                cin = jnp.where(sub == step + 1, pltpu.roll(nxt, 1, 0), cin)
            else:
                cin = jnp.where(sub == LRU_NSUB - 2 - step, pltpu.roll(nxt, LRU_NSUB - 1, 0), cin)
        starts.append(cin)

    def scan_body(t, carry):
        out = []
        for n, (d, i) in enumerate(chains):
            h = a_scr[d, i, step_rows(d, t), :] * carry[n] + u_scr[d, i, step_rows(d, t), :]
            h_scr[d, i, step_rows(d, t), :] = h
            out.append(h)
        return tuple(out)

    lax.fori_loop(0, LRU_SUBLEN, scan_body, tuple(starts), unroll=8)

    def gated_body(s, carry):
        r0 = pl.multiple_of(s * LRU_TBLOCK, LRU_TBLOCK)
        hsum = jnp.concatenate(
            [jnp.concatenate([h_scr[0, i, pitched_rows(s, sb), :] + h_scr[1, i, pitched_rows(s, sb), :]
                              for sb in range(LRU_SUBS_PER_BLOCK)], axis=0) for i in range(nsl)], axis=1)
        y_scr[c, pl.ds(r0, LRU_TBLOCK), :] = (
            _gelu_tanh(gate_ref[pl.ds(r0, LRU_TBLOCK), :].astype(F32)) * hsum).astype(BF16)
        return carry

    lax.fori_loop(0, seq // LRU_TBLOCK, gated_body, 0, unroll=2)

    @pl.when(c == LRU_NCHUNK - 1)
    def _out_projection():
        def proj_body(s, carry):
            r0 = pl.multiple_of(s * LRU_TBLOCK, LRU_TBLOCK)
            y = jnp.concatenate([y_scr[cc, pl.ds(r0, LRU_TBLOCK), :] for cc in range(LRU_NCHUNK)], axis=1)
            o_ref[pl.ds(r0, LRU_TBLOCK), :] = jnp.dot(
                y, wout_ref[...], preferred_element_type=F32).astype(BF16)
            return carry

        lax.fori_loop(0, seq // LRU_TBLOCK, proj_body, 0, unroll=2)


def _lru(proj3, cw, cb, wg, gb, lam, wout):
    b, seq, _ = proj3.shape
    assert seq == LRU_NSUB * LRU_SUBLEN and seq % LRU_TBLOCK == 0
    nsl = LRU_CHUNK // LANES
    gate_blk0 = D_RNN // LRU_CHUNK
    return pl.pallas_call(
        _lru_kernel,
        grid=(b, LRU_NCHUNK),
        in_specs=[
            pl.BlockSpec((None, seq, D_RNN), lambda i, c: (i, 0, 0), pipeline_mode=pl.Buffered(1)),
            pl.BlockSpec((None, seq, LRU_CHUNK), lambda i, c: (i, 0, gate_blk0 + c)),
            _resident((LRU_CONV, D_RNN)),
            _resident((1, D_RNN)),
            pl.BlockSpec((None, 2, 2, LRU_KSLABS * LANES, LRU_CHUNK), lambda i, c: (c, 0, 0, 0, 0)),
            pl.BlockSpec((None, 4, LRU_CHUNK), lambda i, c: (c, 0, 0)),
            pl.BlockSpec((None, 2, LRU_CHUNK), lambda i, c: (c, 0, 0)),
            _resident((D_RNN, D_MODEL)),
        ],
        out_specs=pl.BlockSpec((None, seq, D_MODEL), lambda i, c: (i, 0, 0)),
        out_shape=jax.ShapeDtypeStruct((b, seq, D_MODEL), BF16),
        scratch_shapes=[
            pltpu.VMEM((D_RNN // LANES, seq, LANES), F32),
            pltpu.VMEM((LRU_NCHUNK, seq, LRU_CHUNK), BF16),
            pltpu.VMEM((2, nsl, LRU_NSUB * LRU_PITCH, LANES), F32),
            pltpu.VMEM((2, nsl, LRU_NSUB * LRU_PITCH, LANES), F32),
            pltpu.VMEM((2, nsl, LRU_NSUB * LRU_PITCH, LANES), F32),
            pltpu.VMEM((seq + 16, LANES), F32),
        ],
        compiler_params=_cparams(("parallel", "arbitrary")),
        name="rglru",
    )(proj3, proj3, cw, cb, wg, gb, lam, wout)


def _attn_kernel(*refs, rpb, fused, row_max):
    n_in = 1 if fused else 3
    qkv_refs = refs[:n_in]
    bias_ref, shift_ref, o_ref, lse_ref, kt_scr = refs[n_in:]
    length = o_ref.shape[0]
    nkb = length // ATT_QBLOCK
    lane = lax.broadcasted_iota(jnp.int32, (ATT_QBLOCK, LANES), 1)
    lo_half = lane < HEAD_DIM
    lo_keys = lax.broadcasted_iota(jnp.int32, (2 * ATT_QBLOCK, LANES), 1) < HEAD_DIM
    npairs = GROUP_COLS // LANES

    def scores_block(qp_of, kt_of, vw_of, bias):
        outs = []
        lse_tile = jnp.zeros((ATT_QBLOCK, LANES), F32)
        for p in range(npairs):
            qp, kt, vw = qp_of(p), kt_of(p), vw_of(p)
            full, shifts = [], []
            for hh in range(LANES // HEAD_DIM):
                own = lo_half if hh == 0 else jnp.logical_not(lo_half)
                qm = jnp.where(own, qp, jnp.zeros_like(qp))
                s = jnp.dot(qm, kt, preferred_element_type=F32) + bias
                if row_max:
                    m = jnp.max(s, axis=-1, keepdims=True)
                    s = s - m
                    shifts.append(m)
                e = jnp.exp2(s).astype(BF16)
                v1 = jnp.where(lo_keys if hh == 0 else jnp.logical_not(lo_keys), vw, jnp.ones_like(vw))
                full.append(jnp.dot(e, v1, preferred_element_type=F32))
            numer = jnp.where(lo_half, full[0], full[1])
            denom = pltpu.roll(jnp.where(lo_half, full[1], full[0]), HEAD_DIM, 1)
            outs.append(numer * (1.0 / denom))
            shift = jnp.where(lo_half, shifts[0], shifts[1]) if row_max else shift_ref[...]
            lse = LN2 * (shift + jnp.log2(denom))
            lse_tile = jnp.where(jnp.logical_or(lane == p, lane == HEAD_DIM + p), lse, lse_tile)
        return outs, lse_tile

    for r in range(rpb):
        def cols(part, p, r=r):
            base = ((r * 3 + part) * GROUP_COLS if fused else 0) + p * LANES
            return slice(base, base + LANES)

        q_ref, k_ref, v_ref = (qkv_refs[0],) * 3 if fused else qkv_refs

        def transpose_body(kb, carry):
            k0 = pl.multiple_of(kb * ATT_QBLOCK, ATT_QBLOCK)
            for p in range(npairs):
                kt_scr[p, kb] = k_ref[pl.ds(k0, ATT_QBLOCK), cols(1, p)].T
            return carry

        lax.fori_loop(0, nkb, transpose_body, 0, unroll=min(nkb, 4))

        def store(rows_list, outs, lse_tile, r=r):
            for dst_rows, src in rows_list:
                for p in range(npairs):
                    c0 = r * GROUP_COLS + p * LANES
                    o_ref[dst_rows, c0:c0 + LANES] = outs[p][src].astype(BF16)
                lse_ref[dst_rows, r * LANES:(r + 1) * LANES] = lse_tile[src]

        def interior(i, carry):
            k0 = pl.multiple_of(i * ATT_QBLOCK, ATT_QBLOCK)
            q0 = pl.multiple_of(k0 + ATT_HALF, ATT_HALF)
            outs, lse_tile = scores_block(
                lambda p: q_ref[pl.ds(q0, ATT_QBLOCK), cols(0, p)],
                lambda p: jnp.concatenate([kt_scr[p, i], kt_scr[p, i + 1]], axis=1),
                lambda p: v_ref[pl.ds(k0, 2 * ATT_QBLOCK), cols(2, p)],
                bias_ref[0])
            store([(pl.ds(q0, ATT_QBLOCK), slice(None))], outs, lse_tile)
            return carry

        if nkb > 1:
            lax.fori_loop(0, nkb - 1, interior, 0, unroll=3)

        head_rows, tail_rows = slice(0, ATT_HALF), slice(length - ATT_HALF, length)
        tail_keys = slice(length - ATT_QBLOCK, length)
        outs, lse_tile = scores_block(
            lambda p: jnp.concatenate([q_ref[head_rows, cols(0, p)], q_ref[tail_rows, cols(0, p)]], axis=0),
            lambda p: jnp.concatenate([kt_scr[p, 0], kt_scr[p, nkb - 1]], axis=1),
            lambda p: jnp.concatenate([v_ref[0:ATT_QBLOCK, cols(2, p)], v_ref[tail_keys, cols(2, p)]], axis=0),
            bias_ref[1])
        store([(head_rows, slice(0, ATT_HALF)), (tail_rows, slice(ATT_HALF, ATT_QBLOCK))], outs, lse_tile)


def _attn_bias():
    i = jnp.arange(ATT_QBLOCK)[:, None]
    j = jnp.arange(2 * ATT_QBLOCK)[None, :]
    interior = jnp.abs(i + ATT_HALF - j) <= ATT_HALF
    first = (i < ATT_HALF) & (j < ATT_QBLOCK) & (jnp.abs(i - j) <= ATT_HALF)
    last = (i >= ATT_HALF) & (j >= ATT_QBLOCK) & (jnp.abs(i - (j - ATT_QBLOCK)) <= ATT_HALF)
    return jnp.stack([jnp.where(interior, 0.0, NEG), jnp.where(first | last, 0.0, NEG)]).astype(F32)


def _attention_group(qkv, b, seq, window, dilation, score_bound):
    length = seq // dilation
    assert (window // 2) // dilation == ATT_HALF and length % ATT_QBLOCK == 0
    pv = qkv.reshape(b, length, qkv.shape[1])
    fused = dilation > 1
    rpb = min(dilation, max(1, ATT_MIN_STEP_TOKENS // length))
    if fused:
        qkv_specs = [pl.BlockSpec((None, length, rpb * 3 * GROUP_COLS), lambda i, m: (i, 0, m))]
    else:
        qkv_specs = [pl.BlockSpec((None, length, GROUP_COLS), lambda i, m, part=part: (i, 0, QKV0_CHUNK0 + part))
                     for part in range(3)]

    def call(row_max, bias, shift):
        return pl.pallas_call(
            functools.partial(_attn_kernel, rpb=rpb, fused=fused, row_max=row_max),
            grid=(b, dilation // rpb),
            in_specs=qkv_specs + [_resident(bias.shape), _resident(shift.shape)],
            out_specs=[
                pl.BlockSpec((None, length, rpb * GROUP_COLS), lambda i, m: (i, 0, m)),
                pl.BlockSpec((None, length, rpb * LANES), lambda i, m: (i, 0, m)),
            ],
            out_shape=[
                jax.ShapeDtypeStruct((b, length, dilation * GROUP_COLS), BF16),
                jax.ShapeDtypeStruct((b, length, dilation * LANES), F32),
            ],
            scratch_shapes=[pltpu.VMEM((GROUP_COLS // LANES, length // ATT_QBLOCK, LANES, ATT_QBLOCK), BF16)],
            compiler_params=_cparams(("parallel", "parallel")),
            name=f"attn_d{dilation}" + ("_rowmax" if row_max else ""),
        )(*([pv] * len(qkv_specs)), bias, shift)

    bias = _attn_bias()
    shift = jnp.full((1, LANES), score_bound, F32)
    o, lse = lax.cond(score_bound <= ATT_MAX_CONST_SHIFT,
                      lambda: call(False, bias - score_bound, shift),
                      lambda: call(True, bias, shift))
    return o.reshape(b * length, dilation * GROUP_COLS), lse.reshape(b * length, dilation * LANES)


def _merge_kernel(x_ref, a_ref, o0_ref, o1_ref, o2_ref, l0_ref, l1_ref, l2_ref, ga_ref, gb_ref,
                  watt_ref, wo_ref, ex_ref, y_ref, o_scr, l_scr):
    tm = x_ref.shape[0]
    nt = GROUP_COLS // LANES
    for gi, (og_ref, lg_ref) in enumerate(((o1_ref, l1_ref), (o2_ref, l2_ref))):
        dil = ATT_GROUPS[gi + 1][1]
        for m in range(dil):
            rows = pl.ds(m, tm // dil, stride=dil)
            l_scr[gi, rows, :] = lg_ref[:, m * LANES:(m + 1) * LANES]
            for t in range(nt):
                col = m * GROUP_COLS + t * LANES
                o_scr[gi, t, rows, :] = og_ref[:, col:col + LANES].astype(F32)
    lses = [l0_ref[...], l_scr[0], l_scr[1]]
    group_out = [o0_ref[...].astype(F32)] + [
        jnp.concatenate([o_scr[gi, t] for t in range(nt)], axis=1) for gi in range(2)]
    mx = jnp.maximum(jnp.maximum(lses[0], lses[1]), lses[2])
    es = [jnp.exp(l - mx) for l in lses]
    den = es[0] + es[1] + es[2]
    ex = ex_ref[...]
    o = None
    for e, og in zip(es, group_out):
        w = e / den
        hi = w.astype(BF16)
        lo = (w - hi.astype(F32)).astype(BF16)
        wx = jnp.dot(hi, ex, preferred_element_type=F32) + jnp.dot(lo, ex, preferred_element_type=F32)
        t = wx * og
        o = t if o is None else o + t
    b_out = jnp.dot(o.astype(BF16), watt_ref[...], preferred_element_type=F32)
    mix = (_sigmoid(ga_ref[...].astype(F32)) * a_ref[...].astype(F32)
           + _sigmoid(gb_ref[...].astype(F32)) * b_out)
    y_ref[...] = x_ref[...] + jnp.dot(mix.astype(BF16), wo_ref[...], preferred_element_type=F32)


def _merge(x2, a_out, outs, lses, mgates, watt, wo, ex):
    n = x2.shape[0]
    tm = TOKEN_TILE
    row = lambda cols, dil=1: pl.BlockSpec((tm // dil, cols * dil), lambda i: (i, 0))
    d1, d2 = ATT_GROUPS[1][1], ATT_GROUPS[2][1]
    return pl.pallas_call(
        _merge_kernel,
        grid=(n // tm,),
        in_specs=[
            row(D_MODEL), row(D_MODEL),
            row(GROUP_COLS), row(GROUP_COLS, d1), row(GROUP_COLS, d2),
            row(LANES), row(LANES, d1), row(LANES, d2),
            pl.BlockSpec((tm, D_MODEL), lambda i: (i, 0)),
            pl.BlockSpec((tm, D_MODEL), lambda i: (i, 1)),
            _resident((GROUP_COLS, D_MODEL)),
            _resident((D_MODEL, D_MODEL)),
            _resident((LANES, GROUP_COLS)),
        ],
        out_specs=row(D_MODEL),
        out_shape=jax.ShapeDtypeStruct((n, D_MODEL), F32),
        scratch_shapes=[
            pltpu.VMEM((2, GROUP_COLS // LANES, tm, LANES), F32),
            pltpu.VMEM((2, tm, LANES), F32),
        ],
        compiler_params=_cparams(("parallel",)),
        name="merge",
    )(x2, a_out, *outs, *lses, mgates, mgates, watt, wo, ex)


FFN_HALO = 16
FFN_TOKEN_TILE = 1024
FFN_CHUNK = 1024


def _ffn_kernel(x_ref, xp_ref, xn_ref, g2_ref, wup_ref, cw_ref, cb_ref, wdn_ref, y_ref, lhs_scr,
                *, tiles_per_seq):
    tm = x_ref.shape[0]
    ti = pl.program_id(0) % tiles_per_seq
    g2 = g2_ref[...]
    x = x_ref[...]
    keep_prev = jnp.where(ti == 0, 0.0, 1.0)
    keep_next = jnp.where(ti == tiles_per_seq - 1, 0.0, 1.0)
    xn = _rms_rows(x, g2).astype(BF16)
    lhs_scr[0:FFN_HALO, :] = (_rms_rows(xp_ref[...], g2) * keep_prev).astype(BF16)
    lhs_scr[FFN_HALO:FFN_HALO + tm, :] = xn
    lhs_scr[FFN_HALO + tm:, :] = (_rms_rows(xn_ref[...], g2) * keep_next).astype(BF16)
    acc = x
    for j in range(D_FF // FFN_CHUNK):
        sl = slice(j * FFN_CHUNK, (j + 1) * FFN_CHUNK)
        gext =jnp.dot(lhs_scr[...], wup_ref[:, sl], preferred_element_type=F32)
        rows = gext.shape[0]
        gc = cb_ref[:, sl]
        for k in range(FFN_CONV):
            shift = (FFN_CONV // 2 - k) % rows
            gk = gext if shift == 0 else pltpu.roll(gext, shift, 0)
            gc = gc + gk[FFN_HALO:FFN_HALO + tm, :] * cw_ref[k:k + 1, sl]
        val = jnp.dot(xn, wup_ref[:, D_FF + j * FFN_CHUNK:D_FF + (j + 1) * FFN_CHUNK],
                      preferred_element_type=F32)
        h = (_gelu_tanh(gc) * val).astype(BF16)
        acc = acc + jnp.dot(h, wdn_ref[sl, :], preferred_element_type=F32)
    y_ref[...] = acc


def _ffn(x2, seq, g2, wup, cw, cb, wdn):
    n = x2.shape[0]
    tm = FFN_TOKEN_TILE
    hb = tm // FFN_HALO
    last = n // FFN_HALO - 1
    return pl.pallas_call(
        functools.partial(_ffn_kernel, tiles_per_seq=seq // tm),
        grid=(n // tm,),
        in_specs=[
            pl.BlockSpec((tm, D_MODEL), lambda i: (i, 0)),
            pl.BlockSpec((FFN_HALO, D_MODEL), lambda i: (jnp.maximum(i * hb - 1, 0), 0)),
            pl.BlockSpec((FFN_HALO, D_MODEL), lambda i: (jnp.minimum((i + 1) * hb, last), 0)),
            _resident((1, D_MODEL)),
            _resident((D_MODEL, 2 * D_FF)),
            _resident((FFN_CONV, D_FF)),
            _resident((1, D_FF)),
            _resident((D_FF, D_MODEL)),
        ],
        out_specs=pl.BlockSpec((tm, D_MODEL), lambda i: (i, 0)),
        out_shape=jax.ShapeDtypeStruct((n, D_MODEL), F32),
        scratch_shapes=[pltpu.VMEM((tm + 2 * FFN_HALO, D_MODEL), BF16)],
        compiler_params=_cparams(("parallel",)),
        name="convffn",
    )(x2, x2, x2, g2, wup, cw, cb, wdn)


def _rope_tables(seq):
    pos = jnp.arange(seq, dtype=F32)
    inv = ROPE_THETA ** (-jnp.arange(0, ROT_DIM, 2, dtype=F32) / ROT_DIM)
    ang = pos[:, None] * inv[None, :]
    cos, sin = jnp.cos(ang), jnp.sin(ang)
    half = ROT_DIM // 2
    pad = HEAD_DIM - ROT_DIM
    one = jnp.ones((seq, pad), F32)
    zero = jnp.zeros((seq, pad), F32)
    zh = jnp.zeros((seq, half), F32)
    per_head = lambda parts: jnp.tile(jnp.concatenate(parts, axis=1), (1, LANES // HEAD_DIM))
    ra = per_head([cos, cos, one])
    rm = per_head([-sin, zh, zero])
    rp = per_head([zh, sin, zero])
    return ra, rm, rp


def _block_diag_dense(w):
    nb, c, d = w.shape
    return jnp.einsum('ncd,nm->ncmd', w, jnp.eye(nb, dtype=w.dtype)).reshape(nb * c, nb * d)


def _prep_layer(p, seq):
    q = {}
    q['g1'] = p['norm1_g'].reshape(1, D_MODEL)
    q_lo = 2 * D_RNN
    seg = lambda part, g: (q_lo + part * ATT_COLS + g * GROUP_COLS, q_lo + part * ATT_COLS + (g + 1) * GROUP_COLS)
    qkv = lambda g: [seg(part, g) for part in range(3)]
    w_in = p['w_in'].astype(BF16)
    take = lambda segs: jnp.concatenate([w_in[:, lo:hi] for lo, hi in segs], axis=1)
    q['w_main'] = take([(0, q_lo)] + qkv(0))
    q['w_rest'] = take([(q_lo + 3 * ATT_COLS, IN_COLS)] + qkv(1) + qkv(2))
    head = jnp.arange(MXU_DIM) // HEAD_DIM
    q['bd'] = (head[:, None] == head[None, :]).astype(BF16)
    q['qg'] = jnp.tile(p['q_norm_g'], (1, HEADS_PER_GROUP)) * (HEAD_DIM ** -0.5 * LOG2E)
    q['kg'] = jnp.tile(p['k_norm_g'], (1, HEADS_PER_GROUP))
    q['score_bound'] = (HEAD_DIM * ATT_BOUND_MARGIN) * (jnp.max(jnp.abs(q['qg']), axis=1)
                                                        * jnp.max(jnp.abs(q['kg']), axis=1))
    q['rope'] = _rope_tables(seq)
    q['lru_cw'] = p['lru_conv_w']
    q['lru_cb'] = p['lru_conv_b'].reshape(1, D_RNN)
    dense = jnp.stack([jnp.stack([_block_diag_dense(p['lru_wa'][d]), _block_diag_dense(p['lru_wx'][d])])
                       for d in range(2)])
    dense = (0.5 * dense).astype(BF16)
    chunks = []
    for c in range(LRU_NCHUNK):
        k0 = min(max(2 * c - 1, 0), D_RNN // LANES - LRU_KSLABS) * LANES
        chunks.append(dense[:, :, k0:k0 + LRU_KSLABS * LANES, c * LRU_CHUNK:(c + 1) * LRU_CHUNK])
    q['lru_wg'] = jnp.stack(chunks)
    gb = 0.5 * jnp.stack([p['lru_ba'][0], p['lru_bx'][0], p['lru_ba'][1], p['lru_bx'][1]])
    q['lru_gb'] = gb.reshape(4, LRU_NCHUNK, LRU_CHUNK).transpose(1, 0, 2)
    q['lru_lam'] = p['lru_lambda'].reshape(2, LRU_NCHUNK, LRU_CHUNK).transpose(1, 0, 2)
    q['lru_wout'] = p['w_lru_out'].astype(BF16)
    q['watt'] = p['w_att_out'].astype(BF16)
    q['wo'] = p['w_o'].astype(BF16)
    lse_lane = jnp.arange(LANES)
    lane_head = jnp.where(lse_lane % HEAD_DIM < GROUP_COLS // LANES,
                          2 * (lse_lane % HEAD_DIM) + lse_lane // HEAD_DIM, -1)
    q['expand'] = (lane_head[:, None] == jnp.arange(GROUP_COLS)[None, :] // HEAD_DIM).astype(BF16)
    q['g2'] = p['norm2_g'].reshape(1, D_MODEL)
    q['wup'] = p['w_up'].astype(BF16)
    q['ffn_cw'] = p['ffn_conv_w']
    q['ffn_cb'] = p['ffn_conv_b'].reshape(1, D_FF)
    q['wdn'] = p['w_down'].astype(BF16)
    return q


def _layer(x, q):
    b, seq, _ = x.shape
    x2 = x.reshape(b * seq, D_MODEL)
    main, mgates, qkv1, qkv2 = _inproj(x2, seq, q['g1'], q['w_main'], q['w_rest'], q['bd'], q['qg'], q['kg'],
                                       *q['rope'])
    a_out = _lru(main.reshape(b, seq, MAIN_COLS), q['lru_cw'], q['lru_cb'], q['lru_wg'], q['lru_gb'],
                 q['lru_lam'], q['lru_wout'])
    outs, lses = [], []
    for g, (qkv, (window, dilation)) in enumerate(zip((main, qkv1, qkv2), ATT_GROUPS)):
        o, lse = _attention_group(qkv, b, seq, window, dilation, q['score_bound'][g])
        outs.append(o)
        lses.append(lse)
    x1 = _merge(x2, a_out.reshape(b * seq, D_MODEL), outs, lses, mgates, q['watt'], q['wo'], q['expand'])
    y = _ffn(x1, seq, q['g2'], q['wup'], q['ffn_cw'], q['ffn_cb'], q['wdn'])
    return y.reshape(b, seq, D_MODEL)


def kernel(x_prompt, x_sample, norm1_g, w_in, lru_conv_w, lru_conv_b, lru_wa, lru_ba, lru_wx, lru_bx,
           lru_lambda, w_lru_out, q_norm_g, k_norm_g, w_att_out, w_o, norm2_g, w_up, ffn_conv_w,
           ffn_conv_b, w_down):
    params = dict(norm1_g=norm1_g, w_in=w_in, lru_conv_w=lru_conv_w, lru_conv_b=lru_conv_b,
                  lru_wa=lru_wa, lru_ba=lru_ba, lru_wx=lru_wx, lru_bx=lru_bx, lru_lambda=lru_lambda,
                  w_lru_out=w_lru_out, q_norm_g=q_norm_g, k_norm_g=k_norm_g, w_att_out=w_att_out,
                  w_o=w_o, norm2_g=norm2_g, w_up=w_up, ffn_conv_w=ffn_conv_w, ffn_conv_b=ffn_conv_b,
                  w_down=w_down)
    depth = norm1_g.shape[0]
    seq = x_prompt.shape[1]
    assert x_sample.shape[1] == seq
    layers = [_prep_layer({name: p[l] for name, p in params.items()}, seq) for l in range(depth)]
    ys = []
    for x in (x_prompt, x_sample):
        for q in layers:
            x = _layer(x, q)
        ys.append(x)
    return tuple(ys)
```

```python
import functools

import jax
import jax.numpy as jnp
from jax import lax
from jax.experimental import pallas as pl
from jax.experimental.pallas import tpu as pltpu

F32 = jnp.float32
BF16 = jnp.bfloat16

D_MODEL = 1024
D_RNN = 1280
LRU_BLOCKS = 16
LRU_BW = D_RNN // LRU_BLOCKS
LRU_C = 8.0
LRU_CONV = 4
ATT_GROUPS = ((128, 1), (512, 4), (2048, 16))
N_GROUPS = len(ATT_GROUPS)
HEADS_PER_GROUP = 8
HEAD_DIM = 64
GROUP_COLS = HEADS_PER_GROUP * HEAD_DIM
ATT_COLS = N_GROUPS * GROUP_COLS
ROT_DIM = HEAD_DIM // 4
ROPE_THETA = 500000.0
D_FF = 3 * D_MODEL
FFN_CONV = 3
EPS = 1e-6
NEG = -1e30
LOG2E = 1.4426950408889634
LN2 = 0.6931471805599453
IN_COLS = 2 * D_RNN + 3 * ATT_COLS + 2 * D_MODEL

LANES = 128
MXU_DIM = 256
VMEM_LIMIT_BYTES = 56 * 1024 * 1024
LRU_VMEM_LIMIT_BYTES = 58 * 1024 * 1024

CHUNK = 512
MAIN_COLS = 2 * D_RNN + 3 * GROUP_COLS
QKV0_CHUNK0 = 2 * D_RNN // CHUNK
MGATE_COLS = 2 * D_MODEL

TOKEN_TILE = 512
INPROJ_TOKEN_TILE = 1024
ATT_QBLOCK = 128
ATT_HALF = 64
ATT_MIN_STEP_TOKENS = 512
ATT_MAX_CONST_SHIFT = 60.0
ATT_BOUND_MARGIN = 1.02

LRU_CHUNK = 256
LRU_NCHUNK = D_RNN // LRU_CHUNK
LRU_KSLABS = 4
LRU_TBLOCK = 256
LRU_SUBLEN = 128
LRU_NSUB = 16
LRU_PITCH = LRU_SUBLEN + 8
LRU_SUBS_PER_BLOCK = LRU_TBLOCK // LRU_SUBLEN


def _cparams(sem, vmem_limit=VMEM_LIMIT_BYTES):
    return pltpu.CompilerParams(dimension_semantics=sem, vmem_limit_bytes=vmem_limit)


def _resident(shape):
    return pl.BlockSpec(shape, lambda *_: (0,) * len(shape), pipeline_mode=pl.Buffered(1))


def _sigmoid(x):
    return 0.5 * jnp.tanh(0.5 * x) + 0.5


def _gelu_tanh(x):
    return 0.5 * x * (1.0 + jnp.tanh(0.7978845608028654 * (x + 0.044715 * (x * x * x))))


def _rms_rows(x, g):
    ms = jnp.mean(x * x, axis=-1, keepdims=True)
    return x * lax.rsqrt(ms + EPS) * g


def _inproj_plan(call):
    if call == 0:
        return [((0, j - QKV0_CHUNK0) if j >= QKV0_CHUNK0 else None, 0, j * CHUNK)
                for j in range(MAIN_COLS // CHUNK)]
    plain = [(None, 0, j * CHUNK) for j in range(MGATE_COLS // CHUNK)]
    return plain + [((g, part), g, None) for g in (1, 2) for part in range(3)]


def _inproj_kernel(x_ref, g1_ref, w_ref, bd_ref, qg_ref, kg_ref, ra_ref, rm_ref, rp_ref, *outs, plan):
    tm = x_ref.shape[0]
    tile_scr = outs[-1]
    xn = _rms_rows(x_ref[...], g1_ref[...]).astype(BF16)
    ra = ra_ref[...]
    rm = rm_ref[...]
    rp = rp_ref[...]
    heavy = [j for j in range(len(plan)) if plan[j][0] is not None][::-1]
    plain = [j for j in range(len(plan)) if plan[j][0] is None]
    order = []
    while heavy or plain:
        take = -(-len(heavy) // max(len(plain), 1))
        order += heavy[:take] + plain[:1]
        heavy, plain = heavy[take:], plain[1:]
    for j in order:
        role, dst, c0 = plan[j]
        o_ref = outs[dst]
        acc = jnp.dot(xn, w_ref[:, j * CHUNK:(j + 1) * CHUNK], preferred_element_type=F32)
        if role is None:
            o_ref[:, c0:c0 + CHUNK] = acc.astype(BF16)
            continue
        g, part = role
        tiles = []
        if part < 2:
            gain = (qg_ref if part == 0 else kg_ref)[g:g + 1, :]
            sq = (acc * acc).astype(BF16)
            bd = bd_ref[...]
            for t in range(CHUNK // MXU_DIM):
                sl = slice(t * MXU_DIM, (t + 1) * MXU_DIM)
                ss = jnp.dot(sq[:, sl], bd, preferred_element_type=F32)
                y = acc[:, sl] * lax.rsqrt(ss * (1.0 / HEAD_DIM) + EPS) * gain[:, sl]
                for u in range(MXU_DIM // LANES):
                    yt = y[:, u * LANES:(u + 1) * LANES]
                    tiles.append(yt * ra + pltpu.roll(yt, LANES - ROT_DIM // 2, 1) * rm
                                 + pltpu.roll(yt, ROT_DIM // 2, 1) * rp)
        else:
            tiles = [acc[:, t * LANES:(t + 1) * LANES] for t in range(CHUNK // LANES)]
        if g == 0:
            for t, val in enumerate(tiles):
                o_ref[:, c0 + t * LANES:c0 + (t + 1) * LANES] = val.astype(BF16)
        else:
            dil = ATT_GROUPS[g][1]
            for t, val in enumerate(tiles):
                tile_scr[t] = val
            for m in range(dil):
                for t in range(CHUNK // LANES):
                    col = (m * 3 + part) * GROUP_COLS + t * LANES
                    o_ref[:, col:col + LANES] = tile_scr[t, pl.ds(m, tm // dil, stride=dil), :].astype(BF16)


def _inproj(x2, seq, g1, w_main, w_rest, bd, qg, kg, ra, rm, rp):
    n = x2.shape[0]
    tm = INPROJ_TOKEN_TILE
    tiles_per_seq = seq // tm
    rope_spec = pl.BlockSpec((tm, LANES), lambda i: (i % tiles_per_seq, 0))

    def call(which, w, widths, dils, scratch, name):
        return pl.pallas_call(
            functools.partial(_inproj_kernel, plan=_inproj_plan(which)),
            grid=(n // tm,),
            in_specs=[
                pl.BlockSpec((tm, D_MODEL), lambda i: (i, 0)),
                _resident((1, D_MODEL)),
                _resident(w.shape),
                _resident((MXU_DIM, MXU_DIM)),
                _resident((N_GROUPS, GROUP_COLS)),
                _resident((N_GROUPS, GROUP_COLS)),
                rope_spec, rope_spec, rope_spec,
            ],
            out_specs=[pl.BlockSpec((tm // d, c * d), lambda i: (i, 0)) for c, d in zip(widths, dils)],
            out_shape=[jax.ShapeDtypeStruct((n // d, c * d), BF16) for c, d in zip(widths, dils)],
            scratch_shapes=scratch,
            compiler_params=_cparams(("parallel",)),
            name=name,
        )(x2, g1, w, bd, qg, kg, ra, rm, rp)

    d1, d2 = ATT_GROUPS[1][1], ATT_GROUPS[2][1]
    main, = call(0, w_main, [MAIN_COLS], [1], [], "inproj_main")
    mgates, qkv1, qkv2 = call(1, w_rest, [MGATE_COLS, 3 * GROUP_COLS, 3 * GROUP_COLS], [1, d1, d2],
                              [pltpu.VMEM((CHUNK // LANES, tm, LANES), F32)], "inproj_rest")
    return main, mgates, qkv1, qkv2


def _lru_kslab0(c):
    return jnp.minimum(jnp.maximum(2 * c - 1, 0), D_RNN // LANES - LRU_KSLABS)


LRU_CONV_ROWS = 128


def _lru_kernel(x_ref, gate_ref, cw_ref, cb_ref, wg_ref, gb_ref, lam_ref, wout_ref, o_ref,
                xc_scr, acc_scr, a_scr, u_scr, h_scr, pad_scr):
    c = pl.program_id(1)
    seq = x_ref.shape[0]
    nsl = LRU_CHUNK // LANES
    chains = [(d, i) for d in range(2) for i in range(nsl)]

    @pl.when(c == 0)
    def _conv():
        pad_scr[0:8, :] = jnp.zeros((8, LANES), F32)
        pad_scr[seq + 8:seq + 16, :] = jnp.zeros((8, LANES), F32)
        for j in range(D_RNN // LANES):
            sl = slice(j * LANES, (j + 1) * LANES)
            pad_scr[8:seq + 8, :] = x_ref[:, sl].astype(F32)
            bias = cb_ref[:, sl]
            taps = [cw_ref[k:k + 1, sl] for k in range(LRU_CONV)]

            def conv_body(rb, carry):
                r0 = pl.multiple_of(rb * LRU_CONV_ROWS, LRU_CONV_ROWS)
                xc = bias
                for k in range(LRU_CONV):
                    off = 8 + k - LRU_CONV // 2
                    xc = xc + pad_scr[pl.ds(r0 + off, LRU_CONV_ROWS), :] * taps[k]
                xc_scr[j, pl.ds(r0, LRU_CONV_ROWS), :] = xc
                return carry

            lax.fori_loop(0, seq // LRU_CONV_ROWS, conv_body, 0)

    k0 = _lru_kslab0(c)
    lam = lam_ref[...]
    half_l2 = (-0.5 * LRU_C * LOG2E) * (jnp.maximum(-lam, 0.0) + jnp.log1p(jnp.exp(-jnp.abs(lam))))

    def pitched_rows(s, sb):
        return pl.ds(pl.multiple_of((s * LRU_SUBS_PER_BLOCK + sb) * LRU_PITCH, 8), LRU_SUBLEN)

    def gates_body(s, carry):
        r0 = pl.multiple_of(s * LRU_TBLOCK, LRU_TBLOCK)
        lhs =jnp.concatenate([xc_scr[k0 + i, pl.ds(r0, LRU_TBLOCK), :] for i in range(LRU_KSLABS)],
                              axis=1).astype(BF16)
        half_xc = 0.5 * jnp.concatenate([xc_scr[nsl * c + i, pl.ds(r0, LRU_TBLOCK), :] for i in range(nsl)],
                                        axis=1)
        for d in range(2):
            za = jnp.dot(lhs, wg_ref[d, 0], preferred_element_type=F32) + gb_ref[2 * d:2 * d + 1, :]
            zx = jnp.dot(lhs, wg_ref[d, 1], preferred_element_type=F32) + gb_ref[2 * d + 1:2 * d + 2, :]
            hl = half_l2[d:d + 1, :]
            a = jnp.exp2(jnp.tanh(za) * hl + hl)
            t = 1.0 - a * a
            root = jnp.where(t > 0.0, t * lax.rsqrt(t), 0.0)
            u = root * ((jnp.tanh(zx) + 1.0) * half_xc)
            for i in range(nsl):
                for sb in range(LRU_SUBS_PER_BLOCK):
                    rows = slice(sb * LRU_SUBLEN, (sb + 1) * LRU_SUBLEN)
                    a_scr[d, i, pitched_rows(s, sb), :] = a[rows, i * LANES:(i + 1) * LANES]
                    u_scr[d, i, pitched_rows(s, sb), :] = u[rows, i * LANES:(i + 1) * LANES]
        return carry

    lax.fori_loop(0, seq // LRU_TBLOCK, gates_body, 0, unroll=4)

    def step_rows(d, t):
        r = t if d == 0 else LRU_SUBLEN - 1 - t
        return pl.ds(r, LRU_NSUB, stride=LRU_PITCH)

    def ends_body(t, carry):
        out = []
        for n, (d, i) in enumerate(chains):
            av = a_scr[d, i, step_rows(d, t), :]
            out += [av * carry[2 * n] + u_scr[d, i, step_rows(d, t), :], av * carry[2 * n + 1]]
        return tuple(out)

    init = (jnp.zeros((LRU_NSUB, LANES), F32), jnp.ones((LRU_NSUB, LANES), F32)) * len(chains)
    ends = lax.fori_loop(0, LRU_SUBLEN, ends_body, init, unroll=8)

    sub = lax.broadcasted_iota(jnp.int32, (LRU_NSUB, LANES), 0)
    starts = []
    for n, (d, i) in enumerate(chains):
        h_end, cum_end = ends[2 * n], ends[2 * n + 1]
        cin = jnp.zeros((LRU_NSUB, LANES), F32)
        for step in range(LRU_NSUB - 1):
            nxt = h_end + cum_end * cin
            if d == 0:
                cin = jnp.where(sub == step + 1, pltpu.roll(nxt, 1, 0), cin)
            else:
                cin = jnp.where(sub == LRU_NSUB - 2 - step, pltpu.roll(nxt, LRU_NSUB - 1, 0), cin)
        starts.append(cin)

    def scan_body(t, carry):
        out = []
        for n, (d, i) in enumerate(chains):
            h = a_scr[d, i, step_rows(d, t), :] * carry[n] + u_scr[d, i, step_rows(d, t), :]
            h_scr[d, i, step_rows(d, t), :] = h
            out.append(h)
        return tuple(out)

    lax.fori_loop(0, LRU_SUBLEN, scan_body, tuple(starts), unroll=8)

    def out_body(first, s, carry):
        r0 = pl.multiple_of(s * LRU_TBLOCK, LRU_TBLOCK)
        hsum = jnp.concatenate(
            [jnp.concatenate([h_scr[0, i, pitched_rows(s, sb), :] + h_scr[1, i, pitched_rows(s, sb), :]
                              for sb in range(LRU_SUBS_PER_BLOCK)], axis=0) for i in range(nsl)], axis=1)
        y = (_gelu_tanh(gate_ref[pl.ds(r0, LRU_TBLOCK), :].astype(F32)) * hsum).astype(BF16)
        part = jnp.dot(y, wout_ref[...], preferred_element_type=F32)
        if first:
            acc_scr[pl.ds(r0, LRU_TBLOCK), :] = part
        else:
            acc_scr[pl.ds(r0, LRU_TBLOCK), :] += part
        return carry

    @pl.when(c == 0)
    def _out_first():
        lax.fori_loop(0, seq // LRU_TBLOCK, functools.partial(out_body, True), 0, unroll=4)

    @pl.when(c > 0)
    def _out_rest():
        lax.fori_loop(0, seq // LRU_TBLOCK, functools.partial(out_body, False), 0, unroll=4)

    @pl.when(c == LRU_NCHUNK - 1)
    def _emit():
        o_ref[...] = acc_scr[...].astype(BF16)


def _lru(proj3, cw, cb, wg, gb, lam, wout):
    b, seq, _ = proj3.shape
    assert seq == LRU_NSUB * LRU_SUBLEN and seq % LRU_TBLOCK == 0
    nsl = LRU_CHUNK // LANES
    gate_blk0 = D_RNN // LRU_CHUNK
    return pl.pallas_call(
        _lru_kernel,
        grid=(b, LRU_NCHUNK),
        in_specs=[
            pl.BlockSpec((None, seq, D_RNN), lambda i, c: (i, 0, 0)),
            pl.BlockSpec((None, seq, LRU_CHUNK), lambda i, c: (i, 0, gate_blk0 + c)),
            _resident((LRU_CONV, D_RNN)),
            _resident((1, D_RNN)),
            pl.BlockSpec((None, 2, 2, LRU_KSLABS * LANES, LRU_CHUNK), lambda i, c: (c, 0, 0, 0, 0)),
            pl.BlockSpec((None, 4, LRU_CHUNK), lambda i, c: (c, 0, 0)),
            pl.BlockSpec((None, 2, LRU_CHUNK), lambda i, c: (c, 0, 0)),
            pl.BlockSpec((LRU_CHUNK, D_MODEL), lambda i, c: (c, 0)),
        ],
        out_specs=pl.BlockSpec((None, seq, D_MODEL), lambda i, c: (i, 0, 0)),
        out_shape=jax.ShapeDtypeStruct((b, seq, D_MODEL), BF16),
        scratch_shapes=[
            pltpu.VMEM((D_RNN // LANES, seq, LANES), F32),
            pltpu.VMEM((seq, D_MODEL), F32),
            pltpu.VMEM((2, nsl, LRU_NSUB * LRU_PITCH, LANES), F32),
            pltpu.VMEM((2, nsl, LRU_NSUB * LRU_PITCH, LANES), F32),
            pltpu.VMEM((2, nsl, LRU_NSUB * LRU_PITCH, LANES), F32),
            pltpu.VMEM((seq + 16, LANES), F32),
        ],
        compiler_params=_cparams(("parallel", "arbitrary"), LRU_VMEM_LIMIT_BYTES),
        name="rglru",
    )(proj3, proj3, cw, cb, wg, gb, lam, wout)


def _attn_kernel(*refs, rpb, fused, row_max):
    n_in = 1 if fused else 3
    qkv_refs = refs[:n_in]
    bias_ref, shift_ref, o_ref, lse_ref, kt_scr = refs[n_in:]
    length = o_ref.shape[0]
    nkb = length // ATT_QBLOCK
    lane = lax.broadcasted_iota(jnp.int32, (ATT_QBLOCK, LANES), 1)
    lo_half = lane < HEAD_DIM
    lo_keys = lax.broadcasted_iota(jnp.int32, (2 * ATT_QBLOCK, LANES), 1) < HEAD_DIM
    npairs = GROUP_COLS // LANES

    def scores_block(qp_of, kt_of, vw_of, bias):
        outs = []
        lse_tile = jnp.zeros((ATT_QBLOCK, LANES), F32)
        for p in range(npairs):
            qp, kt, vw = qp_of(p), kt_of(p), vw_of(p)
            full, shifts = [], []
            for hh in range(LANES // HEAD_DIM):
                own = lo_half if hh == 0 else jnp.logical_not(lo_half)
                qm = jnp.where(own, qp, jnp.zeros_like(qp))
                s = jnp.dot(qm, kt, preferred_element_type=F32) + bias
                if row_max:
                    m = jnp.max(s, axis=-1, keepdims=True)
                    s = s - m
                    shifts.append(m)
                e = jnp.exp2(s).astype(BF16)
                v1 = jnp.where(lo_keys if hh == 0 else jnp.logical_not(lo_keys), vw, jnp.ones_like(vw))
                full.append(jnp.dot(e, v1, preferred_element_type=F32))
            numer = jnp.where(lo_half, full[0], full[1])
            denom = pltpu.roll(jnp.where(lo_half, full[1], full[0]), HEAD_DIM, 1)
            outs.append(numer * (1.0 / denom))
            shift = jnp.where(lo_half, shifts[0], shifts[1]) if row_max else shift_ref[...]
            lse = LN2 * (shift + jnp.log2(denom))
            lse_tile = jnp.where(jnp.logical_or(lane == p, lane == HEAD_DIM + p), lse, lse_tile)
        return outs, lse_tile

    for r in range(rpb):
        def cols(part, p, r=r):
            base = ((r * 3 + part) * GROUP_COLS if fused else 0) + p * LANES
            return slice(base, base + LANES)

        q_ref, k_ref, v_ref = (qkv_refs[0],) * 3 if fused else qkv_refs

        def transpose_body(kb, carry):
            k0 = pl.multiple_of(kb * ATT_QBLOCK, ATT_QBLOCK)
            for p in range(npairs):
                kt_scr[p, kb] = k_ref[pl.ds(k0, ATT_QBLOCK), cols(1, p)].T
            return carry

        lax.fori_loop(0, nkb, transpose_body, 0, unroll=min(nkb, 4))

        def store(rows_list, outs, lse_tile, r=r):
            for dst_rows, src in rows_list:
                for p in range(npairs):
                    c0 = r * GROUP_COLS + p * LANES
                    o_ref[dst_rows, c0:c0 + LANES] = outs[p][src].astype(BF16)
                lse_ref[dst_rows, r * LANES:(r + 1) * LANES] = lse_tile[src]

        def interior(i, carry):
            k0 = pl.multiple_of(i * ATT_QBLOCK, ATT_QBLOCK)
            q0 = pl.multiple_of(k0 + ATT_HALF, ATT_HALF)
            outs, lse_tile = scores_block(
                lambda p: q_ref[pl.ds(q0, ATT_QBLOCK), cols(0, p)],
                lambda p: jnp.concatenate([kt_scr[p, i], kt_scr[p, i + 1]], axis=1),
                lambda p: v_ref[pl.ds(k0, 2 * ATT_QBLOCK), cols(2, p)],
                bias_ref[0])
            store([(pl.ds(q0, ATT_QBLOCK), slice(None))], outs, lse_tile)
            return carry

        if nkb > 1:
            lax.fori_loop(0, nkb - 1, interior, 0, unroll=5)

        head_rows, tail_rows = slice(0, ATT_HALF), slice(length - ATT_HALF, length)
        tail_keys = slice(length - ATT_QBLOCK, length)
        outs, lse_tile = scores_block(
            lambda p: jnp.concatenate([q_ref[head_rows, cols(0, p)], q_ref[tail_rows, cols(0, p)]], axis=0),
            lambda p: jnp.concatenate([kt_scr[p, 0], kt_scr[p, nkb - 1]], axis=1),
            lambda p: jnp.concatenate([v_ref[0:ATT_QBLOCK, cols(2, p)], v_ref[tail_keys, cols(2, p)]], axis=0),
            bias_ref[1])
        store([(head_rows, slice(0, ATT_HALF)), (tail_rows, slice(ATT_HALF, ATT_QBLOCK))], outs, lse_tile)


def _attn_bias():
    i = jnp.arange(ATT_QBLOCK)[:, None]
    j = jnp.arange(2 * ATT_QBLOCK)[None, :]
    interior = jnp.abs(i + ATT_HALF - j) <= ATT_HALF
    first = (i < ATT_HALF) & (j < ATT_QBLOCK) & (jnp.abs(i - j) <= ATT_HALF)
    last = (i >= ATT_HALF) & (j >= ATT_QBLOCK) & (jnp.abs(i - (j - ATT_QBLOCK)) <= ATT_HALF)
    return jnp.stack([jnp.where(interior, 0.0, NEG), jnp.where(first | last, 0.0, NEG)]).astype(F32)


def _attention_group(qkv, b, seq, window, dilation, score_bound):
    length = seq // dilation
    assert (window // 2) // dilation == ATT_HALF and length % ATT_QBLOCK == 0
    pv = qkv.reshape(b, length, qkv.shape[1])
    fused = dilation > 1
    rpb = min(dilation, max(1, ATT_MIN_STEP_TOKENS // length))
    if fused:
        qkv_specs = [pl.BlockSpec((None, length, rpb * 3 * GROUP_COLS), lambda i, m: (i, 0, m))]
    else:
        qkv_specs = [pl.BlockSpec((None, length, GROUP_COLS), lambda i, m, part=part: (i, 0, QKV0_CHUNK0 + part))
                     for part in range(3)]

    def call(row_max, bias, shift):
        return pl.pallas_call(
            functools.partial(_attn_kernel, rpb=rpb, fused=fused, row_max=row_max),
            grid=(b, dilation // rpb),
            in_specs=qkv_specs + [_resident(bias.shape), _resident(shift.shape)],
            out_specs=[
                pl.BlockSpec((None, length, rpb * GROUP_COLS), lambda i, m: (i, 0, m)),
                pl.BlockSpec((None, length, rpb * LANES), lambda i, m: (i, 0, m)),
            ],
            out_shape=[
                jax.ShapeDtypeStruct((b, length, dilation * GROUP_COLS), BF16),
                jax.ShapeDtypeStruct((b, length, dilation * LANES), F32),
            ],
            scratch_shapes=[pltpu.VMEM((GROUP_COLS // LANES, length // ATT_QBLOCK, LANES, ATT_QBLOCK), BF16)],
            compiler_params=_cparams(("parallel", "parallel")),
            name=f"attn_d{dilation}" + ("_rowmax" if row_max else ""),
        )(*([pv] * len(qkv_specs)), bias, shift)

    bias = _attn_bias()
    shift = jnp.full((1, LANES), score_bound, F32)
    o, lse = lax.cond(score_bound <= ATT_MAX_CONST_SHIFT,
                      lambda: call(False, bias - score_bound, shift),
                      lambda: call(True, bias, shift))
    return o.reshape(b * length, dilation * GROUP_COLS), lse.reshape(b * length, dilation * LANES)


def _merge_kernel(x_ref, a_ref, o0_ref, o1_ref, o2_ref, l0_ref, l1_ref, l2_ref, ga_ref, gb_ref,
                  watt_ref, wo_ref, ex_ref, y_ref, o_scr, l_scr):
    tm = x_ref.shape[0]
    nt = GROUP_COLS // LANES
    for gi, (og_ref, lg_ref) in enumerate(((o1_ref, l1_ref), (o2_ref, l2_ref))):
        dil = ATT_GROUPS[gi + 1][1]
        for m in range(dil):
            rows = pl.ds(m, tm // dil, stride=dil)
            l_scr[gi, rows, :] = lg_ref[:, m * LANES:(m + 1) * LANES]
            for t in range(nt):
                col = m * GROUP_COLS + t * LANES
                o_scr[gi, t, rows, :] = og_ref[:, col:col + LANES].astype(F32)
    lses = [l0_ref[...], l_scr[0], l_scr[1]]
    group_out = [o0_ref[...].astype(F32)] + [
        jnp.concatenate([o_scr[gi, t] for t in range(nt)], axis=1) for gi in range(2)]
    mx = jnp.maximum(jnp.maximum(lses[0], lses[1]), lses[2])
    es = [jnp.exp(l - mx) for l in lses]
    den = es[0] + es[1] + es[2]
    ex = ex_ref[...]
    o = None
    for e, og in zip(es, group_out):
        w = e / den
        hi = w.astype(BF16)
        lo = (w - hi.astype(F32)).astype(BF16)
        wx = jnp.dot(hi, ex, preferred_element_type=F32) + jnp.dot(lo, ex, preferred_element_type=F32)
        t = wx * og
        o = t if o is None else o + t
    b_out = jnp.dot(o.astype(BF16), watt_ref[...], preferred_element_type=F32)
    mix = (_sigmoid(ga_ref[...].astype(F32)) * a_ref[...].astype(F32)
           + _sigmoid(gb_ref[...].astype(F32)) * b_out)
    y_ref[...] = x_ref[...] + jnp.dot(mix.astype(BF16), wo_ref[...], preferred_element_type=F32)


def _merge(x2, a_out, outs, lses, mgates, watt, wo, ex):
    n = x2.shape[0]
    tm = TOKEN_TILE
    row = lambda cols, dil=1: pl.BlockSpec((tm // dil, cols * dil), lambda i: (i, 0))
    d1, d2 = ATT_GROUPS[1][1], ATT_GROUPS[2][1]
    return pl.pallas_call(
        _merge_kernel,
        grid=(n // tm,),
        in_specs=[
            row(D_MODEL), row(D_MODEL),
            row(GROUP_COLS), row(GROUP_COLS, d1), row(GROUP_COLS, d2),
            row(LANES), row(LANES, d1), row(LANES, d2),
            pl.BlockSpec((tm, D_MODEL), lambda i: (i, 0)),
            pl.BlockSpec((tm, D_MODEL), lambda i: (i, 1)),
            _resident((GROUP_COLS, D_MODEL)),
            _resident((D_MODEL, D_MODEL)),
            _resident((LANES, GROUP_COLS)),
        ],
        out_specs=row(D_MODEL),
        out_shape=jax.ShapeDtypeStruct((n, D_MODEL), F32),
        scratch_shapes=[
            pltpu.VMEM((2, GROUP_COLS // LANES, tm, LANES), F32),
            pltpu.VMEM((2, tm, LANES), F32),
        ],
        compiler_params=_cparams(("parallel",)),
        name="merge",
    )(x2, a_out, *outs, *lses, mgates, mgates, watt, wo, ex)


FFN_HALO = 16
FFN_TOKEN_TILE = 1024
FFN_CHUNK = 1024


def _ffn_kernel(x_ref, xp_ref, xn_ref, g2_ref, wup_ref, cw_ref, cb_ref, wdn_ref, y_ref, lhs_scr,
                *, tiles_per_seq):
    tm = x_ref.shape[0]
    ti = pl.program_id(0) % tiles_per_seq
    g2 = g2_ref[...]
    x = x_ref[...]
    keep_prev = jnp.where(ti == 0, 0.0, 1.0)
    keep_next = jnp.where(ti == tiles_per_seq - 1, 0.0, 1.0)
    xn = _rms_rows(x, g2).astype(BF16)
    lhs_scr[0:FFN_HALO, :] = (_rms_rows(xp_ref[...], g2) * keep_prev).astype(BF16)
    lhs_scr[FFN_HALO:FFN_HALO + tm, :] = xn
    lhs_scr[FFN_HALO + tm:, :] = (_rms_rows(xn_ref[...], g2) * keep_next).astype(BF16)
    acc = x
    for j in range(D_FF // FFN_CHUNK):
        sl = slice(j * FFN_CHUNK, (j + 1) * FFN_CHUNK)
        gext =jnp.dot(lhs_scr[...], wup_ref[:, sl], preferred_element_type=F32)
        rows = gext.shape[0]
        gc = cb_ref[:, sl]
        for k in range(FFN_CONV):
            shift = (FFN_CONV // 2 - k) % rows
            gk = gext if shift == 0 else pltpu.roll(gext, shift, 0)
            gc = gc + gk[FFN_HALO:FFN_HALO + tm, :] * cw_ref[k:k + 1, sl]
        val = jnp.dot(xn, wup_ref[:, D_FF + j * FFN_CHUNK:D_FF + (j + 1) * FFN_CHUNK],
                      preferred_element_type=F32)
        h = (_gelu_tanh(gc) * val).astype(BF16)
        acc = acc + jnp.dot(h, wdn_ref[sl, :], preferred_element_type=F32)
    y_ref[...] = acc


def _ffn(x2, seq, g2, wup, cw, cb, wdn):
    n = x2.shape[0]
    tm = FFN_TOKEN_TILE
    hb = tm // FFN_HALO
    last = n // FFN_HALO - 1
    return pl.pallas_call(
        functools.partial(_ffn_kernel, tiles_per_seq=seq // tm),
        grid=(n // tm,),
        in_specs=[
            pl.BlockSpec((tm, D_MODEL), lambda i: (i, 0)),
            pl.BlockSpec((FFN_HALO, D_MODEL), lambda i: (jnp.maximum(i * hb - 1, 0), 0)),
            pl.BlockSpec((FFN_HALO, D_MODEL), lambda i: (jnp.minimum((i + 1) * hb, last), 0)),
            _resident((1, D_MODEL)),
            _resident((D_MODEL, 2 * D_FF)),
            _resident((FFN_CONV, D_FF)),
            _resident((1, D_FF)),
            _resident((D_FF, D_MODEL)),
        ],
        out_specs=pl.BlockSpec((tm, D_MODEL), lambda i: (i, 0)),
        out_shape=jax.ShapeDtypeStruct((n, D_MODEL), F32),
        scratch_shapes=[pltpu.VMEM((tm + 2 * FFN_HALO, D_MODEL), BF16)],
        compiler_params=_cparams(("parallel",)),
        name="convffn",
    )(x2, x2, x2, g2, wup, cw, cb, wdn)


def _rope_tables(seq):
    pos = jnp.arange(seq, dtype=F32)
    inv = ROPE_THETA ** (-jnp.arange(0, ROT_DIM, 2, dtype=F32) / ROT_DIM)
    ang = pos[:, None] * inv[None, :]
    cos, sin = jnp.cos(ang), jnp.sin(ang)
    half = ROT_DIM // 2
    pad = HEAD_DIM - ROT_DIM
    one = jnp.ones((seq, pad), F32)
    zero = jnp.zeros((seq, pad), F32)
    zh = jnp.zeros((seq, half), F32)
    per_head = lambda parts: jnp.tile(jnp.concatenate(parts, axis=1), (1, LANES // HEAD_DIM))
    ra = per_head([cos, cos, one])
    rm = per_head([-sin, zh, zero])
    rp = per_head([zh, sin, zero])
    return ra, rm, rp


def _block_diag_dense(w):
    nb, c, d = w.shape
    return jnp.einsum('ncd,nm->ncmd', w, jnp.eye(nb, dtype=w.dtype)).reshape(nb * c, nb * d)


def _prep_layer(p, seq):
    q = {}
    q['g1'] = p['norm1_g'].reshape(1, D_MODEL)
    q_lo = 2 * D_RNN
    seg = lambda part, g: (q_lo + part * ATT_COLS + g * GROUP_COLS, q_lo + part * ATT_COLS + (g + 1) * GROUP_COLS)
    qkv = lambda g: [seg(part, g) for part in range(3)]
    w_in = p['w_in'].astype(BF16)
    take = lambda segs: jnp.concatenate([w_in[:, lo:hi] for lo, hi in segs], axis=1)
    q['w_main'] = take([(0, q_lo)] + qkv(0))
    q['w_rest'] = take([(q_lo + 3 * ATT_COLS, IN_COLS)] + qkv(1) + qkv(2))
    head = jnp.arange(MXU_DIM) // HEAD_DIM
    q['bd'] = (head[:, None] == head[None, :]).astype(BF16)
    q['qg'] = jnp.tile(p['q_norm_g'], (1, HEADS_PER_GROUP)) * (HEAD_DIM ** -0.5 * LOG2E)
    q['kg'] = jnp.tile(p['k_norm_g'], (1, HEADS_PER_GROUP))
    q['score_bound'] = (HEAD_DIM * ATT_BOUND_MARGIN) * (jnp.max(jnp.abs(q['qg']), axis=1)
                                                        * jnp.max(jnp.abs(q['kg']), axis=1))
    q['rope'] = _rope_tables(seq)
    q['lru_cw'] = p['lru_conv_w']
    q['lru_cb'] = p['lru_conv_b'].reshape(1, D_RNN)
    dense = jnp.stack([jnp.stack([_block_diag_dense(p['lru_wa'][d]), _block_diag_dense(p['lru_wx'][d])])
                       for d in range(2)])
    dense = (0.5 * dense).astype(BF16)
    chunks = []
    for c in range(LRU_NCHUNK):
        k0 = min(max(2 * c - 1, 0), D_RNN // LANES - LRU_KSLABS) * LANES
        chunks.append(dense[:, :, k0:k0 + LRU_KSLABS * LANES, c * LRU_CHUNK:(c + 1) * LRU_CHUNK])
    q['lru_wg'] = jnp.stack(chunks)
    gb = 0.5 * jnp.stack([p['lru_ba'][0], p['lru_bx'][0], p['lru_ba'][1], p['lru_bx'][1]])
    q['lru_gb'] = gb.reshape(4, LRU_NCHUNK, LRU_CHUNK).transpose(1, 0, 2)
    q['lru_lam'] = p['lru_lambda'].reshape(2, LRU_NCHUNK, LRU_CHUNK).transpose(1, 0, 2)
    q['lru_wout'] = p['w_lru_out'].astype(BF16)
    q['watt'] = p['w_att_out'].astype(BF16)
    q['wo'] = p['w_o'].astype(BF16)
    lse_lane = jnp.arange(LANES)
    lane_head = jnp.where(lse_lane % HEAD_DIM < GROUP_COLS // LANES,
                          2 * (lse_lane % HEAD_DIM) + lse_lane // HEAD_DIM, -1)
    q['expand'] = (lane_head[:, None] == jnp.arange(GROUP_COLS)[None, :] // HEAD_DIM).astype(BF16)
    q['g2'] = p['norm2_g'].reshape(1, D_MODEL)
    q['wup'] = p['w_up'].astype(BF16)
    q['ffn_cw'] = p['ffn_conv_w']
    q['ffn_cb'] = p['ffn_conv_b'].reshape(1, D_FF)
    q['wdn'] = p['w_down'].astype(BF16)
    return q


def _layer(x, q):
    b, seq, _ = x.shape
    x2 = x.reshape(b * seq, D_MODEL)
    main, mgates, qkv1, qkv2 = _inproj(x2, seq, q['g1'], q['w_main'], q['w_rest'], q['bd'], q['qg'], q['kg'],
                                       *q['rope'])
    a_out = _lru(main.reshape(b, seq, MAIN_COLS), q['lru_cw'], q['lru_cb'], q['lru_wg'], q['lru_gb'],
                 q['lru_lam'], q['lru_wout'])
    outs, lses = [], []
    for g, (qkv, (window, dilation)) in enumerate(zip((main, qkv1, qkv2), ATT_GROUPS)):
        o, lse = _attention_group(qkv, b, seq, window, dilation, q['score_bound'][g])
        outs.append(o)
        lses.append(lse)
    x1 = _merge(x2, a_out.reshape(b * seq, D_MODEL), outs, lses, mgates, q['watt'], q['wo'], q['expand'])
    y = _ffn(x1, seq, q['g2'], q['wup'], q['ffn_cw'], q['ffn_cb'], q['wdn'])
    return y.reshape(b, seq, D_MODEL)


def kernel(x_prompt, x_sample, norm1_g, w_in, lru_conv_w, lru_conv_b, lru_wa, lru_ba, lru_wx, lru_bx,
           lru_lambda, w_lru_out, q_norm_g, k_norm_g, w_att_out, w_o, norm2_g, w_up, ffn_conv_w,
           ffn_conv_b, w_down):
    params = dict(norm1_g=norm1_g, w_in=w_in, lru_conv_w=lru_conv_w, lru_conv_b=lru_conv_b,
                  lru_wa=lru_wa, lru_ba=lru_ba, lru_wx=lru_wx, lru_bx=lru_bx, lru_lambda=lru_lambda,
                  w_lru_out=w_lru_out, q_norm_g=q_norm_g, k_norm_g=k_norm_g, w_att_out=w_att_out,
                  w_o=w_o, norm2_g=norm2_g, w_up=w_up, ffn_conv_w=ffn_conv_w, ffn_conv_b=ffn_conv_b,
                  w_down=w_down)
    depth = norm1_g.shape[0]
    seq = x_prompt.shape[1]
    assert x_sample.shape[1] == seq
    layers = [_prep_layer({name: p[l] for name, p in params.items()}, seq) for l in range(depth)]
    ys = []
    for x in (x_prompt, x_sample):
        for q in layers:
            x = _layer(x, q)
        ys.append(x)
    return tuple(ys)
```

```python
import functools

import jax
import jax.numpy as jnp
from jax import lax
from jax.experimental import pallas as pl
from jax.experimental.pallas import tpu as pltpu

F32 = jnp.float32
BF16 = jnp.bfloat16

D_MODEL = 1024
D_RNN = 1280
LRU_BLOCKS = 16
LRU_BW = D_RNN // LRU_BLOCKS
LRU_C = 8.0
LRU_CONV = 4
ATT_GROUPS = ((128, 1), (512, 4), (2048, 16))
N_GROUPS = len(ATT_GROUPS)
HEADS_PER_GROUP = 8
HEAD_DIM = 64
GROUP_COLS = HEADS_PER_GROUP * HEAD_DIM
ATT_COLS = N_GROUPS * GROUP_COLS
ROT_DIM = HEAD_DIM // 4
ROPE_THETA = 500000.0
D_FF = 3 * D_MODEL
FFN_CONV = 3
EPS = 1e-6
NEG = -1e30
LOG2E = 1.4426950408889634
LN2 = 0.6931471805599453
IN_COLS = 2 * D_RNN + 3 * ATT_COLS + 2 * D_MODEL

LANES = 128
MXU_DIM = 256
VMEM_LIMIT_BYTES = 56 * 1024 * 1024
LRU_VMEM_LIMIT_BYTES = 58 * 1024 * 1024

CHUNK = 512
MAIN_COLS = 2 * D_RNN + 3 * GROUP_COLS
QKV0_CHUNK0 = 2 * D_RNN // CHUNK
MGATE_COLS = 2 * D_MODEL

TOKEN_TILE = 512
INPROJ_TOKEN_TILE = 1024
INPROJ_REST_TOKEN_TILE = 1024
ATT_QBLOCK = 128
ATT_HALF = 64
ATT_MIN_STEP_TOKENS = 2048
ATT_MAX_CONST_SHIFT = 60.0
ATT_BOUND_MARGIN = 1.02

LRU_CHUNK = 256
LRU_NCHUNK = D_RNN // LRU_CHUNK
LRU_KSLABS = 4
LRU_TBLOCK = 256
LRU_SUBLEN = 128
LRU_NSUB = 16
LRU_PITCH = LRU_SUBLEN + 8
LRU_SUBS_PER_BLOCK = LRU_TBLOCK // LRU_SUBLEN


def _cparams(sem, vmem_limit=VMEM_LIMIT_BYTES):
    return pltpu.CompilerParams(dimension_semantics=sem, vmem_limit_bytes=vmem_limit)


def _resident(shape):
    return pl.BlockSpec(shape, lambda *_: (0,) * len(shape), pipeline_mode=pl.Buffered(1))


def _sigmoid(x):
    return 0.5 * jnp.tanh(0.5 * x) + 0.5


def _gelu_tanh(x):
    return 0.5 * x * (1.0 + jnp.tanh(0.7978845608028654 * (x + 0.044715 * (x * x * x))))


def _rms_rows(x, g):
    ms = jnp.mean(x * x, axis=-1, keepdims=True)
    return x * lax.rsqrt(ms + EPS) * g


def _inproj_plan(call):
    if call == 0:
        return [((0, j - QKV0_CHUNK0) if j >= QKV0_CHUNK0 else None, 0, j * CHUNK)
                for j in range(MAIN_COLS // CHUNK)]
    plain = [(None, 0, j * CHUNK) for j in range(MGATE_COLS // CHUNK)]
    return plain + [((g, part), g, None) for g in (1, 2) for part in range(3)]


def _inproj_kernel(x_ref, g1_ref, w_ref, bd_ref, qg_ref, kg_ref, ra_ref, rm_ref, rp_ref, *outs, plan):
    tm = x_ref.shape[0]
    tile_scr = outs[-1]
    xn = _rms_rows(x_ref[...], g1_ref[...]).astype(BF16)
    ra = ra_ref[...]
    rm = rm_ref[...]
    rp = rp_ref[...]
    heavy = [j for j in range(len(plan)) if plan[j][0] is not None][::-1]
    plain = [j for j in range(len(plan)) if plan[j][0] is None]
    order = []
    while heavy or plain:
        take = -(-len(heavy) // max(len(plain), 1))
        order += heavy[:take] + plain[:1]
        heavy, plain = heavy[take:], plain[1:]
    for j in order:
        role, dst, c0 = plan[j]
        o_ref = outs[dst]
        acc = jnp.dot(xn, w_ref[:, j * CHUNK:(j + 1) * CHUNK], preferred_element_type=F32)
        if role is None:
            o_ref[:, c0:c0 + CHUNK] = acc.astype(BF16)
            continue
        g, part = role
        tiles = []
        if part < 2:
            gain = (qg_ref if part == 0 else kg_ref)[g:g + 1, :]
            sq = (acc * acc).astype(BF16)
            bd = bd_ref[...]
            for t in range(CHUNK // MXU_DIM):
                sl = slice(t * MXU_DIM, (t + 1) * MXU_DIM)
                ss = jnp.dot(sq[:, sl], bd, preferred_element_type=F32)
                y = acc[:, sl] * lax.rsqrt(ss * (1.0 / HEAD_DIM) + EPS) * gain[:, sl]
                for u in range(MXU_DIM // LANES):
                    yt = y[:, u * LANES:(u + 1) * LANES]
                    tiles.append(yt * ra + pltpu.roll(yt, LANES - ROT_DIM // 2, 1) * rm
                                 + pltpu.roll(yt, ROT_DIM // 2, 1) * rp)
        else:
            tiles = [acc[:, t * LANES:(t + 1) * LANES] for t in range(CHUNK // LANES)]
        if g == 0:
            for t, val in enumerate(tiles):
                o_ref[:, c0 + t * LANES:c0 + (t + 1) * LANES] = val.astype(BF16)
        else:
            dil = ATT_GROUPS[g][1]
            for t, val in enumerate(tiles):
                tile_scr[t] = val
            for m in range(dil):
                for t in range(CHUNK // LANES):
                    col = (m * 3 + part) * GROUP_COLS + t * LANES
                    o_ref[:, col:col + LANES] = tile_scr[t, pl.ds(m, tm // dil, stride=dil), :].astype(BF16)


def _inproj(x2, seq, g1, w_main, w_rest, bd, qg, kg, ra, rm, rp):
    n = x2.shape[0]

    def call(which, tm, w, widths, dils, name):
        tiles_per_seq = seq // tm
        rope_spec = pl.BlockSpec((tm, LANES), lambda i: (i % tiles_per_seq, 0))
        scratch = [pltpu.VMEM((CHUNK // LANES, tm, LANES), F32)] if max(dils) > 1 else []
        return pl.pallas_call(
            functools.partial(_inproj_kernel, plan=_inproj_plan(which)),
            grid=(n // tm,),
            in_specs=[
                pl.BlockSpec((tm, D_MODEL), lambda i: (i, 0)),
                _resident((1, D_MODEL)),
                _resident(w.shape),
                _resident((MXU_DIM, MXU_DIM)),
                _resident((N_GROUPS, GROUP_COLS)),
                _resident((N_GROUPS, GROUP_COLS)),
                rope_spec, rope_spec, rope_spec,
            ],
            out_specs=[pl.BlockSpec((tm // d, c * d), lambda i: (i, 0)) for c, d in zip(widths, dils)],
            out_shape=[jax.ShapeDtypeStruct((n // d, c * d), BF16) for c, d in zip(widths, dils)],
            scratch_shapes=scratch,
            compiler_params=_cparams(("parallel",)),
            name=name,
        )(x2, g1, w, bd, qg, kg, ra, rm, rp)

    d1, d2 = ATT_GROUPS[1][1], ATT_GROUPS[2][1]
    main, = call(0, INPROJ_TOKEN_TILE, w_main, [MAIN_COLS], [1], "inproj_main")
    mgates, qkv1, qkv2 = call(1, INPROJ_REST_TOKEN_TILE, w_rest,
                              [MGATE_COLS, 3 * GROUP_COLS, 3 * GROUP_COLS], [1, d1, d2], "inproj_rest")
    return main, mgates, qkv1, qkv2


def _lru_kslab0(c):
    return jnp.minimum(jnp.maximum(2 * c - 1, 0), D_RNN // LANES - LRU_KSLABS)


LRU_CONV_ROWS = 128


def _lru_kernel(x_ref, gate_ref, cw_ref, cb_ref, wg_ref, gb_ref, lam_ref, wout_ref, o_ref,
                xc_scr, acc_scr, a_scr, u_scr, h_scr, pad_scr):
    c = pl.program_id(1)
    seq = x_ref.shape[0]
    nsl = LRU_CHUNK // LANES
    chains = [(d, i) for d in range(2) for i in range(nsl)]

    @pl.when(c == 0)
    def _conv():
        pad_scr[0:8, :] = jnp.zeros((8, LANES), F32)
        pad_scr[seq + 8:seq + 16, :] = jnp.zeros((8, LANES), F32)
        for j in range(D_RNN // LANES):
            sl = slice(j * LANES, (j + 1) * LANES)
            pad_scr[8:seq + 8, :] = x_ref[:, sl].astype(F32)
            bias = cb_ref[:, sl]
            taps = [cw_ref[k:k + 1, sl] for k in range(LRU_CONV)]

            def conv_body(rb, carry):
                r0 = pl.multiple_of(rb * LRU_CONV_ROWS, LRU_CONV_ROWS)
                xc = bias
                for k in range(LRU_CONV):
                    off = 8 + k - LRU_CONV // 2
                    xc = xc + pad_scr[pl.ds(r0 + off, LRU_CONV_ROWS), :] * taps[k]
                xc_scr[j, pl.ds(r0, LRU_CONV_ROWS), :] = xc
                return carry

            lax.fori_loop(0, seq // LRU_CONV_ROWS, conv_body, 0)

    k0 = _lru_kslab0(c)
    lam = lam_ref[...]
    half_l2 = (-0.5 * LRU_C * LOG2E) * (jnp.maximum(-lam, 0.0) + jnp.log1p(jnp.exp(-jnp.abs(lam))))

    def pitched_rows(s, sb):
        return pl.ds(pl.multiple_of((s * LRU_SUBS_PER_BLOCK + sb) * LRU_PITCH, 8), LRU_SUBLEN)

    def gates_body(s, carry):
        r0 = pl.multiple_of(s * LRU_TBLOCK, LRU_TBLOCK)
        lhs =jnp.concatenate([xc_scr[k0 + i, pl.ds(r0, LRU_TBLOCK), :] for i in range(LRU_KSLABS)],
                              axis=1).astype(BF16)
        half_xc = 0.5 * jnp.concatenate([xc_scr[nsl * c + i, pl.ds(r0, LRU_TBLOCK), :] for i in range(nsl)],
                                        axis=1)
        for d in range(2):
            za = jnp.dot(lhs, wg_ref[d, 0], preferred_element_type=F32) + gb_ref[2 * d:2 * d + 1, :]
            zx = jnp.dot(lhs, wg_ref[d, 1], preferred_element_type=F32) + gb_ref[2 * d + 1:2 * d + 2, :]
            hl = half_l2[d:d + 1, :]
            a = jnp.exp2(jnp.tanh(za) * hl + hl)
            t = 1.0 - a * a
            root = jnp.where(t > 0.0, t * lax.rsqrt(t), 0.0)
            u = root * ((jnp.tanh(zx) + 1.0) * half_xc)
            for i in range(nsl):
                for sb in range(LRU_SUBS_PER_BLOCK):
                    rows = slice(sb * LRU_SUBLEN, (sb + 1) * LRU_SUBLEN)
                    a_scr[d, i, pitched_rows(s, sb), :] = a[rows, i * LANES:(i + 1) * LANES]
                    u_scr[d, i, pitched_rows(s, sb), :] = u[rows, i * LANES:(i + 1) * LANES]
        return carry

    lax.fori_loop(0, seq // LRU_TBLOCK, gates_body, 0, unroll=4)

    def step_rows(d, t):
        r = t if d == 0 else LRU_SUBLEN - 1 - t
        return pl.ds(r, LRU_NSUB, stride=LRU_PITCH)

    def ends_body(t, carry):
        out = []
        for n, (d, i) in enumerate(chains):
            av = a_scr[d, i, step_rows(d, t), :]
            out += [av * carry[2 * n] + u_scr[d, i, step_rows(d, t), :], av * carry[2 * n + 1]]
        return tuple(out)

    init = (jnp.zeros((LRU_NSUB, LANES), F32), jnp.ones((LRU_NSUB, LANES), F32)) * len(chains)
    ends = lax.fori_loop(0, LRU_SUBLEN, ends_body, init, unroll=8)

    sub = lax.broadcasted_iota(jnp.int32, (LRU_NSUB, LANES), 0)
    starts = []
    for n, (d, i) in enumerate(chains):
        h_end, cum_end = ends[2 * n], ends[2 * n + 1]
        cin = jnp.zeros((LRU_NSUB, LANES), F32)
        for step in range(LRU_NSUB - 1):
            nxt = h_end + cum_end * cin
            if d == 0:
                cin = jnp.where(sub == step + 1, pltpu.roll(nxt, 1, 0), cin)
            else:
                cin = jnp.where(sub == LRU_NSUB - 2 - step, pltpu.roll(nxt, LRU_NSUB - 1, 0), cin)
        starts.append(cin)

    def scan_body(t, carry):
        out = []
        for n, (d, i) in enumerate(chains):
            h = a_scr[d, i, step_rows(d, t), :] * carry[n] + u_scr[d, i, step_rows(d, t), :]
            h_scr[d, i, step_rows(d, t), :] = h
            out.append(h)
        return tuple(out)

    lax.fori_loop(0, LRU_SUBLEN, scan_body, tuple(starts), unroll=8)

    def out_body(first, s, carry):
        r0 = pl.multiple_of(s * LRU_TBLOCK, LRU_TBLOCK)
        hsum = jnp.concatenate(
            [jnp.concatenate([h_scr[0, i, pitched_rows(s, sb), :] + h_scr[1, i, pitched_rows(s, sb), :]
                              for sb in range(LRU_SUBS_PER_BLOCK)], axis=0) for i in range(nsl)], axis=1)
        y = (_gelu_tanh(gate_ref[pl.ds(r0, LRU_TBLOCK), :].astype(F32)) * hsum).astype(BF16)
        part = jnp.dot(y, wout_ref[...], preferred_element_type=F32)
        if first:
            acc_scr[pl.ds(r0, LRU_TBLOCK), :] = part
        else:
            acc_scr[pl.ds(r0, LRU_TBLOCK), :] += part
        return carry

    @pl.when(c == 0)
    def _out_first():
        lax.fori_loop(0, seq // LRU_TBLOCK, functools.partial(out_body, True), 0, unroll=4)

    @pl.when(c > 0)
    def _out_rest():
        lax.fori_loop(0, seq // LRU_TBLOCK, functools.partial(out_body, False), 0, unroll=4)

    @pl.when(c == LRU_NCHUNK - 1)
    def _emit():
        o_ref[...] = acc_scr[...].astype(BF16)


def _lru(proj3, cw, cb, wg, gb, lam, wout):
    b, seq, _ = proj3.shape
    assert seq == LRU_NSUB * LRU_SUBLEN and seq % LRU_TBLOCK == 0
    nsl = LRU_CHUNK // LANES
    gate_blk0 = D_RNN // LRU_CHUNK
    return pl.pallas_call(
        _lru_kernel,
        grid=(b, LRU_NCHUNK),
        in_specs=[
            pl.BlockSpec((None, seq, D_RNN), lambda i, c: (i, 0, 0)),
            pl.BlockSpec((None, seq, LRU_CHUNK), lambda i, c: (i, 0, gate_blk0 + c)),
            _resident((LRU_CONV, D_RNN)),
            _resident((1, D_RNN)),
            pl.BlockSpec((None, 2, 2, LRU_KSLABS * LANES, LRU_CHUNK), lambda i, c: (c, 0, 0, 0, 0)),
            pl.BlockSpec((None, 4, LRU_CHUNK), lambda i, c: (c, 0, 0)),
            pl.BlockSpec((None, 2, LRU_CHUNK), lambda i, c: (c, 0, 0)),
            pl.BlockSpec((LRU_CHUNK, D_MODEL), lambda i, c: (c, 0)),
        ],
        out_specs=pl.BlockSpec((None, seq, D_MODEL), lambda i, c: (i, 0, 0)),
        out_shape=jax.ShapeDtypeStruct((b, seq, D_MODEL), BF16),
        scratch_shapes=[
            pltpu.VMEM((D_RNN // LANES, seq, LANES), F32),
            pltpu.VMEM((seq, D_MODEL), F32),
            pltpu.VMEM((2, nsl, LRU_NSUB * LRU_PITCH, LANES), F32),
            pltpu.VMEM((2, nsl, LRU_NSUB * LRU_PITCH, LANES), F32),
            pltpu.VMEM((2, nsl, LRU_NSUB * LRU_PITCH, LANES), F32),
            pltpu.VMEM((seq + 16, LANES), F32),
        ],
        compiler_params=_cparams(("parallel", "arbitrary"), LRU_VMEM_LIMIT_BYTES),
        name="rglru",
    )(proj3, proj3, cw, cb, wg, gb, lam, wout)


def _attn_kernel(*refs, rpb, fused, row_max):
    n_in = 1 if fused else 3
    qkv_refs = refs[:n_in]
    bias_ref, shift_ref, o_ref, lse_ref, kt_scr = refs[n_in:]
    length = o_ref.shape[0]
    nkb = length // ATT_QBLOCK
    lane = lax.broadcasted_iota(jnp.int32, (ATT_QBLOCK, LANES), 1)
    lo_half = lane < HEAD_DIM
    lo_keys = lax.broadcasted_iota(jnp.int32, (2 * ATT_QBLOCK, LANES), 1) < HEAD_DIM
    npairs = GROUP_COLS // LANES

    def scores_block(qp_of, kt_of, vw_of, bias):
        outs = []
        lse_tile = jnp.zeros((ATT_QBLOCK, LANES), F32)
        for p in range(npairs):
            qp, kt, vw = qp_of(p), kt_of(p), vw_of(p)
            full, shifts = [], []
            for hh in range(LANES // HEAD_DIM):
                own = lo_half if hh == 0 else jnp.logical_not(lo_half)
                qm = jnp.where(own, qp, jnp.zeros_like(qp))
                s = jnp.dot(qm, kt, preferred_element_type=F32) + bias
                if row_max:
                    m = jnp.max(s, axis=-1, keepdims=True)
                    s = s - m
                    shifts.append(m)
                e = jnp.exp2(s).astype(BF16)
                v1 = jnp.where(lo_keys if hh == 0 else jnp.logical_not(lo_keys), vw, jnp.ones_like(vw))
                full.append(jnp.dot(e, v1, preferred_element_type=F32))
            numer = jnp.where(lo_half, full[0], full[1])
            denom = pltpu.roll(jnp.where(lo_half, full[1], full[0]), HEAD_DIM, 1)
            outs.append(numer * (1.0 / denom))
            shift = jnp.where(lo_half, shifts[0], shifts[1]) if row_max else shift_ref[...]
            lse = LN2 * (shift + jnp.log2(denom))
            lse_tile = jnp.where(jnp.logical_or(lane == p, lane == HEAD_DIM + p), lse, lse_tile)
        return outs, lse_tile

    for r in range(rpb):
        def cols(part, p, r=r):
            base = ((r * 3 + part) * GROUP_COLS if fused else 0) + p * LANES
            return slice(base, base + LANES)

        q_ref, k_ref, v_ref = (qkv_refs[0],) * 3 if fused else qkv_refs

        def transpose_body(kb, carry):
            k0 = pl.multiple_of(kb * ATT_QBLOCK, ATT_QBLOCK)
            for p in range(npairs):
                kt_scr[p, kb] = k_ref[pl.ds(k0, ATT_QBLOCK), cols(1, p)].T
            return carry

        lax.fori_loop(0, nkb, transpose_body, 0, unroll=min(nkb, 4))

        def store(rows_list, outs, lse_tile, r=r):
            for dst_rows, src in rows_list:
                for p in range(npairs):
                    c0 = r * GROUP_COLS + p * LANES
                    o_ref[dst_rows, c0:c0 + LANES] = outs[p][src].astype(BF16)
                lse_ref[dst_rows, r * LANES:(r + 1) * LANES] = lse_tile[src]

        def interior(i, carry):
            k0 = pl.multiple_of(i * ATT_QBLOCK, ATT_QBLOCK)
            q0 = pl.multiple_of(k0 + ATT_HALF, ATT_HALF)
            outs, lse_tile = scores_block(
                lambda p: q_ref[pl.ds(q0, ATT_QBLOCK), cols(0, p)],
                lambda p: jnp.concatenate([kt_scr[p, i], kt_scr[p, i + 1]], axis=1),
                lambda p: v_ref[pl.ds(k0, 2 * ATT_QBLOCK), cols(2, p)],
                bias_ref[0])
            store([(pl.ds(q0, ATT_QBLOCK), slice(None))], outs, lse_tile)
            return carry

        if nkb > 1:
            lax.fori_loop(0, nkb - 1, interior, 0, unroll=True)

        head_rows, tail_rows = slice(0, ATT_HALF), slice(length - ATT_HALF, length)
        tail_keys = slice(length - ATT_QBLOCK, length)
        outs, lse_tile = scores_block(
            lambda p: jnp.concatenate([q_ref[head_rows, cols(0, p)], q_ref[tail_rows, cols(0, p)]], axis=0),
            lambda p: jnp.concatenate([kt_scr[p, 0], kt_scr[p, nkb - 1]], axis=1),
            lambda p: jnp.concatenate([v_ref[0:ATT_QBLOCK, cols(2, p)], v_ref[tail_keys, cols(2, p)]], axis=0),
            bias_ref[1])
        store([(head_rows, slice(0, ATT_HALF)), (tail_rows, slice(ATT_HALF, ATT_QBLOCK))], outs, lse_tile)


def _attn_bias():
    i = jnp.arange(ATT_QBLOCK)[:, None]
    j = jnp.arange(2 * ATT_QBLOCK)[None, :]
    interior = jnp.abs(i + ATT_HALF - j) <= ATT_HALF
    first = (i < ATT_HALF) & (j < ATT_QBLOCK) & (jnp.abs(i - j) <= ATT_HALF)
    last = (i >= ATT_HALF) & (j >= ATT_QBLOCK) & (jnp.abs(i - (j - ATT_QBLOCK)) <= ATT_HALF)
    return jnp.stack([jnp.where(interior, 0.0, NEG), jnp.where(first | last, 0.0, NEG)]).astype(F32)


def _attention_group(qkv, b, seq, window, dilation, score_bound):
    length = seq // dilation
    assert (window // 2) // dilation == ATT_HALF and length % ATT_QBLOCK == 0
    pv = qkv.reshape(b, length, qkv.shape[1])
    fused = dilation > 1
    rpb = min(dilation, max(1, ATT_MIN_STEP_TOKENS // length))
    if fused:
        qkv_specs = [pl.BlockSpec((None, length, rpb * 3 * GROUP_COLS), lambda i, m: (i, 0, m))]
    else:
        qkv_specs = [pl.BlockSpec((None, length, GROUP_COLS), lambda i, m, part=part: (i, 0, QKV0_CHUNK0 + part))
                     for part in range(3)]

    def call(row_max, bias, shift):
        return pl.pallas_call(
            functools.partial(_attn_kernel, rpb=rpb, fused=fused, row_max=row_max),
            grid=(b, dilation // rpb),
            in_specs=qkv_specs + [_resident(bias.shape), _resident(shift.shape)],
            out_specs=[
                pl.BlockSpec((None, length, rpb * GROUP_COLS), lambda i, m: (i, 0, m)),
                pl.BlockSpec((None, length, rpb * LANES), lambda i, m: (i, 0, m)),
            ],
            out_shape=[
                jax.ShapeDtypeStruct((b, length, dilation * GROUP_COLS), BF16),
                jax.ShapeDtypeStruct((b, length, dilation * LANES), F32),
            ],
            scratch_shapes=[pltpu.VMEM((GROUP_COLS // LANES, length // ATT_QBLOCK, LANES, ATT_QBLOCK), BF16)],
            compiler_params=_cparams(("parallel", "parallel")),
            name=f"attn_d{dilation}" + ("_rowmax" if row_max else ""),
        )(*([pv] * len(qkv_specs)), bias, shift)

    bias = _attn_bias()
    shift = jnp.full((1, LANES), score_bound, F32)
    o, lse = lax.cond(score_bound <= ATT_MAX_CONST_SHIFT,
                      lambda: call(False, bias - score_bound, shift),
                      lambda: call(True, bias, shift))
    return o.reshape(b * length, dilation * GROUP_COLS), lse.reshape(b * length, dilation * LANES)


def _merge_kernel(x_ref, a_ref, o0_ref, o1_ref, o2_ref, l0_ref, l1_ref, l2_ref, ga_ref, gb_ref,
                  watt_ref, wo_ref, ex_ref, y_ref, o_scr, l_scr):
    tm = x_ref.shape[0]
    nt = GROUP_COLS // LANES
    for gi, (og_ref, lg_ref) in enumerate(((o1_ref, l1_ref), (o2_ref, l2_ref))):
        dil = ATT_GROUPS[gi + 1][1]
        for m in range(dil):
            rows = pl.ds(m, tm // dil, stride=dil)
            l_scr[gi, rows, :] = lg_ref[:, m * LANES:(m + 1) * LANES]
            for t in range(nt):
                col = m * GROUP_COLS + t * LANES
                o_scr[gi, t, rows, :] = og_ref[:, col:col + LANES].astype(F32)
    lses = [l0_ref[...], l_scr[0], l_scr[1]]
    group_out = [o0_ref[...].astype(F32)] + [
        jnp.concatenate([o_scr[gi, t] for t in range(nt)], axis=1) for gi in range(2)]
    mx = jnp.maximum(jnp.maximum(lses[0], lses[1]), lses[2])
    es = [jnp.exp(l - mx) for l in lses]
    den = es[0] + es[1] + es[2]
    ex = ex_ref[...]
    o = None
    for e, og in zip(es, group_out):
        w = e / den
        hi = w.astype(BF16)
        lo = (w - hi.astype(F32)).astype(BF16)
        wx = jnp.dot(hi, ex, preferred_element_type=F32) + jnp.dot(lo, ex, preferred_element_type=F32)
        t = wx * og
        o = t if o is None else o + t
    b_out = jnp.dot(o.astype(BF16), watt_ref[...], preferred_element_type=F32)
    mix = (_sigmoid(ga_ref[...].astype(F32)) * a_ref[...].astype(F32)
           + _sigmoid(gb_ref[...].astype(F32)) * b_out)
    y_ref[...] = x_ref[...] + jnp.dot(mix.astype(BF16), wo_ref[...], preferred_element_type=F32)


def _merge(x2, a_out, outs, lses, mgates, watt, wo, ex):
    n = x2.shape[0]
    tm = TOKEN_TILE
    row = lambda cols, dil=1: pl.BlockSpec((tm // dil, cols * dil), lambda i: (i, 0))
    d1, d2 = ATT_GROUPS[1][1], ATT_GROUPS[2][1]
    return pl.pallas_call(
        _merge_kernel,
        grid=(n // tm,),
        in_specs=[
            row(D_MODEL), row(D_MODEL),
            row(GROUP_COLS), row(GROUP_COLS, d1), row(GROUP_COLS, d2),
            row(LANES), row(LANES, d1), row(LANES, d2),
            pl.BlockSpec((tm, D_MODEL), lambda i: (i, 0)),
            pl.BlockSpec((tm, D_MODEL), lambda i: (i, 1)),
            _resident((GROUP_COLS, D_MODEL)),
            _resident((D_MODEL, D_MODEL)),
            _resident((LANES, GROUP_COLS)),
        ],
        out_specs=row(D_MODEL),
        out_shape=jax.ShapeDtypeStruct((n, D_MODEL), F32),
        scratch_shapes=[
            pltpu.VMEM((2, GROUP_COLS // LANES, tm, LANES), F32),
            pltpu.VMEM((2, tm, LANES), F32),
        ],
        compiler_params=_cparams(("parallel",)),
        name="merge",
    )(x2, a_out, *outs, *lses, mgates, mgates, watt, wo, ex)


FFN_HALO = 16
FFN_TOKEN_TILE = 1024
FFN_CHUNK = 1024


def _ffn_kernel(x_ref, xp_ref, xn_ref, g2_ref, wup_ref, cw_ref, cb_ref, wdn_ref, y_ref, lhs_scr,
                *, tiles_per_seq):
    tm = x_ref.shape[0]
    ti = pl.program_id(0) % tiles_per_seq
    g2 = g2_ref[...]
    x = x_ref[...]
    keep_prev = jnp.where(ti == 0, 0.0, 1.0)
    keep_next = jnp.where(ti == tiles_per_seq - 1, 0.0, 1.0)
    xn = _rms_rows(x, g2).astype(BF16)
    lhs_scr[0:FFN_HALO, :] = (_rms_rows(xp_ref[...], g2) * keep_prev).astype(BF16)
    lhs_scr[FFN_HALO:FFN_HALO + tm, :] = xn
    lhs_scr[FFN_HALO + tm:, :] = (_rms_rows(xn_ref[...], g2) * keep_next).astype(BF16)
    acc = x
    for j in range(D_FF // FFN_CHUNK):
        sl = slice(j * FFN_CHUNK, (j + 1) * FFN_CHUNK)
        gext =jnp.dot(lhs_scr[...], wup_ref[:, sl], preferred_element_type=F32)
        rows = gext.shape[0]
        gc = cb_ref[:, sl]
        for k in range(FFN_CONV):
            shift = (FFN_CONV // 2 - k) % rows
            gk = gext if shift == 0 else pltpu.roll(gext, shift, 0)
            gc = gc + gk[FFN_HALO:FFN_HALO + tm, :] * cw_ref[k:k + 1, sl]
        val = jnp.dot(xn, wup_ref[:, D_FF + j * FFN_CHUNK:D_FF + (j + 1) * FFN_CHUNK],
                      preferred_element_type=F32)
        h = (_gelu_tanh(gc) * val).astype(BF16)
        acc = acc + jnp.dot(h, wdn_ref[sl, :], preferred_element_type=F32)
    y_ref[...] = acc


def _ffn(x2, seq, g2, wup, cw, cb, wdn):
    n = x2.shape[0]
    tm = FFN_TOKEN_TILE
    hb = tm // FFN_HALO
    last = n // FFN_HALO - 1
    return pl.pallas_call(
        functools.partial(_ffn_kernel, tiles_per_seq=seq // tm),
        grid=(n // tm,),
        in_specs=[
            pl.BlockSpec((tm, D_MODEL), lambda i: (i, 0)),
            pl.BlockSpec((FFN_HALO, D_MODEL), lambda i: (jnp.maximum(i * hb - 1, 0), 0)),
            pl.BlockSpec((FFN_HALO, D_MODEL), lambda i: (jnp.minimum((i + 1) * hb, last), 0)),
            _resident((1, D_MODEL)),
            _resident((D_MODEL, 2 * D_FF)),
            _resident((FFN_CONV, D_FF)),
            _resident((1, D_FF)),
            _resident((D_FF, D_MODEL)),
        ],
        out_specs=pl.BlockSpec((tm, D_MODEL), lambda i: (i, 0)),
        out_shape=jax.ShapeDtypeStruct((n, D_MODEL), F32),
        scratch_shapes=[pltpu.VMEM((tm + 2 * FFN_HALO, D_MODEL), BF16)],
        compiler_params=_cparams(("parallel",)),
        name="convffn",
    )(x2, x2, x2, g2, wup, cw, cb, wdn)


def _rope_tables(seq):
    pos = jnp.arange(seq, dtype=F32)
    inv = ROPE_THETA ** (-jnp.arange(0, ROT_DIM, 2, dtype=F32) / ROT_DIM)
    ang = pos[:, None] * inv[None, :]
    cos, sin = jnp.cos(ang), jnp.sin(ang)
    half = ROT_DIM // 2
    pad = HEAD_DIM - ROT_DIM
    one = jnp.ones((seq, pad), F32)
    zero = jnp.zeros((seq, pad), F32)
    zh = jnp.zeros((seq, half), F32)
    per_head = lambda parts: jnp.tile(jnp.concatenate(parts, axis=1), (1, LANES // HEAD_DIM))
    ra = per_head([cos, cos, one])
    rm = per_head([-sin, zh, zero])
    rp = per_head([zh, sin, zero])
    return ra, rm, rp


def _block_diag_dense(w):
    nb, c, d = w.shape
    return jnp.einsum('ncd,nm->ncmd', w, jnp.eye(nb, dtype=w.dtype)).reshape(nb * c, nb * d)


def _prep_layer(p, seq):
    q = {}
    q['g1'] = p['norm1_g'].reshape(1, D_MODEL)
    q_lo = 2 * D_RNN
    seg = lambda part, g: (q_lo + part * ATT_COLS + g * GROUP_COLS, q_lo + part * ATT_COLS + (g + 1) * GROUP_COLS)
    qkv = lambda g: [seg(part, g) for part in range(3)]
    w_in = p['w_in'].astype(BF16)
    take = lambda segs: jnp.concatenate([w_in[:, lo:hi] for lo, hi in segs], axis=1)
    q['w_main'] = take([(0, q_lo)] + qkv(0))
    q['w_rest'] = take([(q_lo + 3 * ATT_COLS, IN_COLS)] + qkv(1) + qkv(2))
    head = jnp.arange(MXU_DIM) // HEAD_DIM
    q['bd'] = (head[:, None] == head[None, :]).astype(BF16)
    q['qg'] = jnp.tile(p['q_norm_g'], (1, HEADS_PER_GROUP)) * (HEAD_DIM ** -0.5 * LOG2E)
    q['kg'] = jnp.tile(p['k_norm_g'], (1, HEADS_PER_GROUP))
    q['score_bound'] = (HEAD_DIM * ATT_BOUND_MARGIN) * (jnp.max(jnp.abs(q['qg']), axis=1)
                                                        * jnp.max(jnp.abs(q['kg']), axis=1))
    q['rope'] = _rope_tables(seq)
    q['lru_cw'] = p['lru_conv_w']
    q['lru_cb'] = p['lru_conv_b'].reshape(1, D_RNN)
    dense = jnp.stack([jnp.stack([_block_diag_dense(p['lru_wa'][d]), _block_diag_dense(p['lru_wx'][d])])
                       for d in range(2)])
    dense = (0.5 * dense).astype(BF16)
    chunks = []
    for c in range(LRU_NCHUNK):
        k0 = min(max(2 * c - 1, 0), D_RNN // LANES - LRU_KSLABS) * LANES
        chunks.append(dense[:, :, k0:k0 + LRU_KSLABS * LANES, c * LRU_CHUNK:(c + 1) * LRU_CHUNK])
    q['lru_wg'] = jnp.stack(chunks)
    gb = 0.5 * jnp.stack([p['lru_ba'][0], p['lru_bx'][0], p['lru_ba'][1], p['lru_bx'][1]])
    q['lru_gb'] = gb.reshape(4, LRU_NCHUNK, LRU_CHUNK).transpose(1, 0, 2)
    q['lru_lam'] = p['lru_lambda'].reshape(2, LRU_NCHUNK, LRU_CHUNK).transpose(1, 0, 2)
    q['lru_wout'] = p['w_lru_out'].astype(BF16)
    q['watt'] = p['w_att_out'].astype(BF16)
    q['wo'] = p['w_o'].astype(BF16)
    lse_lane = jnp.arange(LANES)
    lane_head = jnp.where(lse_lane % HEAD_DIM < GROUP_COLS // LANES,
                          2 * (lse_lane % HEAD_DIM) + lse_lane // HEAD_DIM, -1)
    q['expand'] = (lane_head[:, None] == jnp.arange(GROUP_COLS)[None, :] // HEAD_DIM).astype(BF16)
    q['g2'] = p['norm2_g'].reshape(1, D_MODEL)
    q['wup'] = p['w_up'].astype(BF16)
    q['ffn_cw'] = p['ffn_conv_w']
    q['ffn_cb'] = p['ffn_conv_b'].reshape(1, D_FF)
    q['wdn'] = p['w_down'].astype(BF16)
    return q


def _layer(x, q):
    b, seq, _ = x.shape
    x2 = x.reshape(b * seq, D_MODEL)
    main, mgates, qkv1, qkv2 = _inproj(x2, seq, q['g1'], q['w_main'], q['w_rest'], q['bd'], q['qg'], q['kg'],
                                       *q['rope'])
    a_out = _lru(main.reshape(b, seq, MAIN_COLS), q['lru_cw'], q['lru_cb'], q['lru_wg'], q['lru_gb'],
                 q['lru_lam'], q['lru_wout'])
    outs, lses = [], []
    for g, (qkv, (window, dilation)) in enumerate(zip((main, qkv1, qkv2), ATT_GROUPS)):
        o, lse = _attention_group(qkv, b, seq, window, dilation, q['score_bound'][g])
        outs.append(o)
        lses.append(lse)
    x1 = _merge(x2, a_out.reshape(b * seq, D_MODEL), outs, lses, mgates, q['watt'], q['wo'], q['expand'])
    y = _ffn(x1, seq, q['g2'], q['wup'], q['ffn_cw'], q['ffn_cb'], q['wdn'])
    return y.reshape(b, seq, D_MODEL)


def kernel(x_prompt, x_sample, norm1_g, w_in, lru_conv_w, lru_conv_b, lru_wa, lru_ba, lru_wx, lru_bx,
           lru_lambda, w_lru_out, q_norm_g, k_norm_g, w_att_out, w_o, norm2_g, w_up, ffn_conv_w,
           ffn_conv_b, w_down):
    params = dict(norm1_g=norm1_g, w_in=w_in, lru_conv_w=lru_conv_w, lru_conv_b=lru_conv_b,
                  lru_wa=lru_wa, lru_ba=lru_ba, lru_wx=lru_wx, lru_bx=lru_bx, lru_lambda=lru_lambda,
                  w_lru_out=w_lru_out, q_norm_g=q_norm_g, k_norm_g=k_norm_g, w_att_out=w_att_out,
                  w_o=w_o, norm2_g=norm2_g, w_up=w_up, ffn_conv_w=ffn_conv_w, ffn_conv_b=ffn_conv_b,
                  w_down=w_down)
    depth = norm1_g.shape[0]
    seq = x_prompt.shape[1]
    assert x_sample.shape[1] == seq
    layers = [_prep_layer({name: p[l] for name, p in params.items()}, seq) for l in range(depth)]
    ys = []
    for x in (x_prompt, x_sample):
        for q in layers:
            x = _layer(x, q)
        ys.append(x)
    return tuple(ys)
```

```python
import functools

import jax
import jax.numpy as jnp
from jax import lax
from jax.experimental import pallas as pl
from jax.experimental.pallas import tpu as pltpu

F32 = jnp.float32
BF16 = jnp.bfloat16

D_MODEL = 1024
D_RNN = 1280
LRU_BLOCKS = 16
LRU_BW = D_RNN // LRU_BLOCKS
LRU_C = 8.0
LRU_CONV = 4
ATT_GROUPS = ((128, 1), (512, 4), (2048, 16))
N_GROUPS = len(ATT_GROUPS)
HEADS_PER_GROUP = 8
HEAD_DIM = 64
GROUP_COLS = HEADS_PER_GROUP * HEAD_DIM
ATT_COLS = N_GROUPS * GROUP_COLS
ROT_DIM = HEAD_DIM // 4
ROPE_THETA = 500000.0
D_FF = 3 * D_MODEL
FFN_CONV = 3
EPS = 1e-6
NEG = -1e30
LOG2E = 1.4426950408889634
LN2 = 0.6931471805599453
IN_COLS = 2 * D_RNN + 3 * ATT_COLS + 2 * D_MODEL

LANES = 128
MXU_DIM = 256
VMEM_LIMIT_BYTES = 56 * 1024 * 1024
LRU_VMEM_LIMIT_BYTES = 58 * 1024 * 1024

CHUNK = 512
MAIN_COLS = 2 * D_RNN + 3 * GROUP_COLS
QKV0_CHUNK0 = 2 * D_RNN // CHUNK
MGATE_COLS = 2 * D_MODEL

TOKEN_TILE = 512
INPROJ_TOKEN_TILE = 1024
INPROJ_REST_TOKEN_TILE = 1024
ATT_QBLOCK = 128
ATT_HALF = 64
ATT_MIN_STEP_TOKENS = 2048
ATT_MAX_CONST_SHIFT = 60.0
ATT_BOUND_MARGIN = 1.02

LRU_CHUNK = 256
LRU_NCHUNK = D_RNN // LRU_CHUNK
LRU_KSLABS = 4
LRU_TBLOCK = 256
LRU_SUBLEN = 128
LRU_NSUB = 16
LRU_PITCH = LRU_SUBLEN + 8
LRU_SUBS_PER_BLOCK = LRU_TBLOCK // LRU_SUBLEN


def _cparams(sem, vmem_limit=VMEM_LIMIT_BYTES):
    return pltpu.CompilerParams(dimension_semantics=sem, vmem_limit_bytes=vmem_limit)


def _resident(shape):
    return pl.BlockSpec(shape, lambda *_: (0,) * len(shape), pipeline_mode=pl.Buffered(1))


def _sigmoid(x):
    return 0.5 * jnp.tanh(0.5 * x) + 0.5


def _gelu_tanh(x):
    return 0.5 * x * (1.0 + jnp.tanh(0.7978845608028654 * (x + 0.044715 * (x * x * x))))


def _rms_rows(x, g):
    ms = jnp.mean(x * x, axis=-1, keepdims=True)
    return x * lax.rsqrt(ms + EPS) * g


def _inproj_plan(call):
    if call == 0:
        return [((0, j - QKV0_CHUNK0) if j >= QKV0_CHUNK0 else None, 0, j * CHUNK)
                for j in range(MAIN_COLS // CHUNK)]
    plain = [(None, 0, j * CHUNK) for j in range(MGATE_COLS // CHUNK)]
    return plain + [((g, part), g, None) for g in (1, 2) for part in range(3)]


def _inproj_kernel(x_ref, g1_ref, w_ref, bd_ref, qg_ref, kg_ref, ra_ref, rm_ref, rp_ref, *outs, plan):
    tm = x_ref.shape[0]
    tile_scr = outs[-1]
    xn = _rms_rows(x_ref[...], g1_ref[...]).astype(BF16)
    ra = ra_ref[...]
    rm = rm_ref[...]
    rp = rp_ref[...]
    heavy = [j for j in range(len(plan)) if plan[j][0] is not None][::-1]
    plain = [j for j in range(len(plan)) if plan[j][0] is None]
    order = []
    while heavy or plain:
        take = -(-len(heavy) // max(len(plain), 1))
        order += heavy[:take] + plain[:1]
        heavy, plain = heavy[take:], plain[1:]
    for j in order:
        role, dst, c0 = plan[j]
        o_ref = outs[dst]
        acc = jnp.dot(xn, w_ref[:, j * CHUNK:(j + 1) * CHUNK], preferred_element_type=F32)
        if role is None:
            o_ref[:, c0:c0 + CHUNK] = acc.astype(BF16)
            continue
        g, part = role
        tiles = []
        if part < 2:
            gain = (qg_ref if part == 0 else kg_ref)[g:g + 1, :]
            sq = (acc * acc).astype(BF16)
            bd = bd_ref[...]
            for t in range(CHUNK // MXU_DIM):
                sl = slice(t * MXU_DIM, (t + 1) * MXU_DIM)
                ss = jnp.dot(sq[:, sl], bd, preferred_element_type=F32)
                y = acc[:, sl] * lax.rsqrt(ss * (1.0 / HEAD_DIM) + EPS) * gain[:, sl]
                for u in range(MXU_DIM // LANES):
                    yt = y[:, u * LANES:(u + 1) * LANES]
                    tiles.append(yt * ra + pltpu.roll(yt, LANES - ROT_DIM // 2, 1) * rm
                                 + pltpu.roll(yt, ROT_DIM // 2, 1) * rp)
        else:
            tiles = [acc[:, t * LANES:(t + 1) * LANES] for t in range(CHUNK // LANES)]
        if g == 0:
            for t, val in enumerate(tiles):
                o_ref[:, c0 + t * LANES:c0 + (t + 1) * LANES] = val.astype(BF16)
        else:
            dil = ATT_GROUPS[g][1]
            for t, val in enumerate(tiles):
                tile_scr[t] = val
            for m in range(dil):
                for t in range(CHUNK // LANES):
                    col = (m * 3 + part) * GROUP_COLS + t * LANES
                    o_ref[:, col:col + LANES] = tile_scr[t, pl.ds(m, tm // dil, stride=dil), :].astype(BF16)


def _inproj(x2, seq, g1, w_main, w_rest, bd, qg, kg, ra, rm, rp):
    n = x2.shape[0]

    def call(which, tm, w, widths, dils, name):
        tiles_per_seq = seq // tm
        rope_spec = pl.BlockSpec((tm, LANES), lambda i: (i % tiles_per_seq, 0))
        scratch = [pltpu.VMEM((CHUNK // LANES, tm, LANES), F32)] if max(dils) > 1 else []
        return pl.pallas_call(
            functools.partial(_inproj_kernel, plan=_inproj_plan(which)),
            grid=(n // tm,),
            in_specs=[
                pl.BlockSpec((tm, D_MODEL), lambda i: (i, 0)),
                _resident((1, D_MODEL)),
                _resident(w.shape),
                _resident((MXU_DIM, MXU_DIM)),
                _resident((N_GROUPS, GROUP_COLS)),
                _resident((N_GROUPS, GROUP_COLS)),
                rope_spec, rope_spec, rope_spec,
            ],
            out_specs=[pl.BlockSpec((tm // d, c * d), lambda i: (i, 0)) for c, d in zip(widths, dils)],
            out_shape=[jax.ShapeDtypeStruct((n // d, c * d), BF16) for c, d in zip(widths, dils)],
            scratch_shapes=scratch,
            compiler_params=_cparams(("parallel",)),
            name=name,
        )(x2, g1, w, bd, qg, kg, ra, rm, rp)

    d1, d2 = ATT_GROUPS[1][1], ATT_GROUPS[2][1]
    main, = call(0, INPROJ_TOKEN_TILE, w_main, [MAIN_COLS], [1], "inproj_main")
    mgates, qkv1, qkv2 = call(1, INPROJ_REST_TOKEN_TILE, w_rest,
                              [MGATE_COLS, 3 * GROUP_COLS, 3 * GROUP_COLS], [1, d1, d2], "inproj_rest")
    return main, mgates, qkv1, qkv2


def _lru_kslab0(c):
    return jnp.minimum(jnp.maximum(2 * c - 1, 0), D_RNN // LANES - LRU_KSLABS)


LRU_CONV_ROWS = 128


def _lru_kernel(x_ref, gate_ref, cw_ref, cb_ref, wg_ref, gb_ref, lam_ref, wout_ref, o_ref,
                xc_scr, acc_scr, a_scr, u_scr, h_scr, pad_scr):
    c = pl.program_id(1)
    seq = x_ref.shape[0]
    nsl = LRU_CHUNK // LANES
    chains = [(d, i) for d in range(2) for i in range(nsl)]

    @pl.when(c == 0)
    def _conv():
        pad_scr[0:8, :] = jnp.zeros((8, LANES), F32)
        pad_scr[seq + 8:seq + 16, :] = jnp.zeros((8, LANES), F32)
        for j in range(D_RNN // LANES):
            sl = slice(j * LANES, (j + 1) * LANES)
            pad_scr[8:seq + 8, :] = x_ref[:, sl].astype(F32)
            bias = cb_ref[:, sl]
            taps = [cw_ref[k:k + 1, sl] for k in range(LRU_CONV)]

            def conv_body(rb, carry):
                r0 = pl.multiple_of(rb * LRU_CONV_ROWS, LRU_CONV_ROWS)
                xc = bias
                for k in range(LRU_CONV):
                    off = 8 + k - LRU_CONV // 2
                    xc = xc + pad_scr[pl.ds(r0 + off, LRU_CONV_ROWS), :] * taps[k]
                xc_scr[j, pl.ds(r0, LRU_CONV_ROWS), :] = xc
                return carry

            lax.fori_loop(0, seq // LRU_CONV_ROWS, conv_body, 0)

    k0 = _lru_kslab0(c)
    lam = lam_ref[...]
    half_l2 = (-0.5 * LRU_C * LOG2E) * (jnp.maximum(-lam, 0.0) + jnp.log1p(jnp.exp(-jnp.abs(lam))))

    def pitched_rows(s, sb):
        return pl.ds(pl.multiple_of((s * LRU_SUBS_PER_BLOCK + sb) * LRU_PITCH, 8), LRU_SUBLEN)

    def gates_body(s, carry):
        r0 = pl.multiple_of(s * LRU_TBLOCK, LRU_TBLOCK)
        lhs =jnp.concatenate([xc_scr[k0 + i, pl.ds(r0, LRU_TBLOCK), :] for i in range(LRU_KSLABS)],
                              axis=1).astype(BF16)
        half_xc = 0.5 * jnp.concatenate([xc_scr[nsl * c + i, pl.ds(r0, LRU_TBLOCK), :] for i in range(nsl)],
                                        axis=1)
        for d in range(2):
            za = jnp.dot(lhs, wg_ref[d, 0], preferred_element_type=F32) + gb_ref[2 * d:2 * d + 1, :]
            zx = jnp.dot(lhs, wg_ref[d, 1], preferred_element_type=F32) + gb_ref[2 * d + 1:2 * d + 2, :]
            hl = half_l2[d:d + 1, :]
            a = jnp.exp2(jnp.tanh(za) * hl + hl)
            t = 1.0 - a * a
            root = jnp.where(t > 0.0, t * lax.rsqrt(t), 0.0)
            u = root * ((jnp.tanh(zx) + 1.0) * half_xc)
            for i in range(nsl):
                for sb in range(LRU_SUBS_PER_BLOCK):
                    rows = slice(sb * LRU_SUBLEN, (sb + 1) * LRU_SUBLEN)
                    a_scr[d, i, pitched_rows(s, sb), :] = a[rows, i * LANES:(i + 1) * LANES]
                    u_scr[d, i, pitched_rows(s, sb), :] = u[rows, i * LANES:(i + 1) * LANES]
        return carry

    lax.fori_loop(0, seq // LRU_TBLOCK, gates_body, 0, unroll=True)

    def step_rows(d, t):
        r = t if d == 0 else LRU_SUBLEN - 1 - t
        return pl.ds(r, LRU_NSUB, stride=LRU_PITCH)

    def ends_body(t, carry):
        out = []
        for n, (d, i) in enumerate(chains):
            av = a_scr[d, i, step_rows(d, t), :]
            out += [av * carry[2 * n] + u_scr[d, i, step_rows(d, t), :], av * carry[2 * n + 1]]
        return tuple(out)

    init = (jnp.zeros((LRU_NSUB, LANES), F32), jnp.ones((LRU_NSUB, LANES), F32)) * len(chains)
    ends = lax.fori_loop(0, LRU_SUBLEN, ends_body, init, unroll=16)

    sub = lax.broadcasted_iota(jnp.int32, (LRU_NSUB, LANES), 0)
    starts = []
    for n, (d, i) in enumerate(chains):
        h_end, cum_end = ends[2 * n], ends[2 * n + 1]
        cin = jnp.zeros((LRU_NSUB, LANES), F32)
        for step in range(LRU_NSUB - 1):
            nxt = h_end + cum_end * cin
            if d == 0:
                cin = jnp.where(sub == step + 1, pltpu.roll(nxt, 1, 0), cin)
            else:
                cin = jnp.where(sub == LRU_NSUB - 2 - step, pltpu.roll(nxt, LRU_NSUB - 1, 0), cin)
        starts.append(cin)

    def scan_body(t, carry):
        out = []
        for n, (d, i) in enumerate(chains):
            h = a_scr[d, i, step_rows(d, t), :] * carry[n] + u_scr[d, i, step_rows(d, t), :]
            h_scr[d, i, step_rows(d, t), :] = h
            out.append(h)
        return tuple(out)

    lax.fori_loop(0, LRU_SUBLEN, scan_body, tuple(starts), unroll=16)

    def out_body(first, s, carry):
        r0 = pl.multiple_of(s * LRU_TBLOCK, LRU_TBLOCK)
        hsum = jnp.concatenate(
            [jnp.concatenate([h_scr[0, i, pitched_rows(s, sb), :] + h_scr[1, i, pitched_rows(s, sb), :]
                              for sb in range(LRU_SUBS_PER_BLOCK)], axis=0) for i in range(nsl)], axis=1)
        y = (_gelu_tanh(gate_ref[pl.ds(r0, LRU_TBLOCK), :].astype(F32)) * hsum).astype(BF16)
        part = jnp.dot(y, wout_ref[...], preferred_element_type=F32)
        if first:
            acc_scr[pl.ds(r0, LRU_TBLOCK), :] = part
        else:
            acc_scr[pl.ds(r0, LRU_TBLOCK), :] += part
        return carry

    @pl.when(c == 0)
    def _out_first():
        lax.fori_loop(0, seq // LRU_TBLOCK, functools.partial(out_body, True), 0, unroll=True)

    @pl.when(c > 0)
    def _out_rest():
        lax.fori_loop(0, seq // LRU_TBLOCK, functools.partial(out_body, False), 0, unroll=True)

    @pl.when(c == LRU_NCHUNK - 1)
    def _emit():
        o_ref[...] = acc_scr[...].astype(BF16)


def _lru(proj3, cw, cb, wg, gb, lam, wout):
    b, seq, _ = proj3.shape
    assert seq == LRU_NSUB * LRU_SUBLEN and seq % LRU_TBLOCK == 0
    nsl = LRU_CHUNK // LANES
    gate_blk0 = D_RNN // LRU_CHUNK
    return pl.pallas_call(
        _lru_kernel,
        grid=(b, LRU_NCHUNK),
        in_specs=[
            pl.BlockSpec((None, seq, D_RNN), lambda i, c: (i, 0, 0)),
            pl.BlockSpec((None, seq, LRU_CHUNK), lambda i, c: (i, 0, gate_blk0 + c)),
            _resident((LRU_CONV, D_RNN)),
            _resident((1, D_RNN)),
            pl.BlockSpec((None, 2, 2, LRU_KSLABS * LANES, LRU_CHUNK), lambda i, c: (c, 0, 0, 0, 0)),
            pl.BlockSpec((None, 4, LRU_CHUNK), lambda i, c: (c, 0, 0)),
            pl.BlockSpec((None, 2, LRU_CHUNK), lambda i, c: (c, 0, 0)),
            pl.BlockSpec((LRU_CHUNK, D_MODEL), lambda i, c: (c, 0)),
        ],
        out_specs=pl.BlockSpec((None, seq, D_MODEL), lambda i, c: (i, 0, 0)),
        out_shape=jax.ShapeDtypeStruct((b, seq, D_MODEL), BF16),
        scratch_shapes=[
            pltpu.VMEM((D_RNN // LANES, seq, LANES), F32),
            pltpu.VMEM((seq, D_MODEL), F32),
            pltpu.VMEM((2, nsl, LRU_NSUB * LRU_PITCH, LANES), F32),
            pltpu.VMEM((2, nsl, LRU_NSUB * LRU_PITCH, LANES), F32),
            pltpu.VMEM((2, nsl, LRU_NSUB * LRU_PITCH, LANES), F32),
            pltpu.VMEM((seq + 16, LANES), F32),
        ],
        compiler_params=_cparams(("parallel", "arbitrary"), LRU_VMEM_LIMIT_BYTES),
        name="rglru",
    )(proj3, proj3, cw, cb, wg, gb, lam, wout)


def _attn_kernel(*refs, rpb, fused, row_max):
    n_in = 1 if fused else 3
    qkv_refs = refs[:n_in]
    bias_ref, shift_ref, o_ref, lse_ref, kt_scr = refs[n_in:]
    length = o_ref.shape[0]
    nkb = length // ATT_QBLOCK
    lane = lax.broadcasted_iota(jnp.int32, (ATT_QBLOCK, LANES), 1)
    lo_half = lane < HEAD_DIM
    lo_keys = lax.broadcasted_iota(jnp.int32, (2 * ATT_QBLOCK, LANES), 1) < HEAD_DIM
    npairs = GROUP_COLS // LANES

    def scores_block(qp_of, kt_of, vw_of, bias):
        outs = []
        lse_tile = jnp.zeros((ATT_QBLOCK, LANES), F32)
        for p in range(npairs):
            qp, kt, vw = qp_of(p), kt_of(p), vw_of(p)
            full, shifts = [], []
            for hh in range(LANES // HEAD_DIM):
                own = lo_half if hh == 0 else jnp.logical_not(lo_half)
                qm = jnp.where(own, qp, jnp.zeros_like(qp))
                s = jnp.dot(qm, kt, preferred_element_type=F32) + bias
                if row_max:
                    m = jnp.max(s, axis=-1, keepdims=True)
                    s = s - m
                    shifts.append(m)
                e = jnp.exp2(s).astype(BF16)
                v1 = jnp.where(lo_keys if hh == 0 else jnp.logical_not(lo_keys), vw, jnp.ones_like(vw))
                full.append(jnp.dot(e, v1, preferred_element_type=F32))
            numer = jnp.where(lo_half, full[0], full[1])
            denom = pltpu.roll(jnp.where(lo_half, full[1], full[0]), HEAD_DIM, 1)
            outs.append(numer * (1.0 / denom))
            shift = jnp.where(lo_half, shifts[0], shifts[1]) if row_max else shift_ref[...]
            lse = LN2 * (shift + jnp.log2(denom))
            lse_tile = jnp.where(jnp.logical_or(lane == p, lane == HEAD_DIM + p), lse, lse_tile)
        return outs, lse_tile

    for r in range(rpb):
        def cols(part, p, r=r):
            base = ((r * 3 + part) * GROUP_COLS if fused else 0) + p * LANES
            return slice(base, base + LANES)

        q_ref, k_ref, v_ref = (qkv_refs[0],) * 3 if fused else qkv_refs

        def transpose_body(kb, carry):
            k0 = pl.multiple_of(kb * ATT_QBLOCK, ATT_QBLOCK)
            for p in range(npairs):
                kt_scr[p, kb] = k_ref[pl.ds(k0, ATT_QBLOCK), cols(1, p)].T
            return carry

        lax.fori_loop(0, nkb, transpose_body, 0, unroll=min(nkb, 4))

        def store(rows_list, outs, lse_tile, r=r):
            for dst_rows, src in rows_list:
                for p in range(npairs):
                    c0 = r * GROUP_COLS + p * LANES
                    o_ref[dst_rows, c0:c0 + LANES] = outs[p][src].astype(BF16)
                lse_ref[dst_rows, r * LANES:(r + 1) * LANES] = lse_tile[src]

        def interior(i, carry):
            k0 = pl.multiple_of(i * ATT_QBLOCK, ATT_QBLOCK)
            q0 = pl.multiple_of(k0 + ATT_HALF, ATT_HALF)
            outs, lse_tile = scores_block(
                lambda p: q_ref[pl.ds(q0, ATT_QBLOCK), cols(0, p)],
                lambda p: jnp.concatenate([kt_scr[p, i], kt_scr[p, i + 1]], axis=1),
                lambda p: v_ref[pl.ds(k0, 2 * ATT_QBLOCK), cols(2, p)],
                bias_ref[0])
            store([(pl.ds(q0, ATT_QBLOCK), slice(None))], outs, lse_tile)
            return carry

        if nkb > 1:
            lax.fori_loop(0, nkb - 1, interior, 0, unroll=True)

        head_rows, tail_rows = slice(0, ATT_HALF), slice(length - ATT_HALF, length)
        tail_keys = slice(length - ATT_QBLOCK, length)
        outs, lse_tile = scores_block(
            lambda p: jnp.concatenate([q_ref[head_rows, cols(0, p)], q_ref[tail_rows, cols(0, p)]], axis=0),
            lambda p: jnp.concatenate([kt_scr[p, 0], kt_scr[p, nkb - 1]], axis=1),
            lambda p: jnp.concatenate([v_ref[0:ATT_QBLOCK, cols(2, p)], v_ref[tail_keys, cols(2, p)]], axis=0),
            bias_ref[1])
        store([(head_rows, slice(0, ATT_HALF)), (tail_rows, slice(ATT_HALF, ATT_QBLOCK))], outs, lse_tile)


def _attn_bias():
    i = jnp.arange(ATT_QBLOCK)[:, None]
    j = jnp.arange(2 * ATT_QBLOCK)[None, :]
    interior = jnp.abs(i + ATT_HALF - j) <= ATT_HALF
    first = (i < ATT_HALF) & (j < ATT_QBLOCK) & (jnp.abs(i - j) <= ATT_HALF)
    last = (i >= ATT_HALF) & (j >= ATT_QBLOCK) & (jnp.abs(i - (j - ATT_QBLOCK)) <= ATT_HALF)
    return jnp.stack([jnp.where(interior, 0.0, NEG), jnp.where(first | last, 0.0, NEG)]).astype(F32)


def _attention_group(qkv, b, seq, window, dilation, score_bound):
    length = seq // dilation
    assert (window // 2) // dilation == ATT_HALF and length % ATT_QBLOCK == 0
    pv = qkv.reshape(b, length, qkv.shape[1])
    fused = dilation > 1
    rpb = min(dilation, max(1, ATT_MIN_STEP_TOKENS // length))
    if fused:
        qkv_specs = [pl.BlockSpec((None, length, rpb * 3 * GROUP_COLS), lambda i, m: (i, 0, m))]
    else:
        qkv_specs = [pl.BlockSpec((None, length, GROUP_COLS), lambda i, m, part=part: (i, 0, QKV0_CHUNK0 + part))
                     for part in range(3)]

    def call(row_max, bias, shift):
        return pl.pallas_call(
            functools.partial(_attn_kernel, rpb=rpb, fused=fused, row_max=row_max),
            grid=(b, dilation // rpb),
            in_specs=qkv_specs + [_resident(bias.shape), _resident(shift.shape)],
            out_specs=[
                pl.BlockSpec((None, length, rpb * GROUP_COLS), lambda i, m: (i, 0, m)),
                pl.BlockSpec((None, length, rpb * LANES), lambda i, m: (i, 0, m)),
            ],
            out_shape=[
                jax.ShapeDtypeStruct((b, length, dilation * GROUP_COLS), BF16),
                jax.ShapeDtypeStruct((b, length, dilation * LANES), F32),
            ],
            scratch_shapes=[pltpu.VMEM((GROUP_COLS // LANES, length // ATT_QBLOCK, LANES, ATT_QBLOCK), BF16)],
            compiler_params=_cparams(("parallel", "parallel")),
            name=f"attn_d{dilation}" + ("_rowmax" if row_max else ""),
        )(*([pv] * len(qkv_specs)), bias, shift)

    bias = _attn_bias()
    shift = jnp.full((1, LANES), score_bound, F32)
    o, lse = lax.cond(score_bound <= ATT_MAX_CONST_SHIFT,
                      lambda: call(False, bias - score_bound, shift),
                      lambda: call(True, bias, shift))
    return o.reshape(b * length, dilation * GROUP_COLS), lse.reshape(b * length, dilation * LANES)


def _merge_kernel(x_ref, a_ref, o0_ref, o1_ref, o2_ref, l0_ref, l1_ref, l2_ref, ga_ref, gb_ref,
                  watt_ref, wo_ref, ex_ref, y_ref, o_scr, l_scr):
    tm = x_ref.shape[0]
    nt = GROUP_COLS // LANES
    for gi, (og_ref, lg_ref) in enumerate(((o1_ref, l1_ref), (o2_ref, l2_ref))):
        dil = ATT_GROUPS[gi + 1][1]
        for m in range(dil):
            rows = pl.ds(m, tm // dil, stride=dil)
            l_scr[gi, rows, :] = lg_ref[:, m * LANES:(m + 1) * LANES]
            for t in range(nt):
                col = m * GROUP_COLS + t * LANES
                o_scr[gi, t, rows, :] = og_ref[:, col:col + LANES].astype(F32)
    lses = [l0_ref[...], l_scr[0], l_scr[1]]
    group_out = [o0_ref[...].astype(F32)] + [
        jnp.concatenate([o_scr[gi, t] for t in range(nt)], axis=1) for gi in range(2)]
    mx = jnp.maximum(jnp.maximum(lses[0], lses[1]), lses[2])
    es = [jnp.exp(l - mx) for l in lses]
    den = es[0] + es[1] + es[2]
    ex = ex_ref[...]
    o = None
    for e, og in zip(es, group_out):
        w = e / den
        hi = w.astype(BF16)
        lo = (w - hi.astype(F32)).astype(BF16)
        wx = jnp.dot(hi, ex, preferred_element_type=F32) + jnp.dot(lo, ex, preferred_element_type=F32)
        t = wx * og
        o = t if o is None else o + t
    b_out = jnp.dot(o.astype(BF16), watt_ref[...], preferred_element_type=F32)
    mix = (_sigmoid(ga_ref[...].astype(F32)) * a_ref[...].astype(F32)
           + _sigmoid(gb_ref[...].astype(F32)) * b_out)
    y_ref[...] = x_ref[...] + jnp.dot(mix.astype(BF16), wo_ref[...], preferred_element_type=F32)


def _merge(x2, a_out, outs, lses, mgates, watt, wo, ex):
    n = x2.shape[0]
    tm = TOKEN_TILE
    row = lambda cols, dil=1: pl.BlockSpec((tm // dil, cols * dil), lambda i: (i, 0))
    d1, d2 = ATT_GROUPS[1][1], ATT_GROUPS[2][1]
    return pl.pallas_call(
        _merge_kernel,
        grid=(n // tm,),
        in_specs=[
            row(D_MODEL), row(D_MODEL),
            row(GROUP_COLS), row(GROUP_COLS, d1), row(GROUP_COLS, d2),
            row(LANES), row(LANES, d1), row(LANES, d2),
            pl.BlockSpec((tm, D_MODEL), lambda i: (i, 0)),
            pl.BlockSpec((tm, D_MODEL), lambda i: (i, 1)),
            _resident((GROUP_COLS, D_MODEL)),
            _resident((D_MODEL, D_MODEL)),
            _resident((LANES, GROUP_COLS)),
        ],
        out_specs=row(D_MODEL),
        out_shape=jax.ShapeDtypeStruct((n, D_MODEL), F32),
        scratch_shapes=[
            pltpu.VMEM((2, GROUP_COLS // LANES, tm, LANES), F32),
            pltpu.VMEM((2, tm, LANES), F32),
        ],
        compiler_params=_cparams(("parallel",)),
        name="merge",
    )(x2, a_out, *outs, *lses, mgates, mgates, watt, wo, ex)


FFN_HALO = 16
FFN_TOKEN_TILE = 1024
FFN_CHUNK = 1024


def _ffn_kernel(x_ref, xp_ref, xn_ref, g2_ref, wup_ref, cw_ref, cb_ref, wdn_ref, y_ref, lhs_scr,
                *, tiles_per_seq):
    tm = x_ref.shape[0]
    ti = pl.program_id(0) % tiles_per_seq
    g2 = g2_ref[...]
    x = x_ref[...]
    keep_prev = jnp.where(ti == 0, 0.0, 1.0)
    keep_next = jnp.where(ti == tiles_per_seq - 1, 0.0, 1.0)
    xn = _rms_rows(x, g2).astype(BF16)
    lhs_scr[0:FFN_HALO, :] = (_rms_rows(xp_ref[...], g2) * keep_prev).astype(BF16)
    lhs_scr[FFN_HALO:FFN_HALO + tm, :] = xn
    lhs_scr[FFN_HALO + tm:, :] = (_rms_rows(xn_ref[...], g2) * keep_next).astype(BF16)
    acc = x
    for j in range(D_FF // FFN_CHUNK):
        sl = slice(j * FFN_CHUNK, (j + 1) * FFN_CHUNK)
        gext =jnp.dot(lhs_scr[...], wup_ref[:, sl], preferred_element_type=F32)
        rows = gext.shape[0]
        gc = cb_ref[:, sl]
        for k in range(FFN_CONV):
            shift = (FFN_CONV // 2 - k) % rows
            gk = gext if shift == 0 else pltpu.roll(gext, shift, 0)
            gc = gc + gk[FFN_HALO:FFN_HALO + tm, :] * cw_ref[k:k + 1, sl]
        val = jnp.dot(xn, wup_ref[:, D_FF + j * FFN_CHUNK:D_FF + (j + 1) * FFN_CHUNK],
                      preferred_element_type=F32)
        h = (_gelu_tanh(gc) * val).astype(BF16)
        acc = acc + jnp.dot(h, wdn_ref[sl, :], preferred_element_type=F32)
    y_ref[...] = acc


def _ffn(x2, seq, g2, wup, cw, cb, wdn):
    n = x2.shape[0]
    tm = FFN_TOKEN_TILE
    hb = tm // FFN_HALO
    last = n // FFN_HALO - 1
    return pl.pallas_call(
        functools.partial(_ffn_kernel, tiles_per_seq=seq // tm),
        grid=(n // tm,),
        in_specs=[
            pl.BlockSpec((tm, D_MODEL), lambda i: (i, 0)),
            pl.BlockSpec((FFN_HALO, D_MODEL), lambda i: (jnp.maximum(i * hb - 1, 0), 0)),
            pl.BlockSpec((FFN_HALO, D_MODEL), lambda i: (jnp.minimum((i + 1) * hb, last), 0)),
            _resident((1, D_MODEL)),
            _resident((D_MODEL, 2 * D_FF)),
            _resident((FFN_CONV, D_FF)),
            _resident((1, D_FF)),
            _resident((D_FF, D_MODEL)),
        ],
        out_specs=pl.BlockSpec((tm, D_MODEL), lambda i: (i, 0)),
        out_shape=jax.ShapeDtypeStruct((n, D_MODEL), F32),
        scratch_shapes=[pltpu.VMEM((tm + 2 * FFN_HALO, D_MODEL), BF16)],
        compiler_params=_cparams(("parallel",)),
        name="convffn",
    )(x2, x2, x2, g2, wup, cw, cb, wdn)


def _rope_tables(seq):
    pos = jnp.arange(seq, dtype=F32)
    inv = ROPE_THETA ** (-jnp.arange(0, ROT_DIM, 2, dtype=F32) / ROT_DIM)
    ang = pos[:, None] * inv[None, :]
    cos, sin = jnp.cos(ang), jnp.sin(ang)
    half = ROT_DIM // 2
    pad = HEAD_DIM - ROT_DIM
    one = jnp.ones((seq, pad), F32)
    zero = jnp.zeros((seq, pad), F32)
    zh = jnp.zeros((seq, half), F32)
    per_head = lambda parts: jnp.tile(jnp.concatenate(parts, axis=1), (1, LANES // HEAD_DIM))
    ra = per_head([cos, cos, one])
    rm = per_head([-sin, zh, zero])
    rp = per_head([zh, sin, zero])
    return ra, rm, rp


def _block_diag_dense(w):
    nb, c, d = w.shape
    return jnp.einsum('ncd,nm->ncmd', w, jnp.eye(nb, dtype=w.dtype)).reshape(nb * c, nb * d)


def _prep_layer(p, seq):
    q = {}
    q['g1'] = p['norm1_g'].reshape(1, D_MODEL)
    q_lo = 2 * D_RNN
    seg = lambda part, g: (q_lo + part * ATT_COLS + g * GROUP_COLS, q_lo + part * ATT_COLS + (g + 1) * GROUP_COLS)
    qkv = lambda g: [seg(part, g) for part in range(3)]
    w_in = p['w_in'].astype(BF16)
    take = lambda segs: jnp.concatenate([w_in[:, lo:hi] for lo, hi in segs], axis=1)
    q['w_main'] = take([(0, q_lo)] + qkv(0))
    q['w_rest'] = take([(q_lo + 3 * ATT_COLS, IN_COLS)] + qkv(1) + qkv(2))
    head = jnp.arange(MXU_DIM) // HEAD_DIM
    q['bd'] = (head[:, None] == head[None, :]).astype(BF16)
    q['qg'] = jnp.tile(p['q_norm_g'], (1, HEADS_PER_GROUP)) * (HEAD_DIM ** -0.5 * LOG2E)
    q['kg'] = jnp.tile(p['k_norm_g'], (1, HEADS_PER_GROUP))
    q['score_bound'] = (HEAD_DIM * ATT_BOUND_MARGIN) * (jnp.max(jnp.abs(q['qg']), axis=1)
                                                        * jnp.max(jnp.abs(q['kg']), axis=1))
    q['rope'] = _rope_tables(seq)
    q['lru_cw'] = p['lru_conv_w']
    q['lru_cb'] = p['lru_conv_b'].reshape(1, D_RNN)
    dense = jnp.stack([jnp.stack([_block_diag_dense(p['lru_wa'][d]), _block_diag_dense(p['lru_wx'][d])])
                       for d in range(2)])
    dense = (0.5 * dense).astype(BF16)
    chunks = []
    for c in range(LRU_NCHUNK):
        k0 = min(max(2 * c - 1, 0), D_RNN // LANES - LRU_KSLABS) * LANES
        chunks.append(dense[:, :, k0:k0 + LRU_KSLABS * LANES, c * LRU_CHUNK:(c + 1) * LRU_CHUNK])
    q['lru_wg'] = jnp.stack(chunks)
    gb = 0.5 * jnp.stack([p['lru_ba'][0], p['lru_bx'][0], p['lru_ba'][1], p['lru_bx'][1]])
    q['lru_gb'] = gb.reshape(4, LRU_NCHUNK, LRU_CHUNK).transpose(1, 0, 2)
    q['lru_lam'] = p['lru_lambda'].reshape(2, LRU_NCHUNK, LRU_CHUNK).transpose(1, 0, 2)
    q['lru_wout'] = p['w_lru_out'].astype(BF16)
    q['watt'] = p['w_att_out'].astype(BF16)
    q['wo'] = p['w_o'].astype(BF16)
    lse_lane = jnp.arange(LANES)
    lane_head = jnp.where(lse_lane % HEAD_DIM < GROUP_COLS // LANES,
                          2 * (lse_lane % HEAD_DIM) + lse_lane // HEAD_DIM, -1)
    q['expand'] = (lane_head[:, None] == jnp.arange(GROUP_COLS)[None, :] // HEAD_DIM).astype(BF16)
    q['g2'] = p['norm2_g'].reshape(1, D_MODEL)
    q['wup'] = p['w_up'].astype(BF16)
    q['ffn_cw'] = p['ffn_conv_w']
    q['ffn_cb'] = p['ffn_conv_b'].reshape(1, D_FF)
    q['wdn'] = p['w_down'].astype(BF16)
    return q


def _layer(x, q):
    b, seq, _ = x.shape
    x2 = x.reshape(b * seq, D_MODEL)
    main, mgates, qkv1, qkv2 = _inproj(x2, seq, q['g1'], q['w_main'], q['w_rest'], q['bd'], q['qg'], q['kg'],
                                       *q['rope'])
    a_out = _lru(main.reshape(b, seq, MAIN_COLS), q['lru_cw'], q['lru_cb'], q['lru_wg'], q['lru_gb'],
                 q['lru_lam'], q['lru_wout'])
    outs, lses = [], []
    for g, (qkv, (window, dilation)) in enumerate(zip((main, qkv1, qkv2), ATT_GROUPS)):
        o, lse = _attention_group(qkv, b, seq, window, dilation, q['score_bound'][g])
        outs.append(o)
        lses.append(lse)
    x1 = _merge(x2, a_out.reshape(b * seq, D_MODEL), outs, lses, mgates, q['watt'], q['wo'], q['expand'])
    y = _ffn(x1, seq, q['g2'], q['wup'], q['ffn_cw'], q['ffn_cb'], q['wdn'])
    return y.reshape(b, seq, D_MODEL)


def kernel(x_prompt, x_sample, norm1_g, w_in, lru_conv_w, lru_conv_b, lru_wa, lru_ba, lru_wx, lru_bx,
           lru_lambda, w_lru_out, q_norm_g, k_norm_g, w_att_out, w_o, norm2_g, w_up, ffn_conv_w,
           ffn_conv_b, w_down):
    params = dict(norm1_g=norm1_g, w_in=w_in, lru_conv_w=lru_conv_w, lru_conv_b=lru_conv_b,
                  lru_wa=lru_wa, lru_ba=lru_ba, lru_wx=lru_wx, lru_bx=lru_bx, lru_lambda=lru_lambda,
                  w_lru_out=w_lru_out, q_norm_g=q_norm_g, k_norm_g=k_norm_g, w_att_out=w_att_out,
                  w_o=w_o, norm2_g=norm2_g, w_up=w_up, ffn_conv_w=ffn_conv_w, ffn_conv_b=ffn_conv_b,
                  w_down=w_down)
    depth = norm1_g.shape[0]
    seq = x_prompt.shape[1]
    assert x_sample.shape[1] == seq
    layers = [_prep_layer({name: p[l] for name, p in params.items()}, seq) for l in range(depth)]
    ys = []
    for x in (x_prompt, x_sample):
        for q in layers:
            x = _layer(x, q)
        ys.append(x)
    return tuple(ys)
```

```python
import functools

import jax
import jax.numpy as jnp
from jax import lax
from jax.experimental import pallas as pl
from jax.experimental.pallas import tpu as pltpu

F32 = jnp.float32
BF16 = jnp.bfloat16

D_MODEL = 1024
D_RNN = 1280
LRU_BLOCKS = 16
LRU_BW = D_RNN // LRU_BLOCKS
LRU_C = 8.0
LRU_CONV = 4
ATT_GROUPS = ((128, 1), (512, 4), (2048, 16))
N_GROUPS = len(ATT_GROUPS)
HEADS_PER_GROUP = 8
HEAD_DIM = 64
GROUP_COLS = HEADS_PER_GROUP * HEAD_DIM
ATT_COLS = N_GROUPS * GROUP_COLS
ROT_DIM = HEAD_DIM // 4
ROPE_THETA = 500000.0
D_FF = 3 * D_MODEL
FFN_CONV = 3
EPS = 1e-6
NEG = -1e30
LOG2E = 1.4426950408889634
LN2 = 0.6931471805599453
IN_COLS = 2 * D_RNN + 3 * ATT_COLS + 2 * D_MODEL

LANES = 128
MXU_DIM = 256
VMEM_LIMIT_BYTES = 56 * 1024 * 1024
LRU_VMEM_LIMIT_BYTES = 58 * 1024 * 1024

CHUNK = 512
MAIN_COLS = 2 * D_RNN + 3 * GROUP_COLS
QKV0_CHUNK0 = 2 * D_RNN // CHUNK
MGATE_COLS = 2 * D_MODEL

TOKEN_TILE = 512
INPROJ_TOKEN_TILE = 1024
INPROJ_REST_TOKEN_TILE = 1024
ATT_QBLOCK = 128
ATT_HALF = 64
ATT_MIN_STEP_TOKENS = 2048
ATT_MAX_CONST_SHIFT = 60.0
ATT_BOUND_MARGIN = 1.02

LRU_CHUNK = 256
LRU_NCHUNK = D_RNN // LRU_CHUNK
LRU_KSLABS = 4
LRU_TBLOCK = 256
LRU_SUBLEN = 128
LRU_NSUB = 16
LRU_PITCH = LRU_SUBLEN + 8
LRU_SUBS_PER_BLOCK = LRU_TBLOCK // LRU_SUBLEN


def _cparams(sem, vmem_limit=VMEM_LIMIT_BYTES):
    return pltpu.CompilerParams(dimension_semantics=sem, vmem_limit_bytes=vmem_limit)


def _resident(shape):
    return pl.BlockSpec(shape, lambda *_: (0,) * len(shape), pipeline_mode=pl.Buffered(1))


def _sigmoid(x):
    return 0.5 * jnp.tanh(0.5 * x) + 0.5


def _gelu_tanh(x):
    return 0.5 * x * (1.0 + jnp.tanh(0.7978845608028654 * (x + 0.044715 * (x * x * x))))


def _rms_rows(x, g):
    ms = jnp.mean(x * x, axis=-1, keepdims=True)
    return x * lax.rsqrt(ms + EPS) * g


def _inproj_plan(call):
    if call == 0:
        return [((0, j - QKV0_CHUNK0) if j >= QKV0_CHUNK0 else None, 0, j * CHUNK)
                for j in range(MAIN_COLS // CHUNK)]
    plain = [(None, 0, j * CHUNK) for j in range(MGATE_COLS // CHUNK)]
    return plain + [((g, part), g, None) for g in (1, 2) for part in range(3)]


def _inproj_kernel(x_ref, g1_ref, w_ref, bd_ref, qg_ref, kg_ref, ra_ref, rm_ref, rp_ref, *outs, plan):
    tm = x_ref.shape[0]
    tile_scr = outs[-1]
    xn = _rms_rows(x_ref[...], g1_ref[...]).astype(BF16)
    ra = ra_ref[...]
    rm = rm_ref[...]
    rp = rp_ref[...]
    heavy = [j for j in range(len(plan)) if plan[j][0] is not None][::-1]
    plain = [j for j in range(len(plan)) if plan[j][0] is None]
    order = []
    while heavy or plain:
        take = -(-len(heavy) // max(len(plain), 1))
        order += heavy[:take] + plain[:1]
        heavy, plain = heavy[take:], plain[1:]
    for j in order:
        role, dst, c0 = plan[j]
        o_ref = outs[dst]
        acc = jnp.dot(xn, w_ref[:, j * CHUNK:(j + 1) * CHUNK], preferred_element_type=F32)
        if role is None:
            o_ref[:, c0:c0 + CHUNK] = acc.astype(BF16)
            continue
        g, part = role
        tiles = []
        if part < 2:
            gain = (qg_ref if part == 0 else kg_ref)[g:g + 1, :]
            sq = (acc * acc).astype(BF16)
            bd = bd_ref[...]
            for t in range(CHUNK // MXU_DIM):
                sl = slice(t * MXU_DIM, (t + 1) * MXU_DIM)
                ss = jnp.dot(sq[:, sl], bd, preferred_element_type=F32)
                y = acc[:, sl] * lax.rsqrt(ss * (1.0 / HEAD_DIM) + EPS) * gain[:, sl]
                for u in range(MXU_DIM // LANES):
                    yt = y[:, u * LANES:(u + 1) * LANES]
                    tiles.append(yt * ra + pltpu.roll(yt, LANES - ROT_DIM // 2, 1) * rm
                                 + pltpu.roll(yt, ROT_DIM // 2, 1) * rp)
        else:
            tiles = [acc[:, t * LANES:(t + 1) * LANES] for t in range(CHUNK // LANES)]
        if g == 0:
            for t, val in enumerate(tiles):
                o_ref[:, c0 + t * LANES:c0 + (t + 1) * LANES] = val.astype(BF16)
        else:
            dil = ATT_GROUPS[g][1]
            for t, val in enumerate(tiles):
                tile_scr[t] = val
            for m in range(dil):
                for t in range(CHUNK // LANES):
                    col = (m * 3 + part) * GROUP_COLS + t * LANES
                    o_ref[:, col:col + LANES] = tile_scr[t, pl.ds(m, tm // dil, stride=dil), :].astype(BF16)


def _inproj(x2, seq, g1, w_main, w_rest, bd, qg, kg, ra, rm, rp):
    n = x2.shape[0]

    def call(which, tm, w, widths, dils, name):
        tiles_per_seq = seq // tm
        rope_spec = pl.BlockSpec((tm, LANES), lambda i: (i % tiles_per_seq, 0))
        scratch = [pltpu.VMEM((CHUNK // LANES, tm, LANES), F32)] if max(dils) > 1 else []
        return pl.pallas_call(
            functools.partial(_inproj_kernel, plan=_inproj_plan(which)),
            grid=(n // tm,),
            in_specs=[
                pl.BlockSpec((tm, D_MODEL), lambda i: (i, 0)),
                _resident((1, D_MODEL)),
                _resident(w.shape),
                _resident((MXU_DIM, MXU_DIM)),
                _resident((N_GROUPS, GROUP_COLS)),
                _resident((N_GROUPS, GROUP_COLS)),
                rope_spec, rope_spec, rope_spec,
            ],
            out_specs=[pl.BlockSpec((tm // d, c * d), lambda i: (i, 0)) for c, d in zip(widths, dils)],
            out_shape=[jax.ShapeDtypeStruct((n // d, c * d), BF16) for c, d in zip(widths, dils)],
            scratch_shapes=scratch,
            compiler_params=_cparams(("parallel",)),
            name=name,
        )(x2, g1, w, bd, qg, kg, ra, rm, rp)

    d1, d2 = ATT_GROUPS[1][1], ATT_GROUPS[2][1]
    main, = call(0, INPROJ_TOKEN_TILE, w_main, [MAIN_COLS], [1], "inproj_main")
    mgates, qkv1, qkv2 = call(1, INPROJ_REST_TOKEN_TILE, w_rest,
                              [MGATE_COLS, 3 * GROUP_COLS, 3 * GROUP_COLS], [1, d1, d2], "inproj_rest")
    return main, mgates, qkv1, qkv2


def _lru_kslab0(c):
    return jnp.minimum(jnp.maximum(2 * c - 1, 0), D_RNN // LANES - LRU_KSLABS)


LRU_CONV_ROWS = 128


def _lru_kernel(x_ref, gate_ref, cw_ref, cb_ref, wg_ref, gb_ref, lam_ref, wout_ref, o_ref,
                xc_scr, acc_scr, a_scr, u_scr, h_scr, pad_scr):
    c = pl.program_id(1)
    seq = x_ref.shape[0]
    nsl = LRU_CHUNK // LANES
    chains = [(d, i) for d in range(2) for i in range(nsl)]

    @pl.when(c == 0)
    def _conv():
        pad_scr[0:8, :] = jnp.zeros((8, LANES), F32)
        pad_scr[seq + 8:seq + 16, :] = jnp.zeros((8, LANES), F32)
        for j in range(D_RNN // LANES):
            sl = slice(j * LANES, (j + 1) * LANES)
            pad_scr[8:seq + 8, :] = x_ref[:, sl].astype(F32)
            bias = cb_ref[:, sl]
            taps = [cw_ref[k:k + 1, sl] for k in range(LRU_CONV)]

            def conv_body(rb, carry):
                r0 = pl.multiple_of(rb * LRU_CONV_ROWS, LRU_CONV_ROWS)
                xc = bias
                for k in range(LRU_CONV):
                    off = 8 + k - LRU_CONV // 2
                    xc = xc + pad_scr[pl.ds(r0 + off, LRU_CONV_ROWS), :] * taps[k]
                xc_scr[j, pl.ds(r0, LRU_CONV_ROWS), :] = xc
                return carry

            lax.fori_loop(0, seq // LRU_CONV_ROWS, conv_body, 0, unroll=4)

    k0 = _lru_kslab0(c)
    lam = lam_ref[...]
    half_l2 = (-0.5 * LRU_C * LOG2E) * (jnp.maximum(-lam, 0.0) + jnp.log1p(jnp.exp(-jnp.abs(lam))))

    def pitched_rows(s, sb):
        return pl.ds(pl.multiple_of((s * LRU_SUBS_PER_BLOCK + sb) * LRU_PITCH, 8), LRU_SUBLEN)

    def gates_body(s, carry):
        r0 = pl.multiple_of(s * LRU_TBLOCK, LRU_TBLOCK)
        lhs =jnp.concatenate([xc_scr[k0 + i, pl.ds(r0, LRU_TBLOCK), :] for i in range(LRU_KSLABS)],
                              axis=1).astype(BF16)
        half_xc = 0.5 * jnp.concatenate([xc_scr[nsl * c + i, pl.ds(r0, LRU_TBLOCK), :] for i in range(nsl)],
                                        axis=1)
        for d in range(2):
            za = jnp.dot(lhs, wg_ref[d, 0], preferred_element_type=F32) + gb_ref[2 * d:2 * d + 1, :]
            zx = jnp.dot(lhs, wg_ref[d, 1], preferred_element_type=F32) + gb_ref[2 * d + 1:2 * d + 2, :]
            hl = half_l2[d:d + 1, :]
            a = jnp.exp2(jnp.tanh(za) * hl + hl)
            t = 1.0 - a * a
            root = jnp.where(t > 0.0, t * lax.rsqrt(t), 0.0)
            u = root * ((jnp.tanh(zx) + 1.0) * half_xc)
            for i in range(nsl):
                for sb in range(LRU_SUBS_PER_BLOCK):
                    rows = slice(sb * LRU_SUBLEN, (sb + 1) * LRU_SUBLEN)
                    a_scr[d, i, pitched_rows(s, sb), :] = a[rows, i * LANES:(i + 1) * LANES]
                    u_scr[d, i, pitched_rows(s, sb), :] = u[rows, i * LANES:(i + 1) * LANES]
        return carry

    lax.fori_loop(0, seq // LRU_TBLOCK, gates_body, 0, unroll=True)

    def step_rows(d, t):
        r = t if d == 0 else LRU_SUBLEN - 1 - t
        return pl.ds(r, LRU_NSUB, stride=LRU_PITCH)

    def ends_body(t, carry):
        out = []
        for n, (d, i) in enumerate(chains):
            av = a_scr[d, i, step_rows(d, t), :]
            out += [av * carry[2 * n] + u_scr[d, i, step_rows(d, t), :], av * carry[2 * n + 1]]
        return tuple(out)

    init = (jnp.zeros((LRU_NSUB, LANES), F32), jnp.ones((LRU_NSUB, LANES), F32)) * len(chains)
    ends = lax.fori_loop(0, LRU_SUBLEN, ends_body, init, unroll=32)

    sub = lax.broadcasted_iota(jnp.int32, (LRU_NSUB, LANES), 0)
    starts = []
    for n, (d, i) in enumerate(chains):
        h_end, cum_end = ends[2 * n], ends[2 * n + 1]
        cin = jnp.zeros((LRU_NSUB, LANES), F32)
        for step in range(LRU_NSUB - 1):
            nxt = h_end + cum_end * cin
            if d == 0:
                cin = jnp.where(sub == step + 1, pltpu.roll(nxt, 1, 0), cin)
            else:
                cin = jnp.where(sub == LRU_NSUB - 2 - step, pltpu.roll(nxt, LRU_NSUB - 1, 0), cin)
        starts.append(cin)

    def scan_body(t, carry):
        out = []
        for n, (d, i) in enumerate(chains):
            h = a_scr[d, i, step_rows(d, t), :] * carry[n] + u_scr[d, i, step_rows(d, t), :]
            h_scr[d, i, step_rows(d, t), :] = h
            out.append(h)
        return tuple(out)

    lax.fori_loop(0, LRU_SUBLEN, scan_body, tuple(starts), unroll=32)

    def out_body(first, s, carry):
        r0 = pl.multiple_of(s * LRU_TBLOCK, LRU_TBLOCK)
        hsum = jnp.concatenate(
            [jnp.concatenate([h_scr[0, i, pitched_rows(s, sb), :] + h_scr[1, i, pitched_rows(s, sb), :]
                              for sb in range(LRU_SUBS_PER_BLOCK)], axis=0) for i in range(nsl)], axis=1)
        y = (_gelu_tanh(gate_ref[pl.ds(r0, LRU_TBLOCK), :].astype(F32)) * hsum).astype(BF16)
        part = jnp.dot(y, wout_ref[...], preferred_element_type=F32)
        if first:
            acc_scr[pl.ds(r0, LRU_TBLOCK), :] = part
        else:
            acc_scr[pl.ds(r0, LRU_TBLOCK), :] += part
        return carry

    @pl.when(c == 0)
    def _out_first():
        lax.fori_loop(0, seq // LRU_TBLOCK, functools.partial(out_body, True), 0, unroll=True)

    @pl.when(c > 0)
    def _out_rest():
        lax.fori_loop(0, seq // LRU_TBLOCK, functools.partial(out_body, False), 0, unroll=True)

    @pl.when(c == LRU_NCHUNK - 1)
    def _emit():
        o_ref[...] = acc_scr[...].astype(BF16)


def _lru(proj3, cw, cb, wg, gb, lam, wout):
    b, seq, _ = proj3.shape
    assert seq == LRU_NSUB * LRU_SUBLEN and seq % LRU_TBLOCK == 0
    nsl = LRU_CHUNK // LANES
    gate_blk0 = D_RNN // LRU_CHUNK
    return pl.pallas_call(
        _lru_kernel,
        grid=(b, LRU_NCHUNK),
        in_specs=[
            pl.BlockSpec((None, seq, D_RNN), lambda i, c: (i, 0, 0)),
            pl.BlockSpec((None, seq, LRU_CHUNK), lambda i, c: (i, 0, gate_blk0 + c)),
            _resident((LRU_CONV, D_RNN)),
            _resident((1, D_RNN)),
            pl.BlockSpec((None, 2, 2, LRU_KSLABS * LANES, LRU_CHUNK), lambda i, c: (c, 0, 0, 0, 0)),
            pl.BlockSpec((None, 4, LRU_CHUNK), lambda i, c: (c, 0, 0)),
            pl.BlockSpec((None, 2, LRU_CHUNK), lambda i, c: (c, 0, 0)),
            pl.BlockSpec((LRU_CHUNK, D_MODEL), lambda i, c: (c, 0)),
        ],
        out_specs=pl.BlockSpec((None, seq, D_MODEL), lambda i, c: (i, 0, 0)),
        out_shape=jax.ShapeDtypeStruct((b, seq, D_MODEL), BF16),
        scratch_shapes=[
            pltpu.VMEM((D_RNN // LANES, seq, LANES), F32),
            pltpu.VMEM((seq, D_MODEL), F32),
            pltpu.VMEM((2, nsl, LRU_NSUB * LRU_PITCH, LANES), F32),
            pltpu.VMEM((2, nsl, LRU_NSUB * LRU_PITCH, LANES), F32),
            pltpu.VMEM((2, nsl, LRU_NSUB * LRU_PITCH, LANES), F32),
            pltpu.VMEM((seq + 16, LANES), F32),
        ],
        compiler_params=_cparams(("parallel", "arbitrary"), LRU_VMEM_LIMIT_BYTES),
        name="rglru",
    )(proj3, proj3, cw, cb, wg, gb, lam, wout)


def _attn_kernel(*refs, rpb, fused, row_max):
    n_in = 1 if fused else 3
    qkv_refs = refs[:n_in]
    bias_ref, shift_ref, o_ref, lse_ref, kt_scr = refs[n_in:]
    length = o_ref.shape[0]
    nkb = length // ATT_QBLOCK
    lane = lax.broadcasted_iota(jnp.int32, (ATT_QBLOCK, LANES), 1)
    lo_half = lane < HEAD_DIM
    lo_keys = lax.broadcasted_iota(jnp.int32, (2 * ATT_QBLOCK, LANES), 1) < HEAD_DIM
    npairs = GROUP_COLS // LANES

    def scores_block(qp_of, kt_of, vw_of, bias):
        outs = []
        lse_tile = jnp.zeros((ATT_QBLOCK, LANES), F32)
        for p in range(npairs):
            qp, kt, vw = qp_of(p), kt_of(p), vw_of(p)
            full, shifts = [], []
            for hh in range(LANES // HEAD_DIM):
                own = lo_half if hh == 0 else jnp.logical_not(lo_half)
                qm = jnp.where(own, qp, jnp.zeros_like(qp))
                s = jnp.dot(qm, kt, preferred_element_type=F32) + bias
                if row_max:
                    m = jnp.max(s, axis=-1, keepdims=True)
                    s = s - m
                    shifts.append(m)
                e = jnp.exp2(s).astype(BF16)
                v1 = jnp.where(lo_keys if hh == 0 else jnp.logical_not(lo_keys), vw, jnp.ones_like(vw))
                full.append(jnp.dot(e, v1, preferred_element_type=F32))
            numer = jnp.where(lo_half, full[0], full[1])
            denom = pltpu.roll(jnp.where(lo_half, full[1], full[0]), HEAD_DIM, 1)
            outs.append(numer * (1.0 / denom))
            shift = jnp.where(lo_half, shifts[0], shifts[1]) if row_max else shift_ref[...]
            lse = LN2 * (shift + jnp.log2(denom))
            lse_tile = jnp.where(jnp.logical_or(lane == p, lane == HEAD_DIM + p), lse, lse_tile)
        return outs, lse_tile

    for r in range(rpb):
        def cols(part, p, r=r):
            base = ((r * 3 + part) * GROUP_COLS if fused else 0) + p * LANES
            return slice(base, base + LANES)

        q_ref, k_ref, v_ref = (qkv_refs[0],) * 3 if fused else qkv_refs

        def transpose_body(kb, carry):
            k0 = pl.multiple_of(kb * ATT_QBLOCK, ATT_QBLOCK)
            for p in range(npairs):
                kt_scr[p, kb] = k_ref[pl.ds(k0, ATT_QBLOCK), cols(1, p)].T
            return carry

        lax.fori_loop(0, nkb, transpose_body, 0, unroll=True)

        def store(rows_list, outs, lse_tile, r=r):
            for dst_rows, src in rows_list:
                for p in range(npairs):
                    c0 = r * GROUP_COLS + p * LANES
                    o_ref[dst_rows, c0:c0 + LANES] = outs[p][src].astype(BF16)
                lse_ref[dst_rows, r * LANES:(r + 1) * LANES] = lse_tile[src]

        def interior(i, carry):
            k0 = pl.multiple_of(i * ATT_QBLOCK, ATT_QBLOCK)
            q0 = pl.multiple_of(k0 + ATT_HALF, ATT_HALF)
            outs, lse_tile = scores_block(
                lambda p: q_ref[pl.ds(q0, ATT_QBLOCK), cols(0, p)],
                lambda p: jnp.concatenate([kt_scr[p, i], kt_scr[p, i + 1]], axis=1),
                lambda p: v_ref[pl.ds(k0, 2 * ATT_QBLOCK), cols(2, p)],
                bias_ref[0])
            store([(pl.ds(q0, ATT_QBLOCK), slice(None))], outs, lse_tile)
            return carry

        if nkb > 1:
            lax.fori_loop(0, nkb - 1, interior, 0, unroll=True)

        head_rows, tail_rows = slice(0, ATT_HALF), slice(length - ATT_HALF, length)
        tail_keys = slice(length - ATT_QBLOCK, length)
        outs, lse_tile = scores_block(
            lambda p: jnp.concatenate([q_ref[head_rows, cols(0, p)], q_ref[tail_rows, cols(0, p)]], axis=0),
            lambda p: jnp.concatenate([kt_scr[p, 0], kt_scr[p, nkb - 1]], axis=1),
            lambda p: jnp.concatenate([v_ref[0:ATT_QBLOCK, cols(2, p)], v_ref[tail_keys, cols(2, p)]], axis=0),
            bias_ref[1])
        store([(head_rows, slice(0, ATT_HALF)), (tail_rows, slice(ATT_HALF, ATT_QBLOCK))], outs, lse_tile)


def _attn_bias():
    i = jnp.arange(ATT_QBLOCK)[:, None]
    j = jnp.arange(2 * ATT_QBLOCK)[None, :]
    interior = jnp.abs(i + ATT_HALF - j) <= ATT_HALF
    first = (i < ATT_HALF) & (j < ATT_QBLOCK) & (jnp.abs(i - j) <= ATT_HALF)
    last = (i >= ATT_HALF) & (j >= ATT_QBLOCK) & (jnp.abs(i - (j - ATT_QBLOCK)) <= ATT_HALF)
    return jnp.stack([jnp.where(interior, 0.0, NEG), jnp.where(first | last, 0.0, NEG)]).astype(F32)


def _attention_group(qkv, b, seq, window, dilation, score_bound):
    length = seq // dilation
    assert (window // 2) // dilation == ATT_HALF and length % ATT_QBLOCK == 0
    pv = qkv.reshape(b, length, qkv.shape[1])
    fused = dilation > 1
    rpb = min(dilation, max(1, ATT_MIN_STEP_TOKENS // length))
    if fused:
        qkv_specs = [pl.BlockSpec((None, length, rpb * 3 * GROUP_COLS), lambda i, m: (i, 0, m))]
    else:
        qkv_specs = [pl.BlockSpec((None, length, GROUP_COLS), lambda i, m, part=part: (i, 0, QKV0_CHUNK0 + part))
                     for part in range(3)]

    def call(row_max, bias, shift):
        return pl.pallas_call(
            functools.partial(_attn_kernel, rpb=rpb, fused=fused, row_max=row_max),
            grid=(b, dilation // rpb),
            in_specs=qkv_specs + [_resident(bias.shape), _resident(shift.shape)],
            out_specs=[
                pl.BlockSpec((None, length, rpb * GROUP_COLS), lambda i, m: (i, 0, m)),
                pl.BlockSpec((None, length, rpb * LANES), lambda i, m: (i, 0, m)),
            ],
            out_shape=[
                jax.ShapeDtypeStruct((b, length, dilation * GROUP_COLS), BF16),
                jax.ShapeDtypeStruct((b, length, dilation * LANES), F32),
            ],
            scratch_shapes=[pltpu.VMEM((GROUP_COLS // LANES, length // ATT_QBLOCK, LANES, ATT_QBLOCK), BF16)],
            compiler_params=_cparams(("parallel", "parallel")),
            name=f"attn_d{dilation}" + ("_rowmax" if row_max else ""),
        )(*([pv] * len(qkv_specs)), bias, shift)

    bias = _attn_bias()
    shift = jnp.full((1, LANES), score_bound, F32)
    o, lse = lax.cond(score_bound <= ATT_MAX_CONST_SHIFT,
                      lambda: call(False, bias - score_bound, shift),
                      lambda: call(True, bias, shift))
    return o.reshape(b * length, dilation * GROUP_COLS), lse.reshape(b * length, dilation * LANES)


def _merge_kernel(x_ref, a_ref, o0_ref, o1_ref, o2_ref, l0_ref, l1_ref, l2_ref, ga_ref, gb_ref,
                  watt_ref, wo_ref, ex_ref, y_ref, o_scr, l_scr):
    tm = x_ref.shape[0]
    nt = GROUP_COLS // LANES
    for gi, (og_ref, lg_ref) in enumerate(((o1_ref, l1_ref), (o2_ref, l2_ref))):
        dil = ATT_GROUPS[gi + 1][1]
        for m in range(dil):
            rows = pl.ds(m, tm // dil, stride=dil)
            l_scr[gi, rows, :] = lg_ref[:, m * LANES:(m + 1) * LANES]
            for t in range(nt):
                col = m * GROUP_COLS + t * LANES
                o_scr[gi, t, rows, :] = og_ref[:, col:col + LANES].astype(F32)
    lses = [l0_ref[...], l_scr[0], l_scr[1]]
    group_out = [o0_ref[...].astype(F32)] + [
        jnp.concatenate([o_scr[gi, t] for t in range(nt)], axis=1) for gi in range(2)]
    mx = jnp.maximum(jnp.maximum(lses[0], lses[1]), lses[2])
    es = [jnp.exp(l - mx) for l in lses]
    den = es[0] + es[1] + es[2]
    ex = ex_ref[...]
    o = None
    for e, og in zip(es, group_out):
        w = e / den
        hi = w.astype(BF16)
        lo = (w - hi.astype(F32)).astype(BF16)
        wx = jnp.dot(hi, ex, preferred_element_type=F32) + jnp.dot(lo, ex, preferred_element_type=F32)
        t = wx * og
        o = t if o is None else o + t
    b_out = jnp.dot(o.astype(BF16), watt_ref[...], preferred_element_type=F32)
    mix = (_sigmoid(ga_ref[...].astype(F32)) * a_ref[...].astype(F32)
           + _sigmoid(gb_ref[...].astype(F32)) * b_out)
    y_ref[...] = x_ref[...] + jnp.dot(mix.astype(BF16), wo_ref[...], preferred_element_type=F32)


def _merge(x2, a_out, outs, lses, mgates, watt, wo, ex):
    n = x2.shape[0]
    tm = TOKEN_TILE
    row = lambda cols, dil=1: pl.BlockSpec((tm // dil, cols * dil), lambda i: (i, 0))
    d1, d2 = ATT_GROUPS[1][1], ATT_GROUPS[2][1]
    return pl.pallas_call(
        _merge_kernel,
        grid=(n // tm,),
        in_specs=[
            row(D_MODEL), row(D_MODEL),
            row(GROUP_COLS), row(GROUP_COLS, d1), row(GROUP_COLS, d2),
            row(LANES), row(LANES, d1), row(LANES, d2),
            pl.BlockSpec((tm, D_MODEL), lambda i: (i, 0)),
            pl.BlockSpec((tm, D_MODEL), lambda i: (i, 1)),
            _resident((GROUP_COLS, D_MODEL)),
            _resident((D_MODEL, D_MODEL)),
            _resident((LANES, GROUP_COLS)),
        ],
        out_specs=row(D_MODEL),
        out_shape=jax.ShapeDtypeStruct((n, D_MODEL), F32),
        scratch_shapes=[
            pltpu.VMEM((2, GROUP_COLS // LANES, tm, LANES), F32),
            pltpu.VMEM((2, tm, LANES), F32),
        ],
        compiler_params=_cparams(("parallel",)),
        name="merge",
    )(x2, a_out, *outs, *lses, mgates, mgates, watt, wo, ex)


FFN_HALO = 16
FFN_TOKEN_TILE = 1024
FFN_CHUNK = 1024


def _ffn_kernel(x_ref, xp_ref, xn_ref, g2_ref, wup_ref, cw_ref, cb_ref, wdn_ref, y_ref, lhs_scr,
                *, tiles_per_seq):
    tm = x_ref.shape[0]
    ti = pl.program_id(0) % tiles_per_seq
    g2 = g2_ref[...]
    x = x_ref[...]
    keep_prev = jnp.where(ti == 0, 0.0, 1.0)
    keep_next = jnp.where(ti == tiles_per_seq - 1, 0.0, 1.0)
    xn = _rms_rows(x, g2).astype(BF16)
    lhs_scr[0:FFN_HALO, :] = (_rms_rows(xp_ref[...], g2) * keep_prev).astype(BF16)
    lhs_scr[FFN_HALO:FFN_HALO + tm, :] = xn
    lhs_scr[FFN_HALO + tm:, :] = (_rms_rows(xn_ref[...], g2) * keep_next).astype(BF16)
    acc = x
    for j in range(D_FF // FFN_CHUNK):
        sl = slice(j * FFN_CHUNK, (j + 1) * FFN_CHUNK)
        gext =jnp.dot(lhs_scr[...], wup_ref[:, sl], preferred_element_type=F32)
        rows = gext.shape[0]
        gc = cb_ref[:, sl]
        for k in range(FFN_CONV):
            shift = (FFN_CONV // 2 - k) % rows
            gk = gext if shift == 0 else pltpu.roll(gext, shift, 0)
            gc = gc + gk[FFN_HALO:FFN_HALO + tm, :] * cw_ref[k:k + 1, sl]
        val = jnp.dot(xn, wup_ref[:, D_FF + j * FFN_CHUNK:D_FF + (j + 1) * FFN_CHUNK],
                      preferred_element_type=F32)
        h = (_gelu_tanh(gc) * val).astype(BF16)
        acc = acc + jnp.dot(h, wdn_ref[sl, :], preferred_element_type=F32)
    y_ref[...] = acc


def _ffn(x2, seq, g2, wup, cw, cb, wdn):
    n = x2.shape[0]
    tm = FFN_TOKEN_TILE
    hb = tm // FFN_HALO
    last = n // FFN_HALO - 1
    return pl.pallas_call(
        functools.partial(_ffn_kernel, tiles_per_seq=seq // tm),
        grid=(n // tm,),
        in_specs=[
            pl.BlockSpec((tm, D_MODEL), lambda i: (i, 0)),
            pl.BlockSpec((FFN_HALO, D_MODEL), lambda i: (jnp.maximum(i * hb - 1, 0), 0)),
            pl.BlockSpec((FFN_HALO, D_MODEL), lambda i: (jnp.minimum((i + 1) * hb, last), 0)),
            _resident((1, D_MODEL)),
            _resident((D_MODEL, 2 * D_FF)),
            _resident((FFN_CONV, D_FF)),
            _resident((1, D_FF)),
            _resident((D_FF, D_MODEL)),
        ],
        out_specs=pl.BlockSpec((tm, D_MODEL), lambda i: (i, 0)),
        out_shape=jax.ShapeDtypeStruct((n, D_MODEL), F32),
        scratch_shapes=[pltpu.VMEM((tm + 2 * FFN_HALO, D_MODEL), BF16)],
        compiler_params=_cparams(("parallel",)),
        name="convffn",
    )(x2, x2, x2, g2, wup, cw, cb, wdn)


def _rope_tables(seq):
    pos = jnp.arange(seq, dtype=F32)
    inv = ROPE_THETA ** (-jnp.arange(0, ROT_DIM, 2, dtype=F32) / ROT_DIM)
    ang = pos[:, None] * inv[None, :]
    cos, sin = jnp.cos(ang), jnp.sin(ang)
    half = ROT_DIM // 2
    pad = HEAD_DIM - ROT_DIM
    one = jnp.ones((seq, pad), F32)
    zero = jnp.zeros((seq, pad), F32)
    zh = jnp.zeros((seq, half), F32)
    per_head = lambda parts: jnp.tile(jnp.concatenate(parts, axis=1), (1, LANES // HEAD_DIM))
    ra = per_head([cos, cos, one])
    rm = per_head([-sin, zh, zero])
    rp = per_head([zh, sin, zero])
    return ra, rm, rp


def _block_diag_dense(w):
    nb, c, d = w.shape
    return jnp.einsum('ncd,nm->ncmd', w, jnp.eye(nb, dtype=w.dtype)).reshape(nb * c, nb * d)


def _prep_layer(p, seq):
    q = {}
    q['g1'] = p['norm1_g'].reshape(1, D_MODEL)
    q_lo = 2 * D_RNN
    seg = lambda part, g: (q_lo + part * ATT_COLS + g * GROUP_COLS, q_lo + part * ATT_COLS + (g + 1) * GROUP_COLS)
    qkv = lambda g: [seg(part, g) for part in range(3)]
    w_in = p['w_in'].astype(BF16)
    take = lambda segs: jnp.concatenate([w_in[:, lo:hi] for lo, hi in segs], axis=1)
    q['w_main'] = take([(0, q_lo)] + qkv(0))
    q['w_rest'] = take([(q_lo + 3 * ATT_COLS, IN_COLS)] + qkv(1) + qkv(2))
    head = jnp.arange(MXU_DIM) // HEAD_DIM
    q['bd'] = (head[:, None] == head[None, :]).astype(BF16)
    q['qg'] = jnp.tile(p['q_norm_g'], (1, HEADS_PER_GROUP)) * (HEAD_DIM ** -0.5 * LOG2E)
    q['kg'] = jnp.tile(p['k_norm_g'], (1, HEADS_PER_GROUP))
    q['score_bound'] = (HEAD_DIM * ATT_BOUND_MARGIN) * (jnp.max(jnp.abs(q['qg']), axis=1)
                                                        * jnp.max(jnp.abs(q['kg']), axis=1))
    q['rope'] = _rope_tables(seq)
    q['lru_cw'] = p['lru_conv_w']
    q['lru_cb'] = p['lru_conv_b'].reshape(1, D_RNN)
    dense = jnp.stack([jnp.stack([_block_diag_dense(p['lru_wa'][d]), _block_diag_dense(p['lru_wx'][d])])
                       for d in range(2)])
    dense = (0.5 * dense).astype(BF16)
    chunks = []
    for c in range(LRU_NCHUNK):
        k0 = min(max(2 * c - 1, 0), D_RNN // LANES - LRU_KSLABS) * LANES
        chunks.append(dense[:, :, k0:k0 + LRU_KSLABS * LANES, c * LRU_CHUNK:(c + 1) * LRU_CHUNK])
    q['lru_wg'] = jnp.stack(chunks)
    gb = 0.5 * jnp.stack([p['lru_ba'][0], p['lru_bx'][0], p['lru_ba'][1], p['lru_bx'][1]])
    q['lru_gb'] = gb.reshape(4, LRU_NCHUNK, LRU_CHUNK).transpose(1, 0, 2)
    q['lru_lam'] = p['lru_lambda'].reshape(2, LRU_NCHUNK, LRU_CHUNK).transpose(1, 0, 2)
    q['lru_wout'] = p['w_lru_out'].astype(BF16)
    q['watt'] = p['w_att_out'].astype(BF16)
    q['wo'] = p['w_o'].astype(BF16)
    lse_lane = jnp.arange(LANES)
    lane_head = jnp.where(lse_lane % HEAD_DIM < GROUP_COLS // LANES,
                          2 * (lse_lane % HEAD_DIM) + lse_lane // HEAD_DIM, -1)
    q['expand'] = (lane_head[:, None] == jnp.arange(GROUP_COLS)[None, :] // HEAD_DIM).astype(BF16)
    q['g2'] = p['norm2_g'].reshape(1, D_MODEL)
    q['wup'] = p['w_up'].astype(BF16)
    q['ffn_cw'] = p['ffn_conv_w']
    q['ffn_cb'] = p['ffn_conv_b'].reshape(1, D_FF)
    q['wdn'] = p['w_down'].astype(BF16)
    return q


def _layer(x, q):
    b, seq, _ = x.shape
    x2 = x.reshape(b * seq, D_MODEL)
    main, mgates, qkv1, qkv2 = _inproj(x2, seq, q['g1'], q['w_main'], q['w_rest'], q['bd'], q['qg'], q['kg'],
                                       *q['rope'])
    a_out = _lru(main.reshape(b, seq, MAIN_COLS), q['lru_cw'], q['lru_cb'], q['lru_wg'], q['lru_gb'],
                 q['lru_lam'], q['lru_wout'])
    outs, lses = [], []
    for g, (qkv, (window, dilation)) in enumerate(zip((main, qkv1, qkv2), ATT_GROUPS)):
        o, lse = _attention_group(qkv, b, seq, window, dilation, q['score_bound'][g])
        outs.append(o)
        lses.append(lse)
    x1 = _merge(x2, a_out.reshape(b * seq, D_MODEL), outs, lses, mgates, q['watt'], q['wo'], q['expand'])
    y = _ffn(x1, seq, q['g2'], q['wup'], q['ffn_cw'], q['ffn_cb'], q['wdn'])
    return y.reshape(b, seq, D_MODEL)


def kernel(x_prompt, x_sample, norm1_g, w_in, lru_conv_w, lru_conv_b, lru_wa, lru_ba, lru_wx, lru_bx,
           lru_lambda, w_lru_out, q_norm_g, k_norm_g, w_att_out, w_o, norm2_g, w_up, ffn_conv_w,
           ffn_conv_b, w_down):
    params = dict(norm1_g=norm1_g, w_in=w_in, lru_conv_w=lru_conv_w, lru_conv_b=lru_conv_b,
                  lru_wa=lru_wa, lru_ba=lru_ba, lru_wx=lru_wx, lru_bx=lru_bx, lru_lambda=lru_lambda,
                  w_lru_out=w_lru_out, q_norm_g=q_norm_g, k_norm_g=k_norm_g, w_att_out=w_att_out,
                  w_o=w_o, norm2_g=norm2_g, w_up=w_up, ffn_conv_w=ffn_conv_w, ffn_conv_b=ffn_conv_b,
                  w_down=w_down)
    depth = norm1_g.shape[0]
    seq = x_prompt.shape[1]
    assert x_sample.shape[1] == seq
    layers = [_prep_layer({name: p[l] for name, p in params.items()}, seq) for l in range(depth)]
    ys = []
    for x in (x_prompt, x_sample):
        for q in layers:
            x = _layer(x, q)
        ys.append(x)
    return tuple(ys)
```

```python
import functools

import jax
import jax.numpy as jnp
from jax import lax
from jax.experimental import pallas as pl
from jax.experimental.pallas import tpu as pltpu

F32 = jnp.float32
BF16 = jnp.bfloat16

D_MODEL = 1024
D_RNN = 1280
LRU_BLOCKS = 16
LRU_BW = D_RNN // LRU_BLOCKS
LRU_C = 8.0
LRU_CONV = 4
ATT_GROUPS = ((128, 1), (512, 4), (2048, 16))
N_GROUPS = len(ATT_GROUPS)
HEADS_PER_GROUP = 8
HEAD_DIM = 64
GROUP_COLS = HEADS_PER_GROUP * HEAD_DIM
ATT_COLS = N_GROUPS * GROUP_COLS
ROT_DIM = HEAD_DIM // 4
ROPE_THETA = 500000.0
D_FF = 3 * D_MODEL
FFN_CONV = 3
EPS = 1e-6
NEG = -1e30
LOG2E = 1.4426950408889634
LN2 = 0.6931471805599453
IN_COLS = 2 * D_RNN + 3 * ATT_COLS + 2 * D_MODEL

LANES = 128
MXU_DIM = 256
VMEM_LIMIT_BYTES = 56 * 1024 * 1024
LRU_VMEM_LIMIT_BYTES = 58 * 1024 * 1024

CHUNK = 512
MAIN_COLS = 2 * D_RNN + 3 * GROUP_COLS
QKV0_CHUNK0 = 2 * D_RNN // CHUNK
MGATE_COLS = 2 * D_MODEL

TOKEN_TILE = 512
INPROJ_TOKEN_TILE = 1024
INPROJ_REST_TOKEN_TILE = 1024
ATT_QBLOCK = 128
ATT_HALF = 64
ATT_MIN_STEP_TOKENS = 2048
ATT_MAX_CONST_SHIFT = 60.0
ATT_BOUND_MARGIN = 1.02

LRU_CHUNK = 256
LRU_NCHUNK = D_RNN // LRU_CHUNK
LRU_KSLABS = 4
LRU_TBLOCK = 256
LRU_SUBLEN = 128
LRU_NSUB = 16
LRU_PITCH = LRU_SUBLEN + 8
LRU_SUBS_PER_BLOCK = LRU_TBLOCK // LRU_SUBLEN


def _cparams(sem, vmem_limit=VMEM_LIMIT_BYTES):
    return pltpu.CompilerParams(dimension_semantics=sem, vmem_limit_bytes=vmem_limit)


def _resident(shape):
    return pl.BlockSpec(shape, lambda *_: (0,) * len(shape), pipeline_mode=pl.Buffered(1))


def _sigmoid(x):
    return 0.5 * jnp.tanh(0.5 * x) + 0.5


def _gelu_tanh(x):
    return 0.5 * x * (1.0 + jnp.tanh(0.7978845608028654 * (x + 0.044715 * (x * x * x))))


def _rms_rows(x, g):
    ms = jnp.mean(x * x, axis=-1, keepdims=True)
    return x * lax.rsqrt(ms + EPS) * g


def _inproj_plan(call):
    if call == 0:
        return [((0, j - QKV0_CHUNK0) if j >= QKV0_CHUNK0 else None, 0, j * CHUNK)
                for j in range(MAIN_COLS // CHUNK)]
    plain = [(None, 0, j * CHUNK) for j in range(MGATE_COLS // CHUNK)]
    return plain + [((g, part), g, None) for g in (1, 2) for part in range(3)]


def _inproj_kernel(x_ref, g1_ref, w_ref, bd_ref, qg_ref, kg_ref, ra_ref, rm_ref, rp_ref, *outs, plan):
    tm = x_ref.shape[0]
    tile_scr = outs[-1]
    xn = _rms_rows(x_ref[...], g1_ref[...]).astype(BF16)
    ra = ra_ref[...]
    rm = rm_ref[...]
    rp = rp_ref[...]
    heavy = [j for j in range(len(plan)) if plan[j][0] is not None][::-1]
    plain = [j for j in range(len(plan)) if plan[j][0] is None]
    order = []
    while heavy or plain:
        take = -(-len(heavy) // max(len(plain), 1))
        order += heavy[:take] + plain[:1]
        heavy, plain = heavy[take:], plain[1:]
    for j in order:
        role, dst, c0 = plan[j]
        o_ref = outs[dst]
        acc = jnp.dot(xn, w_ref[:, j * CHUNK:(j + 1) * CHUNK], preferred_element_type=F32)
        if role is None:
            o_ref[:, c0:c0 + CHUNK] = acc.astype(BF16)
            continue
        g, part = role
        tiles = []
        if part < 2:
            gain = (qg_ref if part == 0 else kg_ref)[g:g + 1, :]
            sq = (acc * acc).astype(BF16)
            bd = bd_ref[...]
            for t in range(CHUNK // MXU_DIM):
                sl = slice(t * MXU_DIM, (t + 1) * MXU_DIM)
                ss = jnp.dot(sq[:, sl], bd, preferred_element_type=F32)
                y = acc[:, sl] * lax.rsqrt(ss * (1.0 / HEAD_DIM) + EPS) * gain[:, sl]
                for u in range(MXU_DIM // LANES):
                    yt = y[:, u * LANES:(u + 1) * LANES]
                    tiles.append(yt * ra + pltpu.roll(yt, LANES - ROT_DIM // 2, 1) * rm
                                 + pltpu.roll(yt, ROT_DIM // 2, 1) * rp)
        else:
            tiles = [acc[:, t * LANES:(t + 1) * LANES] for t in range(CHUNK // LANES)]
        if g == 0:
            for t, val in enumerate(tiles):
                o_ref[:, c0 + t * LANES:c0 + (t + 1) * LANES] = val.astype(BF16)
        else:
            dil = ATT_GROUPS[g][1]
            for t, val in enumerate(tiles):
                tile_scr[t] = val
            for m in range(dil):
                for t in range(CHUNK // LANES):
                    col = (m * 3 + part) * GROUP_COLS + t * LANES
                    o_ref[:, col:col + LANES] = tile_scr[t, pl.ds(m, tm // dil, stride=dil), :].astype(BF16)


def _inproj(x2, seq, g1, w_main, w_rest, bd, qg, kg, ra, rm, rp):
    n = x2.shape[0]

    def call(which, tm, w, widths, dils, name):
        tiles_per_seq = seq // tm
        rope_spec = pl.BlockSpec((tm, LANES), lambda i: (i % tiles_per_seq, 0))
        scratch = [pltpu.VMEM((CHUNK // LANES, tm, LANES), F32)] if max(dils) > 1 else []
        return pl.pallas_call(
            functools.partial(_inproj_kernel, plan=_inproj_plan(which)),
            grid=(n // tm,),
            in_specs=[
                pl.BlockSpec((tm, D_MODEL), lambda i: (i, 0)),
                _resident((1, D_MODEL)),
                _resident(w.shape),
                _resident((MXU_DIM, MXU_DIM)),
                _resident((N_GROUPS, GROUP_COLS)),
                _resident((N_GROUPS, GROUP_COLS)),
                rope_spec, rope_spec, rope_spec,
            ],
            out_specs=[pl.BlockSpec((tm // d, c * d), lambda i: (i, 0)) for c, d in zip(widths, dils)],
            out_shape=[jax.ShapeDtypeStruct((n // d, c * d), BF16) for c, d in zip(widths, dils)],
            scratch_shapes=scratch,
            compiler_params=_cparams(("parallel",)),
            name=name,
        )(x2, g1, w, bd, qg, kg, ra, rm, rp)

    d1, d2 = ATT_GROUPS[1][1], ATT_GROUPS[2][1]
    main, = call(0, INPROJ_TOKEN_TILE, w_main, [MAIN_COLS], [1], "inproj_main")
    mgates, qkv1, qkv2 = call(1, INPROJ_REST_TOKEN_TILE, w_rest,
                              [MGATE_COLS, 3 * GROUP_COLS, 3 * GROUP_COLS], [1, d1, d2], "inproj_rest")
    return main, mgates, qkv1, qkv2


def _lru_kslab0(c):
    return jnp.minimum(jnp.maximum(2 * c - 1, 0), D_RNN // LANES - LRU_KSLABS)


LRU_CONV_ROWS = 128


def _lru_kernel(x_ref, gate_ref, cw_ref, cb_ref, wg_ref, gb_ref, lam_ref, wout_ref, o_ref,
                xc_scr, acc_scr, a_scr, u_scr, h_scr, pad_scr):
    c = pl.program_id(1)
    seq = x_ref.shape[0]
    nsl = LRU_CHUNK // LANES
    chains = [(d, i) for d in range(2) for i in range(nsl)]

    @pl.when(c == 0)
    def _conv():
        pad_scr[0:8, :] = jnp.zeros((8, LANES), F32)
        pad_scr[seq + 8:seq + 16, :] = jnp.zeros((8, LANES), F32)
        for j in range(D_RNN // LANES):
            sl = slice(j * LANES, (j + 1) * LANES)
            pad_scr[8:seq + 8, :] = x_ref[:, sl].astype(F32)
            bias = cb_ref[:, sl]
            taps = [cw_ref[k:k + 1, sl] for k in range(LRU_CONV)]

            def conv_body(rb, carry):
                r0 = pl.multiple_of(rb * LRU_CONV_ROWS, LRU_CONV_ROWS)
                xc = bias
                for k in range(LRU_CONV):
                    off = 8 + k - LRU_CONV // 2
                    xc = xc + pad_scr[pl.ds(r0 + off, LRU_CONV_ROWS), :] * taps[k]
                xc_scr[j, pl.ds(r0, LRU_CONV_ROWS), :] = xc
                return carry

            lax.fori_loop(0, seq // LRU_CONV_ROWS, conv_body, 0, unroll=4)

    k0 = _lru_kslab0(c)
    lam = lam_ref[...]
    half_l2 = (-0.5 * LRU_C * LOG2E) * (jnp.maximum(-lam, 0.0) + jnp.log1p(jnp.exp(-jnp.abs(lam))))

    def pitched_rows(s, sb):
        return pl.ds(pl.multiple_of((s * LRU_SUBS_PER_BLOCK + sb) * LRU_PITCH, 8), LRU_SUBLEN)

    def gates_body(s, carry):
        r0 = pl.multiple_of(s * LRU_TBLOCK, LRU_TBLOCK)
        lhs =jnp.concatenate([xc_scr[k0 + i, pl.ds(r0, LRU_TBLOCK), :] for i in range(LRU_KSLABS)],
                              axis=1).astype(BF16)
        half_xc = 0.5 * jnp.concatenate([xc_scr[nsl * c + i, pl.ds(r0, LRU_TBLOCK), :] for i in range(nsl)],
                                        axis=1)
        for d in range(2):
            za = jnp.dot(lhs, wg_ref[d, 0], preferred_element_type=F32) + gb_ref[2 * d:2 * d + 1, :]
            zx = jnp.dot(lhs, wg_ref[d, 1], preferred_element_type=F32) + gb_ref[2 * d + 1:2 * d + 2, :]
            hl = half_l2[d:d + 1, :]
            a = jnp.exp2(jnp.tanh(za) * hl + hl)
            t = 1.0 - a * a
            root = jnp.where(t > 0.0, t * lax.rsqrt(t), 0.0)
            u = root * ((jnp.tanh(zx) + 1.0) * half_xc)
            for i in range(nsl):
                for sb in range(LRU_SUBS_PER_BLOCK):
                    rows = slice(sb * LRU_SUBLEN, (sb + 1) * LRU_SUBLEN)
                    a_scr[d, i, pitched_rows(s, sb), :] = a[rows, i * LANES:(i + 1) * LANES]
                    u_scr[d, i, pitched_rows(s, sb), :] = u[rows, i * LANES:(i + 1) * LANES]
        return carry

    lax.fori_loop(0, seq // LRU_TBLOCK, gates_body, 0, unroll=True)

    def step_rows(d, t):
        r = t if d == 0 else LRU_SUBLEN - 1 - t
        return pl.ds(r, LRU_NSUB, stride=LRU_PITCH)

    def ends_body(t, carry):
        out = []
        for n, (d, i) in enumerate(chains):
            av = a_scr[d, i, step_rows(d, t), :]
            out += [av * carry[2 * n] + u_scr[d, i, step_rows(d, t), :], av * carry[2 * n + 1]]
        return tuple(out)

    init = (jnp.zeros((LRU_NSUB, LANES), F32), jnp.ones((LRU_NSUB, LANES), F32)) * len(chains)
    ends = lax.fori_loop(0, LRU_SUBLEN, ends_body, init, unroll=32)

    sub = lax.broadcasted_iota(jnp.int32, (LRU_NSUB, LANES), 0)
    starts = []
    for n, (d, i) in enumerate(chains):
        h_end, cum_end = ends[2 * n], ends[2 * n + 1]
        cin = jnp.zeros((LRU_NSUB, LANES), F32)
        for step in range(LRU_NSUB - 1):
            nxt = h_end + cum_end * cin
            if d == 0:
                cin = jnp.where(sub == step + 1, pltpu.roll(nxt, 1, 0), cin)
            else:
                cin = jnp.where(sub == LRU_NSUB - 2 - step, pltpu.roll(nxt, LRU_NSUB - 1, 0), cin)
        starts.append(cin)

    def scan_body(t, carry):
        out = []
        for n, (d, i) in enumerate(chains):
            h = a_scr[d, i, step_rows(d, t), :] * carry[n] + u_scr[d, i, step_rows(d, t), :]
            h_scr[d, i, step_rows(d, t), :] = h
            out.append(h)
        return tuple(out)

    lax.fori_loop(0, LRU_SUBLEN, scan_body, tuple(starts), unroll=32)

    def out_body(first, s, carry):
        r0 = pl.multiple_of(s * LRU_TBLOCK, LRU_TBLOCK)
        hsum = jnp.concatenate(
            [jnp.concatenate([h_scr[0, i, pitched_rows(s, sb), :] + h_scr[1, i, pitched_rows(s, sb), :]
                              for sb in range(LRU_SUBS_PER_BLOCK)], axis=0) for i in range(nsl)], axis=1)
        y = (_gelu_tanh(gate_ref[pl.ds(r0, LRU_TBLOCK), :].astype(F32)) * hsum).astype(BF16)
        part = jnp.dot(y, wout_ref[...], preferred_element_type=F32)
        if first:
            acc_scr[pl.ds(r0, LRU_TBLOCK), :] = part
        else:
            acc_scr[pl.ds(r0, LRU_TBLOCK), :] += part
        return carry

    @pl.when(c == 0)
    def _out_first():
        lax.fori_loop(0, seq // LRU_TBLOCK, functools.partial(out_body, True), 0, unroll=True)

    @pl.when(c > 0)
    def _out_rest():
        lax.fori_loop(0, seq // LRU_TBLOCK, functools.partial(out_body, False), 0, unroll=True)

    @pl.when(c == LRU_NCHUNK - 1)
    def _emit():
        o_ref[...] = acc_scr[...].astype(BF16)


def _lru(proj3, cw, cb, wg, gb, lam, wout):
    b, seq, _ = proj3.shape
    assert seq == LRU_NSUB * LRU_SUBLEN and seq % LRU_TBLOCK == 0
    nsl = LRU_CHUNK // LANES
    gate_blk0 = D_RNN // LRU_CHUNK
    return pl.pallas_call(
        _lru_kernel,
        grid=(b, LRU_NCHUNK),
        in_specs=[
            pl.BlockSpec((None, seq, D_RNN), lambda i, c: (i, 0, 0)),
            pl.BlockSpec((None, seq, LRU_CHUNK), lambda i, c: (i, 0, gate_blk0 + c)),
            _resident((LRU_CONV, D_RNN)),
            _resident((1, D_RNN)),
            pl.BlockSpec((None, 2, 2, LRU_KSLABS * LANES, LRU_CHUNK), lambda i, c: (c, 0, 0, 0, 0)),
            pl.BlockSpec((None, 4, LRU_CHUNK), lambda i, c: (c, 0, 0)),
            pl.BlockSpec((None, 2, LRU_CHUNK), lambda i, c: (c, 0, 0)),
            pl.BlockSpec((LRU_CHUNK, D_MODEL), lambda i, c: (c, 0)),
        ],
        out_specs=pl.BlockSpec((None, seq, D_MODEL), lambda i, c: (i, 0, 0)),
        out_shape=jax.ShapeDtypeStruct((b, seq, D_MODEL), BF16),
        scratch_shapes=[
            pltpu.VMEM((D_RNN // LANES, seq, LANES), F32),
            pltpu.VMEM((seq, D_MODEL), F32),
            pltpu.VMEM((2, nsl, LRU_NSUB * LRU_PITCH, LANES), F32),
            pltpu.VMEM((2, nsl, LRU_NSUB * LRU_PITCH, LANES), F32),
            pltpu.VMEM((2, nsl, LRU_NSUB * LRU_PITCH, LANES), F32),
            pltpu.VMEM((seq + 16, LANES), F32),
        ],
        compiler_params=_cparams(("parallel", "arbitrary"), LRU_VMEM_LIMIT_BYTES),
        name="rglru",
    )(proj3, proj3, cw, cb, wg, gb, lam, wout)


def _attn_kernel(*refs, rpb, fused, row_max):
    n_in = 1 if fused else 3
    qkv_refs = refs[:n_in]
    bias_ref, shift_ref, o_ref, lse_ref, kt_scr = refs[n_in:]
    length = o_ref.shape[0]
    nkb = length // ATT_QBLOCK
    lane = lax.broadcasted_iota(jnp.int32, (ATT_QBLOCK, LANES), 1)
    lo_half = lane < HEAD_DIM
    lo_keys = lax.broadcasted_iota(jnp.int32, (2 * ATT_QBLOCK, LANES), 1) < HEAD_DIM
    npairs = GROUP_COLS // LANES

    def scores_block(qp_of, kt_of, vw_of, bias):
        outs = []
        lse_tile = jnp.zeros((ATT_QBLOCK, LANES), F32)
        for p in range(npairs):
            qp, kt, vw = qp_of(p), kt_of(p), vw_of(p)
            full, shifts = [], []
            for hh in range(LANES // HEAD_DIM):
                own = lo_half if hh == 0 else jnp.logical_not(lo_half)
                qm = jnp.where(own, qp, jnp.zeros_like(qp))
                s = jnp.dot(qm, kt, preferred_element_type=F32) + bias
                if row_max:
                    m = jnp.max(s, axis=-1, keepdims=True)
                    s = s - m
                    shifts.append(m)
                e = jnp.exp2(s).astype(BF16)
                v1 = jnp.where(lo_keys if hh == 0 else jnp.logical_not(lo_keys), vw, jnp.ones_like(vw))
                full.append(jnp.dot(e, v1, preferred_element_type=F32))
            numer = jnp.where(lo_half, full[0], full[1])
            denom = pltpu.roll(jnp.where(lo_half, full[1], full[0]), HEAD_DIM, 1)
            outs.append(numer * (1.0 / denom))
            shift = jnp.where(lo_half, shifts[0], shifts[1]) if row_max else shift_ref[...]
            lse = LN2 * (shift + jnp.log2(denom))
            lse_tile = jnp.where(jnp.logical_or(lane == p, lane == HEAD_DIM + p), lse, lse_tile)
        return outs, lse_tile

    for r in range(rpb):
        def cols(part, p, r=r):
            base = ((r * 3 + part) * GROUP_COLS if fused else 0) + p * LANES
            return slice(base, base + LANES)

        q_ref, k_ref, v_ref = (qkv_refs[0],) * 3 if fused else qkv_refs

        def transpose_body(kb, carry):
            k0 = pl.multiple_of(kb * ATT_QBLOCK, ATT_QBLOCK)
            for p in range(npairs):
                kt_scr[p, kb] = k_ref[pl.ds(k0, ATT_QBLOCK), cols(1, p)].T
            return carry

        lax.fori_loop(0, nkb, transpose_body, 0, unroll=True)

        def store(rows_list, outs, lse_tile, r=r):
            for dst_rows, src in rows_list:
                for p in range(npairs):
                    c0 = r * GROUP_COLS + p * LANES
                    o_ref[dst_rows, c0:c0 + LANES] = outs[p][src].astype(BF16)
                lse_ref[dst_rows, r * LANES:(r + 1) * LANES] = lse_tile[src]

        def interior(i, carry):
            k0 = pl.multiple_of(i * ATT_QBLOCK, ATT_QBLOCK)
            q0 = pl.multiple_of(k0 + ATT_HALF, ATT_HALF)
            outs, lse_tile = scores_block(
                lambda p: q_ref[pl.ds(q0, ATT_QBLOCK), cols(0, p)],
                lambda p: jnp.concatenate([kt_scr[p, i], kt_scr[p, i + 1]], axis=1),
                lambda p: v_ref[pl.ds(k0, 2 * ATT_QBLOCK), cols(2, p)],
                bias_ref[0])
            store([(pl.ds(q0, ATT_QBLOCK), slice(None))], outs, lse_tile)
            return carry

        if nkb > 1:
            lax.fori_loop(0, nkb - 1, interior, 0, unroll=True)

        head_rows, tail_rows = slice(0, ATT_HALF), slice(length - ATT_HALF, length)
        tail_keys = slice(length - ATT_QBLOCK, length)
        outs, lse_tile = scores_block(
            lambda p: jnp.concatenate([q_ref[head_rows, cols(0, p)], q_ref[tail_rows, cols(0, p)]], axis=0),
            lambda p: jnp.concatenate([kt_scr[p, 0], kt_scr[p, nkb - 1]], axis=1),
            lambda p: jnp.concatenate([v_ref[0:ATT_QBLOCK, cols(2, p)], v_ref[tail_keys, cols(2, p)]], axis=0),
            bias_ref[1])
        store([(head_rows, slice(0, ATT_HALF)), (tail_rows, slice(ATT_HALF, ATT_QBLOCK))], outs, lse_tile)


def _attn_bias():
    i = jnp.arange(ATT_QBLOCK)[:, None]
    j = jnp.arange(2 * ATT_QBLOCK)[None, :]
    interior = jnp.abs(i + ATT_HALF - j) <= ATT_HALF
    first = (i < ATT_HALF) & (j < ATT_QBLOCK) & (jnp.abs(i - j) <= ATT_HALF)
    last = (i >= ATT_HALF) & (j >= ATT_QBLOCK) & (jnp.abs(i - (j - ATT_QBLOCK)) <= ATT_HALF)
    return jnp.stack([jnp.where(interior, 0.0, NEG), jnp.where(first | last, 0.0, NEG)]).astype(F32)


def _attention_group(qkv, b, seq, window, dilation, score_bound):
    length = seq // dilation
    assert (window // 2) // dilation == ATT_HALF and length % ATT_QBLOCK == 0
    pv = qkv.reshape(b, length, qkv.shape[1])
    fused = dilation > 1
    rpb = min(dilation, max(1, ATT_MIN_STEP_TOKENS // length))
    if fused:
        qkv_specs = [pl.BlockSpec((None, length, rpb * 3 * GROUP_COLS), lambda i, m: (i, 0, m))]
    else:
        qkv_specs = [pl.BlockSpec((None, length, GROUP_COLS), lambda i, m, part=part: (i, 0, QKV0_CHUNK0 + part))
                     for part in range(3)]

    def call(row_max, bias, shift):
        return pl.pallas_call(
            functools.partial(_attn_kernel, rpb=rpb, fused=fused, row_max=row_max),
            grid=(b, dilation // rpb),
            in_specs=qkv_specs + [_resident(bias.shape), _resident(shift.shape)],
            out_specs=[
                pl.BlockSpec((None, length, rpb * GROUP_COLS), lambda i, m: (i, 0, m)),
                pl.BlockSpec((None, length, rpb * LANES), lambda i, m: (i, 0, m)),
            ],
            out_shape=[
                jax.ShapeDtypeStruct((b, length, dilation * GROUP_COLS), BF16),
                jax.ShapeDtypeStruct((b, length, dilation * LANES), F32),
            ],
            scratch_shapes=[pltpu.VMEM((GROUP_COLS // LANES, length // ATT_QBLOCK, LANES, ATT_QBLOCK), BF16)],
            compiler_params=_cparams(("parallel", "parallel")),
            name=f"attn_d{dilation}" + ("_rowmax" if row_max else ""),
        )(*([pv] * len(qkv_specs)), bias, shift)

    bias = _attn_bias()
    shift = jnp.full((1, LANES), score_bound, F32)
    o, lse = lax.cond(score_bound <= ATT_MAX_CONST_SHIFT,
                      lambda: call(False, bias - score_bound, shift),
                      lambda: call(True, bias, shift))
    return o.reshape(b * length, dilation * GROUP_COLS), lse.reshape(b * length, dilation * LANES)


def _merge_kernel(x_ref, a_ref, o0_ref, o1_ref, o2_ref, l0_ref, l1_ref, l2_ref, ga_ref, gb_ref,
                  watt_ref, wo_ref, ex_ref, y_ref, o_scr, l_scr):
    tm = x_ref.shape[0]
    nt = GROUP_COLS // LANES
    for gi, (og_ref, lg_ref) in enumerate(((o1_ref, l1_ref), (o2_ref, l2_ref))):
        dil = ATT_GROUPS[gi + 1][1]
        for m in range(dil):
            rows = pl.ds(m, tm // dil, stride=dil)
            l_scr[gi, rows, :] = lg_ref[:, m * LANES:(m + 1) * LANES]
            for t in range(nt):
                col = m * GROUP_COLS + t * LANES
                o_scr[gi, t, rows, :] = og_ref[:, col:col + LANES].astype(F32)
    lses = [l0_ref[...], l_scr[0], l_scr[1]]
    group_out = [o0_ref[...].astype(F32)] + [
        jnp.concatenate([o_scr[gi, t] for t in range(nt)], axis=1) for gi in range(2)]
    mx = jnp.maximum(jnp.maximum(lses[0], lses[1]), lses[2])
    es = [jnp.exp(l - mx) for l in lses]
    den = es[0] + es[1] + es[2]
    ex = ex_ref[...]
    o = None
    for e, og in zip(es, group_out):
        w = e / den
        hi = w.astype(BF16)
        lo = (w - hi.astype(F32)).astype(BF16)
        wx = jnp.dot(hi, ex, preferred_element_type=F32) + jnp.dot(lo, ex, preferred_element_type=F32)
        t = wx * og
        o = t if o is None else o + t
    b_out = jnp.dot(o.astype(BF16), watt_ref[...], preferred_element_type=F32)
    mix = (_sigmoid(ga_ref[...].astype(F32)) * a_ref[...].astype(F32)
           + _sigmoid(gb_ref[...].astype(F32)) * b_out)
    y_ref[...] = x_ref[...] + jnp.dot(mix.astype(BF16), wo_ref[...], preferred_element_type=F32)


def _merge(x2, a_out, outs, lses, mgates, watt, wo, ex):
    n = x2.shape[0]
    tm = TOKEN_TILE
    row = lambda cols, dil=1: pl.BlockSpec((tm // dil, cols * dil), lambda i: (i, 0))
    d1, d2 = ATT_GROUPS[1][1], ATT_GROUPS[2][1]
    return pl.pallas_call(
        _merge_kernel,
        grid=(n // tm,),
        in_specs=[
            row(D_MODEL), row(D_MODEL),
            row(GROUP_COLS), row(GROUP_COLS, d1), row(GROUP_COLS, d2),
            row(LANES), row(LANES, d1), row(LANES, d2),
            pl.BlockSpec((tm, D_MODEL), lambda i: (i, 0)),
            pl.BlockSpec((tm, D_MODEL), lambda i: (i, 1)),
            _resident((GROUP_COLS, D_MODEL)),
            _resident((D_MODEL, D_MODEL)),
            _resident((LANES, GROUP_COLS)),
        ],
        out_specs=row(D_MODEL),
        out_shape=jax.ShapeDtypeStruct((n, D_MODEL), F32),
        scratch_shapes=[
            pltpu.VMEM((2, GROUP_COLS // LANES, tm, LANES), F32),
            pltpu.VMEM((2, tm, LANES), F32),
        ],
        compiler_params=_cparams(("parallel",)),
        name="merge",
    )(x2, a_out, *outs, *lses, mgates, mgates, watt, wo, ex)


FFN_HALO = 16
FFN_TOKEN_TILE = 1024
FFN_CHUNK = 1024


def _ffn_kernel(x_ref, xp_ref, xn_ref, g2_ref, wup_ref, cw_ref, cb_ref, wdn_ref, y_ref, lhs_scr,
                *, tiles_per_seq):
    tm = x_ref.shape[0]
    ti = pl.program_id(0) % tiles_per_seq
    g2 = g2_ref[...]
    x = x_ref[...]
    keep_prev = jnp.where(ti == 0, 0.0, 1.0)
    keep_next = jnp.where(ti == tiles_per_seq - 1, 0.0, 1.0)
    xn = _rms_rows(x, g2).astype(BF16)
    lhs_scr[0:FFN_HALO, :] = (_rms_rows(xp_ref[...], g2) * keep_prev).astype(BF16)
    lhs_scr[FFN_HALO:FFN_HALO + tm, :] = xn
    lhs_scr[FFN_HALO + tm:, :] = (_rms_rows(xn_ref[...], g2) * keep_next).astype(BF16)
    acc = x
    for j in range(D_FF // FFN_CHUNK):
        sl = slice(j * FFN_CHUNK, (j + 1) * FFN_CHUNK)
        gext =jnp.dot(lhs_scr[...], wup_ref[:, sl], preferred_element_type=F32)
        rows = gext.shape[0]
        gc = cb_ref[:, sl]
        for k in range(FFN_CONV):
            shift = (FFN_CONV // 2 - k) % rows
            gk = gext if shift == 0 else pltpu.roll(gext, shift, 0)
            gc = gc + gk[FFN_HALO:FFN_HALO + tm, :] * cw_ref[k:k + 1, sl]
        val = jnp.dot(xn, wup_ref[:, D_FF + j * FFN_CHUNK:D_FF + (j + 1) * FFN_CHUNK],
                      preferred_element_type=F32)
        h = (_gelu_tanh(gc) * val).astype(BF16)
        acc = acc + jnp.dot(h, wdn_ref[sl, :], preferred_element_type=F32)
    y_ref[...] = acc


def _ffn(x2, seq, g2, wup, cw, cb, wdn):
    n = x2.shape[0]
    tm = FFN_TOKEN_TILE
    hb = tm // FFN_HALO
    last = n // FFN_HALO - 1
    return pl.pallas_call(
        functools.partial(_ffn_kernel, tiles_per_seq=seq // tm),
        grid=(n // tm,),
        in_specs=[
            pl.BlockSpec((tm, D_MODEL), lambda i: (i, 0)),
            pl.BlockSpec((FFN_HALO, D_MODEL), lambda i: (jnp.maximum(i * hb - 1, 0), 0)),
            pl.BlockSpec((FFN_HALO, D_MODEL), lambda i: (jnp.minimum((i + 1) * hb, last), 0)),
            _resident((1, D_MODEL)),
            _resident((D_MODEL, 2 * D_FF)),
            _resident((FFN_CONV, D_FF)),
            _resident((1, D_FF)),
            _resident((D_FF, D_MODEL)),
        ],
        out_specs=pl.BlockSpec((tm, D_MODEL), lambda i: (i, 0)),
        out_shape=jax.ShapeDtypeStruct((n, D_MODEL), F32),
        scratch_shapes=[pltpu.VMEM((tm + 2 * FFN_HALO, D_MODEL), BF16)],
        compiler_params=_cparams(("parallel",)),
        name="convffn",
    )(x2, x2, x2, g2, wup, cw, cb, wdn)


def _rope_tables(seq):
    pos = jnp.arange(seq, dtype=F32)
    inv = ROPE_THETA ** (-jnp.arange(0, ROT_DIM, 2, dtype=F32) / ROT_DIM)
    ang = pos[:, None] * inv[None, :]
    cos, sin = jnp.cos(ang), jnp.sin(ang)
    half = ROT_DIM // 2
    pad = HEAD_DIM - ROT_DIM
    one = jnp.ones((seq, pad), F32)
    zero = jnp.zeros((seq, pad), F32)
    zh = jnp.zeros((seq, half), F32)
    per_head = lambda parts: jnp.tile(jnp.concatenate(parts, axis=1), (1, LANES // HEAD_DIM))
    ra = per_head([cos, cos, one])
    rm = per_head([-sin, zh, zero])
    rp = per_head([zh, sin, zero])
    return ra, rm, rp


def _gate_chunk_weights(blocks, c):
    row0 = min(max(2 * c - 1, 0), D_RNN // LANES - LRU_KSLABS) * LANES
    col0, col1 = c * LRU_CHUNK, (c + 1) * LRU_CHUNK
    pieces = []
    for n in range(col0 // LRU_BW, -(-col1 // LRU_BW)):
        lo, hi = max(n * LRU_BW, col0), min((n + 1) * LRU_BW, col1)
        top = n * LRU_BW - row0
        assert 0 <= top and top + LRU_BW <= LRU_KSLABS * LANES
        piece = blocks[:, :, n, :, lo - n * LRU_BW:hi - n * LRU_BW]
        pieces.append(jnp.pad(piece, ((0, 0), (0, 0), (top, LRU_KSLABS * LANES - top - LRU_BW), (0, 0))))
    return jnp.concatenate(pieces, axis=-1)


def _prep_layer(p, seq):
    q = {}
    q['g1'] = p['norm1_g'].reshape(1, D_MODEL)
    q_lo = 2 * D_RNN
    seg = lambda part, g: (q_lo + part * ATT_COLS + g * GROUP_COLS, q_lo + part * ATT_COLS + (g + 1) * GROUP_COLS)
    qkv = lambda g: [seg(part, g) for part in range(3)]
    w_in = p['w_in'].astype(BF16)
    take = lambda segs: jnp.concatenate([w_in[:, lo:hi] for lo, hi in segs], axis=1)
    q['w_main'] = take([(0, q_lo)] + qkv(0))
    q['w_rest'] = take([(q_lo + 3 * ATT_COLS, IN_COLS)] + qkv(1) + qkv(2))
    head = jnp.arange(MXU_DIM) // HEAD_DIM
    q['bd'] = (head[:, None] == head[None, :]).astype(BF16)
    q['qg'] = jnp.tile(p['q_norm_g'], (1, HEADS_PER_GROUP)) * (HEAD_DIM ** -0.5 * LOG2E)
    q['kg'] = jnp.tile(p['k_norm_g'], (1, HEADS_PER_GROUP))
    q['score_bound'] = (HEAD_DIM * ATT_BOUND_MARGIN) * (jnp.max(jnp.abs(q['qg']), axis=1)
                                                        * jnp.max(jnp.abs(q['kg']), axis=1))
    q['rope'] = _rope_tables(seq)
    q['lru_cw'] = p['lru_conv_w']
    q['lru_cb'] = p['lru_conv_b'].reshape(1, D_RNN)
    blocks = (0.5 * jnp.stack([jnp.stack([p['lru_wa'][d], p['lru_wx'][d]]) for d in range(2)])).astype(BF16)
    q['lru_wg'] = jnp.stack([_gate_chunk_weights(blocks, c) for c in range(LRU_NCHUNK)])
    gb = 0.5 * jnp.stack([p['lru_ba'][0], p['lru_bx'][0], p['lru_ba'][1], p['lru_bx'][1]])
    q['lru_gb'] = gb.reshape(4, LRU_NCHUNK, LRU_CHUNK).transpose(1, 0, 2)
    q['lru_lam'] = p['lru_lambda'].reshape(2, LRU_NCHUNK, LRU_CHUNK).transpose(1, 0, 2)
    q['lru_wout'] = p['w_lru_out'].astype(BF16)
    q['watt'] = p['w_att_out'].astype(BF16)
    q['wo'] = p['w_o'].astype(BF16)
    lse_lane = jnp.arange(LANES)
    lane_head = jnp.where(lse_lane % HEAD_DIM < GROUP_COLS // LANES,
                          2 * (lse_lane % HEAD_DIM) + lse_lane // HEAD_DIM, -1)
    q['expand'] = (lane_head[:, None] == jnp.arange(GROUP_COLS)[None, :] // HEAD_DIM).astype(BF16)
    q['g2'] = p['norm2_g'].reshape(1, D_MODEL)
    q['wup'] = p['w_up'].astype(BF16)
    q['ffn_cw'] = p['ffn_conv_w']
    q['ffn_cb'] = p['ffn_conv_b'].reshape(1, D_FF)
    q['wdn'] = p['w_down'].astype(BF16)
    return q


def _layer(x, q):
    b, seq, _ = x.shape
    x2 = x.reshape(b * seq, D_MODEL)
    main, mgates, qkv1, qkv2 = _inproj(x2, seq, q['g1'], q['w_main'], q['w_rest'], q['bd'], q['qg'], q['kg'],
                                       *q['rope'])
    a_out = _lru(main.reshape(b, seq, MAIN_COLS), q['lru_cw'], q['lru_cb'], q['lru_wg'], q['lru_gb'],
                 q['lru_lam'], q['lru_wout'])
    outs, lses = [], []
    for g, (qkv, (window, dilation)) in enumerate(zip((main, qkv1, qkv2), ATT_GROUPS)):
        o, lse = _attention_group(qkv, b, seq, window, dilation, q['score_bound'][g])
        outs.append(o)
        lses.append(lse)
    x1 = _merge(x2, a_out.reshape(b * seq, D_MODEL), outs, lses, mgates, q['watt'], q['wo'], q['expand'])
    y = _ffn(x1, seq, q['g2'], q['wup'], q['ffn_cw'], q['ffn_cb'], q['wdn'])
    return y.reshape(b, seq, D_MODEL)


def kernel(x_prompt, x_sample, norm1_g, w_in, lru_conv_w, lru_conv_b, lru_wa, lru_ba, lru_wx, lru_bx,
           lru_lambda, w_lru_out, q_norm_g, k_norm_g, w_att_out, w_o, norm2_g, w_up, ffn_conv_w,
           ffn_conv_b, w_down):
    params = dict(norm1_g=norm1_g, w_in=w_in, lru_conv_w=lru_conv_w, lru_conv_b=lru_conv_b,
                  lru_wa=lru_wa, lru_ba=lru_ba, lru_wx=lru_wx, lru_bx=lru_bx, lru_lambda=lru_lambda,
                  w_lru_out=w_lru_out, q_norm_g=q_norm_g, k_norm_g=k_norm_g, w_att_out=w_att_out,
                  w_o=w_o, norm2_g=norm2_g, w_up=w_up, ffn_conv_w=ffn_conv_w, ffn_conv_b=ffn_conv_b,
                  w_down=w_down)
    depth = norm1_g.shape[0]
    seq = x_prompt.shape[1]
    assert x_sample.shape[1] == seq
    layers = [_prep_layer({name: p[l] for name, p in params.items()}, seq) for l in range(depth)]
    ys = []
    for x in (x_prompt, x_sample):
        for q in layers:
            x = _layer(x, q)
        ys.append(x)
    return tuple(ys)
```

```python
import functools

import jax
import jax.numpy as jnp
from jax import lax
from jax.experimental import pallas as pl
from jax.experimental.pallas import tpu as pltpu

F32 = jnp.float32
BF16 = jnp.bfloat16

D_MODEL = 1024
D_RNN = 1280
LRU_BLOCKS = 16
LRU_BW = D_RNN // LRU_BLOCKS
LRU_C = 8.0
LRU_CONV = 4
ATT_GROUPS = ((128, 1), (512, 4), (2048, 16))
N_GROUPS = len(ATT_GROUPS)
HEADS_PER_GROUP = 8
HEAD_DIM = 64
GROUP_COLS = HEADS_PER_GROUP * HEAD_DIM
ATT_COLS = N_GROUPS * GROUP_COLS
ROT_DIM = HEAD_DIM // 4
ROPE_THETA = 500000.0
D_FF = 3 * D_MODEL
FFN_CONV = 3
EPS = 1e-6
NEG = -1e30
LOG2E = 1.4426950408889634
LN2 = 0.6931471805599453
IN_COLS = 2 * D_RNN + 3 * ATT_COLS + 2 * D_MODEL

LANES = 128
MXU_DIM = 256
VMEM_LIMIT_BYTES = 56 * 1024 * 1024
LARGE_VMEM_LIMIT_BYTES = 58 * 1024 * 1024

CHUNK = 512
MAIN_COLS = 2 * D_RNN + 3 * GROUP_COLS
QKV0_CHUNK0 = 2 * D_RNN // CHUNK
MGATE_COLS = 2 * D_MODEL

TOKEN_TILE = 512
INPROJ_TOKEN_TILE = 1024
INPROJ_REST_TOKEN_TILE = 1024
ATT_QBLOCK = 128
ATT_HALF = 64
ATT_MIN_STEP_TOKENS = 2048
ATT_MAX_CONST_SHIFT = 60.0
ATT_BOUND_MARGIN = 1.02

LRU_CHUNK = 256
LRU_NCHUNK = D_RNN // LRU_CHUNK
LRU_KSLABS = 4
LRU_TBLOCK = 256
LRU_SUBLEN = 128
LRU_NSUB = 16
LRU_PITCH = LRU_SUBLEN + 8
LRU_SUBS_PER_BLOCK = LRU_TBLOCK // LRU_SUBLEN


def _cparams(sem, vmem_limit=VMEM_LIMIT_BYTES):
    return pltpu.CompilerParams(dimension_semantics=sem, vmem_limit_bytes=vmem_limit)


def _resident(shape):
    return pl.BlockSpec(shape, lambda *_: (0,) * len(shape), pipeline_mode=pl.Buffered(1))


def _sigmoid(x):
    return 0.5 * jnp.tanh(0.5 * x) + 0.5


def _gelu_tanh(x):
    return 0.5 * x * (1.0 + jnp.tanh(0.7978845608028654 * (x + 0.044715 * (x * x * x))))


def _rms_rows(x, g):
    ms = jnp.mean(x * x, axis=-1, keepdims=True)
    return x * lax.rsqrt(ms + EPS) * g


def _inproj_plan(call):
    if call == 0:
        return [((0, j - QKV0_CHUNK0) if j >= QKV0_CHUNK0 else None, 0, j * CHUNK)
                for j in range(MAIN_COLS // CHUNK)]
    plain = [(None, 0, j * CHUNK) for j in range(MGATE_COLS // CHUNK)]
    return plain + [((g, part), g, None) for g in (1, 2) for part in range(3)]


def _inproj_kernel(x_ref, g1_ref, w_ref, bd_ref, qg_ref, kg_ref, ra_ref, rm_ref, rp_ref, *outs, plan):
    tm = x_ref.shape[0]
    tile_scr = outs[-1]
    xn = _rms_rows(x_ref[...], g1_ref[...]).astype(BF16)
    ra = ra_ref[...]
    rm = rm_ref[...]
    rp = rp_ref[...]
    heavy = [j for j in range(len(plan)) if plan[j][0] is not None][::-1]
    plain = [j for j in range(len(plan)) if plan[j][0] is None]
    order = []
    while heavy or plain:
        take = -(-len(heavy) // max(len(plain), 1))
        order += heavy[:take] + plain[:1]
        heavy, plain = heavy[take:], plain[1:]
    for j in order:
        role, dst, c0 = plan[j]
        o_ref = outs[dst]
        acc = jnp.dot(xn, w_ref[:, j * CHUNK:(j + 1) * CHUNK], preferred_element_type=F32)
        if role is None:
            o_ref[:, c0:c0 + CHUNK] = acc.astype(BF16)
            continue
        g, part = role
        tiles = []
        if part < 2:
            gain = (qg_ref if part == 0 else kg_ref)[g:g + 1, :]
            sq = (acc * acc).astype(BF16)
            bd = bd_ref[...]
            for t in range(CHUNK // MXU_DIM):
                sl = slice(t * MXU_DIM, (t + 1) * MXU_DIM)
                ss = jnp.dot(sq[:, sl], bd, preferred_element_type=F32)
                y = acc[:, sl] * lax.rsqrt(ss * (1.0 / HEAD_DIM) + EPS) * gain[:, sl]
                for u in range(MXU_DIM // LANES):
                    yt = y[:, u * LANES:(u + 1) * LANES]
                    tiles.append(yt * ra + pltpu.roll(yt, LANES - ROT_DIM // 2, 1) * rm
                                 + pltpu.roll(yt, ROT_DIM // 2, 1) * rp)
        else:
            tiles = [acc[:, t * LANES:(t + 1) * LANES] for t in range(CHUNK // LANES)]
        if g == 0:
            for t, val in enumerate(tiles):
                o_ref[:, c0 + t * LANES:c0 + (t + 1) * LANES] = val.astype(BF16)
        else:
            dil = ATT_GROUPS[g][1]
            for t, val in enumerate(tiles):
                tile_scr[t] = val
            for m in range(dil):
                for t in range(CHUNK // LANES):
                    col = (m * 3 + part) * GROUP_COLS + t * LANES
                    o_ref[:, col:col + LANES] = tile_scr[t, pl.ds(m, tm // dil, stride=dil), :].astype(BF16)


def _inproj(x2, seq, g1, w_main, w_rest, bd, qg, kg, ra, rm, rp):
    n = x2.shape[0]

    def call(which, tm, w, widths, dils, name):
        tiles_per_seq = seq // tm
        rope_spec = pl.BlockSpec((tm, LANES), lambda i: (i % tiles_per_seq, 0))
        scratch = [pltpu.VMEM((CHUNK // LANES, tm, LANES), F32)] if max(dils) > 1 else []
        return pl.pallas_call(
            functools.partial(_inproj_kernel, plan=_inproj_plan(which)),
            grid=(n // tm,),
            in_specs=[
                pl.BlockSpec((tm, D_MODEL), lambda i: (i, 0)),
                _resident((1, D_MODEL)),
                _resident(w.shape),
                _resident((MXU_DIM, MXU_DIM)),
                _resident((N_GROUPS, GROUP_COLS)),
                _resident((N_GROUPS, GROUP_COLS)),
                rope_spec, rope_spec, rope_spec,
            ],
            out_specs=[pl.BlockSpec((tm // d, c * d), lambda i: (i, 0)) for c, d in zip(widths, dils)],
            out_shape=[jax.ShapeDtypeStruct((n // d, c * d), BF16) for c, d in zip(widths, dils)],
            scratch_shapes=scratch,
            compiler_params=_cparams(("parallel",)),
            name=name,
        )(x2, g1, w, bd, qg, kg, ra, rm, rp)

    d1, d2 = ATT_GROUPS[1][1], ATT_GROUPS[2][1]
    main, = call(0, INPROJ_TOKEN_TILE, w_main, [MAIN_COLS], [1], "inproj_main")
    mgates, qkv1, qkv2 = call(1, INPROJ_REST_TOKEN_TILE, w_rest,
                              [MGATE_COLS, 3 * GROUP_COLS, 3 * GROUP_COLS], [1, d1, d2], "inproj_rest")
    return main, mgates, qkv1, qkv2


def _lru_kslab0(c):
    return jnp.minimum(jnp.maximum(2 * c - 1, 0), D_RNN // LANES - LRU_KSLABS)


LRU_CONV_ROWS = 128


def _lru_kernel(x_ref, gate_ref, cw_ref, cb_ref, wg_ref, gb_ref, lam_ref, wout_ref, o_ref,
                xc_scr, acc_scr, a_scr, u_scr, h_scr, pad_scr):
    c = pl.program_id(1)
    seq = x_ref.shape[0]
    nsl = LRU_CHUNK // LANES
    chains = [(d, i) for d in range(2) for i in range(nsl)]

    @pl.when(c == 0)
    def _conv():
        pad_scr[0:8, :] = jnp.zeros((8, LANES), F32)
        pad_scr[seq + 8:seq + 16, :] = jnp.zeros((8, LANES), F32)
        for j in range(D_RNN // LANES):
            sl = slice(j * LANES, (j + 1) * LANES)
            pad_scr[8:seq + 8, :] = x_ref[:, sl].astype(F32)
            bias = cb_ref[:, sl]
            taps = [cw_ref[k:k + 1, sl] for k in range(LRU_CONV)]

            def conv_body(rb, carry):
                r0 = pl.multiple_of(rb * LRU_CONV_ROWS, LRU_CONV_ROWS)
                xc = bias
                for k in range(LRU_CONV):
                    off = 8 + k - LRU_CONV // 2
                    xc = xc + pad_scr[pl.ds(r0 + off, LRU_CONV_ROWS), :] * taps[k]
                xc_scr[j, pl.ds(r0, LRU_CONV_ROWS), :] = xc
                return carry

            lax.fori_loop(0, seq // LRU_CONV_ROWS, conv_body, 0, unroll=4)

    k0 = _lru_kslab0(c)
    lam = lam_ref[...]
    half_l2 = (-0.5 * LRU_C * LOG2E) * (jnp.maximum(-lam, 0.0) + jnp.log1p(jnp.exp(-jnp.abs(lam))))

    def pitched_rows(s, sb):
        return pl.ds(pl.multiple_of((s * LRU_SUBS_PER_BLOCK + sb) * LRU_PITCH, 8), LRU_SUBLEN)

    def gates_body(s, carry):
        r0 = pl.multiple_of(s * LRU_TBLOCK, LRU_TBLOCK)
        lhs =jnp.concatenate([xc_scr[k0 + i, pl.ds(r0, LRU_TBLOCK), :] for i in range(LRU_KSLABS)],
                              axis=1).astype(BF16)
        half_xc = 0.5 * jnp.concatenate([xc_scr[nsl * c + i, pl.ds(r0, LRU_TBLOCK), :] for i in range(nsl)],
                                        axis=1)
        for d in range(2):
            za = jnp.dot(lhs, wg_ref[d, 0], preferred_element_type=F32) + gb_ref[2 * d:2 * d + 1, :]
            zx = jnp.dot(lhs, wg_ref[d, 1], preferred_element_type=F32) + gb_ref[2 * d + 1:2 * d + 2, :]
            hl = half_l2[d:d + 1, :]
            a = jnp.exp2(jnp.tanh(za) * hl + hl)
            t = 1.0 - a * a
            root = jnp.where(t > 0.0, t * lax.rsqrt(t), 0.0)
            u = root * ((jnp.tanh(zx) + 1.0) * half_xc)
            for i in range(nsl):
                for sb in range(LRU_SUBS_PER_BLOCK):
                    rows = slice(sb * LRU_SUBLEN, (sb + 1) * LRU_SUBLEN)
                    a_scr[d, i, pitched_rows(s, sb), :] = a[rows, i * LANES:(i + 1) * LANES]
                    u_scr[d, i, pitched_rows(s, sb), :] = u[rows, i * LANES:(i + 1) * LANES]
        return carry

    lax.fori_loop(0, seq // LRU_TBLOCK, gates_body, 0, unroll=True)

    def step_rows(d, t):
        r = t if d == 0 else LRU_SUBLEN - 1 - t
        return pl.ds(r, LRU_NSUB, stride=LRU_PITCH)

    def ends_body(t, carry):
        out = []
        for n, (d, i) in enumerate(chains):
            av = a_scr[d, i, step_rows(d, t), :]
            out += [av * carry[2 * n] + u_scr[d, i, step_rows(d, t), :], av * carry[2 * n + 1]]
        return tuple(out)

    init = (jnp.zeros((LRU_NSUB, LANES), F32), jnp.ones((LRU_NSUB, LANES), F32)) * len(chains)
    ends = lax.fori_loop(0, LRU_SUBLEN, ends_body, init, unroll=32)

    sub = lax.broadcasted_iota(jnp.int32, (LRU_NSUB, LANES), 0)
    starts = []
    for n, (d, i) in enumerate(chains):
        h_end, cum_end = ends[2 * n], ends[2 * n + 1]
        cin = jnp.zeros((LRU_NSUB, LANES), F32)
        for step in range(LRU_NSUB - 1):
            nxt = h_end + cum_end * cin
            if d == 0:
                cin = jnp.where(sub == step + 1, pltpu.roll(nxt, 1, 0), cin)
            else:
                cin = jnp.where(sub == LRU_NSUB - 2 - step, pltpu.roll(nxt, LRU_NSUB - 1, 0), cin)
        starts.append(cin)

    def scan_body(t, carry):
        out = []
        for n, (d, i) in enumerate(chains):
            h = a_scr[d, i, step_rows(d, t), :] * carry[n] + u_scr[d, i, step_rows(d, t), :]
            h_scr[d, i, step_rows(d, t), :] = h
            out.append(h)
        return tuple(out)

    lax.fori_loop(0, LRU_SUBLEN, scan_body, tuple(starts), unroll=32)

    def out_body(first, s, carry):
        r0 = pl.multiple_of(s * LRU_TBLOCK, LRU_TBLOCK)
        hsum = jnp.concatenate(
            [jnp.concatenate([h_scr[0, i, pitched_rows(s, sb), :] + h_scr[1, i, pitched_rows(s, sb), :]
                              for sb in range(LRU_SUBS_PER_BLOCK)], axis=0) for i in range(nsl)], axis=1)
        y = (_gelu_tanh(gate_ref[pl.ds(r0, LRU_TBLOCK), :].astype(F32)) * hsum).astype(BF16)
        part = jnp.dot(y, wout_ref[...], preferred_element_type=F32)
        if first:
            acc_scr[pl.ds(r0, LRU_TBLOCK), :] = part
        else:
            acc_scr[pl.ds(r0, LRU_TBLOCK), :] += part
        return carry

    @pl.when(c == 0)
    def _out_first():
        lax.fori_loop(0, seq // LRU_TBLOCK, functools.partial(out_body, True), 0, unroll=True)

    @pl.when(c > 0)
    def _out_rest():
        lax.fori_loop(0, seq // LRU_TBLOCK, functools.partial(out_body, False), 0, unroll=True)

    @pl.when(c == LRU_NCHUNK - 1)
    def _emit():
        o_ref[...] = acc_scr[...].astype(BF16)


def _lru(proj3, cw, cb, wg, gb, lam, wout):
    b, seq, _ = proj3.shape
    assert seq == LRU_NSUB * LRU_SUBLEN and seq % LRU_TBLOCK == 0
    nsl = LRU_CHUNK // LANES
    gate_blk0 = D_RNN // LRU_CHUNK
    return pl.pallas_call(
        _lru_kernel,
        grid=(b, LRU_NCHUNK),
        in_specs=[
            pl.BlockSpec((None, seq, D_RNN), lambda i, c: (i, 0, 0)),
            pl.BlockSpec((None, seq, LRU_CHUNK), lambda i, c: (i, 0, gate_blk0 + c)),
            _resident((LRU_CONV, D_RNN)),
            _resident((1, D_RNN)),
            pl.BlockSpec((None, 2, 2, LRU_KSLABS * LANES, LRU_CHUNK), lambda i, c: (c, 0, 0, 0, 0)),
            pl.BlockSpec((None, 4, LRU_CHUNK), lambda i, c: (c, 0, 0)),
            pl.BlockSpec((None, 2, LRU_CHUNK), lambda i, c: (c, 0, 0)),
            pl.BlockSpec((LRU_CHUNK, D_MODEL), lambda i, c: (c, 0)),
        ],
        out_specs=pl.BlockSpec((None, seq, D_MODEL), lambda i, c: (i, 0, 0)),
        out_shape=jax.ShapeDtypeStruct((b, seq, D_MODEL), BF16),
        scratch_shapes=[
            pltpu.VMEM((D_RNN // LANES, seq, LANES), F32),
            pltpu.VMEM((seq, D_MODEL), F32),
            pltpu.VMEM((2, nsl, LRU_NSUB * LRU_PITCH, LANES), F32),
            pltpu.VMEM((2, nsl, LRU_NSUB * LRU_PITCH, LANES), F32),
            pltpu.VMEM((2, nsl, LRU_NSUB * LRU_PITCH, LANES), F32),
            pltpu.VMEM((seq + 16, LANES), F32),
        ],
        compiler_params=_cparams(("parallel", "arbitrary"), LARGE_VMEM_LIMIT_BYTES),
        name="rglru",
    )(proj3, proj3, cw, cb, wg, gb, lam, wout)


def _attn_kernel(*refs, rpb, fused, row_max):
    n_in = 1 if fused else 3
    qkv_refs = refs[:n_in]
    bias_ref, shift_ref, o_ref, lse_ref, kt_scr = refs[n_in:]
    length = o_ref.shape[0]
    nkb = length // ATT_QBLOCK
    lane = lax.broadcasted_iota(jnp.int32, (ATT_QBLOCK, LANES), 1)
    lo_half = lane < HEAD_DIM
    lo_keys = lax.broadcasted_iota(jnp.int32, (2 * ATT_QBLOCK, LANES), 1) < HEAD_DIM
    npairs = GROUP_COLS // LANES

    def scores_block(qp_of, kt_of, vw_of, bias):
        outs = []
        lse_tile = jnp.zeros((ATT_QBLOCK, LANES), F32)
        for p in range(npairs):
            qp, kt, vw = qp_of(p), kt_of(p), vw_of(p)
            full, shifts = [], []
            for hh in range(LANES // HEAD_DIM):
                own = lo_half if hh == 0 else jnp.logical_not(lo_half)
                qm = jnp.where(own, qp, jnp.zeros_like(qp))
                s = jnp.dot(qm, kt, preferred_element_type=F32) + bias
                if row_max:
                    m = jnp.max(s, axis=-1, keepdims=True)
                    s = s - m
                    shifts.append(m)
                e = jnp.exp2(s).astype(BF16)
                v1 = jnp.where(lo_keys if hh == 0 else jnp.logical_not(lo_keys), vw, jnp.ones_like(vw))
                full.append(jnp.dot(e, v1, preferred_element_type=F32))
            numer = jnp.where(lo_half, full[0], full[1])
            denom = pltpu.roll(jnp.where(lo_half, full[1], full[0]), HEAD_DIM, 1)
            outs.append(numer * (1.0 / denom))
            shift = jnp.where(lo_half, shifts[0], shifts[1]) if row_max else shift_ref[...]
            lse = LN2 * (shift + jnp.log2(denom))
            lse_tile = jnp.where(jnp.logical_or(lane == p, lane == HEAD_DIM + p), lse, lse_tile)
        return outs, lse_tile

    for r in range(rpb):
        def cols(part, p, r=r):
            base = ((r * 3 + part) * GROUP_COLS if fused else 0) + p * LANES
            return slice(base, base + LANES)

        q_ref, k_ref, v_ref = (qkv_refs[0],) * 3 if fused else qkv_refs

        def transpose_body(kb, carry):
            k0 = pl.multiple_of(kb * ATT_QBLOCK, ATT_QBLOCK)
            for p in range(npairs):
                kt_scr[p, kb] = k_ref[pl.ds(k0, ATT_QBLOCK), cols(1, p)].T
            return carry

        lax.fori_loop(0, nkb, transpose_body, 0, unroll=True)

        def store(rows_list, outs, lse_tile, r=r):
            for dst_rows, src in rows_list:
                for p in range(npairs):
                    c0 = r * GROUP_COLS + p * LANES
                    o_ref[dst_rows, c0:c0 + LANES] = outs[p][src].astype(BF16)
                lse_ref[dst_rows, r * LANES:(r + 1) * LANES] = lse_tile[src]

        def interior(i, carry):
            k0 = pl.multiple_of(i * ATT_QBLOCK, ATT_QBLOCK)
            q0 = pl.multiple_of(k0 + ATT_HALF, ATT_HALF)
            outs, lse_tile = scores_block(
                lambda p: q_ref[pl.ds(q0, ATT_QBLOCK), cols(0, p)],
                lambda p: jnp.concatenate([kt_scr[p, i], kt_scr[p, i + 1]], axis=1),
                lambda p: v_ref[pl.ds(k0, 2 * ATT_QBLOCK), cols(2, p)],
                bias_ref[0])
            store([(pl.ds(q0, ATT_QBLOCK), slice(None))], outs, lse_tile)
            return carry

        if nkb > 1:
            lax.fori_loop(0, nkb - 1, interior, 0, unroll=True)

        head_rows, tail_rows = slice(0, ATT_HALF), slice(length - ATT_HALF, length)
        tail_keys = slice(length - ATT_QBLOCK, length)
        outs, lse_tile = scores_block(
            lambda p: jnp.concatenate([q_ref[head_rows, cols(0, p)], q_ref[tail_rows, cols(0, p)]], axis=0),
            lambda p: jnp.concatenate([kt_scr[p, 0], kt_scr[p, nkb - 1]], axis=1),
            lambda p: jnp.concatenate([v_ref[0:ATT_QBLOCK, cols(2, p)], v_ref[tail_keys, cols(2, p)]], axis=0),
            bias_ref[1])
        store([(head_rows, slice(0, ATT_HALF)), (tail_rows, slice(ATT_HALF, ATT_QBLOCK))], outs, lse_tile)


def _attn_bias():
    i = jnp.arange(ATT_QBLOCK)[:, None]
    j = jnp.arange(2 * ATT_QBLOCK)[None, :]
    interior = jnp.abs(i + ATT_HALF - j) <= ATT_HALF
    first = (i < ATT_HALF) & (j < ATT_QBLOCK) & (jnp.abs(i - j) <= ATT_HALF)
    last = (i >= ATT_HALF) & (j >= ATT_QBLOCK) & (jnp.abs(i - (j - ATT_QBLOCK)) <= ATT_HALF)
    return jnp.stack([jnp.where(interior, 0.0, NEG), jnp.where(first | last, 0.0, NEG)]).astype(F32)


def _attention_group(qkv, b, seq, window, dilation, score_bound):
    length = seq // dilation
    assert (window // 2) // dilation == ATT_HALF and length % ATT_QBLOCK == 0
    pv = qkv.reshape(b, length, qkv.shape[1])
    fused = dilation > 1
    rpb = min(dilation, max(1, ATT_MIN_STEP_TOKENS // length))
    if fused:
        qkv_specs = [pl.BlockSpec((None, length, rpb * 3 * GROUP_COLS), lambda i, m: (i, 0, m))]
    else:
        qkv_specs = [pl.BlockSpec((None, length, GROUP_COLS), lambda i, m, part=part: (i, 0, QKV0_CHUNK0 + part))
                     for part in range(3)]

    def call(row_max, bias, shift):
        return pl.pallas_call(
            functools.partial(_attn_kernel, rpb=rpb, fused=fused, row_max=row_max),
            grid=(b, dilation // rpb),
            in_specs=qkv_specs + [_resident(bias.shape), _resident(shift.shape)],
            out_specs=[
                pl.BlockSpec((None, length, rpb * GROUP_COLS), lambda i, m: (i, 0, m)),
                pl.BlockSpec((None, length, rpb * LANES), lambda i, m: (i, 0, m)),
            ],
            out_shape=[
                jax.ShapeDtypeStruct((b, length, dilation * GROUP_COLS), BF16),
                jax.ShapeDtypeStruct((b, length, dilation * LANES), F32),
            ],
            scratch_shapes=[pltpu.VMEM((GROUP_COLS // LANES, length // ATT_QBLOCK, LANES, ATT_QBLOCK), BF16)],
            compiler_params=_cparams(("parallel", "parallel")),
            name=f"attn_d{dilation}" + ("_rowmax" if row_max else ""),
        )(*([pv] * len(qkv_specs)), bias, shift)

    bias = _attn_bias()
    shift = jnp.full((1, LANES), score_bound, F32)
    o, lse = lax.cond(score_bound <= ATT_MAX_CONST_SHIFT,
                      lambda: call(False, bias - score_bound, shift),
                      lambda: call(True, bias, shift))
    return o.reshape(b * length, dilation * GROUP_COLS), lse.reshape(b * length, dilation * LANES)


def _merge_kernel(a_ref, o0_ref, o1_ref, o2_ref, l0_ref, l1_ref, l2_ref, ga_ref, gb_ref,
                  watt_ref, wo_ref, ex_ref, y_ref, o_scr, l_scr):
    tm = a_ref.shape[0]
    nt = GROUP_COLS // LANES
    for gi, (og_ref, lg_ref) in enumerate(((o1_ref, l1_ref), (o2_ref, l2_ref))):
        dil = ATT_GROUPS[gi + 1][1]
        for m in range(dil):
            rows = pl.ds(m, tm // dil, stride=dil)
            l_scr[gi, rows, :] = lg_ref[:, m * LANES:(m + 1) * LANES]
            for t in range(nt):
                col = m * GROUP_COLS + t * LANES
                o_scr[gi, t, rows, :] = og_ref[:, col:col + LANES].astype(F32)
    lses = [l0_ref[...], l_scr[0], l_scr[1]]
    group_out = [o0_ref[...].astype(F32)] + [
        jnp.concatenate([o_scr[gi, t] for t in range(nt)], axis=1) for gi in range(2)]
    mx = jnp.maximum(jnp.maximum(lses[0], lses[1]), lses[2])
    es = [jnp.exp(l - mx) for l in lses]
    den = es[0] + es[1] + es[2]
    ex = ex_ref[...]
    o = None
    for e, og in zip(es, group_out):
        w = e / den
        hi = w.astype(BF16)
        lo = (w - hi.astype(F32)).astype(BF16)
        wx = jnp.dot(hi, ex, preferred_element_type=F32) + jnp.dot(lo, ex, preferred_element_type=F32)
        t = wx * og
        o = t if o is None else o + t
    b_out = jnp.dot(o.astype(BF16), watt_ref[...], preferred_element_type=F32)
    mix = (_sigmoid(ga_ref[...].astype(F32)) * a_ref[...].astype(F32)
           + _sigmoid(gb_ref[...].astype(F32)) * b_out)
    y_ref[...] = jnp.dot(mix.astype(BF16), wo_ref[...], preferred_element_type=F32).astype(BF16)


def _merge(a_out, outs, lses, mgates, watt, wo, ex):
    n = a_out.shape[0]
    tm = TOKEN_TILE
    row = lambda cols, dil=1: pl.BlockSpec((tm // dil, cols * dil), lambda i: (i, 0))
    d1, d2 = ATT_GROUPS[1][1], ATT_GROUPS[2][1]
    return pl.pallas_call(
        _merge_kernel,
        grid=(n // tm,),
        in_specs=[
            row(D_MODEL),
            row(GROUP_COLS), row(GROUP_COLS, d1), row(GROUP_COLS, d2),
            row(LANES), row(LANES, d1), row(LANES, d2),
            pl.BlockSpec((tm, D_MODEL), lambda i: (i, 0)),
            pl.BlockSpec((tm, D_MODEL), lambda i: (i, 1)),
            _resident((GROUP_COLS, D_MODEL)),
            _resident((D_MODEL, D_MODEL)),
            _resident((LANES, GROUP_COLS)),
        ],
        out_specs=row(D_MODEL),
        out_shape=jax.ShapeDtypeStruct((n, D_MODEL), BF16),
        scratch_shapes=[
            pltpu.VMEM((2, GROUP_COLS // LANES, tm, LANES), F32),
            pltpu.VMEM((2, tm, LANES), F32),
        ],
        compiler_params=_cparams(("parallel",)),
        name="merge",
    )(a_out, *outs, *lses, mgates, mgates, watt, wo, ex)


FFN_HALO = 16
FFN_TOKEN_TILE = 1024
FFN_CHUNK = 1024


def _ffn_kernel(x_ref, xp_ref, xn_ref, m_ref, mp_ref, mn_ref, g2_ref, wup_ref, cw_ref, cb_ref, wdn_ref,
                y_ref, lhs_scr, *, tiles_per_seq):
    tm = x_ref.shape[0]
    ti = pl.program_id(0) % tiles_per_seq
    g2 = g2_ref[...]
    x = x_ref[...] + m_ref[...].astype(F32)
    keep_prev = jnp.where(ti == 0, 0.0, 1.0)
    keep_next = jnp.where(ti == tiles_per_seq - 1, 0.0, 1.0)
    xn = _rms_rows(x, g2).astype(BF16)
    lhs_scr[0:FFN_HALO, :] = (_rms_rows(xp_ref[...] + mp_ref[...].astype(F32), g2) * keep_prev).astype(BF16)
    lhs_scr[FFN_HALO:FFN_HALO + tm, :] = xn
    lhs_scr[FFN_HALO + tm:, :] = (_rms_rows(xn_ref[...] + mn_ref[...].astype(F32), g2) * keep_next).astype(BF16)
    acc = x
    for j in range(D_FF // FFN_CHUNK):
        sl = slice(j * FFN_CHUNK, (j + 1) * FFN_CHUNK)
        gext =jnp.dot(lhs_scr[...], wup_ref[:, sl], preferred_element_type=F32)
        rows = gext.shape[0]
        gc = cb_ref[:, sl]
        for k in range(FFN_CONV):
            shift = (FFN_CONV // 2 - k) % rows
            gk = gext if shift == 0 else pltpu.roll(gext, shift, 0)
            gc = gc + gk[FFN_HALO:FFN_HALO + tm, :] * cw_ref[k:k + 1, sl]
        val = jnp.dot(xn, wup_ref[:, D_FF + j * FFN_CHUNK:D_FF + (j + 1) * FFN_CHUNK],
                      preferred_element_type=F32)
        h = (_gelu_tanh(gc) * val).astype(BF16)
        acc = acc + jnp.dot(h, wdn_ref[sl, :], preferred_element_type=F32)
    y_ref[...] = acc


def _ffn(x2, m2, seq, g2, wup, cw, cb, wdn):
    n = x2.shape[0]
    tm = FFN_TOKEN_TILE
    hb = tm // FFN_HALO
    last = n // FFN_HALO - 1
    tile_and_halos = [
        pl.BlockSpec((tm, D_MODEL), lambda i: (i, 0)),
        pl.BlockSpec((FFN_HALO, D_MODEL), lambda i: (jnp.maximum(i * hb - 1, 0), 0)),
        pl.BlockSpec((FFN_HALO, D_MODEL), lambda i: (jnp.minimum((i + 1) * hb, last), 0)),
    ]
    return pl.pallas_call(
        functools.partial(_ffn_kernel, tiles_per_seq=seq // tm),
        grid=(n // tm,),
        in_specs=tile_and_halos + tile_and_halos + [
            _resident((1, D_MODEL)),
            _resident((D_MODEL, 2 * D_FF)),
            _resident((FFN_CONV, D_FF)),
            _resident((1, D_FF)),
            _resident((D_FF, D_MODEL)),
        ],
        out_specs=pl.BlockSpec((tm, D_MODEL), lambda i: (i, 0)),
        out_shape=jax.ShapeDtypeStruct((n, D_MODEL), F32),
        scratch_shapes=[pltpu.VMEM((tm + 2 * FFN_HALO, D_MODEL), BF16)],
        compiler_params=_cparams(("parallel",), LARGE_VMEM_LIMIT_BYTES),
        name="convffn",
    )(x2, x2, x2, m2, m2, m2, g2, wup, cw, cb, wdn)


def _rope_tables(seq):
    pos = jnp.arange(seq, dtype=F32)
    inv = ROPE_THETA ** (-jnp.arange(0, ROT_DIM, 2, dtype=F32) / ROT_DIM)
    ang = pos[:, None] * inv[None, :]
    cos, sin = jnp.cos(ang), jnp.sin(ang)
    half = ROT_DIM // 2
    pad = HEAD_DIM - ROT_DIM
    one = jnp.ones((seq, pad), F32)
    zero = jnp.zeros((seq, pad), F32)
    zh = jnp.zeros((seq, half), F32)
    per_head = lambda parts: jnp.tile(jnp.concatenate(parts, axis=1), (1, LANES // HEAD_DIM))
    ra = per_head([cos, cos, one])
    rm = per_head([-sin, zh, zero])
    rp = per_head([zh, sin, zero])
    return ra, rm, rp


def _gate_chunk_weights(blocks, c):
    row0 = min(max(2 * c - 1, 0), D_RNN // LANES - LRU_KSLABS) * LANES
    col0, col1 = c * LRU_CHUNK, (c + 1) * LRU_CHUNK
    pieces = []
    for n in range(col0 // LRU_BW, -(-col1 // LRU_BW)):
        lo, hi = max(n * LRU_BW, col0), min((n + 1) * LRU_BW, col1)
        top = n * LRU_BW - row0
        assert 0 <= top and top + LRU_BW <= LRU_KSLABS * LANES
        piece = blocks[:, :, n, :, lo - n * LRU_BW:hi - n * LRU_BW]
        pieces.append(jnp.pad(piece, ((0, 0), (0, 0), (top, LRU_KSLABS * LANES - top - LRU_BW), (0, 0))))
    return jnp.concatenate(pieces, axis=-1)


def _prep_layer(p, seq):
    q = {}
    q['g1'] = p['norm1_g'].reshape(1, D_MODEL)
    q_lo = 2 * D_RNN
    seg = lambda part, g: (q_lo + part * ATT_COLS + g * GROUP_COLS, q_lo + part * ATT_COLS + (g + 1) * GROUP_COLS)
    qkv = lambda g: [seg(part, g) for part in range(3)]
    w_in = p['w_in'].astype(BF16)
    take = lambda segs: jnp.concatenate([w_in[:, lo:hi] for lo, hi in segs], axis=1)
    q['w_main'] = take([(0, q_lo)] + qkv(0))
    q['w_rest'] = take([(q_lo + 3 * ATT_COLS, IN_COLS)] + qkv(1) + qkv(2))
    head = jnp.arange(MXU_DIM) // HEAD_DIM
    q['bd'] = (head[:, None] == head[None, :]).astype(BF16)
    q['qg'] = jnp.tile(p['q_norm_g'], (1, HEADS_PER_GROUP)) * (HEAD_DIM ** -0.5 * LOG2E)
    q['kg'] = jnp.tile(p['k_norm_g'], (1, HEADS_PER_GROUP))
    q['score_bound'] = (HEAD_DIM * ATT_BOUND_MARGIN) * (jnp.max(jnp.abs(q['qg']), axis=1)
                                                        * jnp.max(jnp.abs(q['kg']), axis=1))
    q['rope'] = _rope_tables(seq)
    q['lru_cw'] = p['lru_conv_w']
    q['lru_cb'] = p['lru_conv_b'].reshape(1, D_RNN)
    blocks = (0.5 * jnp.stack([jnp.stack([p['lru_wa'][d], p['lru_wx'][d]]) for d in range(2)])).astype(BF16)
    q['lru_wg'] = jnp.stack([_gate_chunk_weights(blocks, c) for c in range(LRU_NCHUNK)])
    gb = 0.5 * jnp.stack([p['lru_ba'][0], p['lru_bx'][0], p['lru_ba'][1], p['lru_bx'][1]])
    q['lru_gb'] = gb.reshape(4, LRU_NCHUNK, LRU_CHUNK).transpose(1, 0, 2)
    q['lru_lam'] = p['lru_lambda'].reshape(2, LRU_NCHUNK, LRU_CHUNK).transpose(1, 0, 2)
    q['lru_wout'] = p['w_lru_out'].astype(BF16)
    q['watt'] = p['w_att_out'].astype(BF16)
    q['wo'] = p['w_o'].astype(BF16)
    lse_lane = jnp.arange(LANES)
    lane_head = jnp.where(lse_lane % HEAD_DIM < GROUP_COLS // LANES,
                          2 * (lse_lane % HEAD_DIM) + lse_lane // HEAD_DIM, -1)
    q['expand'] = (lane_head[:, None] == jnp.arange(GROUP_COLS)[None, :] // HEAD_DIM).astype(BF16)
    q['g2'] = p['norm2_g'].reshape(1, D_MODEL)
    q['wup'] = p['w_up'].astype(BF16)
    q['ffn_cw'] = p['ffn_conv_w']
    q['ffn_cb'] = p['ffn_conv_b'].reshape(1, D_FF)
    q['wdn'] = p['w_down'].astype(BF16)
    return q


def _layer(x, q):
    b, seq, _ = x.shape
    x2 = x.reshape(b * seq, D_MODEL)
    main, mgates, qkv1, qkv2 = _inproj(x2, seq, q['g1'], q['w_main'], q['w_rest'], q['bd'], q['qg'], q['kg'],
                                       *q['rope'])
    a_out = _lru(main.reshape(b, seq, MAIN_COLS), q['lru_cw'], q['lru_cb'], q['lru_wg'], q['lru_gb'],
                 q['lru_lam'], q['lru_wout'])
    outs, lses = [], []
    for g, (qkv, (window, dilation)) in enumerate(zip((main, qkv1, qkv2), ATT_GROUPS)):
        o, lse = _attention_group(qkv, b, seq, window, dilation, q['score_bound'][g])
        outs.append(o)
        lses.append(lse)
    mixed = _merge(a_out.reshape(b * seq, D_MODEL), outs, lses, mgates, q['watt'], q['wo'], q['expand'])
    y = _ffn(x2, mixed, seq, q['g2'], q['wup'], q['ffn_cw'], q['ffn_cb'], q['wdn'])
    return y.reshape(b, seq, D_MODEL)


def kernel(x_prompt, x_sample, norm1_g, w_in, lru_conv_w, lru_conv_b, lru_wa, lru_ba, lru_wx, lru_bx,
           lru_lambda, w_lru_out, q_norm_g, k_norm_g, w_att_out, w_o, norm2_g, w_up, ffn_conv_w,
           ffn_conv_b, w_down):
    params = dict(norm1_g=norm1_g, w_in=w_in, lru_conv_w=lru_conv_w, lru_conv_b=lru_conv_b,
                  lru_wa=lru_wa, lru_ba=lru_ba, lru_wx=lru_wx, lru_bx=lru_bx, lru_lambda=lru_lambda,
                  w_lru_out=w_lru_out, q_norm_g=q_norm_g, k_norm_g=k_norm_g, w_att_out=w_att_out,
                  w_o=w_o, norm2_g=norm2_g, w_up=w_up, ffn_conv_w=ffn_conv_w, ffn_conv_b=ffn_conv_b,
                  w_down=w_down)
    depth = norm1_g.shape[0]
    seq = x_prompt.shape[1]
    assert x_sample.shape[1] == seq
    layers = [_prep_layer({name: p[l] for name, p in params.items()}, seq) for l in range(depth)]
    ys = []
    for x in (x_prompt, x_sample):
        for q in layers:
            x = _layer(x, q)
        ys.append(x)
    return tuple(ys)
```

```python
import functools

import jax
import jax.numpy as jnp
from jax import lax
from jax.experimental import pallas as pl
from jax.experimental.pallas import tpu as pltpu

F32 = jnp.float32
BF16 = jnp.bfloat16

D_MODEL = 1024
D_RNN = 1280
LRU_BLOCKS = 16
LRU_BW = D_RNN // LRU_BLOCKS
LRU_C = 8.0
LRU_CONV = 4
ATT_GROUPS = ((128, 1), (512, 4), (2048, 16))
N_GROUPS = len(ATT_GROUPS)
HEADS_PER_GROUP = 8
HEAD_DIM = 64
GROUP_COLS = HEADS_PER_GROUP * HEAD_DIM
ATT_COLS = N_GROUPS * GROUP_COLS
ROT_DIM = HEAD_DIM // 4
ROPE_THETA = 500000.0
D_FF = 3 * D_MODEL
FFN_CONV = 3
EPS = 1e-6
NEG = -1e30
LOG2E = 1.4426950408889634
LN2 = 0.6931471805599453
IN_COLS = 2 * D_RNN + 3 * ATT_COLS + 2 * D_MODEL

LANES = 128
MXU_DIM = 256
VMEM_LIMIT_BYTES = 56 * 1024 * 1024
LRU_VMEM_LIMIT_BYTES = 58 * 1024 * 1024

CHUNK = 512
MAIN_COLS = 2 * D_RNN + 3 * GROUP_COLS
QKV0_CHUNK0 = 2 * D_RNN // CHUNK
MGATE_COLS = 2 * D_MODEL

TOKEN_TILE = 512
INPROJ_TOKEN_TILE = 1024
INPROJ_REST_TOKEN_TILE = 1024
ATT_QBLOCK = 128
ATT_HALF = 64
ATT_MIN_STEP_TOKENS = 2048
ATT_MAX_CONST_SHIFT = 60.0
ATT_BOUND_MARGIN = 1.02

LRU_CHUNK = 256
LRU_NCHUNK = D_RNN // LRU_CHUNK
LRU_KSLABS = 4
LRU_TBLOCK = 256
LRU_SUBLEN = 128
LRU_NSUB = 16
LRU_PITCH = LRU_SUBLEN + 8
LRU_SUBS_PER_BLOCK = LRU_TBLOCK // LRU_SUBLEN


def _cparams(sem, vmem_limit=VMEM_LIMIT_BYTES):
    return pltpu.CompilerParams(dimension_semantics=sem, vmem_limit_bytes=vmem_limit)


def _resident(shape):
    return pl.BlockSpec(shape, lambda *_: (0,) * len(shape), pipeline_mode=pl.Buffered(1))


def _sigmoid(x):
    return 0.5 * jnp.tanh(0.5 * x) + 0.5


def _gelu_tanh(x):
    return 0.5 * x * (1.0 + jnp.tanh(0.7978845608028654 * (x + 0.044715 * (x * x * x))))


def _rms_rows(x, g):
    ms = jnp.mean(x * x, axis=-1, keepdims=True)
    return x * lax.rsqrt(ms + EPS) * g


def _inproj_plan(call):
    if call == 0:
        return [((0, j - QKV0_CHUNK0) if j >= QKV0_CHUNK0 else None, 0, j * CHUNK)
                for j in range(MAIN_COLS // CHUNK)]
    plain = [(None, 0, j * CHUNK) for j in range(MGATE_COLS // CHUNK)]
    return plain + [((g, part), g, None) for g in (1, 2) for part in range(3)]


def _inproj_kernel(x_ref, g1_ref, w_ref, bd_ref, qg_ref, kg_ref, ra_ref, rm_ref, rp_ref, *outs, plan):
    tm = x_ref.shape[0]
    tile_scr = outs[-1]
    xn = _rms_rows(x_ref[...], g1_ref[...]).astype(BF16)
    ra = ra_ref[...]
    rm = rm_ref[...]
    rp = rp_ref[...]
    heavy = [j for j in range(len(plan)) if plan[j][0] is not None][::-1]
    plain = [j for j in range(len(plan)) if plan[j][0] is None]
    order = []
    while heavy or plain:
        take = -(-len(heavy) // max(len(plain), 1))
        order += heavy[:take] + plain[:1]
        heavy, plain = heavy[take:], plain[1:]
    for j in order:
        role, dst, c0 = plan[j]
        o_ref = outs[dst]
        acc = jnp.dot(xn, w_ref[:, j * CHUNK:(j + 1) * CHUNK], preferred_element_type=F32)
        if role is None:
            o_ref[:, c0:c0 + CHUNK] = acc.astype(BF16)
            continue
        g, part = role
        tiles = []
        if part < 2:
            gain = (qg_ref if part == 0 else kg_ref)[g:g + 1, :]
            sq = (acc * acc).astype(BF16)
            bd = bd_ref[...]
            for t in range(CHUNK // MXU_DIM):
                sl = slice(t * MXU_DIM, (t + 1) * MXU_DIM)
                ss = jnp.dot(sq[:, sl], bd, preferred_element_type=F32)
                y = acc[:, sl] * lax.rsqrt(ss * (1.0 / HEAD_DIM) + EPS) * gain[:, sl]
                for u in range(MXU_DIM // LANES):
                    yt = y[:, u * LANES:(u + 1) * LANES]
                    tiles.append(yt * ra + pltpu.roll(yt, LANES - ROT_DIM // 2, 1) * rm
                                 + pltpu.roll(yt, ROT_DIM // 2, 1) * rp)
        else:
            tiles = [acc[:, t * LANES:(t + 1) * LANES] for t in range(CHUNK // LANES)]
        if g == 0:
            for t, val in enumerate(tiles):
                o_ref[:, c0 + t * LANES:c0 + (t + 1) * LANES] = val.astype(BF16)
        else:
            dil = ATT_GROUPS[g][1]
            for t, val in enumerate(tiles):
                tile_scr[t] = val
            for m in range(dil):
                for t in range(CHUNK // LANES):
                    col = (m * 3 + part) * GROUP_COLS + t * LANES
                    o_ref[:, col:col + LANES] = tile_scr[t, pl.ds(m, tm // dil, stride=dil), :].astype(BF16)


def _inproj(x2, seq, g1, w_main, w_rest, bd, qg, kg, ra, rm, rp):
    n = x2.shape[0]

    def call(which, tm, w, widths, dils, name):
        tiles_per_seq = seq // tm
        rope_spec = pl.BlockSpec((tm, LANES), lambda i: (i % tiles_per_seq, 0))
        scratch = [pltpu.VMEM((CHUNK // LANES, tm, LANES), F32)] if max(dils) > 1 else []
        return pl.pallas_call(
            functools.partial(_inproj_kernel, plan=_inproj_plan(which)),
            grid=(n // tm,),
            in_specs=[
                pl.BlockSpec((tm, D_MODEL), lambda i: (i, 0)),
                _resident((1, D_MODEL)),
                _resident(w.shape),
                _resident((MXU_DIM, MXU_DIM)),
                _resident((N_GROUPS, GROUP_COLS)),
                _resident((N_GROUPS, GROUP_COLS)),
                rope_spec, rope_spec, rope_spec,
            ],
            out_specs=[pl.BlockSpec((tm // d, c * d), lambda i: (i, 0)) for c, d in zip(widths, dils)],
            out_shape=[jax.ShapeDtypeStruct((n // d, c * d), BF16) for c, d in zip(widths, dils)],
            scratch_shapes=scratch,
            compiler_params=_cparams(("parallel",)),
            name=name,
        )(x2, g1, w, bd, qg, kg, ra, rm, rp)

    d1, d2 = ATT_GROUPS[1][1], ATT_GROUPS[2][1]
    main, = call(0, INPROJ_TOKEN_TILE, w_main, [MAIN_COLS], [1], "inproj_main")
    mgates, qkv1, qkv2 = call(1, INPROJ_REST_TOKEN_TILE, w_rest,
                              [MGATE_COLS, 3 * GROUP_COLS, 3 * GROUP_COLS], [1, d1, d2], "inproj_rest")
    return main, mgates, qkv1, qkv2


def _lru_kslab0(c):
    return jnp.minimum(jnp.maximum(2 * c - 1, 0), D_RNN // LANES - LRU_KSLABS)


LRU_CONV_ROWS = 128


def _lru_kernel(x_ref, gate_ref, cw_ref, cb_ref, wg_ref, gb_ref, lam_ref, wout_ref, o_ref,
                xc_scr, acc_scr, a_scr, u_scr, h_scr, pad_scr):
    c = pl.program_id(1)
    seq = x_ref.shape[0]
    nsl = LRU_CHUNK // LANES
    chains = [(d, i) for d in range(2) for i in range(nsl)]

    @pl.when(c == 0)
    def _conv():
        pad_scr[0:8, :] = jnp.zeros((8, LANES), F32)
        pad_scr[seq + 8:seq + 16, :] = jnp.zeros((8, LANES), F32)
        for j in range(D_RNN // LANES):
            sl = slice(j * LANES, (j + 1) * LANES)
            pad_scr[8:seq + 8, :] = x_ref[:, sl].astype(F32)
            bias = cb_ref[:, sl]
            taps = [cw_ref[k:k + 1, sl] for k in range(LRU_CONV)]

            def conv_body(rb, carry):
                r0 = pl.multiple_of(rb * LRU_CONV_ROWS, LRU_CONV_ROWS)
                xc = bias
                for k in range(LRU_CONV):
                    off = 8 + k - LRU_CONV // 2
                    xc = xc + pad_scr[pl.ds(r0 + off, LRU_CONV_ROWS), :] * taps[k]
                xc_scr[j, pl.ds(r0, LRU_CONV_ROWS), :] = xc
                return carry

            lax.fori_loop(0, seq // LRU_CONV_ROWS, conv_body, 0, unroll=4)

    k0 = _lru_kslab0(c)
    lam = lam_ref[...]
    half_l2 = (-0.5 * LRU_C * LOG2E) * (jnp.maximum(-lam, 0.0) + jnp.log1p(jnp.exp(-jnp.abs(lam))))

    def pitched_rows(s, sb):
        return pl.ds(pl.multiple_of((s * LRU_SUBS_PER_BLOCK + sb) * LRU_PITCH, 8), LRU_SUBLEN)

    def gates_body(s, carry):
        r0 = pl.multiple_of(s * LRU_TBLOCK, LRU_TBLOCK)
        lhs =jnp.concatenate([xc_scr[k0 + i, pl.ds(r0, LRU_TBLOCK), :] for i in range(LRU_KSLABS)],
                              axis=1).astype(BF16)
        half_xc = jnp.concatenate([xc_scr[nsl * c + i, pl.ds(r0, LRU_TBLOCK), :] for i in range(nsl)], axis=1)
        for d in range(2):
            za = jnp.dot(lhs, wg_ref[d, 0], preferred_element_type=F32) + gb_ref[2 * d:2 * d + 1, :]
            zx = jnp.dot(lhs, wg_ref[d, 1], preferred_element_type=F32) + gb_ref[2 * d + 1:2 * d + 2, :]
            hl = half_l2[d:d + 1, :]
            a = jnp.exp2(jnp.tanh(za) * hl + hl)
            t = 1.0 - a * a
            root = jnp.where(t > 0.0, t * lax.rsqrt(t), 0.0)
            u = root * ((jnp.tanh(zx) + 1.0) * half_xc)
            for i in range(nsl):
                for sb in range(LRU_SUBS_PER_BLOCK):
                    rows = slice(sb * LRU_SUBLEN, (sb + 1) * LRU_SUBLEN)
                    a_scr[d, i, pitched_rows(s, sb), :] = a[rows, i * LANES:(i + 1) * LANES]
                    u_scr[d, i, pitched_rows(s, sb), :] = u[rows, i * LANES:(i + 1) * LANES]
        return carry

    lax.fori_loop(0, seq // LRU_TBLOCK, gates_body, 0, unroll=True)

    def step_rows(d, t):
        r = t if d == 0 else LRU_SUBLEN - 1 - t
        return pl.ds(r, LRU_NSUB, stride=LRU_PITCH)

    def ends_body(t, carry):
        out = []
        for n, (d, i) in enumerate(chains):
            av = a_scr[d, i, step_rows(d, t), :]
            out += [av * carry[2 * n] + u_scr[d, i, step_rows(d, t), :], av * carry[2 * n + 1]]
        return tuple(out)

    init = (jnp.zeros((LRU_NSUB, LANES), F32), jnp.ones((LRU_NSUB, LANES), F32)) * len(chains)
    ends = lax.fori_loop(0, LRU_SUBLEN, ends_body, init, unroll=32)

    sub = lax.broadcasted_iota(jnp.int32, (LRU_NSUB, LANES), 0)
    starts = []
    for n, (d, i) in enumerate(chains):
        h_end, cum_end = ends[2 * n], ends[2 * n + 1]
        cin = jnp.zeros((LRU_NSUB, LANES), F32)
        for step in range(LRU_NSUB - 1):
            nxt = h_end + cum_end * cin
            if d == 0:
                cin = jnp.where(sub == step + 1, pltpu.roll(nxt, 1, 0), cin)
            else:
                cin = jnp.where(sub == LRU_NSUB - 2 - step, pltpu.roll(nxt, LRU_NSUB - 1, 0), cin)
        starts.append(cin)

    def scan_body(t, carry):
        out = []
        for n, (d, i) in enumerate(chains):
            h = a_scr[d, i, step_rows(d, t), :] * carry[n] + u_scr[d, i, step_rows(d, t), :]
            h_scr[d, i, step_rows(d, t), :] = h
            out.append(h)
        return tuple(out)

    lax.fori_loop(0, LRU_SUBLEN, scan_body, tuple(starts), unroll=32)

    def out_body(first, s, carry):
        r0 = pl.multiple_of(s * LRU_TBLOCK, LRU_TBLOCK)
        hsum = jnp.concatenate(
            [jnp.concatenate([h_scr[0, i, pitched_rows(s, sb), :] + h_scr[1, i, pitched_rows(s, sb), :]
                              for sb in range(LRU_SUBS_PER_BLOCK)], axis=0) for i in range(nsl)], axis=1)
        y = (_gelu_tanh(gate_ref[pl.ds(r0, LRU_TBLOCK), :].astype(F32)) * hsum).astype(BF16)
        part = jnp.dot(y, wout_ref[...], preferred_element_type=F32)
        if first:
            acc_scr[pl.ds(r0, LRU_TBLOCK), :] = part
        else:
            acc_scr[pl.ds(r0, LRU_TBLOCK), :] += part
        return carry

    @pl.when(c == 0)
    def _out_first():
        lax.fori_loop(0, seq // LRU_TBLOCK, functools.partial(out_body, True), 0, unroll=True)

    @pl.when(c > 0)
    def _out_rest():
        lax.fori_loop(0, seq // LRU_TBLOCK, functools.partial(out_body, False), 0, unroll=True)

    @pl.when(c == LRU_NCHUNK - 1)
    def _emit():
        o_ref[...] = acc_scr[...].astype(BF16)


def _lru(proj3, cw, cb, wg, gb, lam, wout):
    b, seq, _ = proj3.shape
    assert seq == LRU_NSUB * LRU_SUBLEN and seq % LRU_TBLOCK == 0
    nsl = LRU_CHUNK // LANES
    gate_blk0 = D_RNN // LRU_CHUNK
    return pl.pallas_call(
        _lru_kernel,
        grid=(b, LRU_NCHUNK),
        in_specs=[
            pl.BlockSpec((None, seq, D_RNN), lambda i, c: (i, 0, 0)),
            pl.BlockSpec((None, seq, LRU_CHUNK), lambda i, c: (i, 0, gate_blk0 + c)),
            _resident((LRU_CONV, D_RNN)),
            _resident((1, D_RNN)),
            pl.BlockSpec((None, 2, 2, LRU_KSLABS * LANES, LRU_CHUNK), lambda i, c: (c, 0, 0, 0, 0)),
            pl.BlockSpec((None, 4, LRU_CHUNK), lambda i, c: (c, 0, 0)),
            pl.BlockSpec((None, 2, LRU_CHUNK), lambda i, c: (c, 0, 0)),
            pl.BlockSpec((LRU_CHUNK, D_MODEL), lambda i, c: (c, 0)),
        ],
        out_specs=pl.BlockSpec((None, seq, D_MODEL), lambda i, c: (i, 0, 0)),
        out_shape=jax.ShapeDtypeStruct((b, seq, D_MODEL), BF16),
        scratch_shapes=[
            pltpu.VMEM((D_RNN // LANES, seq, LANES), F32),
            pltpu.VMEM((seq, D_MODEL), F32),
            pltpu.VMEM((2, nsl, LRU_NSUB * LRU_PITCH, LANES), F32),
            pltpu.VMEM((2, nsl, LRU_NSUB * LRU_PITCH, LANES), F32),
            pltpu.VMEM((2, nsl, LRU_NSUB * LRU_PITCH, LANES), F32),
            pltpu.VMEM((seq + 16, LANES), F32),
        ],
        compiler_params=_cparams(("parallel", "arbitrary"), LRU_VMEM_LIMIT_BYTES),
        name="rglru",
    )(proj3, proj3, cw, cb, wg, gb, lam, wout)


def _attn_kernel(*refs, rpb, fused, row_max):
    n_in = 1 if fused else 3
    qkv_refs = refs[:n_in]
    bias_ref, shift_ref, o_ref, lse_ref, kt_scr = refs[n_in:]
    length = o_ref.shape[0]
    nkb = length // ATT_QBLOCK
    lane = lax.broadcasted_iota(jnp.int32, (ATT_QBLOCK, LANES), 1)
    lo_half = lane < HEAD_DIM
    lo_keys = lax.broadcasted_iota(jnp.int32, (2 * ATT_QBLOCK, LANES), 1) < HEAD_DIM
    npairs = GROUP_COLS // LANES

    def scores_block(qp_of, kt_of, vw_of, bias):
        outs = []
        lse_tile = jnp.zeros((ATT_QBLOCK, LANES), F32)
        for p in range(npairs):
            qp, kt, vw = qp_of(p), kt_of(p), vw_of(p)
            full, shifts = [], []
            for hh in range(LANES // HEAD_DIM):
                own = lo_half if hh == 0 else jnp.logical_not(lo_half)
                qm = jnp.where(own, qp, jnp.zeros_like(qp))
                s = jnp.dot(qm, kt, preferred_element_type=F32) + bias
                if row_max:
                    m = jnp.max(s, axis=-1, keepdims=True)
                    s = s - m
                    shifts.append(m)
                e = jnp.exp2(s).astype(BF16)
                v1 = jnp.where(lo_keys if hh == 0 else jnp.logical_not(lo_keys), vw, jnp.ones_like(vw))
                full.append(jnp.dot(e, v1, preferred_element_type=F32))
            numer = jnp.where(lo_half, full[0], full[1])
            denom = pltpu.roll(jnp.where(lo_half, full[1], full[0]), HEAD_DIM, 1)
            outs.append(numer * (1.0 / denom))
            shift = jnp.where(lo_half, shifts[0], shifts[1]) if row_max else shift_ref[...]
            lse = LN2 * (shift + jnp.log2(denom))
            lse_tile = jnp.where(jnp.logical_or(lane == p, lane == HEAD_DIM + p), lse, lse_tile)
        return outs, lse_tile

    for r in range(rpb):
        def cols(part, p, r=r):
            base = ((r * 3 + part) * GROUP_COLS if fused else 0) + p * LANES
            return slice(base, base + LANES)

        q_ref, k_ref, v_ref = (qkv_refs[0],) * 3 if fused else qkv_refs

        def transpose_body(kb, carry):
            k0 = pl.multiple_of(kb * ATT_QBLOCK, ATT_QBLOCK)
            for p in range(npairs):
                kt_scr[p, kb] = k_ref[pl.ds(k0, ATT_QBLOCK), cols(1, p)].T
            return carry

        lax.fori_loop(0, nkb, transpose_body, 0, unroll=True)

        def store(rows_list, outs, lse_tile, r=r):
            for dst_rows, src in rows_list:
                for p in range(npairs):
                    c0 = r * GROUP_COLS + p * LANES
                    o_ref[dst_rows, c0:c0 + LANES] = outs[p][src].astype(BF16)
                lse_ref[dst_rows, r * LANES:(r + 1) * LANES] = lse_tile[src]

        def interior(i, carry):
            k0 = pl.multiple_of(i * ATT_QBLOCK, ATT_QBLOCK)
            q0 = pl.multiple_of(k0 + ATT_HALF, ATT_HALF)
            outs, lse_tile = scores_block(
                lambda p: q_ref[pl.ds(q0, ATT_QBLOCK), cols(0, p)],
                lambda p: jnp.concatenate([kt_scr[p, i], kt_scr[p, i + 1]], axis=1),
                lambda p: v_ref[pl.ds(k0, 2 * ATT_QBLOCK), cols(2, p)],
                bias_ref[0])
            store([(pl.ds(q0, ATT_QBLOCK), slice(None))], outs, lse_tile)
            return carry

        if nkb > 1:
            lax.fori_loop(0, nkb - 1, interior, 0, unroll=True)

        head_rows, tail_rows = slice(0, ATT_HALF), slice(length - ATT_HALF, length)
        tail_keys = slice(length - ATT_QBLOCK, length)
        outs, lse_tile = scores_block(
            lambda p: jnp.concatenate([q_ref[head_rows, cols(0, p)], q_ref[tail_rows, cols(0, p)]], axis=0),
            lambda p: jnp.concatenate([kt_scr[p, 0], kt_scr[p, nkb - 1]], axis=1),
            lambda p: jnp.concatenate([v_ref[0:ATT_QBLOCK, cols(2, p)], v_ref[tail_keys, cols(2, p)]], axis=0),
            bias_ref[1])
        store([(head_rows, slice(0, ATT_HALF)), (tail_rows, slice(ATT_HALF, ATT_QBLOCK))], outs, lse_tile)


def _attn_bias():
    i = jnp.arange(ATT_QBLOCK)[:, None]
    j = jnp.arange(2 * ATT_QBLOCK)[None, :]
    interior = jnp.abs(i + ATT_HALF - j) <= ATT_HALF
    first = (i < ATT_HALF) & (j < ATT_QBLOCK) & (jnp.abs(i - j) <= ATT_HALF)
    last = (i >= ATT_HALF) & (j >= ATT_QBLOCK) & (jnp.abs(i - (j - ATT_QBLOCK)) <= ATT_HALF)
    return jnp.stack([jnp.where(interior, 0.0, NEG), jnp.where(first | last, 0.0, NEG)]).astype(F32)


def _attention_group(qkv, b, seq, window, dilation, score_bound):
    length = seq // dilation
    assert (window // 2) // dilation == ATT_HALF and length % ATT_QBLOCK == 0
    pv = qkv.reshape(b, length, qkv.shape[1])
    fused = dilation > 1
    rpb = min(dilation, max(1, ATT_MIN_STEP_TOKENS // length))
    if fused:
        qkv_specs = [pl.BlockSpec((None, length, rpb * 3 * GROUP_COLS), lambda i, m: (i, 0, m))]
    else:
        qkv_specs = [pl.BlockSpec((None, length, GROUP_COLS), lambda i, m, part=part: (i, 0, QKV0_CHUNK0 + part))
                     for part in range(3)]

    def call(row_max, bias, shift):
        return pl.pallas_call(
            functools.partial(_attn_kernel, rpb=rpb, fused=fused, row_max=row_max),
            grid=(b, dilation // rpb),
            in_specs=qkv_specs + [_resident(bias.shape), _resident(shift.shape)],
            out_specs=[
                pl.BlockSpec((None, length, rpb * GROUP_COLS), lambda i, m: (i, 0, m)),
                pl.BlockSpec((None, length, rpb * LANES), lambda i, m: (i, 0, m)),
            ],
            out_shape=[
                jax.ShapeDtypeStruct((b, length, dilation * GROUP_COLS), BF16),
                jax.ShapeDtypeStruct((b, length, dilation * LANES), F32),
            ],
            scratch_shapes=[pltpu.VMEM((GROUP_COLS // LANES, length // ATT_QBLOCK, LANES, ATT_QBLOCK), BF16)],
            compiler_params=_cparams(("parallel", "parallel")),
            name=f"attn_d{dilation}" + ("_rowmax" if row_max else ""),
        )(*([pv] * len(qkv_specs)), bias, shift)

    bias = _attn_bias()
    shift = jnp.full((1, LANES), score_bound, F32)
    o, lse = lax.cond(score_bound <= ATT_MAX_CONST_SHIFT,
                      lambda: call(False, bias - score_bound, shift),
                      lambda: call(True, bias, shift))
    return o.reshape(b * length, dilation * GROUP_COLS), lse.reshape(b * length, dilation * LANES)


def _merge_kernel(x_ref, a_ref, o0_ref, o1_ref, o2_ref, l0_ref, l1_ref, l2_ref, ga_ref, gb_ref,
                  watt_ref, wo_ref, ex_ref, y_ref, o_scr, l_scr):
    tm = x_ref.shape[0]
    nt = GROUP_COLS // LANES
    for gi, (og_ref, lg_ref) in enumerate(((o1_ref, l1_ref), (o2_ref, l2_ref))):
        dil = ATT_GROUPS[gi + 1][1]
        for m in range(dil):
            rows = pl.ds(m, tm // dil, stride=dil)
            l_scr[gi, rows, :] = lg_ref[:, m * LANES:(m + 1) * LANES]
            for t in range(nt):
                col = m * GROUP_COLS + t * LANES
                o_scr[gi, t, rows, :] = og_ref[:, col:col + LANES].astype(F32)
    lses = [l0_ref[...], l_scr[0], l_scr[1]]
    group_out = [o0_ref[...].astype(F32)] + [
        jnp.concatenate([o_scr[gi, t] for t in range(nt)], axis=1) for gi in range(2)]
    mx = jnp.maximum(jnp.maximum(lses[0], lses[1]), lses[2])
    es = [jnp.exp(l - mx) for l in lses]
    den = es[0] + es[1] + es[2]
    ex = ex_ref[...]
    o = None
    for e, og in zip(es, group_out):
        w = e / den
        hi = w.astype(BF16)
        lo = (w - hi.astype(F32)).astype(BF16)
        wx = jnp.dot(hi, ex, preferred_element_type=F32) + jnp.dot(lo, ex, preferred_element_type=F32)
        t = wx * og
        o = t if o is None else o + t
    b_out = jnp.dot(o.astype(BF16), watt_ref[...], preferred_element_type=F32)
    mix = (_sigmoid(ga_ref[...].astype(F32)) * a_ref[...].astype(F32)
           + _sigmoid(gb_ref[...].astype(F32)) * b_out)
    y_ref[...] = x_ref[...] + jnp.dot(mix.astype(BF16), wo_ref[...], preferred_element_type=F32)


def _merge(x2, a_out, outs, lses, mgates, watt, wo, ex):
    n = x2.shape[0]
    tm = TOKEN_TILE
    row = lambda cols, dil=1: pl.BlockSpec((tm // dil, cols * dil), lambda i: (i, 0))
    d1, d2 = ATT_GROUPS[1][1], ATT_GROUPS[2][1]
    return pl.pallas_call(
        _merge_kernel,
        grid=(n // tm,),
        in_specs=[
            row(D_MODEL), row(D_MODEL),
            row(GROUP_COLS), row(GROUP_COLS, d1), row(GROUP_COLS, d2),
            row(LANES), row(LANES, d1), row(LANES, d2),
            pl.BlockSpec((tm, D_MODEL), lambda i: (i, 0)),
            pl.BlockSpec((tm, D_MODEL), lambda i: (i, 1)),
            _resident((GROUP_COLS, D_MODEL)),
            _resident((D_MODEL, D_MODEL)),
            _resident((LANES, GROUP_COLS)),
        ],
        out_specs=row(D_MODEL),
        out_shape=jax.ShapeDtypeStruct((n, D_MODEL), F32),
        scratch_shapes=[
            pltpu.VMEM((2, GROUP_COLS // LANES, tm, LANES), F32),
            pltpu.VMEM((2, tm, LANES), F32),
        ],
        compiler_params=_cparams(("parallel",)),
        name="merge",
    )(x2, a_out, *outs, *lses, mgates, mgates, watt, wo, ex)


FFN_HALO = 16
FFN_TOKEN_TILE = 1024
FFN_CHUNK = 1024


def _ffn_kernel(x_ref, xp_ref, xn_ref, g2_ref, wup_ref, cw_ref, cb_ref, wdn_ref, y_ref, lhs_scr,
                *, tiles_per_seq):
    tm = x_ref.shape[0]
    ti = pl.program_id(0) % tiles_per_seq
    g2 = g2_ref[...]
    x = x_ref[...]
    keep_prev = jnp.where(ti == 0, 0.0, 1.0)
    keep_next = jnp.where(ti == tiles_per_seq - 1, 0.0, 1.0)
    xn = _rms_rows(x, g2).astype(BF16)
    lhs_scr[0:FFN_HALO, :] = (_rms_rows(xp_ref[...], g2) * keep_prev).astype(BF16)
    lhs_scr[FFN_HALO:FFN_HALO + tm, :] = xn
    lhs_scr[FFN_HALO + tm:, :] = (_rms_rows(xn_ref[...], g2) * keep_next).astype(BF16)
    acc = x
    for j in range(D_FF // FFN_CHUNK):
        sl = slice(j * FFN_CHUNK, (j + 1) * FFN_CHUNK)
        gext =jnp.dot(lhs_scr[...], wup_ref[:, sl], preferred_element_type=F32)
        rows = gext.shape[0]
        gc = cb_ref[:, sl]
        for k in range(FFN_CONV):
            shift = (FFN_CONV // 2 - k) % rows
            gk = gext if shift == 0 else pltpu.roll(gext, shift, 0)
            gc = gc + gk[FFN_HALO:FFN_HALO + tm, :] * cw_ref[k:k + 1, sl]
        val = jnp.dot(xn, wup_ref[:, D_FF + j * FFN_CHUNK:D_FF + (j + 1) * FFN_CHUNK],
                      preferred_element_type=F32)
        h = (_gelu_tanh(gc) * val).astype(BF16)
        acc = acc + jnp.dot(h, wdn_ref[sl, :], preferred_element_type=F32)
    y_ref[...] = acc


def _ffn(x2, seq, g2, wup, cw, cb, wdn):
    n = x2.shape[0]
    tm = FFN_TOKEN_TILE
    hb = tm // FFN_HALO
    last = n // FFN_HALO - 1
    return pl.pallas_call(
        functools.partial(_ffn_kernel, tiles_per_seq=seq // tm),
        grid=(n // tm,),
        in_specs=[
            pl.BlockSpec((tm, D_MODEL), lambda i: (i, 0)),
            pl.BlockSpec((FFN_HALO, D_MODEL), lambda i: (jnp.maximum(i * hb - 1, 0), 0)),
            pl.BlockSpec((FFN_HALO, D_MODEL), lambda i: (jnp.minimum((i + 1) * hb, last), 0)),
            _resident((1, D_MODEL)),
            _resident((D_MODEL, 2 * D_FF)),
            _resident((FFN_CONV, D_FF)),
            _resident((1, D_FF)),
            _resident((D_FF, D_MODEL)),
        ],
        out_specs=pl.BlockSpec((tm, D_MODEL), lambda i: (i, 0)),
        out_shape=jax.ShapeDtypeStruct((n, D_MODEL), F32),
        scratch_shapes=[pltpu.VMEM((tm + 2 * FFN_HALO, D_MODEL), BF16)],
        compiler_params=_cparams(("parallel",)),
        name="convffn",
    )(x2, x2, x2, g2, wup, cw, cb, wdn)


def _rope_tables(seq):
    pos = jnp.arange(seq, dtype=F32)
    inv = ROPE_THETA ** (-jnp.arange(0, ROT_DIM, 2, dtype=F32) / ROT_DIM)
    ang = pos[:, None] * inv[None, :]
    cos, sin = jnp.cos(ang), jnp.sin(ang)
    half = ROT_DIM // 2
    pad = HEAD_DIM - ROT_DIM
    one = jnp.ones((seq, pad), F32)
    zero = jnp.zeros((seq, pad), F32)
    zh = jnp.zeros((seq, half), F32)
    per_head = lambda parts: jnp.tile(jnp.concatenate(parts, axis=1), (1, LANES // HEAD_DIM))
    ra = per_head([cos, cos, one])
    rm = per_head([-sin, zh, zero])
    rp = per_head([zh, sin, zero])
    return ra, rm, rp


def _gate_chunk_weights(blocks, c):
    row0 = min(max(2 * c - 1, 0), D_RNN // LANES - LRU_KSLABS) * LANES
    col0, col1 = c * LRU_CHUNK, (c + 1) * LRU_CHUNK
    pieces = []
    for n in range(col0 // LRU_BW, -(-col1 // LRU_BW)):
        lo, hi = max(n * LRU_BW, col0), min((n + 1) * LRU_BW, col1)
        top = n * LRU_BW - row0
        assert 0 <= top and top + LRU_BW <= LRU_KSLABS * LANES
        piece = blocks[:, :, n, :, lo - n * LRU_BW:hi - n * LRU_BW]
        pieces.append(jnp.pad(piece, ((0, 0), (0, 0), (top, LRU_KSLABS * LANES - top - LRU_BW), (0, 0))))
    return jnp.concatenate(pieces, axis=-1)


def _prep_layer(p, seq):
    q = {}
    q['g1'] = p['norm1_g'].reshape(1, D_MODEL)
    q_lo = 2 * D_RNN
    seg = lambda part, g: (q_lo + part * ATT_COLS + g * GROUP_COLS, q_lo + part * ATT_COLS + (g + 1) * GROUP_COLS)
    qkv = lambda g: [seg(part, g) for part in range(3)]
    w_in = p['w_in'].astype(BF16)
    take = lambda segs: jnp.concatenate([w_in[:, lo:hi] for lo, hi in segs], axis=1)
    q['w_main'] = take([(0, q_lo)] + qkv(0))
    q['w_rest'] = take([(q_lo + 3 * ATT_COLS, IN_COLS)] + qkv(1) + qkv(2))
    head = jnp.arange(MXU_DIM) // HEAD_DIM
    q['bd'] = (head[:, None] == head[None, :]).astype(BF16)
    q['qg'] = jnp.tile(p['q_norm_g'], (1, HEADS_PER_GROUP)) * (HEAD_DIM ** -0.5 * LOG2E)
    q['kg'] = jnp.tile(p['k_norm_g'], (1, HEADS_PER_GROUP))
    q['score_bound'] = (HEAD_DIM * ATT_BOUND_MARGIN) * (jnp.max(jnp.abs(q['qg']), axis=1)
                                                        * jnp.max(jnp.abs(q['kg']), axis=1))
    q['rope'] = _rope_tables(seq)
    q['lru_cw'] = 0.5 * p['lru_conv_w']
    q['lru_cb'] = 0.5 * p['lru_conv_b'].reshape(1, D_RNN)
    blocks = jnp.stack([jnp.stack([p['lru_wa'][d], p['lru_wx'][d]]) for d in range(2)]).astype(BF16)
    q['lru_wg'] = jnp.stack([_gate_chunk_weights(blocks, c) for c in range(LRU_NCHUNK)])
    gb = 0.5 * jnp.stack([p['lru_ba'][0], p['lru_bx'][0], p['lru_ba'][1], p['lru_bx'][1]])
    q['lru_gb'] = gb.reshape(4, LRU_NCHUNK, LRU_CHUNK).transpose(1, 0, 2)
    q['lru_lam'] = p['lru_lambda'].reshape(2, LRU_NCHUNK, LRU_CHUNK).transpose(1, 0, 2)
    q['lru_wout'] = p['w_lru_out'].astype(BF16)
    q['watt'] = p['w_att_out'].astype(BF16)
    q['wo'] = p['w_o'].astype(BF16)
    lse_lane = jnp.arange(LANES)
    lane_head = jnp.where(lse_lane % HEAD_DIM < GROUP_COLS // LANES,
                          2 * (lse_lane % HEAD_DIM) + lse_lane // HEAD_DIM, -1)
    q['expand'] = (lane_head[:, None] == jnp.arange(GROUP_COLS)[None, :] // HEAD_DIM).astype(BF16)
    q['g2'] = p['norm2_g'].reshape(1, D_MODEL)
    q['wup'] = p['w_up'].astype(BF16)
    q['ffn_cw'] = p['ffn_conv_w']
    q['ffn_cb'] = p['ffn_conv_b'].reshape(1, D_FF)
    q['wdn'] = p['w_down'].astype(BF16)
    return q


def _layer(x, q):
    b, seq, _ = x.shape
    x2 = x.reshape(b * seq, D_MODEL)
    main, mgates, qkv1, qkv2 = _inproj(x2, seq, q['g1'], q['w_main'], q['w_rest'], q['bd'], q['qg'], q['kg'],
                                       *q['rope'])
    a_out = _lru(main.reshape(b, seq, MAIN_COLS), q['lru_cw'], q['lru_cb'], q['lru_wg'], q['lru_gb'],
                 q['lru_lam'], q['lru_wout'])
    outs, lses = [], []
    for g, (qkv, (window, dilation)) in enumerate(zip((main, qkv1, qkv2), ATT_GROUPS)):
        o, lse = _attention_group(qkv, b, seq, window, dilation, q['score_bound'][g])
        outs.append(o)
        lses.append(lse)
    x1 = _merge(x2, a_out.reshape(b * seq, D_MODEL), outs, lses, mgates, q['watt'], q['wo'], q['expand'])
    y = _ffn(x1, seq, q['g2'], q['wup'], q['ffn_cw'], q['ffn_cb'], q['wdn'])
    return y.reshape(b, seq, D_MODEL)


def kernel(x_prompt, x_sample, norm1_g, w_in, lru_conv_w, lru_conv_b, lru_wa, lru_ba, lru_wx, lru_bx,
           lru_lambda, w_lru_out, q_norm_g, k_norm_g, w_att_out, w_o, norm2_g, w_up, ffn_conv_w,
           ffn_conv_b, w_down):
    params = dict(norm1_g=norm1_g, w_in=w_in, lru_conv_w=lru_conv_w, lru_conv_b=lru_conv_b,
                  lru_wa=lru_wa, lru_ba=lru_ba, lru_wx=lru_wx, lru_bx=lru_bx, lru_lambda=lru_lambda,
                  w_lru_out=w_lru_out, q_norm_g=q_norm_g, k_norm_g=k_norm_g, w_att_out=w_att_out,
                  w_o=w_o, norm2_g=norm2_g, w_up=w_up, ffn_conv_w=ffn_conv_w, ffn_conv_b=ffn_conv_b,
                  w_down=w_down)
    depth = norm1_g.shape[0]
    seq = x_prompt.shape[1]
    assert x_sample.shape[1] == seq
    layers = [_prep_layer({name: p[l] for name, p in params.items()}, seq) for l in range(depth)]
    ys = []
    for x in (x_prompt, x_sample):
        for q in layers:
            x = _layer(x, q)
        ys.append(x)
    return tuple(ys)
```

```python
import functools

import jax
import jax.numpy as jnp
from jax import lax
from jax.experimental import pallas as pl
from jax.experimental.pallas import tpu as pltpu

F32 = jnp.float32
BF16 = jnp.bfloat16

D_MODEL = 1024
D_RNN = 1280
LRU_BLOCKS = 16
LRU_BW = D_RNN // LRU_BLOCKS
LRU_C = 8.0
LRU_CONV = 4
ATT_GROUPS = ((128, 1), (512, 4), (2048, 16))
N_GROUPS = len(ATT_GROUPS)
HEADS_PER_GROUP = 8
HEAD_DIM = 64
GROUP_COLS = HEADS_PER_GROUP * HEAD_DIM
ATT_COLS = N_GROUPS * GROUP_COLS
ROT_DIM = HEAD_DIM // 4
ROPE_THETA = 500000.0
D_FF = 3 * D_MODEL
FFN_CONV = 3
EPS = 1e-6
NEG = -1e30
LOG2E = 1.4426950408889634
LN2 = 0.6931471805599453
IN_COLS = 2 * D_RNN + 3 * ATT_COLS + 2 * D_MODEL

LANES = 128
MXU_DIM = 256
VMEM_LIMIT_BYTES = 56 * 1024 * 1024
LRU_VMEM_LIMIT_BYTES = 58 * 1024 * 1024

CHUNK = 512
MAIN_COLS = 2 * D_RNN + 3 * GROUP_COLS
QKV0_CHUNK0 = 2 * D_RNN // CHUNK
MGATE_COLS = 2 * D_MODEL

TOKEN_TILE = 512
INPROJ_TOKEN_TILE = 1024
INPROJ_REST_TOKEN_TILE = 1024
ATT_QBLOCK = 128
ATT_HALF = 64
ATT_MIN_STEP_TOKENS = 2048
ATT_MAX_CONST_SHIFT = 60.0
ATT_BOUND_MARGIN = 1.02

LRU_CHUNK = 256
LRU_NCHUNK = D_RNN // LRU_CHUNK
LRU_KSLABS = 4
LRU_TBLOCK = 256
LRU_SUBLEN = 128
LRU_NSUB = 16
LRU_PITCH = LRU_SUBLEN + 8
LRU_SUBS_PER_BLOCK = LRU_TBLOCK // LRU_SUBLEN


def _cparams(sem, vmem_limit=VMEM_LIMIT_BYTES):
    return pltpu.CompilerParams(dimension_semantics=sem, vmem_limit_bytes=vmem_limit)


def _resident(shape):
    return pl.BlockSpec(shape, lambda *_: (0,) * len(shape), pipeline_mode=pl.Buffered(1))


def _sigmoid(x):
    return 0.5 * jnp.tanh(0.5 * x) + 0.5


def _gelu_tanh(x):
    return 0.5 * x * (1.0 + jnp.tanh(0.7978845608028654 * (x + 0.044715 * (x * x * x))))


def _rms_rows(x, g):
    ms = jnp.mean(x * x, axis=-1, keepdims=True)
    return x * lax.rsqrt(ms + EPS) * g


def _inproj_plan(call):
    if call == 0:
        return [((0, j - QKV0_CHUNK0) if j >= QKV0_CHUNK0 else None, 0, j * CHUNK)
                for j in range(MAIN_COLS // CHUNK)]
    plain = [(None, 0, j * CHUNK) for j in range(MGATE_COLS // CHUNK)]
    return plain + [((g, part), g, None) for g in (1, 2) for part in range(3)]


def _inproj_kernel(x_ref, g1_ref, w_ref, bd_ref, qg_ref, kg_ref, ra_ref, rm_ref, rp_ref, *outs, plan):
    tm = x_ref.shape[0]
    tile_scr = outs[-1]
    xn = _rms_rows(x_ref[...], g1_ref[...]).astype(BF16)
    ra = ra_ref[...]
    rm = rm_ref[...]
    rp = rp_ref[...]
    heavy = [j for j in range(len(plan)) if plan[j][0] is not None][::-1]
    plain = [j for j in range(len(plan)) if plan[j][0] is None]
    order = []
    while heavy or plain:
        take = -(-len(heavy) // max(len(plain), 1))
        order += heavy[:take] + plain[:1]
        heavy, plain = heavy[take:], plain[1:]
    for j in order:
        role, dst, c0 = plan[j]
        o_ref = outs[dst]
        acc = jnp.dot(xn, w_ref[:, j * CHUNK:(j + 1) * CHUNK], preferred_element_type=F32)
        if role is None:
            o_ref[:, c0:c0 + CHUNK] = acc.astype(BF16)
            continue
        g, part = role
        tiles = []
        if part < 2:
            gain = (qg_ref if part == 0 else kg_ref)[g:g + 1, :]
            sq = (acc * acc).astype(BF16)
            bd = bd_ref[...]
            for t in range(CHUNK // MXU_DIM):
                sl = slice(t * MXU_DIM, (t + 1) * MXU_DIM)
                ss = jnp.dot(sq[:, sl], bd, preferred_element_type=F32)
                y = acc[:, sl] * lax.rsqrt(ss * (1.0 / HEAD_DIM) + EPS) * gain[:, sl]
                for u in range(MXU_DIM // LANES):
                    yt = y[:, u * LANES:(u + 1) * LANES]
                    tiles.append(yt * ra + pltpu.roll(yt, LANES - ROT_DIM // 2, 1) * rm
                                 + pltpu.roll(yt, ROT_DIM // 2, 1) * rp)
        else:
            tiles = [acc[:, t * LANES:(t + 1) * LANES] for t in range(CHUNK // LANES)]
        if g == 0:
            for t, val in enumerate(tiles):
                o_ref[:, c0 + t * LANES:c0 + (t + 1) * LANES] = val.astype(BF16)
        else:
            dil = ATT_GROUPS[g][1]
            for t, val in enumerate(tiles):
                tile_scr[t] = val
            for m in range(dil):
                for t in range(CHUNK // LANES):
                    col = (m * 3 + part) * GROUP_COLS + t * LANES
                    o_ref[:, col:col + LANES] = tile_scr[t, pl.ds(m, tm // dil, stride=dil), :].astype(BF16)


def _inproj(x2, seq, g1, w_main, w_rest, bd, qg, kg, ra, rm, rp):
    n = x2.shape[0]

    def call(which, tm, w, widths, dils, name):
        tiles_per_seq = seq // tm
        rope_spec = pl.BlockSpec((tm, LANES), lambda i: (i % tiles_per_seq, 0))
        scratch = [pltpu.VMEM((CHUNK // LANES, tm, LANES), F32)] if max(dils) > 1 else []
        return pl.pallas_call(
            functools.partial(_inproj_kernel, plan=_inproj_plan(which)),
            grid=(n // tm,),
            in_specs=[
                pl.BlockSpec((tm, D_MODEL), lambda i: (i, 0)),
                _resident((1, D_MODEL)),
                _resident(w.shape),
                _resident((MXU_DIM, MXU_DIM)),
                _resident((N_GROUPS, GROUP_COLS)),
                _resident((N_GROUPS, GROUP_COLS)),
                rope_spec, rope_spec, rope_spec,
            ],
            out_specs=[pl.BlockSpec((tm // d, c * d), lambda i: (i, 0)) for c, d in zip(widths, dils)],
            out_shape=[jax.ShapeDtypeStruct((n // d, c * d), BF16) for c, d in zip(widths, dils)],
            scratch_shapes=scratch,
            compiler_params=_cparams(("parallel",)),
            name=name,
        )(x2, g1, w, bd, qg, kg, ra, rm, rp)

    d1, d2 = ATT_GROUPS[1][1], ATT_GROUPS[2][1]
    main, = call(0, INPROJ_TOKEN_TILE, w_main, [MAIN_COLS], [1], "inproj_main")
    mgates, qkv1, qkv2 = call(1, INPROJ_REST_TOKEN_TILE, w_rest,
                              [MGATE_COLS, 3 * GROUP_COLS, 3 * GROUP_COLS], [1, d1, d2], "inproj_rest")
    return main, mgates, qkv1, qkv2


def _lru_kslab0(c):
    return jnp.minimum(jnp.maximum(2 * c - 1, 0), D_RNN // LANES - LRU_KSLABS)


LRU_CONV_ROWS = 128


def _lru_kernel(x_ref, gate_ref, cw_ref, cb_ref, wg_ref, gb_ref, lam_ref, wout_ref, o_ref,
                xc_scr, acc_scr, a_scr, u_scr, h_scr, pad_scr):
    c = pl.program_id(1)
    seq = x_ref.shape[0]
    nsl = LRU_CHUNK // LANES
    chains = [(d, i) for d in range(2) for i in range(nsl)]

    @pl.when(c == 0)
    def _conv():
        pad_scr[0:8, :] = jnp.zeros((8, LANES), F32)
        pad_scr[seq + 8:seq + 16, :] = jnp.zeros((8, LANES), F32)
        for j in range(D_RNN // LANES):
            sl = slice(j * LANES, (j + 1) * LANES)
            pad_scr[8:seq + 8, :] = x_ref[:, sl].astype(F32)
            bias = cb_ref[:, sl]
            taps = [cw_ref[k:k + 1, sl] for k in range(LRU_CONV)]

            def conv_body(rb, carry):
                r0 = pl.multiple_of(rb * LRU_CONV_ROWS, LRU_CONV_ROWS)
                xc = bias
                for k in range(LRU_CONV):
                    off = 8 + k - LRU_CONV // 2
                    xc = xc + pad_scr[pl.ds(r0 + off, LRU_CONV_ROWS), :] * taps[k]
                xc_scr[j, pl.ds(r0, LRU_CONV_ROWS), :] = xc
                return carry

            lax.fori_loop(0, seq // LRU_CONV_ROWS, conv_body, 0, unroll=4)

    k0 = _lru_kslab0(c)
    lam = lam_ref[...]
    half_l2 = (-0.5 * LRU_C * LOG2E) * (jnp.maximum(-lam, 0.0) + jnp.log1p(jnp.exp(-jnp.abs(lam))))

    def pitched_rows(s, sb):
        return pl.ds(pl.multiple_of((s * LRU_SUBS_PER_BLOCK + sb) * LRU_PITCH, 8), LRU_SUBLEN)

    def gates_body(s, carry):
        r0 = pl.multiple_of(s * LRU_TBLOCK, LRU_TBLOCK)
        lhs = jnp.concatenate([xc_scr[k0 + i, pl.ds(r0, LRU_TBLOCK), :] for i in range(LRU_KSLABS)],
                              axis=1).astype(BF16)
        half_xc = jnp.concatenate([xc_scr[nsl * c + i, pl.ds(r0, LRU_TBLOCK), :] for i in range(nsl)], axis=1)
        for d in range(2):
            za = jnp.dot(lhs, wg_ref[d, 0], preferred_element_type=F32) + gb_ref[2 * d:2 * d + 1, :]
            zx = jnp.dot(lhs, wg_ref[d, 1], preferred_element_type=F32) + gb_ref[2 * d + 1:2 * d + 2, :]
            hl = half_l2[d:d + 1, :]
            a = jnp.exp2(jnp.tanh(za) * hl + hl)
            t = 1.0 - a * a
            root = jnp.where(t > 0.0, t * lax.rsqrt(t), 0.0)
            u = root * ((jnp.tanh(zx) + 1.0) * half_xc)
            for i in range(nsl):
                for sb in range(LRU_SUBS_PER_BLOCK):
                    rows = slice(sb * LRU_SUBLEN, (sb + 1) * LRU_SUBLEN)
                    a_scr[d, i, pitched_rows(s, sb), :] = a[rows, i * LANES:(i + 1) * LANES]
                    u_scr[d, i, pitched_rows(s, sb), :] = u[rows, i * LANES:(i + 1) * LANES]
        return carry

    lax.fori_loop(0, seq // LRU_TBLOCK, gates_body, 0, unroll=True)

    def step_rows(d, t):
        r = t if d == 0 else LRU_SUBLEN - 1 - t
        return pl.ds(r, LRU_NSUB, stride=LRU_PITCH)

    def ends_body(t, carry):
        out = []
        for n, (d, i) in enumerate(chains):
            av = a_scr[d, i, step_rows(d, t), :]
            out += [av * carry[2 * n] + u_scr[d, i, step_rows(d, t), :], av * carry[2 * n + 1]]
        return tuple(out)

    init = (jnp.zeros((LRU_NSUB, LANES), F32), jnp.ones((LRU_NSUB, LANES), F32)) * len(chains)
    ends = lax.fori_loop(0, LRU_SUBLEN, ends_body, init, unroll=32)

    sub = lax.broadcasted_iota(jnp.int32, (LRU_NSUB, LANES), 0)
    starts = []
    for n, (d, i) in enumerate(chains):
        h_end, cum_end = ends[2 * n], ends[2 * n + 1]
        cin = jnp.zeros((LRU_NSUB, LANES), F32)
        for step in range(LRU_NSUB - 1):
            nxt = h_end + cum_end * cin
            if d == 0:
                cin = jnp.where(sub == step + 1, pltpu.roll(nxt, 1, 0), cin)
            else:
                cin = jnp.where(sub == LRU_NSUB - 2 - step, pltpu.roll(nxt, LRU_NSUB - 1, 0), cin)
        starts.append(cin)

    def scan_body(t, carry):
        out = []
        for n, (d, i) in enumerate(chains):
            h = a_scr[d, i, step_rows(d, t), :] * carry[n] + u_scr[d, i, step_rows(d, t), :]
            h_scr[d, i, step_rows(d, t), :] = h
            out.append(h)
        return tuple(out)

    lax.fori_loop(0, LRU_SUBLEN, scan_body, tuple(starts), unroll=32)

    def out_body(first, s, carry):
        r0 = pl.multiple_of(s * LRU_TBLOCK, LRU_TBLOCK)
        hsum = jnp.concatenate(
            [jnp.concatenate([h_scr[0, i, pitched_rows(s, sb), :] + h_scr[1, i, pitched_rows(s, sb), :]
                              for sb in range(LRU_SUBS_PER_BLOCK)], axis=0) for i in range(nsl)], axis=1)
        y = (_gelu_tanh(gate_ref[pl.ds(r0, LRU_TBLOCK), :].astype(F32)) * hsum).astype(BF16)
        part = jnp.dot(y, wout_ref[...], preferred_element_type=F32)
        if first:
            acc_scr[pl.ds(r0, LRU_TBLOCK), :] = part
        else:
            acc_scr[pl.ds(r0, LRU_TBLOCK), :] += part
        return carry

    @pl.when(c == 0)
    def _out_first():
        lax.fori_loop(0, seq // LRU_TBLOCK, functools.partial(out_body, True), 0, unroll=True)

    @pl.when(c > 0)
    def _out_rest():
        lax.fori_loop(0, seq // LRU_TBLOCK, functools.partial(out_body, False), 0, unroll=True)

    @pl.when(c == LRU_NCHUNK - 1)
    def _emit():
        o_ref[...] = acc_scr[...].astype(BF16)


def _lru(proj3, cw, cb, wg, gb, lam, wout):
    b, seq, _ = proj3.shape
    assert seq == LRU_NSUB * LRU_SUBLEN and seq % LRU_TBLOCK == 0
    nsl = LRU_CHUNK // LANES
    gate_blk0 = D_RNN // LRU_CHUNK
    return pl.pallas_call(
        _lru_kernel,
        grid=(b, LRU_NCHUNK),
        in_specs=[
            pl.BlockSpec((None, seq, D_RNN), lambda i, c: (i, 0, 0)),
            pl.BlockSpec((None, seq, LRU_CHUNK), lambda i, c: (i, 0, gate_blk0 + c)),
            _resident((LRU_CONV, D_RNN)),
            _resident((1, D_RNN)),
            pl.BlockSpec((None, 2, 2, LRU_KSLABS * LANES, LRU_CHUNK), lambda i, c: (c, 0, 0, 0, 0)),
            pl.BlockSpec((None, 4, LRU_CHUNK), lambda i, c: (c, 0, 0)),
            pl.BlockSpec((None, 2, LRU_CHUNK), lambda i, c: (c, 0, 0)),
            pl.BlockSpec((LRU_CHUNK, D_MODEL), lambda i, c: (c, 0)),
        ],
        out_specs=pl.BlockSpec((None, seq, D_MODEL), lambda i, c: (i, 0, 0)),
        out_shape=jax.ShapeDtypeStruct((b, seq, D_MODEL), BF16),
        scratch_shapes=[
            pltpu.VMEM((D_RNN // LANES, seq, LANES), F32),
            pltpu.VMEM((seq, D_MODEL), F32),
            pltpu.VMEM((2, nsl, LRU_NSUB * LRU_PITCH, LANES), F32),
            pltpu.VMEM((2, nsl, LRU_NSUB * LRU_PITCH, LANES), F32),
            pltpu.VMEM((2, nsl, LRU_NSUB * LRU_PITCH, LANES), F32),
            pltpu.VMEM((seq + 16, LANES), F32),
        ],
        compiler_params=_cparams(("parallel", "arbitrary"), LRU_VMEM_LIMIT_BYTES),
        name="rglru",
    )(proj3, proj3, cw, cb, wg, gb, lam, wout)


def _attn_kernel(*refs, rpb, fused, row_max):
    n_in = 1 if fused else 3
    qkv_refs = refs[:n_in]
    bias_ref, shift_ref, o_ref, lse_ref, kt_scr = refs[n_in:]
    length = o_ref.shape[0]
    nkb = length // ATT_QBLOCK
    lane = lax.broadcasted_iota(jnp.int32, (ATT_QBLOCK, LANES), 1)
    lo_half = lane < HEAD_DIM
    lo_keys = lax.broadcasted_iota(jnp.int32, (2 * ATT_QBLOCK, LANES), 1) < HEAD_DIM
    npairs = GROUP_COLS // LANES

    def scores_block(qp_of, kt_of, vw_of, bias):
        outs = []
        lse_tile = jnp.zeros((ATT_QBLOCK, LANES), F32)
        for p in range(npairs):
            qp, kt, vw = qp_of(p), kt_of(p), vw_of(p)
            full, shifts = [], []
            for hh in range(LANES // HEAD_DIM):
                own = lo_half if hh == 0 else jnp.logical_not(lo_half)
                qm = jnp.where(own, qp, jnp.zeros_like(qp))
                s = jnp.dot(qm, kt, preferred_element_type=F32) + bias
                if row_max:
                    m = jnp.max(s, axis=-1, keepdims=True)
                    s = s - m
                    shifts.append(m)
                e = jnp.exp2(s).astype(BF16)
                v1 = jnp.where(lo_keys if hh == 0 else jnp.logical_not(lo_keys), vw, jnp.ones_like(vw))
                full.append(jnp.dot(e, v1, preferred_element_type=F32))
            numer = jnp.where(lo_half, full[0], full[1])
            denom = pltpu.roll(jnp.where(lo_half, full[1], full[0]), HEAD_DIM, 1)
            outs.append(numer * (1.0 / denom))
            shift = jnp.where(lo_half, shifts[0], shifts[1]) if row_max else shift_ref[...]
            lse = LN2 * (shift + jnp.log2(denom))
            lse_tile = jnp.where(jnp.logical_or(lane == p, lane == HEAD_DIM + p), lse, lse_tile)
        return outs, lse_tile

    for r in range(rpb):
        def cols(part, p, r=r):
            base = ((r * 3 + part) * GROUP_COLS if fused else 0) + p * LANES
            return slice(base, base + LANES)

        q_ref, k_ref, v_ref = (qkv_refs[0],) * 3 if fused else qkv_refs

        def transpose_body(kb, carry):
            k0 = pl.multiple_of(kb * ATT_QBLOCK, ATT_QBLOCK)
            for p in range(npairs):
                kt_scr[p, kb] = k_ref[pl.ds(k0, ATT_QBLOCK), cols(1, p)].T
            return carry

        lax.fori_loop(0, nkb, transpose_body, 0, unroll=True)

        def store(rows_list, outs, lse_tile, r=r):
            for dst_rows, src in rows_list:
                for p in range(npairs):
                    c0 = r * GROUP_COLS + p * LANES
                    o_ref[dst_rows, c0:c0 + LANES] = outs[p][src].astype(BF16)
                lse_ref[dst_rows, r * LANES:(r + 1) * LANES] = lse_tile[src]

        def interior(i, carry):
            k0 = pl.multiple_of(i * ATT_QBLOCK, ATT_QBLOCK)
            q0 = pl.multiple_of(k0 + ATT_HALF, ATT_HALF)
            outs, lse_tile = scores_block(
                lambda p: q_ref[pl.ds(q0, ATT_QBLOCK), cols(0, p)],
                lambda p: jnp.concatenate([kt_scr[p, i], kt_scr[p, i + 1]], axis=1),
                lambda p: v_ref[pl.ds(k0, 2 * ATT_QBLOCK), cols(2, p)],
                bias_ref[0])
            store([(pl.ds(q0, ATT_QBLOCK), slice(None))], outs, lse_tile)
            return carry

        if nkb > 1:
            lax.fori_loop(0, nkb - 1, interior, 0, unroll=True)

        head_rows, tail_rows = slice(0, ATT_HALF), slice(length - ATT_HALF, length)
        tail_keys = slice(length - ATT_QBLOCK, length)
        outs, lse_tile = scores_block(
            lambda p: jnp.concatenate([q_ref[head_rows, cols(0, p)], q_ref[tail_rows, cols(0, p)]], axis=0),
            lambda p: jnp.concatenate([kt_scr[p, 0], kt_scr[p, nkb - 1]], axis=1),
            lambda p: jnp.concatenate([v_ref[0:ATT_QBLOCK, cols(2, p)], v_ref[tail_keys, cols(2, p)]], axis=0),
            bias_ref[1])
        store([(head_rows, slice(0, ATT_HALF)), (tail_rows, slice(ATT_HALF, ATT_QBLOCK))], outs, lse_tile)


def _attn_bias():
    i = jnp.arange(ATT_QBLOCK)[:, None]
    j = jnp.arange(2 * ATT_QBLOCK)[None, :]
    interior = jnp.abs(i + ATT_HALF - j) <= ATT_HALF
    first = (i < ATT_HALF) & (j < ATT_QBLOCK) & (jnp.abs(i - j) <= ATT_HALF)
    last = (i >= ATT_HALF) & (j >= ATT_QBLOCK) & (jnp.abs(i - (j - ATT_QBLOCK)) <= ATT_HALF)
    return jnp.stack([jnp.where(interior, 0.0, NEG), jnp.where(first | last, 0.0, NEG)]).astype(F32)


def _attention_group(qkv, b, seq, window, dilation, score_bound):
    length = seq // dilation
    assert (window // 2) // dilation == ATT_HALF and length % ATT_QBLOCK == 0
    pv = qkv.reshape(b, length, qkv.shape[1])
    fused = dilation > 1
    rpb = min(dilation, max(1, ATT_MIN_STEP_TOKENS // length))
    if fused:
        qkv_specs = [pl.BlockSpec((None, length, rpb * 3 * GROUP_COLS), lambda i, m: (i, 0, m))]
    else:
        qkv_specs = [pl.BlockSpec((None, length, GROUP_COLS), lambda i, m, part=part: (i, 0, QKV0_CHUNK0 + part))
                     for part in range(3)]

    def call(row_max, bias, shift):
        return pl.pallas_call(
            functools.partial(_attn_kernel, rpb=rpb, fused=fused, row_max=row_max),
            grid=(b, dilation // rpb),
            in_specs=qkv_specs + [_resident(bias.shape), _resident(shift.shape)],
            out_specs=[
                pl.BlockSpec((None, length, rpb * GROUP_COLS), lambda i, m: (i, 0, m)),
                pl.BlockSpec((None, length, rpb * LANES), lambda i, m: (i, 0, m)),
            ],
            out_shape=[
                jax.ShapeDtypeStruct((b, length, dilation * GROUP_COLS), BF16),
                jax.ShapeDtypeStruct((b, length, dilation * LANES), F32),
            ],
            scratch_shapes=[pltpu.VMEM((GROUP_COLS // LANES, length // ATT_QBLOCK, LANES, ATT_QBLOCK), BF16)],
            compiler_params=_cparams(("parallel", "parallel")),
            name=f"attn_d{dilation}" + ("_rowmax" if row_max else ""),
        )(*([pv] * len(qkv_specs)), bias, shift)

    bias = _attn_bias()
    shift = jnp.full((1, LANES), score_bound, F32)
    o, lse = lax.cond(score_bound <= ATT_MAX_CONST_SHIFT,
                      lambda: call(False, bias - score_bound, shift),
                      lambda: call(True, bias, shift))
    return o.reshape(b * length, dilation * GROUP_COLS), lse.reshape(b * length, dilation * LANES)


def _merge_kernel(x_ref, a_ref, o0_ref, o1_ref, o2_ref, l0_ref, l1_ref, l2_ref, ga_ref, gb_ref,
                  watt_ref, wo_ref, ex_ref, y_ref, o_scr, l_scr):
    tm = x_ref.shape[0]
    nt = GROUP_COLS // LANES
    for gi, (og_ref, lg_ref) in enumerate(((o1_ref, l1_ref), (o2_ref, l2_ref))):
        dil = ATT_GROUPS[gi + 1][1]
        for m in range(dil):
            rows = pl.ds(m, tm // dil, stride=dil)
            l_scr[gi, rows, :] = lg_ref[:, m * LANES:(m + 1) * LANES]
            for t in range(nt):
                col = m * GROUP_COLS + t * LANES
                o_scr[gi, t, rows, :] = og_ref[:, col:col + LANES].astype(F32)
    lses = [l0_ref[...], l_scr[0], l_scr[1]]
    group_out = [o0_ref[...].astype(F32)] + [
        jnp.concatenate([o_scr[gi, t] for t in range(nt)], axis=1) for gi in range(2)]
    mx = jnp.maximum(jnp.maximum(lses[0], lses[1]), lses[2])
    es = [jnp.exp(l - mx) for l in lses]
    den = es[0] + es[1] + es[2]
    ex = ex_ref[...]
    o = None
    for e, og in zip(es, group_out):
        w = e / den
        hi = w.astype(BF16)
        lo = (w - hi.astype(F32)).astype(BF16)
        wx = jnp.dot(hi, ex, preferred_element_type=F32) + jnp.dot(lo, ex, preferred_element_type=F32)
        t = wx * og
        o = t if o is None else o + t
    b_out = jnp.dot(o.astype(BF16), watt_ref[...], preferred_element_type=F32)
    mix = (_sigmoid(ga_ref[...].astype(F32)) * a_ref[...].astype(F32)
           + _sigmoid(gb_ref[...].astype(F32)) * b_out)
    y_ref[...] = x_ref[...] + jnp.dot(mix.astype(BF16), wo_ref[...], preferred_element_type=F32)


def _merge(x2, a_out, outs, lses, mgates, watt, wo, ex):
    n = x2.shape[0]
    tm = TOKEN_TILE
    row = lambda cols, dil=1: pl.BlockSpec((tm // dil, cols * dil), lambda i: (i, 0))
    d1, d2 = ATT_GROUPS[1][1], ATT_GROUPS[2][1]
    return pl.pallas_call(
        _merge_kernel,
        grid=(n // tm,),
        in_specs=[
            row(D_MODEL), row(D_MODEL),
            row(GROUP_COLS), row(GROUP_COLS, d1), row(GROUP_COLS, d2),
            row(LANES), row(LANES, d1), row(LANES, d2),
            pl.BlockSpec((tm, D_MODEL), lambda i: (i, 0)),
            pl.BlockSpec((tm, D_MODEL), lambda i: (i, 1)),
            _resident((GROUP_COLS, D_MODEL)),
            _resident((D_MODEL, D_MODEL)),
            _resident((LANES, GROUP_COLS)),
        ],
        out_specs=row(D_MODEL),
        out_shape=jax.ShapeDtypeStruct((n, D_MODEL), F32),
        scratch_shapes=[
            pltpu.VMEM((2, GROUP_COLS // LANES, tm, LANES), F32),
            pltpu.VMEM((2, tm, LANES), F32),
        ],
        compiler_params=_cparams(("parallel",)),
        name="merge",
    )(x2, a_out, *outs, *lses, mgates, mgates, watt, wo, ex)


FFN_HALO = 16
FFN_TOKEN_TILE = 1024
FFN_CHUNK = 1024


def _ffn_kernel(x_ref, xp_ref, xn_ref, g2_ref, wup_ref, cw_ref, cb_ref, wdn_ref, y_ref, lhs_scr,
                *, tiles_per_seq):
    tm = x_ref.shape[0]
    ti = pl.program_id(0) % tiles_per_seq
    g2 = g2_ref[...]
    x = x_ref[...]
    keep_prev = jnp.where(ti == 0, 0.0, 1.0)
    keep_next = jnp.where(ti == tiles_per_seq - 1, 0.0, 1.0)
    xn = _rms_rows(x, g2).astype(BF16)
    lhs_scr[0:FFN_HALO, :] = (_rms_rows(xp_ref[...], g2) * keep_prev).astype(BF16)
    lhs_scr[FFN_HALO:FFN_HALO + tm, :] = xn
    lhs_scr[FFN_HALO + tm:, :] = (_rms_rows(xn_ref[...], g2) * keep_next).astype(BF16)
    acc = x
    for j in range(D_FF // FFN_CHUNK):
        sl = slice(j * FFN_CHUNK, (j + 1) * FFN_CHUNK)
        gext = jnp.dot(lhs_scr[...], wup_ref[:, sl], preferred_element_type=F32)
        rows = gext.shape[0]
        gc = cb_ref[:, sl]
        for k in range(FFN_CONV):
            shift = (FFN_CONV // 2 - k) % rows
            gk = gext if shift == 0 else pltpu.roll(gext, shift, 0)
            gc = gc + gk[FFN_HALO:FFN_HALO + tm, :] * cw_ref[k:k + 1, sl]
        val = jnp.dot(xn, wup_ref[:, D_FF + j * FFN_CHUNK:D_FF + (j + 1) * FFN_CHUNK],
                      preferred_element_type=F32)
        h = (_gelu_tanh(gc) * val).astype(BF16)
        acc = acc + jnp.dot(h, wdn_ref[sl, :], preferred_element_type=F32)
    y_ref[...] = acc


def _ffn(x2, seq, g2, wup, cw, cb, wdn):
    n = x2.shape[0]
    tm = FFN_TOKEN_TILE
    hb = tm // FFN_HALO
    last = n // FFN_HALO - 1
    return pl.pallas_call(
        functools.partial(_ffn_kernel, tiles_per_seq=seq // tm),
        grid=(n // tm,),
        in_specs=[
            pl.BlockSpec((tm, D_MODEL), lambda i: (i, 0)),
            pl.BlockSpec((FFN_HALO, D_MODEL), lambda i: (jnp.maximum(i * hb - 1, 0), 0)),
            pl.BlockSpec((FFN_HALO, D_MODEL), lambda i: (jnp.minimum((i + 1) * hb, last), 0)),
            _resident((1, D_MODEL)),
            _resident((D_MODEL, 2 * D_FF)),
            _resident((FFN_CONV, D_FF)),
            _resident((1, D_FF)),
            _resident((D_FF, D_MODEL)),
        ],
        out_specs=pl.BlockSpec((tm, D_MODEL), lambda i: (i, 0)),
        out_shape=jax.ShapeDtypeStruct((n, D_MODEL), F32),
        scratch_shapes=[pltpu.VMEM((tm + 2 * FFN_HALO, D_MODEL), BF16)],
        compiler_params=_cparams(("parallel",)),
        name="convffn",
    )(x2, x2, x2, g2, wup, cw, cb, wdn)


def _rope_tables(seq):
    pos = jnp.arange(seq, dtype=F32)
    inv = ROPE_THETA ** (-jnp.arange(0, ROT_DIM, 2, dtype=F32) / ROT_DIM)
    ang = pos[:, None] * inv[None, :]
    cos, sin = jnp.cos(ang), jnp.sin(ang)
    half = ROT_DIM // 2
    pad = HEAD_DIM - ROT_DIM
    one = jnp.ones((seq, pad), F32)
    zero = jnp.zeros((seq, pad), F32)
    zh = jnp.zeros((seq, half), F32)
    per_head = lambda parts: jnp.tile(jnp.concatenate(parts, axis=1), (1, LANES // HEAD_DIM))
    ra = per_head([cos, cos, one])
    rm = per_head([-sin, zh, zero])
    rp = per_head([zh, sin, zero])
    return ra, rm, rp


def _gate_chunk_weights(blocks, c):
    row0 = min(max(2 * c - 1, 0), D_RNN // LANES - LRU_KSLABS) * LANES
    col0, col1 = c * LRU_CHUNK, (c + 1) * LRU_CHUNK
    pieces = []
    for n in range(col0 // LRU_BW, -(-col1 // LRU_BW)):
        lo, hi = max(n * LRU_BW, col0), min((n + 1) * LRU_BW, col1)
        top = n * LRU_BW - row0
        assert 0 <= top and top + LRU_BW <= LRU_KSLABS * LANES
        piece = blocks[:, :, n, :, lo - n * LRU_BW:hi - n * LRU_BW]
        pieces.append(jnp.pad(piece, ((0, 0), (0, 0), (top, LRU_KSLABS * LANES - top - LRU_BW), (0, 0))))
    return jnp.concatenate(pieces, axis=-1)


def _prep_layer(p, seq):
    q = {}
    q['g1'] = p['norm1_g'].reshape(1, D_MODEL)
    q_lo = 2 * D_RNN
    seg = lambda part, g: (q_lo + part * ATT_COLS + g * GROUP_COLS, q_lo + part * ATT_COLS + (g + 1) * GROUP_COLS)
    qkv = lambda g: [seg(part, g) for part in range(3)]
    w_in = p['w_in'].astype(BF16)
    take = lambda segs: jnp.concatenate([w_in[:, lo:hi] for lo, hi in segs], axis=1)
    q['w_main'] = take([(0, q_lo)] + qkv(0))
    q['w_rest'] = take([(q_lo + 3 * ATT_COLS, IN_COLS)] + qkv(1) + qkv(2))
    head = jnp.arange(MXU_DIM) // HEAD_DIM
    q['bd'] = (head[:, None] == head[None, :]).astype(BF16)
    q['qg'] = jnp.tile(p['q_norm_g'], (1, HEADS_PER_GROUP)) * (HEAD_DIM ** -0.5 * LOG2E)
    q['kg'] = jnp.tile(p['k_norm_g'], (1, HEADS_PER_GROUP))
    q['score_bound'] = (HEAD_DIM * ATT_BOUND_MARGIN) * (jnp.max(jnp.abs(q['qg']), axis=1)
                                                        * jnp.max(jnp.abs(q['kg']), axis=1))
    q['rope'] = _rope_tables(seq)
    q['lru_cw'] = 0.5 * p['lru_conv_w']
    q['lru_cb'] = 0.5 * p['lru_conv_b'].reshape(1, D_RNN)
    blocks = jnp.stack([jnp.stack([p['lru_wa'][d], p['lru_wx'][d]]) for d in range(2)]).astype(BF16)
    q['lru_wg'] = jnp.stack([_gate_chunk_weights(blocks, c) for c in range(LRU_NCHUNK)])
    gb = 0.5 * jnp.stack([p['lru_ba'][0], p['lru_bx'][0], p['lru_ba'][1], p['lru_bx'][1]])
    q['lru_gb'] = gb.reshape(4, LRU_NCHUNK, LRU_CHUNK).transpose(1, 0, 2)
    q['lru_lam'] = p['lru_lambda'].reshape(2, LRU_NCHUNK, LRU_CHUNK).transpose(1, 0, 2)
    q['lru_wout'] = p['w_lru_out'].astype(BF16)
    q['watt'] = p['w_att_out'].astype(BF16)
    q['wo'] = p['w_o'].astype(BF16)
    lse_lane = jnp.arange(LANES)
    lane_head = jnp.where(lse_lane % HEAD_DIM < GROUP_COLS // LANES,
                          2 * (lse_lane % HEAD_DIM) + lse_lane // HEAD_DIM, -1)
    q['expand'] = (lane_head[:, None] == jnp.arange(GROUP_COLS)[None, :] // HEAD_DIM).astype(BF16)
    q['g2'] = p['norm2_g'].reshape(1, D_MODEL)
    q['wup'] = p['w_up'].astype(BF16)
    q['ffn_cw'] = p['ffn_conv_w']
    q['ffn_cb'] = p['ffn_conv_b'].reshape(1, D_FF)
    q['wdn'] = p['w_down'].astype(BF16)
    return q


def _layer(x, q):
    b, seq, _ = x.shape
    x2 = x.reshape(b * seq, D_MODEL)
    main, mgates, qkv1, qkv2 = _inproj(x2, seq, q['g1'], q['w_main'], q['w_rest'], q['bd'], q['qg'], q['kg'],
                                       *q['rope'])
    a_out = _lru(main.reshape(b, seq, MAIN_COLS), q['lru_cw'], q['lru_cb'], q['lru_wg'], q['lru_gb'],
                 q['lru_lam'], q['lru_wout'])
    outs, lses = [], []
    for g, (qkv, (window, dilation)) in enumerate(zip((main, qkv1, qkv2), ATT_GROUPS)):
        o, lse = _attention_group(qkv, b, seq, window, dilation, q['score_bound'][g])
        outs.append(o)
        lses.append(lse)
    x1 = _merge(x2, a_out.reshape(b * seq, D_MODEL), outs, lses, mgates, q['watt'], q['wo'], q['expand'])
    y = _ffn(x1, seq, q['g2'], q['wup'], q['ffn_cw'], q['ffn_cb'], q['wdn'])
    return y.reshape(b, seq, D_MODEL)


def kernel(x_prompt, x_sample, norm1_g, w_in, lru_conv_w, lru_conv_b, lru_wa, lru_ba, lru_wx, lru_bx,
           lru_lambda, w_lru_out, q_norm_g, k_norm_g, w_att_out, w_o, norm2_g, w_up, ffn_conv_w,
           ffn_conv_b, w_down):
    params = dict(norm1_g=norm1_g, w_in=w_in, lru_conv_w=lru_conv_w, lru_conv_b=lru_conv_b,
                  lru_wa=lru_wa, lru_ba=lru_ba, lru_wx=lru_wx, lru_bx=lru_bx, lru_lambda=lru_lambda,
                  w_lru_out=w_lru_out, q_norm_g=q_norm_g, k_norm_g=k_norm_g, w_att_out=w_att_out,
                  w_o=w_o, norm2_g=norm2_g, w_up=w_up, ffn_conv_w=ffn_conv_w, ffn_conv_b=ffn_conv_b,
                  w_down=w_down)
    depth = norm1_g.shape[0]
    seq = x_prompt.shape[1]
    assert x_sample.shape[1] == seq
    layers = [_prep_layer({name: p[l] for name, p in params.items()}, seq) for l in range(depth)]
    ys = []
    for x in (x_prompt, x_sample):
        for q in layers:
            x = _layer(x, q)
        ys.append(x)
    return tuple(ys)
```
